```python
import jax
import jax.numpy as jnp
from jax import lax
import numpy as np

D_MODEL = 2048
BATCH = 8
SEQ = 4096
DEPTH = 2

CHUNK = 64
MIX_WIDTH = 2 * D_MODEL
POOL_WIDTH = MIX_WIDTH // 4
N_POOL_GROUPS = 4
POOL_GROUP_DIM = POOL_WIDTH // N_POOL_GROUPS
POOL_WINDOWS = (2, 4, 8, 16)
SB_WIDTH = MIX_WIDTH // 2
SB_HEAD_DIM = 128
SB_HEADS = SB_WIDTH // SB_HEAD_DIM
Q_BLOCK = 128
CONV_CH = MIX_WIDTH // 4
CONV_WIDTH = 3
N_BRANCHES = 3
IN_PROJ_DIM = 2 * POOL_WIDTH + 4 * SB_WIDTH + 4 * CONV_CH
RMS_EPS = 1e-6

kernel_name = "hybrid_pool_stickbreak_shortconv_trunk"


def rms_norm(x, g):
    xf = x.astype(jnp.float32)
    y = xf * lax.rsqrt(jnp.mean(xf * xf, axis=-1, keepdims=True) + RMS_EPS)
    return (y * g.astype(jnp.float32)).astype(x.dtype)


def split_input_projection(proj):
    sizes = (POOL_WIDTH, POOL_WIDTH,
             SB_WIDTH, SB_WIDTH, SB_WIDTH, SB_WIDTH,
             CONV_CH, CONV_CH, CONV_CH, CONV_CH)
    parts, off = [], 0
    for n in sizes:
        parts.append(proj[..., off:off + n])
        off += n
    return parts


def multiscale_pool(xa, pool_w, pool_scale):
    b, s, _ = xa.shape
    xg = xa.astype(jnp.float32).reshape(b, s, N_POOL_GROUPS, POOL_GROUP_DIM)
    cs = jnp.pad(jnp.cumsum(xg, axis=1), ((0, 0), (1, 0), (0, 0), (0, 0)))
    t = jnp.arange(s)
    pooled = []
    for g, w in enumerate(POOL_WINDOWS):
        start = jnp.maximum(t + 1 - w, 0)
        count = (t + 1 - start).astype(jnp.float32)
        win_sum = cs[:, 1:, g] - cs[:, start, g]
        pooled.append(win_sum / count[None, :, None])
    mixed = jnp.stack(pooled, axis=2) - xg
    y = jnp.einsum('bsgc,gcd->bsgd', mixed, pool_w.astype(jnp.float32))
    return (y.reshape(b, s, POOL_WIDTH) * pool_scale).astype(xa.dtype)


def stick_breaking_attention(q, k, v):
    b, s, h, dh = q.shape
    qf = q.astype(jnp.float32) * (dh ** -0.5)
    kf = k.astype(jnp.float32)
    vf = v.astype(jnp.float32)
    outs = []
    for i in range(s // Q_BLOCK):
        q0 = i * Q_BLOCK
        kv_len = q0 + Q_BLOCK
        z = jnp.einsum('bqhd,bkhd->bhqk', qf[:, q0:kv_len], kf[:, :kv_len])
        t_idx = q0 + jnp.arange(Q_BLOCK)[:, None]
        s_idx = jnp.arange(kv_len)[None, :]
        strict = s_idx < t_idx
        log_keep = jnp.where(strict, -jax.nn.softplus(z), 0.0)
        between = lax.cumsum(log_keep, axis=3, reverse=True) - log_keep
        log_a = jax.nn.log_sigmoid(z) + between
        a = jnp.where(strict, jnp.exp(log_a), 0.0)
        outs.append(jnp.einsum('bhqk,bkhd->bqhd', a, vf[:, :kv_len]))
    return jnp.concatenate(outs, axis=1).astype(q.dtype)


def short_gated_conv(u, b_gate, c_gate, conv_w):
    s = u.shape[1]
    v = c_gate * u
    vp = jnp.pad(v, ((0, 0), (CONV_WIDTH - 1, 0), (0, 0)))
    y = conv_w[0] * vp[:, 0:s]
    for i in range(1, CONV_WIDTH):
        y = y + conv_w[i] * vp[:, i:i + s]
    return b_gate * y


def hybrid_layer(x, c, norm_g, w_ada, b_ada, w_in, pool_w, pool_scale, conv_w,
                 w_br_a, w_br_b, w_br_c, w_gate, b_gate, w_out):
    b, s, _ = x.shape
    mod = jax.nn.silu(c) @ w_ada + b_ada
    shift, scale, res_gate = jnp.split(mod, 3, axis=-1)
    h = rms_norm(x, norm_g) * (1.0 + scale[:, None, :]) + shift[:, None, :]
    xa, za, q, k, v, zb, u, bg, cg, zc = split_input_projection(h @ w_in)
    ya = multiscale_pool(xa, pool_w, pool_scale) * jax.nn.silu(za)
    hs = (b, s, SB_HEADS, SB_HEAD_DIM)
    yb = stick_breaking_attention(q.reshape(hs), k.reshape(hs), v.reshape(hs))
    yb = yb.reshape(b, s, SB_WIDTH) * jax.nn.silu(zb)
    yc = short_gated_conv(u, bg, cg, conv_w) * jax.nn.silu(zc)
    gates = jax.nn.sigmoid((h @ w_gate + b_gate).astype(jnp.float32)).astype(x.dtype)
    gates = gates.reshape(b, s, N_BRANCHES, D_MODEL)
    merged = (gates[:, :, 0] * (ya @ w_br_a)
              + gates[:, :, 1] * (yb @ w_br_b)
              + gates[:, :, 2] * (yc @ w_br_c))
    return x + res_gate[:, None, :] * (merged @ w_out)


def _fwd_setup_inputs(seed: int = 0) -> dict:
    key = jax.random.key(seed)
    ks = jax.random.split(key, 18)

    def nrm(k, shape, scale):
        return scale * jax.random.normal(k, shape, jnp.float32)

    return {
        "x": nrm(ks[0], (BATCH, SEQ, D_MODEL), 1.0),
        "c": nrm(ks[1], (BATCH, D_MODEL), 1.0),
        "norm_g": 1.0 + nrm(ks[2], (DEPTH, D_MODEL), 0.1),
        "w_ada": nrm(ks[3], (DEPTH, D_MODEL, 3 * D_MODEL), D_MODEL ** -0.5),
        "b_ada": nrm(ks[4], (DEPTH, 3 * D_MODEL), 0.02),
        "w_in": nrm(ks[5], (DEPTH, D_MODEL, IN_PROJ_DIM), D_MODEL ** -0.5),
        "pool_w": nrm(ks[6], (DEPTH, N_POOL_GROUPS, POOL_GROUP_DIM, POOL_GROUP_DIM), POOL_GROUP_DIM ** -0.5),
        "pool_scale": 1.0 + nrm(ks[7], (DEPTH, POOL_WIDTH), 0.1),
        "conv_w": nrm(ks[8], (DEPTH, CONV_WIDTH, CONV_CH), CONV_WIDTH ** -0.5),
        "w_br_a": nrm(ks[9], (DEPTH, POOL_WIDTH, D_MODEL), POOL_WIDTH ** -0.5),
        "w_br_b": nrm(ks[10], (DEPTH, SB_WIDTH, D_MODEL), SB_WIDTH ** -0.5),
        "w_br_c": nrm(ks[11], (DEPTH, CONV_CH, D_MODEL), CONV_CH ** -0.5),
        "w_gate": nrm(ks[12], (DEPTH, D_MODEL, N_BRANCHES * D_MODEL), D_MODEL ** -0.5),
        "b_gate": nrm(ks[13], (DEPTH, N_BRANCHES * D_MODEL), 0.1),
        "w_out": nrm(ks[14], (DEPTH, D_MODEL, D_MODEL), D_MODEL ** -0.5),
        "final_g": 1.0 + nrm(ks[15], (D_MODEL,), 0.1),
    }


def _fwd_reference(x, c, norm_g, w_ada, b_ada, w_in, pool_w, pool_scale, conv_w,
              w_br_a, w_br_b, w_br_c, w_gate, b_gate, w_out, final_g):
    for l in range(DEPTH):
        x = hybrid_layer(x, c, norm_g[l], w_ada[l], b_ada[l], w_in[l], pool_w[l],
                         pool_scale[l], conv_w[l], w_br_a[l], w_br_b[l], w_br_c[l],
                         w_gate[l], b_gate[l], w_out[l])
    return rms_norm(x, final_g)


import jax as _jax
import jax.numpy as _jnp

TWIN_FORMAT = 'train_step'
FWD_PARAMS = ['x', 'c', 'norm_g', 'w_ada', 'b_ada', 'w_in', 'pool_w', 'pool_scale', 'conv_w', 'w_br_a', 'w_br_b', 'w_br_c', 'w_gate', 'b_gate', 'w_out', 'final_g']
TWIN_WEIGHTS = ['norm_g', 'w_ada', 'b_ada', 'w_in', 'pool_w', 'pool_scale', 'conv_w', 'w_br_a', 'w_br_b', 'w_br_c', 'w_gate', 'b_gate', 'w_out', 'final_g']
TWIN_DIFF_INPUT = 'x'
TWIN_INPUTS = ['x', 'c', 'norm_g', 'w_ada', 'b_ada', 'w_in', 'pool_w', 'pool_scale', 'conv_w', 'w_br_a', 'w_br_b', 'w_br_c', 'w_gate', 'b_gate', 'w_out', 'final_g', 'loss_target', 'm_norm_g', 'm_w_ada', 'm_b_ada', 'm_w_in', 'm_pool_w', 'm_pool_scale', 'm_conv_w', 'm_w_br_a', 'm_w_br_b', 'm_w_br_c', 'm_w_gate', 'm_b_gate', 'm_w_out', 'm_final_g', 'v_norm_g', 'v_w_ada', 'v_b_ada', 'v_w_in', 'v_pool_w', 'v_pool_scale', 'v_conv_w', 'v_w_br_a', 'v_w_br_b', 'v_w_br_c', 'v_w_gate', 'v_b_gate', 'v_w_out', 'v_final_g']
TWIN_OUTPUTS = ['loss', 'grad_x', 'grad_norm_g', 'grad_w_ada', 'grad_b_ada', 'grad_w_in', 'grad_pool_w', 'grad_pool_scale', 'grad_conv_w', 'grad_w_br_a', 'grad_w_br_b', 'grad_w_br_c', 'grad_w_gate', 'grad_b_gate', 'grad_w_out', 'grad_final_g', 'delta_norm_g', 'delta_w_ada', 'delta_b_ada', 'delta_w_in', 'delta_pool_w', 'delta_pool_scale', 'delta_conv_w', 'delta_w_br_a', 'delta_w_br_b', 'delta_w_br_c', 'delta_w_gate', 'delta_b_gate', 'delta_w_out', 'delta_final_g', 'new_m_norm_g', 'new_m_w_ada', 'new_m_b_ada', 'new_m_w_in', 'new_m_pool_w', 'new_m_pool_scale', 'new_m_conv_w', 'new_m_w_br_a', 'new_m_w_br_b', 'new_m_w_br_c', 'new_m_w_gate', 'new_m_b_gate', 'new_m_w_out', 'new_m_final_g', 'new_v_norm_g', 'new_v_w_ada', 'new_v_b_ada', 'new_v_w_in', 'new_v_pool_w', 'new_v_pool_scale', 'new_v_conv_w', 'new_v_w_br_a', 'new_v_w_br_b', 'new_v_w_br_c', 'new_v_w_gate', 'new_v_b_gate', 'new_v_w_out', 'new_v_final_g']
TWIN_LEAF_KINDS = {'loss': 'loss', 'grad_x': 'grad_x', 'grad_norm_g': 'grad_w', 'grad_w_ada': 'grad_w', 'grad_b_ada': 'grad_w', 'grad_w_in': 'grad_w', 'grad_pool_w': 'grad_w', 'grad_pool_scale': 'grad_w', 'grad_conv_w': 'grad_w', 'grad_w_br_a': 'grad_w', 'grad_w_br_b': 'grad_w', 'grad_w_br_c': 'grad_w', 'grad_w_gate': 'grad_w', 'grad_b_gate': 'grad_w', 'grad_w_out': 'grad_w', 'grad_final_g': 'grad_w', 'delta_norm_g': 'delta_w', 'delta_w_ada': 'delta_w', 'delta_b_ada': 'delta_w', 'delta_w_in': 'delta_w', 'delta_pool_w': 'delta_w', 'delta_pool_scale': 'delta_w', 'delta_conv_w': 'delta_w', 'delta_w_br_a': 'delta_w', 'delta_w_br_b': 'delta_w', 'delta_w_br_c': 'delta_w', 'delta_w_gate': 'delta_w', 'delta_b_gate': 'delta_w', 'delta_w_out': 'delta_w', 'delta_final_g': 'delta_w', 'new_m_norm_g': 'new_m', 'new_m_w_ada': 'new_m', 'new_m_b_ada': 'new_m', 'new_m_w_in': 'new_m', 'new_m_pool_w': 'new_m', 'new_m_pool_scale': 'new_m', 'new_m_conv_w': 'new_m', 'new_m_w_br_a': 'new_m', 'new_m_w_br_b': 'new_m', 'new_m_w_br_c': 'new_m', 'new_m_w_gate': 'new_m', 'new_m_b_gate': 'new_m', 'new_m_w_out': 'new_m', 'new_m_final_g': 'new_m', 'new_v_norm_g': 'new_v', 'new_v_w_ada': 'new_v', 'new_v_b_ada': 'new_v', 'new_v_w_in': 'new_v', 'new_v_pool_w': 'new_v', 'new_v_pool_scale': 'new_v', 'new_v_conv_w': 'new_v', 'new_v_w_br_a': 'new_v', 'new_v_w_br_b': 'new_v', 'new_v_w_br_c': 'new_v', 'new_v_w_gate': 'new_v', 'new_v_b_gate': 'new_v', 'new_v_w_out': 'new_v', 'new_v_final_g': 'new_v'}


def _forward(args):
    return _fwd_reference(*[args[k] for k in FWD_PARAMS])


def _output_shape():
    def fwd():
        inp = _fwd_setup_inputs(0)
        return _fwd_reference(*[inp[k] for k in FWD_PARAMS])
    out = _jax.eval_shape(fwd)
    return out.shape, out.dtype

N_MICROBATCH = 1
ADAM_LR = 0.001
ADAM_B1 = 0.9
ADAM_B2 = 0.999
ADAM_EPS = 1e-08
ADAM_WD = 0.01
ADAM_STEP = 10
PER_EXAMPLE_BATCH_AXIS = {'x': 0, 'c': 0, 'loss_target': 0}
SHARED_INPUTS = []
_WEIGHT_DTYPES = {'norm_g': _jnp.float32, 'w_ada': _jnp.float32, 'b_ada': _jnp.float32, 'w_in': _jnp.float32, 'pool_w': _jnp.float32, 'pool_scale': _jnp.float32, 'conv_w': _jnp.float32, 'w_br_a': _jnp.float32, 'w_br_b': _jnp.float32, 'w_br_c': _jnp.float32, 'w_gate': _jnp.float32, 'b_gate': _jnp.float32, 'w_out': _jnp.float32, 'final_g': _jnp.float32}
MOMENT_SCALE = {'norm_g': 1.000365e-01, 'w_ada': 1.591459e-01, 'b_ada': 3.602698e-01, 'w_in': 4.089256e-02, 'pool_w': 3.171573e-02, 'pool_scale': 3.225318e-02, 'conv_w': 7.034844e-02, 'w_br_a': 2.263576e-02, 'w_br_b': 2.278941e-02, 'w_br_c': 5.075210e-02, 'w_gate': 1.534452e-02, 'b_gate': 1.231689e-02, 'w_out': 6.180236e-02, 'final_g': 1.630405e+01}


def _to_microbatches(a, axis):
    t = _jnp.moveaxis(a, axis, 0)
    t = t.reshape((N_MICROBATCH, t.shape[0] // N_MICROBATCH) + t.shape[1:])
    return _jnp.moveaxis(t, 1, axis + 1)


def setup_inputs(seed: int = 0) -> dict:
    inp = _fwd_setup_inputs(seed)
    key = _jax.random.fold_in(_jax.random.key(seed), 7919)
    shape, _ = _output_shape()
    out = dict(inp)
    out["loss_target"] = _jax.random.normal(_jax.random.fold_in(key, 0), shape, _jnp.float32)
    for i, name in enumerate(TWIN_WEIGHTS):
        w = inp[name].astype(_jnp.float32)
        if MOMENT_SCALE is None:
            s = _jnp.sqrt(_jnp.mean(_jnp.square(w)) + 1e-30)
        else:
            s = MOMENT_SCALE[name]
        km, kv = _jax.random.split(_jax.random.fold_in(key, i + 1))
        out[name] = w
        out["m_" + name] = s * _jax.random.normal(km, w.shape, _jnp.float32)
        out["v_" + name] = (s * s) * _jax.random.uniform(kv, w.shape, _jnp.float32, 0.5, 1.5)
    if N_MICROBATCH > 1:
        for name, axis in PER_EXAMPLE_BATCH_AXIS.items():
            out[name] = _to_microbatches(out[name], axis)
    return {'x': out['x'], 'c': out['c'], 'norm_g': out['norm_g'], 'w_ada': out['w_ada'], 'b_ada': out['b_ada'], 'w_in': out['w_in'], 'pool_w': out['pool_w'], 'pool_scale': out['pool_scale'], 'conv_w': out['conv_w'], 'w_br_a': out['w_br_a'], 'w_br_b': out['w_br_b'], 'w_br_c': out['w_br_c'], 'w_gate': out['w_gate'], 'b_gate': out['b_gate'], 'w_out': out['w_out'], 'final_g': out['final_g'], 'loss_target': out['loss_target'], 'm_norm_g': out['m_norm_g'], 'm_w_ada': out['m_w_ada'], 'm_b_ada': out['m_b_ada'], 'm_w_in': out['m_w_in'], 'm_pool_w': out['m_pool_w'], 'm_pool_scale': out['m_pool_scale'], 'm_conv_w': out['m_conv_w'], 'm_w_br_a': out['m_w_br_a'], 'm_w_br_b': out['m_w_br_b'], 'm_w_br_c': out['m_w_br_c'], 'm_w_gate': out['m_w_gate'], 'm_b_gate': out['m_b_gate'], 'm_w_out': out['m_w_out'], 'm_final_g': out['m_final_g'], 'v_norm_g': out['v_norm_g'], 'v_w_ada': out['v_w_ada'], 'v_b_ada': out['v_b_ada'], 'v_w_in': out['v_w_in'], 'v_pool_w': out['v_pool_w'], 'v_pool_scale': out['v_pool_scale'], 'v_conv_w': out['v_conv_w'], 'v_w_br_a': out['v_w_br_a'], 'v_w_br_b': out['v_w_br_b'], 'v_w_br_c': out['v_w_br_c'], 'v_w_gate': out['v_w_gate'], 'v_b_gate': out['v_b_gate'], 'v_w_out': out['v_w_out'], 'v_final_g': out['v_final_g']}


def _loss(weights, diff, rest, loss_target):
    with _jax.named_scope("forward"):
        args = {**rest, TWIN_DIFF_INPUT: diff, **{k: w.astype(_WEIGHT_DTYPES[k]) for k, w in weights.items()}}
        y = _forward(args)
    with _jax.named_scope("loss_head"):
        err = _jnp.square(y.astype(_jnp.float32) - loss_target)
        return 0.5 * _jnp.sum(_jnp.mean(err, axis=-1)) if err.ndim else 0.5 * err


def _adamw(w, g, m, v):
    m = ADAM_B1 * m + (1.0 - ADAM_B1) * g
    v = ADAM_B2 * v + (1.0 - ADAM_B2) * _jnp.square(g)
    m_hat = m / (1.0 - ADAM_B1 ** ADAM_STEP)
    v_hat = v / (1.0 - ADAM_B2 ** ADAM_STEP)
    delta = -ADAM_LR * (m_hat / (_jnp.sqrt(v_hat) + ADAM_EPS) + ADAM_WD * w)
    return delta, m, v


def reference(x, c, norm_g, w_ada, b_ada, w_in, pool_w, pool_scale, conv_w, w_br_a, w_br_b, w_br_c, w_gate, b_gate, w_out, final_g, loss_target, m_norm_g, m_w_ada, m_b_ada, m_w_in, m_pool_w, m_pool_scale, m_conv_w, m_w_br_a, m_w_br_b, m_w_br_c, m_w_gate, m_b_gate, m_w_out, m_final_g, v_norm_g, v_w_ada, v_b_ada, v_w_in, v_pool_w, v_pool_scale, v_conv_w, v_w_br_a, v_w_br_b, v_w_br_c, v_w_gate, v_b_gate, v_w_out, v_final_g):
    given = dict(x=x, c=c, norm_g=norm_g, w_ada=w_ada, b_ada=b_ada, w_in=w_in, pool_w=pool_w, pool_scale=pool_scale, conv_w=conv_w, w_br_a=w_br_a, w_br_b=w_br_b, w_br_c=w_br_c, w_gate=w_gate, b_gate=b_gate, w_out=w_out, final_g=final_g, loss_target=loss_target, m_norm_g=m_norm_g, m_w_ada=m_w_ada, m_b_ada=m_b_ada, m_w_in=m_w_in, m_pool_w=m_pool_w, m_pool_scale=m_pool_scale, m_conv_w=m_conv_w, m_w_br_a=m_w_br_a, m_w_br_b=m_w_br_b, m_w_br_c=m_w_br_c, m_w_gate=m_w_gate, m_b_gate=m_b_gate, m_w_out=m_w_out, m_final_g=m_final_g, v_norm_g=v_norm_g, v_w_ada=v_w_ada, v_b_ada=v_b_ada, v_w_in=v_w_in, v_pool_w=v_pool_w, v_pool_scale=v_pool_scale, v_conv_w=v_conv_w, v_w_br_a=v_w_br_a, v_w_br_b=v_w_br_b, v_w_br_c=v_w_br_c, v_w_gate=v_w_gate, v_b_gate=v_b_gate, v_w_out=v_w_out, v_final_g=v_final_g)
    weights = {n: given[n] for n in TWIN_WEIGHTS}
    shared = {n: given[n] for n in SHARED_INPUTS}
    per_example = {n: given[n] for n in ['x', 'c']}
    grad_fn = _jax.value_and_grad(_loss, argnums=(0, 1))

    def one_microbatch(ex, loss_target):
        ex = dict(ex)
        diff = ex.pop(TWIN_DIFF_INPUT)
        return grad_fn(weights, diff, {**shared, **ex}, loss_target)

    if N_MICROBATCH == 1:
        loss, (grad_w, grad_x) = one_microbatch(per_example, given["loss_target"])
    else:
        def body(carry, xs):
            loss_sum, grad_sum = carry
            l_k, (gw_k, gx_k) = one_microbatch(xs[0], xs[1])
            with _jax.named_scope("update"):
                return (loss_sum + l_k, _jax.tree.map(_jnp.add, grad_sum, gw_k)), gx_k

        init = (_jnp.zeros((), _jnp.float32), _jax.tree.map(_jnp.zeros_like, weights))
        (loss, grad_w), grad_x = _jax.lax.scan(body, init, (per_example, given["loss_target"]))
    with _jax.named_scope("update"):
        delta_w, new_m, new_v = {}, {}, {}
        for n in TWIN_WEIGHTS:
            delta_w[n], new_m[n], new_v[n] = _adamw(weights[n], grad_w[n], given["m_" + n], given["v_" + n])
    return (loss, grad_x, *[grad_w[n] for n in TWIN_WEIGHTS], *[delta_w[n] for n in TWIN_WEIGHTS],
            *[new_m[n] for n in TWIN_WEIGHTS], *[new_v[n] for n in TWIN_WEIGHTS])
```

```python
import functools

import jax
import jax.numpy as jnp
from jax import lax
from jax.experimental import pallas as pl
from jax.experimental.pallas import tpu as pltpu

F32 = jnp.float32
BF16 = jnp.bfloat16
MESH = pl.DeviceIdType.MESH

N_DEV = 8
DEPTH = 2
HEAD_DIM = 128
N_POOL_GROUPS = 4
POOL_WINDOWS = (2, 4, 8, 16)
RMS_EPS = 1e-6
LANE = 128
HALO = 16
KEY_CHUNK = 128
ADAM_LR = 0.001
ADAM_B1 = 0.9
ADAM_B2 = 0.999
ADAM_EPS = 1e-08
ADAM_WD = 0.01
ADAM_STEP = 10


def _cparams(semantics, vmem_mb):
    return pltpu.CompilerParams(dimension_semantics=semantics, vmem_limit_bytes=vmem_mb << 20)


def _sigmoid(z):
    return 1.0 / (1.0 + jnp.exp(-z))


def _mm_nn(a, w, *, out_dtype, tm, name, bias=None, out=None, out_cols=None, out_col_block=0):
    m, k = a.shape
    nb, _, n = w.shape
    total_cols = out.shape[1] if out is not None else (nb * n if out_cols is None else out_cols)

    def body(*refs):
        if out is not None:
            refs = refs[:-2] + refs[-1:]
        if bias is not None:
            a_ref, w_ref, b_ref, o_ref = refs
        else:
            a_ref, w_ref, o_ref = refs
        acc = jnp.dot(a_ref[...], w_ref[...], preferred_element_type=F32)
        if bias is not None:
            acc = acc + b_ref[...]
        o_ref[...] = acc.astype(out_dtype)

    in_specs = [pl.BlockSpec((tm, k), lambda j, i: (i, 0)),
                pl.BlockSpec((None, k, n), lambda j, i: (j, 0, 0))]
    args = [a, w]
    if bias is not None:
        in_specs.append(pl.BlockSpec((1, n), lambda j, i: (0, j)))
        args.append(bias)
    aliases = {}
    if out is not None:
        in_specs.append(pl.BlockSpec(memory_space=pl.ANY))
        aliases = {len(args): 0}
        args.append(out)
    return pl.pallas_call(
        body, grid=(nb, m // tm), in_specs=in_specs,
        out_specs=pl.BlockSpec((tm, n), lambda j, i: (i, out_col_block + j)),
        out_shape=jax.ShapeDtypeStruct((m, total_cols), out_dtype),
        input_output_aliases=aliases, name=name,
        compiler_params=_cparams(("arbitrary", "arbitrary"), 56),
    )(*args)


def _mm_nt(dy, w, *, out_dtype, tm, name, addend=None, col_block=0):
    m = dy.shape[0]
    nb, k, n = w.shape

    def body(*refs):
        if addend is not None:
            dy_ref, w_ref, add_ref, o_ref, acc_ref = refs
        else:
            dy_ref, w_ref, o_ref, acc_ref = refs
        j = pl.program_id(1)
        part = lax.dot_general(dy_ref[...], w_ref[...], (((1,), (1,)), ((), ())),
                               preferred_element_type=F32)

        @pl.when(j == 0)
        def _():
            acc_ref[...] = part if addend is None else part + add_ref[...]

        @pl.when(j > 0)
        def _():
            acc_ref[...] += part

        @pl.when(j == nb - 1)
        def _():
            o_ref[...] = acc_ref[...].astype(out_dtype)

    in_specs = [pl.BlockSpec((tm, n), lambda i, j: (i, col_block + j)),
                pl.BlockSpec((None, k, n), lambda i, j: (j, 0, 0))]
    args = [dy, w]
    if addend is not None:
        in_specs.append(pl.BlockSpec((tm, k), lambda i, j: (i, 0)))
        args.append(addend)
    return pl.pallas_call(
        body, grid=(m // tm, nb), in_specs=in_specs,
        out_specs=pl.BlockSpec((tm, k), lambda i, j: (i, 0)),
        out_shape=jax.ShapeDtypeStruct((m, k), out_dtype),
        scratch_shapes=[pltpu.VMEM((tm, k), F32)], name=name,
        compiler_params=_cparams(("arbitrary", "arbitrary"), 56),
    )(*args)


def _mm_tn(a, dy, *, nb, n, tk, tmm, name, col_block=0, out_dtype=BF16):
    m, k = a.shape

    def body(a_ref, dy_ref, o_ref, acc_ref):
        t = pl.program_id(2)
        part = lax.dot_general(a_ref[...], dy_ref[...], (((0,), (0,)), ((), ())),
                               preferred_element_type=F32)

        @pl.when(t == 0)
        def _():
            acc_ref[...] = part

        @pl.when(t > 0)
        def _():
            acc_ref[...] += part

        @pl.when(t == pl.num_programs(2) - 1)
        def _():
            o_ref[...] = acc_ref[...].astype(out_dtype)

    return pl.pallas_call(
        body, grid=(nb, k // tk, m // tmm),
        in_specs=[pl.BlockSpec((tmm, tk), lambda j, kb, t: (t, kb)),
                  pl.BlockSpec((tmm, n), lambda j, kb, t: (t, col_block + j))],
        out_specs=pl.BlockSpec((None, tk, n), lambda j, kb, t: (j, kb, 0)),
        out_shape=jax.ShapeDtypeStruct((nb, k, n), out_dtype),
        scratch_shapes=[pltpu.VMEM((tk, n), F32)], name=name,
        compiler_params=_cparams(("arbitrary", "arbitrary", "arbitrary"), 56),
    )(a, dy)


def _prenorm(x, g, scale, shift, *, tm, name):
    t, d = x.shape

    def body(x_ref, g_ref, sc_ref, sh_ref, h_ref):
        xv = x_ref[...]
        r = lax.rsqrt(jnp.mean(xv * xv, axis=-1, keepdims=True) + RMS_EPS)
        h_ref[...] = ((xv * r) * g_ref[...] * (1.0 + sc_ref[...]) + sh_ref[...]).astype(BF16)

    row = pl.BlockSpec((1, d), lambda i: (0, 0))
    tile = pl.BlockSpec((tm, d), lambda i: (i, 0))
    return pl.pallas_call(
        body, grid=(t // tm,), in_specs=[tile, row, row, row], out_specs=tile,
        out_shape=jax.ShapeDtypeStruct((t, d), BF16), name=name,
        compiler_params=_cparams(("arbitrary",), 40),
    )(x, g, scale, shift)


def _prenorm_bwd(dh, x, dout, g, scale, *, tm, name):
    t, d = x.shape

    def body(dh_ref, x_ref, do_ref, g_ref, sc_ref, dx_ref, dg_ref, dsc_ref, dsh_ref):
        i = pl.program_id(0)
        xv = x_ref[...]
        dhv = dh_ref[...]
        r = lax.rsqrt(jnp.mean(xv * xv, axis=-1, keepdims=True) + RMS_EPS)
        xn = xv * r
        gain = g_ref[...] * (1.0 + sc_ref[...])
        dxn = dhv * gain
        dx_ref[...] = do_ref[...] + r * (dxn - xn * jnp.mean(dxn * xn, axis=-1, keepdims=True))
        dhxn = dhv * xn
        dg = jnp.sum(dhxn * (1.0 + sc_ref[...]), axis=0, keepdims=True)
        dsc = jnp.sum(dhxn * g_ref[...], axis=0, keepdims=True)
        dsh = jnp.sum(dhv, axis=0, keepdims=True)

        @pl.when(i == 0)
        def _():
            dg_ref[...] = dg
            dsc_ref[...] = dsc
            dsh_ref[...] = dsh

        @pl.when(i > 0)
        def _():
            dg_ref[...] += dg
            dsc_ref[...] += dsc
            dsh_ref[...] += dsh

    row = pl.BlockSpec((1, d), lambda i: (0, 0))
    tile = pl.BlockSpec((tm, d), lambda i: (i, 0))
    vec = jax.ShapeDtypeStruct((1, d), F32)
    return pl.pallas_call(
        body, grid=(t // tm,), in_specs=[tile, tile, tile, row, row],
        out_specs=[tile, row, row, row],
        out_shape=[jax.ShapeDtypeStruct((t, d), F32), vec, vec, vec], name=name,
        compiler_params=_cparams(("arbitrary",), 48),
    )(dh, x, dout, g, scale)


def _out_fwd(merged, w_out, x, res_gate, *, tm, name):
    t, d = x.shape
    k = merged.shape[1]

    def body(m_ref, w_ref, x_ref, rg_ref, xo_ref, mo_ref):
        mo = jnp.dot(m_ref[...], w_ref[...], preferred_element_type=F32)
        mo_ref[...] = mo.astype(BF16)
        xo_ref[...] = x_ref[...] + rg_ref[...] * mo

    tile = pl.BlockSpec((tm, d), lambda i: (i, 0))
    return pl.pallas_call(
        body, grid=(t // tm,),
        in_specs=[pl.BlockSpec((tm, k), lambda i: (i, 0)), pl.BlockSpec((k, d), lambda i: (0, 0)),
                  tile, pl.BlockSpec((1, d), lambda i: (0, 0))],
        out_specs=[tile, tile],
        out_shape=[jax.ShapeDtypeStruct((t, d), F32), jax.ShapeDtypeStruct((t, d), BF16)], name=name,
        compiler_params=_cparams(("arbitrary",), 56),
    )(merged, w_out, x, res_gate)


def _out_bwd(dout, mo, res_gate, *, tm, name):
    t, d = dout.shape

    def body(do_ref, mo_ref, rg_ref, dmo_ref, drg_ref):
        i = pl.program_id(0)
        dov = do_ref[...]
        dmo_ref[...] = (dov * rg_ref[...]).astype(BF16)
        drg = jnp.sum(dov * mo_ref[...].astype(F32), axis=0, keepdims=True)

        @pl.when(i == 0)
        def _():
            drg_ref[...] = drg

        @pl.when(i > 0)
        def _():
            drg_ref[...] += drg

    row = pl.BlockSpec((1, d), lambda i: (0, 0))
    tile = pl.BlockSpec((tm, d), lambda i: (i, 0))
    return pl.pallas_call(
        body, grid=(t // tm,), in_specs=[tile, tile, row], out_specs=[tile, row],
        out_shape=[jax.ShapeDtypeStruct((t, d), BF16), jax.ShapeDtypeStruct((1, d), F32)], name=name,
        compiler_params=_cparams(("arbitrary",), 40),
    )(dout, mo, res_gate)


def _loss_head(x, target, final_g, *, tm, name):
    t, d = x.shape

    def body(x_ref, t_ref, g_ref, dx_ref, loss_ref, dg_ref):
        i = pl.program_id(0)
        xv = x_ref[...]
        r = lax.rsqrt(jnp.mean(xv * xv, axis=-1, keepdims=True) + RMS_EPS)
        xn = xv * r
        err = xn * g_ref[...] - t_ref[...]
        part = (0.5 / d) * jnp.sum(jnp.sum(err * err, axis=1, keepdims=True), axis=0, keepdims=True)
        dy = err * (1.0 / d)
        dxn = dy * g_ref[...]
        dx_ref[...] = r * (dxn - xn * jnp.mean(dxn * xn, axis=-1, keepdims=True))
        dg = jnp.sum(dy * xn, axis=0, keepdims=True)

        @pl.when(i == 0)
        def _():
            loss_ref[...] = jnp.zeros_like(loss_ref) + part
            dg_ref[...] = dg

        @pl.when(i > 0)
        def _():
            loss_ref[...] += part
            dg_ref[...] += dg

    row = pl.BlockSpec((1, d), lambda i: (0, 0))
    tile = pl.BlockSpec((tm, d), lambda i: (i, 0))
    return pl.pallas_call(
        body, grid=(t // tm,), in_specs=[tile, tile, row],
        out_specs=[tile, pl.BlockSpec((8, LANE), lambda i: (0, 0)), row],
        out_shape=[jax.ShapeDtypeStruct((t, d), F32), jax.ShapeDtypeStruct((8, LANE), F32),
                   jax.ShapeDtypeStruct((1, d), F32)], name=name,
        compiler_params=_cparams(("arbitrary",), 40),
    )(x, target, final_g)


def _cur(tm, cw, col):
    return pl.BlockSpec((tm, cw), lambda cb, i: (i, col + cb))


def _prev(tm, cw, col):
    return pl.BlockSpec((HALO, cw), lambda cb, i: (jnp.maximum(i * (tm // HALO) - 1, 0), col + cb))


def _next(tm, cw, col, t):
    last = t // HALO - 1
    return pl.BlockSpec((HALO, cw), lambda cb, i: (jnp.minimum((i + 1) * (tm // HALO), last), col + cb))


def _with_prev(prev_ref, cur, first):
    prev = jnp.where(first, 0.0, prev_ref[...].astype(F32))
    return jnp.concatenate([prev, cur], axis=0)


def _with_next(cur, next_ref, last):
    nxt = jnp.where(last, 0.0, next_ref[...].astype(F32))
    return jnp.concatenate([cur, nxt], axis=0)


def _pool_mixed(xe, cur, g, row0, tm):
    s2 = xe + pltpu.roll(xe, 1, 0)
    s4 = s2 + pltpu.roll(s2, 2, 0)
    s8 = s4 + pltpu.roll(s4, 4, 0)
    s16 = s8 + pltpu.roll(s8, 8, 0)
    win = jnp.where(g == 0, s2, jnp.where(g == 1, s4, jnp.where(g == 2, s8, s16)))[HALO:]
    return win * _pool_inv_count(g, row0, tm, cur.shape[1]) - cur


def _pool_inv_count(g, row0, rows, cols):
    tpos = row0 + lax.broadcasted_iota(jnp.int32, (rows, cols), 0)
    width = jnp.left_shift(2, g)
    return 1.0 / jnp.minimum(tpos + 1, width).astype(F32)


def _pool_fwd(proj, pool_w, pool_scale, *, d, tm, name):
    t = proj.shape[0]
    pw = d // 2
    gd = pw // N_POOL_GROUPS
    za_col = pw // gd

    def body(xa_ref, xp_ref, za_ref, w_ref, ps_ref, ya_ref):
        g = pl.program_id(0)
        i = pl.program_id(1)
        cur = xa_ref[...].astype(F32)
        mixed = _pool_mixed(_with_prev(xp_ref, cur, i == 0), cur, g, i * tm, tm)
        ypre = jnp.dot(mixed.astype(BF16), w_ref[...], preferred_element_type=F32)
        za = za_ref[...].astype(F32)
        ya_ref[...] = (ypre * ps_ref[...] * (za * _sigmoid(za))).astype(BF16)

    return pl.pallas_call(
        body, grid=(N_POOL_GROUPS, t // tm),
        in_specs=[_cur(tm, gd, 0), _prev(tm, gd, 0), _cur(tm, gd, za_col),
                  pl.BlockSpec((None, gd, gd), lambda g, i: (g, 0, 0)),
                  pl.BlockSpec((1, gd), lambda g, i: (0, g))],
        out_specs=pl.BlockSpec((tm, gd), lambda g, i: (i, g)),
        out_shape=jax.ShapeDtypeStruct((t, pw), BF16), name=name,
        compiler_params=_cparams(("arbitrary", "arbitrary"), 40),
    )(proj, proj, proj, pool_w, pool_scale)


def _pool_bwd(dya, proj, pool_w, pool_scale, *, d, tm, name):
    t = proj.shape[0]
    pw = d // 2
    gd = pw // N_POOL_GROUPS
    za_col = pw // gd
    n_i = t // tm

    def body(xa_ref, xp_ref, za_ref, zn_ref, dy_ref, dyn_ref, w_ref, ps_ref,
             dxa_ref, dza_ref, dw_ref, dps_ref):
        g = pl.program_id(0)
        i = pl.program_id(1)
        last = i == n_i - 1
        cur = xa_ref[...].astype(F32)
        mixed = _pool_mixed(_with_prev(xp_ref, cur, i == 0), cur, g, i * tm, tm)
        mixed16 = mixed.astype(BF16)
        ypre = jnp.dot(mixed16, w_ref[...], preferred_element_type=F32)
        za = za_ref[...].astype(F32)
        sg = _sigmoid(za)
        dy = dy_ref[...].astype(F32)
        dza_ref[...] = (dy * (ypre * ps_ref[...]) * (sg * (1.0 + za * (1.0 - sg)))).astype(BF16)
        dysc = dy * (za * sg)
        za_e = _with_next(za, zn_ref, last)
        dy_e = _with_next(dy, dyn_ref, last)
        dypre_e = (dy_e * (za_e * _sigmoid(za_e)) * ps_ref[...]).astype(BF16)
        dm_e = lax.dot_general(dypre_e, w_ref[...], (((1,), (1,)), ((), ())), preferred_element_type=F32)
        rows = tm + HALO
        q = dm_e * _pool_inv_count(g, i * tm, rows, gd)
        f2 = q + pltpu.roll(q, rows - 1, 0)
        f4 = f2 + pltpu.roll(f2, rows - 2, 0)
        f8 = f4 + pltpu.roll(f4, rows - 4, 0)
        f16 = f8 + pltpu.roll(f8, rows - 8, 0)
        ahead = jnp.where(g == 0, f2, jnp.where(g == 1, f4, jnp.where(g == 2, f8, f16)))
        dxa_ref[...] = (ahead[:tm] - dm_e[:tm]).astype(BF16)
        dw = lax.dot_general(mixed16, dypre_e[:tm], (((0,), (0,)), ((), ())), preferred_element_type=F32)
        dps = jnp.sum(dysc * ypre, axis=0, keepdims=True)

        @pl.when(i == 0)
        def _():
            dw_ref[...] = dw
            dps_ref[...] = dps

        @pl.when(i > 0)
        def _():
            dw_ref[...] += dw
            dps_ref[...] += dps

    out_tile = pl.BlockSpec((tm, gd), lambda g, i: (i, g))
    return pl.pallas_call(
        body, grid=(N_POOL_GROUPS, n_i),
        in_specs=[_cur(tm, gd, 0), _prev(tm, gd, 0), _cur(tm, gd, za_col), _next(tm, gd, za_col, t),
                  _cur(tm, gd, 0), _next(tm, gd, 0, t),
                  pl.BlockSpec((None, gd, gd), lambda g, i: (g, 0, 0)),
                  pl.BlockSpec((1, gd), lambda g, i: (0, g))],
        out_specs=[out_tile, out_tile, pl.BlockSpec((None, gd, gd), lambda g, i: (g, 0, 0)),
                   pl.BlockSpec((1, gd), lambda g, i: (0, g))],
        out_shape=[jax.ShapeDtypeStruct((t, pw), BF16), jax.ShapeDtypeStruct((t, pw), BF16),
                   jax.ShapeDtypeStruct((N_POOL_GROUPS, gd, gd), F32), jax.ShapeDtypeStruct((1, pw), F32)],
        name=name, compiler_params=_cparams(("arbitrary", "arbitrary"), 48),
    )(proj, proj, proj, proj, dya, dya, pool_w, pool_scale)


def _conv_cols(d, cw):
    cc = d // 2
    base = (d + 4 * d) // cw
    return base, base + cc // cw, base + 2 * (cc // cw), base + 3 * (cc // cw)


def _conv_fwd(proj, conv_w, *, d, tm, cw, name):
    t = proj.shape[0]
    cc = d // 2
    u_col, b_col, c_col, z_col = _conv_cols(d, cw)

    def body(u_ref, up_ref, c_ref, cp_ref, b_ref, z_ref, w_ref, y_ref):
        i = pl.program_id(1)
        v = u_ref[...].astype(F32) * c_ref[...].astype(F32)
        vprev = up_ref[...].astype(F32) * cp_ref[...].astype(F32)
        ve = jnp.concatenate([jnp.where(i == 0, 0.0, vprev), v], axis=0)
        w = w_ref[...]
        y = w[0:1] * pltpu.roll(ve, 2, 0)[HALO:] + w[1:2] * pltpu.roll(ve, 1, 0)[HALO:] + w[2:3] * v
        z = z_ref[...].astype(F32)
        y_ref[...] = (b_ref[...].astype(F32) * y * (z * _sigmoid(z))).astype(BF16)

    return pl.pallas_call(
        body, grid=(cc // cw, t // tm),
        in_specs=[_cur(tm, cw, u_col), _prev(tm, cw, u_col), _cur(tm, cw, c_col), _prev(tm, cw, c_col),
                  _cur(tm, cw, b_col), _cur(tm, cw, z_col),
                  pl.BlockSpec((8, cw), lambda cb, i: (0, cb))],
        out_specs=pl.BlockSpec((tm, cw), lambda cb, i: (i, cb)),
        out_shape=jax.ShapeDtypeStruct((t, cc), BF16), name=name,
        compiler_params=_cparams(("arbitrary", "arbitrary"), 40),
    )(proj, proj, proj, proj, proj, proj, conv_w)


def _conv_bwd(dyc, proj, conv_w, *, d, tm, cw, name):
    t = proj.shape[0]
    cc = d // 2
    n_i = t // tm
    u_col, b_col, c_col, z_col = _conv_cols(d, cw)

    def body(u_ref, up_ref, c_ref, cp_ref, b_ref, bn_ref, z_ref, zn_ref, dy_ref, dyn_ref, w_ref,
             du_ref, db_ref, dc_ref, dz_ref, dw_ref):
        i = pl.program_id(1)
        last = i == n_i - 1
        u = u_ref[...].astype(F32)
        cg = c_ref[...].astype(F32)
        v = u * cg
        vprev = up_ref[...].astype(F32) * cp_ref[...].astype(F32)
        ve = jnp.concatenate([jnp.where(i == 0, 0.0, vprev), v], axis=0)
        v2 = pltpu.roll(ve, 2, 0)[HALO:]
        v1 = pltpu.roll(ve, 1, 0)[HALO:]
        w = w_ref[...]
        y = w[0:1] * v2 + w[1:2] * v1 + w[2:3] * v
        z = z_ref[...].astype(F32)
        sg = _sigmoid(z)
        bg = b_ref[...].astype(F32)
        dyc_v = dy_ref[...].astype(F32)
        dz_ref[...] = (dyc_v * bg * y * (sg * (1.0 + z * (1.0 - sg)))).astype(BF16)
        db_ref[...] = (dyc_v * y * (z * sg)).astype(BF16)
        ze = _with_next(z, zn_ref, last)
        dy_e = (_with_next(dyc_v, dyn_ref, last) * _with_next(bg, bn_ref, last) * (ze * _sigmoid(ze)))
        rows = tm + HALO
        dy0 = dy_e[:tm]
        dv = (w[2:3] * dy0 + w[1:2] * pltpu.roll(dy_e, rows - 1, 0)[:tm]
              + w[0:1] * pltpu.roll(dy_e, rows - 2, 0)[:tm])
        du_ref[...] = (dv * cg).astype(BF16)
        dc_ref[...] = (dv * u).astype(BF16)
        s0 = jnp.sum(dy0 * v2, axis=0, keepdims=True)
        s1 = jnp.sum(dy0 * v1, axis=0, keepdims=True)
        s2 = jnp.sum(dy0 * v, axis=0, keepdims=True)
        r = lax.broadcasted_iota(jnp.int32, (8, cw), 0)
        dw = jnp.where(r == 0, s0, jnp.where(r == 1, s1, jnp.where(r == 2, s2, 0.0)))

        @pl.when(i == 0)
        def _():
            dw_ref[...] = dw

        @pl.when(i > 0)
        def _():
            dw_ref[...] += dw

    out_tile = pl.BlockSpec((tm, cw), lambda cb, i: (i, cb))
    act = jax.ShapeDtypeStruct((t, cc), BF16)
    return pl.pallas_call(
        body, grid=(cc // cw, n_i),
        in_specs=[_cur(tm, cw, u_col), _prev(tm, cw, u_col), _cur(tm, cw, c_col), _prev(tm, cw, c_col),
                  _cur(tm, cw, b_col), _next(tm, cw, b_col, t), _cur(tm, cw, z_col), _next(tm, cw, z_col, t),
                  _cur(tm, cw, 0), _next(tm, cw, 0, t),
                  pl.BlockSpec((8, cw), lambda cb, i: (0, cb))],
        out_specs=[out_tile, out_tile, out_tile, out_tile, pl.BlockSpec((8, cw), lambda cb, i: (0, cb))],
        out_shape=[act, act, act, act, jax.ShapeDtypeStruct((8, cc), F32)], name=name,
        compiler_params=_cparams(("arbitrary", "arbitrary"), 48),
    )(proj, proj, proj, proj, proj, proj, proj, proj, dyc, dyc, conv_w)


def _sum_matrix(prefix):
    r = lax.broadcasted_iota(jnp.int32, (KEY_CHUNK, 2 * KEY_CHUNK), 0)
    c = lax.broadcasted_iota(jnp.int32, (KEY_CHUNK, 2 * KEY_CHUNK), 1)
    inside = (r <= c) if prefix else (r > c)
    return jnp.where((c >= KEY_CHUNK) | inside, 1.0, 0.0).astype(BF16)


def _chunk_sums(val, mat, two_pass):
    hi = val.astype(BF16)
    both = jnp.dot(hi, mat, preferred_element_type=F32)
    if two_pass:
        lo = (val - hi.astype(F32)).astype(BF16)
        both = both + jnp.dot(lo, mat, preferred_element_type=F32)
    return both[:, :KEY_CHUNK], both[:, KEY_CHUNK:]


def _stick_logits(q, kb, strict):
    z = lax.dot_general(q, kb, (((1,), (1,)), ((), ())), preferred_element_type=F32) * (HEAD_DIM ** -0.5)
    sp = jnp.maximum(z, 0.0) + jnp.log(1.0 + jnp.exp(-jnp.abs(z)))
    log_keep = -sp
    if strict is not None:
        log_keep = jnp.where(strict, log_keep, 0.0)
    return log_keep, z - sp


def _attn_cols(d):
    h = d // HEAD_DIM
    return h, 2 * h, 3 * h, 4 * h


def _sweep(i, tq, chunk, carry, upwards):
    per = tq // KEY_CHUNK
    if upwards:
        carry = lax.fori_loop(0, i * per, lambda it, cr: chunk(it, cr, False), carry)
        for dgl in range(per):
            carry = chunk(i * per + dgl, carry, True)
        return carry
    for dgl in range(per - 1, -1, -1):
        carry = chunk(i * per + dgl, carry, True)
    return lax.fori_loop(0, i * per, lambda it, cr: chunk(i * per - 1 - it, cr, False), carry)


def _strict_mask(i, tq, kc):
    t_idx = i * tq + lax.broadcasted_iota(jnp.int32, (tq, KEY_CHUNK), 0)
    s_idx = kc * KEY_CHUNK + lax.broadcasted_iota(jnp.int32, (tq, KEY_CHUNK), 1)
    return s_idx < t_idx


def _attn_fwd(proj, *, d, tq, name):
    t = proj.shape[0]
    heads = d // HEAD_DIM
    q_col, k_col, v_col, z_col = _attn_cols(d)

    def body(q_ref, k_ref, v_ref, z_ref, o_ref, y_ref, tot_ref):
        i = pl.program_id(1)
        q = q_ref[...]
        mat = _sum_matrix(prefix=False)

        def chunk(kc, carry, masked):
            acc, run = carry
            off = pl.multiple_of(kc * KEY_CHUNK, KEY_CHUNK)
            strict = _strict_mask(i, tq, kc) if masked else None
            log_keep, log_beta = _stick_logits(q, k_ref[pl.ds(off, KEY_CHUNK), :], strict)
            between, total = _chunk_sums(log_keep, mat, True)
            a = jnp.exp(log_beta + between + run)
            if masked:
                a = jnp.where(strict, a, 0.0)
            acc = acc + jnp.dot(a.astype(BF16), v_ref[pl.ds(off, KEY_CHUNK), :], preferred_element_type=F32)
            return acc, run + total

        zero = jnp.zeros((tq, HEAD_DIM), F32)
        o, total = _sweep(i, tq, chunk, (zero, zero), upwards=False)
        o_ref[...] = o.astype(BF16)
        tot_ref[...] = total
        z = z_ref[...].astype(F32)
        y_ref[...] = (o * (z * _sigmoid(z))).astype(BF16)

    tile = lambda col: pl.BlockSpec((tq, HEAD_DIM), lambda h, i: (i, col + h))
    full = lambda col: pl.BlockSpec((t, HEAD_DIM), lambda h, i: (0, col + h))
    act = jax.ShapeDtypeStruct((t, d), BF16)
    return pl.pallas_call(
        body, grid=(heads, t // tq),
        in_specs=[tile(q_col), full(k_col), full(v_col), tile(z_col)],
        out_specs=[tile(0), tile(0), tile(0)], out_shape=[act, act, jax.ShapeDtypeStruct((t, d), F32)],
        name=name, compiler_params=_cparams(("arbitrary", "arbitrary"), 40),
    )(proj, proj, proj, proj)


def _attn_bwd(dyb, o, tot, proj, *, d, tq, name):
    t = proj.shape[0]
    heads = d // HEAD_DIM
    n_i = t // tq
    q_col, k_col, v_col, z_col = _attn_cols(d)
    scale = HEAD_DIM ** -0.5

    def body(q_ref, k_ref, v_ref, z_ref, o_ref, tot_ref, dy_ref, dq_ref, dk_ref, dv_ref, dz_ref, dk_acc, dv_acc):
        i = pl.program_id(1)

        @pl.when(i == 0)
        def _():
            dk_acc[...] = jnp.zeros_like(dk_acc)
            dv_acc[...] = jnp.zeros_like(dv_acc)

        q = q_ref[...]
        z = z_ref[...].astype(F32)
        sg = _sigmoid(z)
        dy = dy_ref[...].astype(F32)
        of = o_ref[...].astype(F32)
        dz_ref[...] = (dy * of * (sg * (1.0 + z * (1.0 - sg)))).astype(BF16)
        do = dy * (z * sg)
        do16 = do.astype(BF16)
        mat = _sum_matrix(prefix=True)

        def chunk(kc, carry, masked):
            dq, keep_left, g_seen = carry
            off = pl.multiple_of(kc * KEY_CHUNK, KEY_CHUNK)
            kb = k_ref[pl.ds(off, KEY_CHUNK), :]
            vb = v_ref[pl.ds(off, KEY_CHUNK), :]
            strict = _strict_mask(i, tq, kc) if masked else None
            log_keep, log_beta = _stick_logits(q, kb, strict)
            upto, total = _chunk_sums(log_keep, mat, True)
            a = jnp.exp(log_beta + (keep_left - upto))
            if masked:
                a = jnp.where(strict, a, 0.0)
            da = lax.dot_general(do16, vb, (((1,), (1,)), ((), ())), preferred_element_type=F32)
            g = da * a
            g_upto, g_total = _chunk_sums(g, mat, False)
            g_before = g_seen + (g_upto - g)
            beta = jnp.exp(log_beta)
            dz_ts = (g * (1.0 - beta) - g_before * beta) * scale
            if masked:
                dz_ts = jnp.where(strict, dz_ts, 0.0)
            dz16 = dz_ts.astype(BF16)
            dq = dq + jnp.dot(dz16, kb, preferred_element_type=F32)
            dk_acc[pl.ds(off, KEY_CHUNK), :] += lax.dot_general(
                dz16, q, (((0,), (0,)), ((), ())), preferred_element_type=F32)
            dv_acc[pl.ds(off, KEY_CHUNK), :] += lax.dot_general(
                a.astype(BF16), do16, (((0,), (0,)), ((), ())), preferred_element_type=F32)
            return dq, keep_left - total, g_seen + g_total

        zero = jnp.zeros((tq, HEAD_DIM), F32)
        dq, _, _ = _sweep(i, tq, chunk, (zero, tot_ref[...], zero), upwards=True)
        dq_ref[...] = dq.astype(BF16)

        @pl.when(i == n_i - 1)
        def _():
            dk_ref[...] = dk_acc[...].astype(BF16)
            dv_ref[...] = dv_acc[...].astype(BF16)

    tile = lambda col: pl.BlockSpec((tq, HEAD_DIM), lambda h, i: (i, col + h))
    full = lambda col: pl.BlockSpec((t, HEAD_DIM), lambda h, i: (0, col + h))
    act = jax.ShapeDtypeStruct((t, d), BF16)
    return pl.pallas_call(
        body, grid=(heads, n_i),
        in_specs=[tile(q_col), full(k_col), full(v_col), tile(z_col), tile(0), tile(0), tile(0)],
        out_specs=[tile(0), full(0), full(0), tile(0)], out_shape=[act, act, act, act],
        scratch_shapes=[pltpu.VMEM((t, HEAD_DIM), F32), pltpu.VMEM((t, HEAD_DIM), F32)], name=name,
        compiler_params=_cparams(("arbitrary", "arbitrary"), 48),
    )(proj, proj, proj, proj, o, tot, dyb)


def _merge_fwd(gl, bst, *, d, tm, tn, name):
    t = gl.shape[0]
    nk = d // tn

    def body(g0, g1, g2, b0, b1, b2, m_ref):
        acc = _sigmoid(g0[...].astype(F32)) * b0[...].astype(F32)
        acc += _sigmoid(g1[...].astype(F32)) * b1[...].astype(F32)
        acc += _sigmoid(g2[...].astype(F32)) * b2[...].astype(F32)
        m_ref[...] = acc.astype(BF16)

    spec = lambda b: pl.BlockSpec((tm, tn), lambda i, kb: (i, b * nk + kb))
    return pl.pallas_call(
        body, grid=(t // tm, nk), in_specs=[spec(0), spec(1), spec(2)] * 2,
        out_specs=pl.BlockSpec((tm, tn), lambda i, kb: (i, kb)),
        out_shape=jax.ShapeDtypeStruct((t, d), BF16), name=name,
        compiler_params=_cparams(("arbitrary", "arbitrary"), 40),
    )(gl, gl, gl, bst, bst, bst)


def _merge_bwd(dmerged, gl, bst, *, d, tm, tn, name):
    t = gl.shape[0]
    nk = d // tn

    def body(dm_ref, g_ref, b_ref, db_ref, dg_ref, dbias_ref):
        i = pl.program_id(2)
        dm = dm_ref[...].astype(F32)
        sg = _sigmoid(g_ref[...].astype(F32))
        db_ref[...] = (dm * sg).astype(BF16)
        dgl = dm * b_ref[...].astype(F32) * (sg * (1.0 - sg))
        dg_ref[...] = dgl.astype(BF16)
        dbias = jnp.sum(dgl, axis=0, keepdims=True)

        @pl.when(i == 0)
        def _():
            dbias_ref[...] = dbias

        @pl.when(i > 0)
        def _():
            dbias_ref[...] += dbias

    wide = pl.BlockSpec((tm, tn), lambda b, kb, i: (i, b * nk + kb))
    act = jax.ShapeDtypeStruct((t, 3 * d), BF16)
    return pl.pallas_call(
        body, grid=(3, nk, t // tm),
        in_specs=[pl.BlockSpec((tm, tn), lambda b, kb, i: (i, kb)), wide, wide],
        out_specs=[wide, wide, pl.BlockSpec((1, tn), lambda b, kb, i: (0, b * nk + kb))],
        out_shape=[act, act, jax.ShapeDtypeStruct((1, 3 * d), F32)], name=name,
        compiler_params=_cparams(("arbitrary", "arbitrary", "arbitrary"), 40),
    )(dmerged, gl, bst)


def _adamw(w, m, v, parts, layer, prev, *, tr, name):
    depth, r, c = w.shape
    n = parts.shape[0]

    def body(*refs):
        w_ref, m_ref, v_ref, p_ref = refs[:4]
        g_ref, d_ref, nm_ref, nv_ref = refs[-4:]
        g = p_ref[0].astype(F32)
        for s in range(1, n):
            g = g + p_ref[s].astype(F32)
        wv = w_ref[...]
        nm = ADAM_B1 * m_ref[...] + (1.0 - ADAM_B1) * g
        nv = ADAM_B2 * v_ref[...] + (1.0 - ADAM_B2) * (g * g)
        m_hat = nm / (1.0 - ADAM_B1 ** ADAM_STEP)
        v_hat = nv / (1.0 - ADAM_B2 ** ADAM_STEP)
        g_ref[...] = g
        nm_ref[...] = nm
        nv_ref[...] = nv
        d_ref[...] = -ADAM_LR * (m_hat / (jnp.sqrt(v_hat) + ADAM_EPS) + ADAM_WD * wv)

    tile = pl.BlockSpec((None, tr, c), lambda i: (layer, i, 0))
    in_specs = [tile, tile, tile, pl.BlockSpec((n, tr, c), lambda i: (0, i, 0))]
    args = [w, m, v, parts]
    aliases = {}
    if prev is not None:
        in_specs += [pl.BlockSpec(memory_space=pl.ANY)] * 4
        aliases = {4 + k: k for k in range(4)}
        args += list(prev)
    full = jax.ShapeDtypeStruct((depth, r, c), F32)
    return pl.pallas_call(
        body, grid=(r // tr,), in_specs=in_specs, out_specs=[tile] * 4, out_shape=[full] * 4,
        input_output_aliases=aliases, name=name, compiler_params=_cparams(("arbitrary",), 48),
    )(*args)


def _adamw_layers(w, m, v, parts_by_layer, *, tr, name):
    lead = w.shape[0]
    flat = lambda a: a.reshape(lead, -1, a.shape[-1])
    w, m, v = flat(w), flat(m), flat(v)
    out = None
    for layer in reversed(range(lead)):
        parts = parts_by_layer[layer]
        parts = parts.reshape(parts.shape[0], -1, parts.shape[-1])
        out = _adamw(w, m, v, parts, layer, out, tr=min(tr, w.shape[1]), name=f"{name}_l{layer}")
    return out


def _place():
    return lax.axis_index("x"), lax.axis_index("y"), lax.axis_index("c")


def _index(px, py, pc):
    return 4 * px + 2 * py + pc


def _whole(ref, idx):
    return ref.at[idx]


def _row_block(rows):
    return lambda ref, idx: ref.at[:, pl.ds(idx * rows, rows), :]


def _lane_block(cols):
    return lambda ref, idx: ref.at[:, pl.ds(idx * cols, cols)]


def _all_gather(shards, out_shapes, placers, *, name):
    n = len(shards)

    def body(*refs):
        src, out = refs[:n], refs[n:2 * n]
        send_sems, recv_sems, local_sems = refs[2 * n:]
        x, y, c = _place()
        me, sibling = (x, y, c), (x, y, 1 - c)
        chips = [(1 - x, y), (x, 1 - y), (1 - x, 1 - y)]

        def copy(a, k, block, to, from_shard=False):
            window = placers[a](out[a], _index(*block))
            return pltpu.make_async_remote_copy(
                src_ref=src[a] if from_shard else window, dst_ref=window,
                send_sem=send_sems.at[a * 7 + k], recv_sem=recv_sems.at[a * 7 + k],
                device_id=to, device_id_type=MESH)

        mine = [pltpu.make_async_copy(src[a], placers[a](out[a], _index(*me)), local_sems.at[a])
                for a in range(n)]
        started = []
        for a in range(n):
            mine[a].start()
            started.append(copy(a, 0, me, sibling, True))
            started += [copy(a, 1 + j, me, (*chip, c), True) for j, chip in enumerate(chips)]
        for cp in started:
            cp.start()
        for j, chip in enumerate(chips):
            for a in range(n):
                copy(a, 1 + j, (*chip, c), me).wait_recv()
                passed = copy(a, 4 + j, (*chip, c), sibling)
                passed.start()
                started.append(passed)
        for a in range(n):
            copy(a, 0, sibling, me).wait_recv()
            for j, chip in enumerate(chips):
                copy(a, 4 + j, (*chip, 1 - c), me).wait_recv()
        for cp in started:
            cp.wait_send()
        for cp in mine:
            cp.wait()

    hbm = pl.BlockSpec(memory_space=pl.ANY)
    return pl.pallas_call(
        body, in_specs=[hbm] * n, out_specs=[hbm] * n, out_shape=out_shapes,
        scratch_shapes=[pltpu.SemaphoreType.DMA((7 * n,)), pltpu.SemaphoreType.DMA((7 * n,)),
                        pltpu.SemaphoreType.DMA((n,))],
        name=name,
    )(*shards)


def _scatter_parts(grads, pickers, part_shapes, *, name):
    n = len(grads)

    def body(*refs):
        src, out = refs[:n], refs[n:2 * n]
        send_sems, recv_sems, local_sems = refs[2 * n:]
        x, y, c = _place()
        my_idx = _index(x, y, c)
        peers = [(x ^ (k >> 2), y ^ ((k >> 1) & 1), c ^ (k & 1)) for k in range(1, N_DEV)]

        def copy(a, k, peer):
            return pltpu.make_async_remote_copy(
                src_ref=pickers[a](src[a], _index(*peer)), dst_ref=out[a].at[my_idx],
                send_sem=send_sems.at[a * 7 + k], recv_sem=recv_sems.at[a * 7 + k],
                device_id=peer, device_id_type=MESH)

        def arrival(a, k, peer):
            return pltpu.make_async_remote_copy(
                src_ref=pickers[a](src[a], my_idx), dst_ref=out[a].at[_index(*peer)],
                send_sem=send_sems.at[a * 7 + k], recv_sem=recv_sems.at[a * 7 + k],
                device_id=peer, device_id_type=MESH)

        mine = [pltpu.make_async_copy(pickers[a](src[a], my_idx), out[a].at[my_idx], local_sems.at[a])
                for a in range(n)]
        sent = [copy(a, k, peer) for a in range(n) for k, peer in enumerate(peers)]
        for cp in mine + sent:
            cp.start()
        for a in range(n):
            for k, peer in enumerate(peers):
                arrival(a, k, peer).wait_recv()
        for cp in sent:
            cp.wait_send()
        for cp in mine:
            cp.wait()

    hbm = pl.BlockSpec(memory_space=pl.ANY)
    return pl.pallas_call(
        body, in_specs=[hbm] * n, out_specs=[hbm] * n, out_shape=part_shapes,
        scratch_shapes=[pltpu.SemaphoreType.DMA((7 * n,)), pltpu.SemaphoreType.DMA((7 * n,)),
                        pltpu.SemaphoreType.DMA((n,))],
        name=name,
    )(*grads)


def _ada_fwd(c8, w_ada, b_ada8, *, name):
    depth, d, n = w_ada.shape

    def body(c_ref, w_ref, b_ref, mod_ref, sc_ref, c_all, prod, got, send_sems, recv_sems):
        x, y, c = _place()
        my_idx = _index(x, y, c)
        peers = [(x ^ (k >> 2), y ^ ((k >> 1) & 1), c ^ (k & 1)) for k in range(1, N_DEV)]

        def slab(ref, idx):
            return ref.at[pl.ds(pl.multiple_of(idx * 8, 8), 8)]

        def c_copy(k, peer, sender):
            return pltpu.make_async_remote_copy(
                src_ref=c_ref, dst_ref=slab(c_all, sender), send_sem=send_sems.at[k],
                recv_sem=recv_sems.at[k], device_id=peer, device_id_type=MESH)

        def m_copy(k, peer, sender):
            return pltpu.make_async_remote_copy(
                src_ref=prod.at[_index(*peer)], dst_ref=got.at[sender], send_sem=send_sems.at[7 + k],
                recv_sem=recv_sems.at[7 + k], device_id=peer, device_id_type=MESH)

        sends = [c_copy(k, peer, my_idx) for k, peer in enumerate(peers)]
        for cp in sends:
            cp.start()
        slab(c_all, my_idx)[...] = c_ref[...]
        for k, peer in enumerate(peers):
            c_copy(k, peer, _index(*peer)).wait_recv()
        cv = c_all[...]
        sc = cv * _sigmoid(cv)
        sc_ref[...] = sc
        for layer in range(depth):
            p = jnp.dot(sc, w_ref[layer], preferred_element_type=F32, precision=lax.Precision.HIGHEST)
            for s in range(N_DEV):
                prod[s, layer] = p[8 * s:8 * s + 8]
        sends2 = [m_copy(k, peer, my_idx) for k, peer in enumerate(peers)]
        for cp in sends2:
            cp.start()
        got[my_idx] = prod[my_idx]
        for k, peer in enumerate(peers):
            m_copy(k, peer, _index(*peer)).wait_recv()
        for layer in range(depth):
            for s in range(N_DEV):
                mod_ref[layer, :, s * n:(s + 1) * n] = got[s, layer] + b_ref[layer, :, s * n:(s + 1) * n]
        for cp in sends + sends2:
            cp.wait_send()

    vmem = pl.BlockSpec(memory_space=pltpu.VMEM)
    return pl.pallas_call(
        body, in_specs=[vmem] * 3, out_specs=[vmem] * 2,
        out_shape=[jax.ShapeDtypeStruct((depth, 8, N_DEV * n), F32),
                   jax.ShapeDtypeStruct((8 * N_DEV, d), F32)],
        scratch_shapes=[pltpu.VMEM((8 * N_DEV, d), F32), pltpu.VMEM((N_DEV, depth, 8, n), F32),
                        pltpu.VMEM((N_DEV, depth, 8, n), F32),
                        pltpu.SemaphoreType.DMA((14,)), pltpu.SemaphoreType.DMA((14,))],
        name=name, compiler_params=pltpu.CompilerParams(vmem_limit_bytes=56 << 20),
    )(c8, w_ada, b_ada8)


def _ada_bwd(sc8, dmod, *, name):
    depth, _, n = dmod.shape
    d = sc8.shape[1]

    def body(sc_ref, dm_ref, o_ref):
        for layer in range(depth):
            o_ref[layer] = lax.dot_general(sc_ref[...], dm_ref[layer], (((0,), (0,)), ((), ())),
                                           preferred_element_type=F32, precision=lax.Precision.HIGHEST)

    vmem = pl.BlockSpec(memory_space=pltpu.VMEM)
    return pl.pallas_call(
        body, in_specs=[vmem] * 2, out_specs=vmem, out_shape=jax.ShapeDtypeStruct((depth, d, n), F32),
        name=name, compiler_params=pltpu.CompilerParams(vmem_limit_bytes=40 << 20),
    )(sc8, dmod)


def _gather_small(vec, *, name):
    r = vec.shape[0]

    def body(v_ref, o_ref, send_sems, recv_sems):
        x, y, c = _place()
        my_idx = _index(x, y, c)
        peers = [(x ^ (k >> 2), y ^ ((k >> 1) & 1), c ^ (k & 1)) for k in range(1, N_DEV)]

        def copy(k, peer, sender):
            return pltpu.make_async_remote_copy(
                src_ref=v_ref, dst_ref=o_ref.at[sender], send_sem=send_sems.at[k],
                recv_sem=recv_sems.at[k], device_id=peer, device_id_type=MESH)

        sends = [copy(k, peer, my_idx) for k, peer in enumerate(peers)]
        for cp in sends:
            cp.start()
        o_ref[my_idx] = v_ref[...]
        for k, peer in enumerate(peers):
            copy(k, peer, _index(*peer)).wait_recv()
        for cp in sends:
            cp.wait_send()

    vmem = pl.BlockSpec(memory_space=pltpu.VMEM)
    return pl.pallas_call(
        body, in_specs=[vmem], out_specs=vmem, out_shape=jax.ShapeDtypeStruct((N_DEV, r, LANE), F32),
        scratch_shapes=[pltpu.SemaphoreType.DMA((7,)), pltpu.SemaphoreType.DMA((7,))], name=name,
    )(vec)


TM = 512
TQ = 256


def _layer_fwd(x, mod, small, gw, *, d, tag):
    shift, scale, res_gate = mod
    h = _prenorm(x, small["norm_g"], scale, shift, tm=TM, name=f"prenorm_{tag}")
    proj = _mm_nn(h, gw["w_in"], out_dtype=BF16, tm=TM, name=f"in_proj_{tag}")
    gl = _mm_nn(h, gw["w_gate"], out_dtype=BF16, tm=TM, bias=small["b_gate"], name=f"gate_proj_{tag}")
    ya = _pool_fwd(proj, gw["pool_w"], small["pool_scale"], d=d, tm=TM, name=f"pool_fwd_{tag}")
    o, yb, tot = _attn_fwd(proj, d=d, tq=TQ, name=f"attn_fwd_{tag}")
    yc = _conv_fwd(proj, gw["conv_w"], d=d, tm=TM, cw=LANE * 2, name=f"conv_fwd_{tag}")
    nblk = gw["w_br_a"].shape[0]
    bst = _mm_nn(ya, gw["w_br_a"], out_dtype=BF16, tm=TM, out_cols=3 * d, name=f"branch_a_{tag}")
    bst = _mm_nn(yb, gw["w_br_b"], out_dtype=BF16, tm=TM, out=bst, out_col_block=1, name=f"branch_b_{tag}")
    bst = _mm_nn(yc, gw["w_br_c"], out_dtype=BF16, tm=TM, out=bst, out_col_block=2 * nblk,
                 name=f"branch_c_{tag}")
    merged = _merge_fwd(gl, bst, d=d, tm=TM, tn=TM, name=f"merge_fwd_{tag}")
    x_new, mo = _out_fwd(merged, gw["w_out"][0], x, res_gate, tm=TM, name=f"out_fwd_{tag}")
    return x_new, dict(x=x, h=h, proj=proj, gl=gl, ya=ya, yb=yb, yc=yc, o=o, tot=tot, bst=bst, merged=merged,
                       mo=mo)


def _layer_bwd(dout, sv, mod, small, gw, *, d, tag):
    shift, scale, res_gate = mod
    nblk = gw["w_br_a"].shape[0]
    dmo, d_res_gate = _out_bwd(dout, sv["mo"], res_gate, tm=TM, name=f"out_bwd_{tag}")
    g_w_out = _mm_tn(sv["merged"], dmo, nb=1, n=d, tk=TM, tmm=TM, name=f"dw_out_{tag}")
    dmerged = _mm_nt(dmo, gw["w_out"], out_dtype=BF16, tm=TM, name=f"dmerged_{tag}")
    dbst, dgl, d_b_gate = _merge_bwd(dmerged, sv["gl"], sv["bst"], d=d, tm=TM, tn=TM, name=f"merge_bwd_{tag}")
    g_w_gate = _mm_tn(sv["h"], dgl, nb=nblk, n=3 * d // nblk, tk=d // 2, tmm=TM, name=f"dw_gate_{tag}")
    dh = _mm_nt(dgl, gw["w_gate"], out_dtype=F32, tm=TM, name=f"dh_gate_{tag}")
    g_w_br_a = _mm_tn(sv["ya"], dbst, nb=nblk, n=d // nblk, tk=d // 2, tmm=TM, name=f"dw_br_a_{tag}")
    g_w_br_b = _mm_tn(sv["yb"], dbst, nb=1, n=d, tk=TM, tmm=TM, col_block=1, name=f"dw_br_b_{tag}")
    g_w_br_c = _mm_tn(sv["yc"], dbst, nb=nblk, n=d // nblk, tk=d // 2, tmm=TM, col_block=2 * nblk,
                      name=f"dw_br_c_{tag}")
    dya = _mm_nt(dbst, gw["w_br_a"], out_dtype=BF16, tm=TM, name=f"dya_{tag}")
    dyb = _mm_nt(dbst, gw["w_br_b"], out_dtype=BF16, tm=TM, col_block=1, name=f"dyb_{tag}")
    dyc = _mm_nt(dbst, gw["w_br_c"], out_dtype=BF16, tm=TM, col_block=2 * nblk, name=f"dyc_{tag}")
    dxa, dza, g_pool_w, d_pool_scale = _pool_bwd(dya, sv["proj"], gw["pool_w"], small["pool_scale"], d=d, tm=TM,
                                                 name=f"pool_bwd_{tag}")
    dq, dk, dv, dzb = _attn_bwd(dyb, sv["o"], sv["tot"], sv["proj"], d=d, tq=TQ, name=f"attn_bwd_{tag}")
    du, dbg, dcg, dzc, d_conv_w = _conv_bwd(dyc, sv["proj"], gw["conv_w"], d=d, tm=TM, cw=LANE * 2,
                                            name=f"conv_bwd_{tag}")
    dproj = jnp.concatenate([dxa, dza, dq, dk, dv, dzb, du, dbg, dcg, dzc], axis=1)
    g_w_in = _mm_tn(sv["h"], dproj, nb=nblk, n=7 * d // nblk, tk=d // 2, tmm=TM, name=f"dw_in_{tag}")
    dh = _mm_nt(dproj, gw["w_in"], out_dtype=F32, tm=TM, addend=dh, name=f"dh_in_{tag}")
    dx, d_norm_g, d_scale, d_shift = _prenorm_bwd(dh, sv["x"], dout, small["norm_g"], scale, tm=TM,
                                                  name=f"prenorm_bwd_{tag}")
    big = dict(w_in=g_w_in, w_gate=g_w_gate, w_br_a=g_w_br_a, w_br_b=g_w_br_b, w_br_c=g_w_br_c,
               w_out=g_w_out, pool_w=g_pool_w.astype(BF16))
    part = dict(norm_g=d_norm_g, mod=jnp.concatenate([d_shift, d_scale, d_res_gate], axis=1),
                pool_scale=d_pool_scale, b_gate=d_b_gate, conv_w=d_conv_w[:3])
    return dx, big, part


BIG = ("w_in", "w_gate", "w_br_a", "w_br_b", "w_br_c", "w_out", "pool_w")
SMALL_PER_LAYER = ("norm_g", "mod", "pool_scale", "b_gate", "conv_w")


def _gather_layer(w, layer, *, d):
    nd = N_DEV
    gd = d // 2 // N_POOL_GROUPS
    cc = d // 2
    shards = [w[k][layer].astype(BF16) for k in BIG]
    shards.append(jnp.pad(w["conv_w"][layer], ((0, 5), (0, 0))))
    blocks = [jax.ShapeDtypeStruct((nd,) + s.shape, BF16) for s in shards[:6]]
    blocks.append(jax.ShapeDtypeStruct((N_POOL_GROUPS, gd, gd), BF16))
    blocks.append(jax.ShapeDtypeStruct((8, cc), F32))
    placers = [_whole] * 6 + [_row_block(gd // nd), _lane_block(cc // nd)]
    got = _all_gather(shards, blocks, placers, name=f"gather_weights_l{layer}")
    gw = dict(zip(BIG + ("conv_w",), got))
    gw["w_br_b"] = gw["w_br_b"].reshape(1, d, d)
    gw["w_out"] = gw["w_out"].reshape(1, d, d)
    return gw


def _scatter_layer(big, layer, *, d):
    nd = N_DEV
    gd = d // 2 // N_POOL_GROUPS
    grads = [big[k] for k in BIG]
    grads[3] = grads[3].reshape(nd, d // nd, d)
    grads[5] = grads[5].reshape(nd, d // nd, d)
    shapes = [jax.ShapeDtypeStruct(g.shape, BF16) for g in grads[:6]]
    shapes.append(jax.ShapeDtypeStruct((nd, N_POOL_GROUPS, gd // nd, gd), BF16))
    pickers = [_whole] * 6 + [_row_block(gd // nd)]
    return dict(zip(BIG, _scatter_parts(grads, pickers, shapes, name=f"scatter_grads_l{layer}")))


def _pack(pieces):
    return jnp.concatenate([p.reshape(-1, LANE) for p in pieces], axis=0)


def kernel(x, c, norm_g, w_ada, b_ada, w_in, pool_w, pool_scale, conv_w, w_br_a, w_br_b, w_br_c, w_gate, b_gate, w_out, final_g, loss_target, m_norm_g, m_w_ada, m_b_ada, m_w_in, m_pool_w, m_pool_scale, m_conv_w, m_w_br_a, m_w_br_b, m_w_br_c, m_w_gate, m_b_gate, m_w_out, m_final_g, v_norm_g, v_w_ada, v_b_ada, v_w_in, v_pool_w, v_pool_scale, v_conv_w, v_w_br_a, v_w_br_b, v_w_br_c, v_w_gate, v_b_gate, v_w_out, v_final_g):
    names = ("norm_g", "w_ada", "b_ada", "w_in", "pool_w", "pool_scale", "conv_w", "w_br_a", "w_br_b",
             "w_br_c", "w_gate", "b_gate", "w_out", "final_g")
    w = dict(zip(names, (norm_g, w_ada, b_ada, w_in, pool_w, pool_scale, conv_w, w_br_a, w_br_b, w_br_c,
                         w_gate, b_gate, w_out, final_g)))
    m = dict(zip(names, (m_norm_g, m_w_ada, m_b_ada, m_w_in, m_pool_w, m_pool_scale, m_conv_w, m_w_br_a,
                         m_w_br_b, m_w_br_c, m_w_gate, m_b_gate, m_w_out, m_final_g)))
    v = dict(zip(names, (v_norm_g, v_w_ada, v_b_ada, v_w_in, v_pool_w, v_pool_scale, v_conv_w, v_w_br_a,
                         v_w_br_b, v_w_br_c, v_w_gate, v_b_gate, v_w_out, v_final_g)))
    _, t, d = x.shape
    depth = norm_g.shape[0]
    cc = d // 2
    my_idx = _index(*_place())

    mod8, sc_all = _ada_fwd(jnp.broadcast_to(c, (8, d)), w_ada,
                            jnp.broadcast_to(b_ada[:, None, :], (depth, 8, 3 * d)), name="ada_fwd")
    mods = [tuple(mod8[l, 0:1, k * d:(k + 1) * d] for k in range(3)) for l in range(depth)]
    small = [dict(norm_g=norm_g[l][None], b_gate=b_gate[l][None], pool_scale=pool_scale[l][None])
             for l in range(depth)]

    gws = [_gather_layer(w, l, d=d) for l in range(depth)]

    xs = x[0]
    saved = []
    for l in range(depth):
        xs, sv = _layer_fwd(xs, mods[l], small[l], gws[l], d=d, tag=f"l{l}")
        saved.append(sv)
    dx, loss8, d_final_g = _loss_head(xs, loss_target[0], final_g[None], tm=TM, name="loss_head")
    loss = lax.psum(loss8[0, 0], ("x", "y", "c"))

    parts, small_parts = [None] * depth, [None] * depth
    for l in reversed(range(depth)):
        dx, big, small_parts[l] = _layer_bwd(dx, saved[l], mods[l], small[l], gws[l], d=d, tag=f"l{l}")
        parts[l] = _scatter_layer(big, l, d=d)

    pieces = [small_parts[l][k] for l in range(depth) for k in SMALL_PER_LAYER] + [d_final_g]
    sizes = [p.size for p in pieces]
    gathered = _gather_small(_pack(pieces), name="gather_small_grads")
    zero_conv = jnp.zeros((3, cc), F32)

    def packed(src):
        per_layer = lambda l: [src["norm_g"][l], src["b_ada"][l], src["pool_scale"][l], src["b_gate"][l],
                               zero_conv]
        return _pack([p for l in range(depth) for p in per_layer(l)] + [src["final_g"]])[None]

    small_out = _adamw(packed(w), packed(m), packed(v), gathered, 0, None, tr=gathered.shape[1],
                       name="adamw_small")

    def unpack(flat):
        flat = flat.reshape(-1)
        out, off = [], 0
        for n in sizes:
            out.append(flat[off:off + n])
            off += n
        return out

    small_res = [unpack(a) for a in small_out]
    k_per = len(SMALL_PER_LAYER)

    def small_stack(which, pos, shape):
        return jnp.stack([small_res[which][l * k_per + pos] for l in range(depth)]).reshape(shape)

    res = {}
    for pos, name in ((0, "norm_g"), (1, "b_ada"), (2, "pool_scale"), (3, "b_gate")):
        res[name] = tuple(small_stack(k, pos, w[name].shape) for k in range(4))
    res["final_g"] = tuple(small_res[k][-1].reshape(final_g.shape) for k in range(4))

    ncol = cc // N_DEV
    conv_total = small_stack(0, 4, (depth * 3, cc))
    conv_mine = lax.dynamic_slice_in_dim(conv_total, my_idx * ncol, ncol, axis=1)
    pad8 = lambda a: jnp.pad(a.reshape(depth * 3, ncol), ((0, 8 - depth * 3), (0, 0)))[None]
    conv_out = _adamw(pad8(conv_w), pad8(m["conv_w"]), pad8(v["conv_w"]), pad8(conv_mine), 0, None, tr=8,
                      name="adamw_conv_w")
    res["conv_w"] = tuple(a[0, :depth * 3].reshape(conv_w.shape) for a in conv_out)

    n_ada = 3 * d // N_DEV
    mod_off = [sum(sizes[:l * k_per + 1]) // LANE for l in range(depth)]
    dmod = jnp.stack([
        lax.dynamic_slice_in_dim(gathered[:, mod_off[l]:mod_off[l] + 3 * d // LANE].reshape(N_DEV, 3 * d),
                                 my_idx * n_ada, n_ada, axis=1) for l in range(depth)])
    g_w_ada = _ada_bwd(sc_all[::8], dmod, name="ada_bwd")
    res["w_ada"] = _adamw_layers(w_ada, m["w_ada"], v["w_ada"], [g_w_ada[l][None] for l in range(depth)],
                                 tr=128, name="adamw_w_ada")

    for name in BIG:
        out = _adamw_layers(w[name], m[name], v[name], [parts[l][name] for l in range(depth)], tr=128,
                            name=f"adamw_{name}")
        res[name] = tuple(a.reshape(w[name].shape) for a in out)

    outs = [loss, dx[None]]
    for k in range(4):
        outs += [res[name][k] for name in names]
    return tuple(outs)
```

```python
import functools

import jax
import jax.numpy as jnp
from jax import lax
from jax.experimental import pallas as pl
from jax.experimental.pallas import tpu as pltpu

F32 = jnp.float32
BF16 = jnp.bfloat16
MESH = pl.DeviceIdType.MESH

N_DEV = 8
DEPTH = 2
HEAD_DIM = 128
N_POOL_GROUPS = 4
POOL_WINDOWS = (2, 4, 8, 16)
RMS_EPS = 1e-6
LANE = 128
HALO = 16
KEY_CHUNK = 128
KEY_BLOCK = 512
ADAM_LR = 0.001
ADAM_B1 = 0.9
ADAM_B2 = 0.999
ADAM_EPS = 1e-08
ADAM_WD = 0.01
ADAM_STEP = 10


def _cparams(semantics, vmem_mb):
    return pltpu.CompilerParams(dimension_semantics=semantics, vmem_limit_bytes=vmem_mb << 20)


def _sigmoid(z):
    return 1.0 / (1.0 + jnp.exp(-z))


def _mm_nn(a, w, *, out_dtype, tm, name, bias=None, out=None, out_cols=None, out_col_block=0):
    m, k = a.shape
    nb, _, n = w.shape
    total_cols = out.shape[1] if out is not None else (nb * n if out_cols is None else out_cols)

    def body(*refs):
        if out is not None:
            refs = refs[:-2] + refs[-1:]
        if bias is not None:
            a_ref, w_ref, b_ref, o_ref = refs
        else:
            a_ref, w_ref, o_ref = refs
        acc = jnp.dot(a_ref[...], w_ref[...], preferred_element_type=F32)
        if bias is not None:
            acc = acc + b_ref[...]
        o_ref[...] = acc.astype(out_dtype)

    in_specs = [pl.BlockSpec((tm, k), lambda j, i: (i, 0)),
                pl.BlockSpec((None, k, n), lambda j, i: (j, 0, 0))]
    args = [a, w]
    if bias is not None:
        in_specs.append(pl.BlockSpec((1, n), lambda j, i: (0, j)))
        args.append(bias)
    aliases = {}
    if out is not None:
        in_specs.append(pl.BlockSpec(memory_space=pl.ANY))
        aliases = {len(args): 0}
        args.append(out)
    return pl.pallas_call(
        body, grid=(nb, m // tm), in_specs=in_specs,
        out_specs=pl.BlockSpec((tm, n), lambda j, i: (i, out_col_block + j)),
        out_shape=jax.ShapeDtypeStruct((m, total_cols), out_dtype),
        input_output_aliases=aliases, name=name,
        compiler_params=_cparams(("arbitrary", "arbitrary"), 56),
    )(*args)


def _mm_nt(dy, w, *, out_dtype, tm, name, addend=None, col_block=0):
    m = dy.shape[0]
    nb, k, n = w.shape

    def body(*refs):
        if addend is not None:
            dy_ref, w_ref, add_ref, o_ref, acc_ref = refs
        else:
            dy_ref, w_ref, o_ref, acc_ref = refs
        j = pl.program_id(1)
        part = lax.dot_general(dy_ref[...], w_ref[...], (((1,), (1,)), ((), ())),
                               preferred_element_type=F32)

        @pl.when(j == 0)
        def _():
            acc_ref[...] = part if addend is None else part + add_ref[...]

        @pl.when(j > 0)
        def _():
            acc_ref[...] += part

        @pl.when(j == nb - 1)
        def _():
            o_ref[...] = acc_ref[...].astype(out_dtype)

    in_specs = [pl.BlockSpec((tm, n), lambda i, j: (i, col_block + j)),
                pl.BlockSpec((None, k, n), lambda i, j: (j, 0, 0))]
    args = [dy, w]
    if addend is not None:
        in_specs.append(pl.BlockSpec((tm, k), lambda i, j: (i, 0)))
        args.append(addend)
    return pl.pallas_call(
        body, grid=(m // tm, nb), in_specs=in_specs,
        out_specs=pl.BlockSpec((tm, k), lambda i, j: (i, 0)),
        out_shape=jax.ShapeDtypeStruct((m, k), out_dtype),
        scratch_shapes=[pltpu.VMEM((tm, k), F32)], name=name,
        compiler_params=_cparams(("arbitrary", "arbitrary"), 56),
    )(*args)


def _mm_tn(a, dy, *, nb, n, tk, tmm, name, col_block=0, out_dtype=BF16):
    m, k = a.shape

    def body(a_ref, dy_ref, o_ref, acc_ref):
        t = pl.program_id(2)
        part = lax.dot_general(a_ref[...], dy_ref[...], (((0,), (0,)), ((), ())),
                               preferred_element_type=F32)

        @pl.when(t == 0)
        def _():
            acc_ref[...] = part

        @pl.when(t > 0)
        def _():
            acc_ref[...] += part

        @pl.when(t == pl.num_programs(2) - 1)
        def _():
            o_ref[...] = acc_ref[...].astype(out_dtype)

    return pl.pallas_call(
        body, grid=(nb, k // tk, m // tmm),
        in_specs=[pl.BlockSpec((tmm, tk), lambda j, kb, t: (t, kb)),
                  pl.BlockSpec((tmm, n), lambda j, kb, t: (t, col_block + j))],
        out_specs=pl.BlockSpec((None, tk, n), lambda j, kb, t: (j, kb, 0)),
        out_shape=jax.ShapeDtypeStruct((nb, k, n), out_dtype),
        scratch_shapes=[pltpu.VMEM((tk, n), F32)], name=name,
        compiler_params=_cparams(("arbitrary", "arbitrary", "arbitrary"), 56),
    )(a, dy)


def _prenorm(x, g, scale, shift, *, tm, name):
    t, d = x.shape

    def body(x_ref, g_ref, sc_ref, sh_ref, h_ref):
        xv = x_ref[...]
        r = lax.rsqrt(jnp.mean(xv * xv, axis=-1, keepdims=True) + RMS_EPS)
        h_ref[...] = ((xv * r) * g_ref[...] * (1.0 + sc_ref[...]) + sh_ref[...]).astype(BF16)

    row = pl.BlockSpec((1, d), lambda i: (0, 0))
    tile = pl.BlockSpec((tm, d), lambda i: (i, 0))
    return pl.pallas_call(
        body, grid=(t // tm,), in_specs=[tile, row, row, row], out_specs=tile,
        out_shape=jax.ShapeDtypeStruct((t, d), BF16), name=name,
        compiler_params=_cparams(("arbitrary",), 40),
    )(x, g, scale, shift)


def _prenorm_bwd(dh, x, dout, g, scale, *, tm, name):
    t, d = x.shape

    def body(dh_ref, x_ref, do_ref, g_ref, sc_ref, dx_ref, dg_ref, dsc_ref, dsh_ref):
        i = pl.program_id(0)
        xv = x_ref[...]
        dhv = dh_ref[...]
        r = lax.rsqrt(jnp.mean(xv * xv, axis=-1, keepdims=True) + RMS_EPS)
        xn = xv * r
        gain = g_ref[...] * (1.0 + sc_ref[...])
        dxn = dhv * gain
        dx_ref[...] = do_ref[...] + r * (dxn - xn * jnp.mean(dxn * xn, axis=-1, keepdims=True))
        dhxn = dhv * xn
        dg = jnp.sum(dhxn * (1.0 + sc_ref[...]), axis=0, keepdims=True)
        dsc = jnp.sum(dhxn * g_ref[...], axis=0, keepdims=True)
        dsh = jnp.sum(dhv, axis=0, keepdims=True)

        @pl.when(i == 0)
        def _():
            dg_ref[...] = dg
            dsc_ref[...] = dsc
            dsh_ref[...] = dsh

        @pl.when(i > 0)
        def _():
            dg_ref[...] += dg
            dsc_ref[...] += dsc
            dsh_ref[...] += dsh

    row = pl.BlockSpec((1, d), lambda i: (0, 0))
    tile = pl.BlockSpec((tm, d), lambda i: (i, 0))
    vec = jax.ShapeDtypeStruct((1, d), F32)
    return pl.pallas_call(
        body, grid=(t // tm,), in_specs=[tile, tile, tile, row, row],
        out_specs=[tile, row, row, row],
        out_shape=[jax.ShapeDtypeStruct((t, d), F32), vec, vec, vec], name=name,
        compiler_params=_cparams(("arbitrary",), 48),
    )(dh, x, dout, g, scale)


def _out_fwd(merged, w_out, x, res_gate, *, tm, name):
    t, d = x.shape
    k = merged.shape[1]

    def body(m_ref, w_ref, x_ref, rg_ref, xo_ref, mo_ref):
        mo = jnp.dot(m_ref[...], w_ref[...], preferred_element_type=F32)
        mo_ref[...] = mo.astype(BF16)
        xo_ref[...] = x_ref[...] + rg_ref[...] * mo

    tile = pl.BlockSpec((tm, d), lambda i: (i, 0))
    return pl.pallas_call(
        body, grid=(t // tm,),
        in_specs=[pl.BlockSpec((tm, k), lambda i: (i, 0)), pl.BlockSpec((k, d), lambda i: (0, 0)),
                  tile, pl.BlockSpec((1, d), lambda i: (0, 0))],
        out_specs=[tile, tile],
        out_shape=[jax.ShapeDtypeStruct((t, d), F32), jax.ShapeDtypeStruct((t, d), BF16)], name=name,
        compiler_params=_cparams(("arbitrary",), 56),
    )(merged, w_out, x, res_gate)


def _out_bwd(dout, mo, res_gate, *, tm, name):
    t, d = dout.shape

    def body(do_ref, mo_ref, rg_ref, dmo_ref, drg_ref):
        i = pl.program_id(0)
        dov = do_ref[...]
        dmo_ref[...] = (dov * rg_ref[...]).astype(BF16)
        drg = jnp.sum(dov * mo_ref[...].astype(F32), axis=0, keepdims=True)

        @pl.when(i == 0)
        def _():
            drg_ref[...] = drg

        @pl.when(i > 0)
        def _():
            drg_ref[...] += drg

    row = pl.BlockSpec((1, d), lambda i: (0, 0))
    tile = pl.BlockSpec((tm, d), lambda i: (i, 0))
    return pl.pallas_call(
        body, grid=(t // tm,), in_specs=[tile, tile, row], out_specs=[tile, row],
        out_shape=[jax.ShapeDtypeStruct((t, d), BF16), jax.ShapeDtypeStruct((1, d), F32)], name=name,
        compiler_params=_cparams(("arbitrary",), 40),
    )(dout, mo, res_gate)


def _loss_head(x, target, final_g, *, tm, name):
    t, d = x.shape

    def body(x_ref, t_ref, g_ref, dx_ref, loss_ref, dg_ref):
        i = pl.program_id(0)
        xv = x_ref[...]
        r = lax.rsqrt(jnp.mean(xv * xv, axis=-1, keepdims=True) + RMS_EPS)
        xn = xv * r
        err = xn * g_ref[...] - t_ref[...]
        part = (0.5 / d) * jnp.sum(jnp.sum(err * err, axis=1, keepdims=True), axis=0, keepdims=True)
        dy = err * (1.0 / d)
        dxn = dy * g_ref[...]
        dx_ref[...] = r * (dxn - xn * jnp.mean(dxn * xn, axis=-1, keepdims=True))
        dg = jnp.sum(dy * xn, axis=0, keepdims=True)

        @pl.when(i == 0)
        def _():
            loss_ref[...] = jnp.zeros_like(loss_ref) + part
            dg_ref[...] = dg

        @pl.when(i > 0)
        def _():
            loss_ref[...] += part
            dg_ref[...] += dg

    row = pl.BlockSpec((1, d), lambda i: (0, 0))
    tile = pl.BlockSpec((tm, d), lambda i: (i, 0))
    return pl.pallas_call(
        body, grid=(t // tm,), in_specs=[tile, tile, row],
        out_specs=[tile, pl.BlockSpec((8, LANE), lambda i: (0, 0)), row],
        out_shape=[jax.ShapeDtypeStruct((t, d), F32), jax.ShapeDtypeStruct((8, LANE), F32),
                   jax.ShapeDtypeStruct((1, d), F32)], name=name,
        compiler_params=_cparams(("arbitrary",), 40),
    )(x, target, final_g)


def _cur(tm, cw, col):
    return pl.BlockSpec((tm, cw), lambda cb, i: (i, col + cb))


def _prev(tm, cw, col):
    return pl.BlockSpec((HALO, cw), lambda cb, i: (jnp.maximum(i * (tm // HALO) - 1, 0), col + cb))


def _next(tm, cw, col, t):
    last = t // HALO - 1
    return pl.BlockSpec((HALO, cw), lambda cb, i: (jnp.minimum((i + 1) * (tm // HALO), last), col + cb))


def _with_prev(prev_ref, cur, first):
    prev = jnp.where(first, 0.0, prev_ref[...].astype(F32))
    return jnp.concatenate([prev, cur], axis=0)


def _with_next(cur, next_ref, last):
    nxt = jnp.where(last, 0.0, next_ref[...].astype(F32))
    return jnp.concatenate([cur, nxt], axis=0)


def _pool_mixed(xe, cur, g, row0, tm):
    s2 = xe + pltpu.roll(xe, 1, 0)
    s4 = s2 + pltpu.roll(s2, 2, 0)
    s8 = s4 + pltpu.roll(s4, 4, 0)
    s16 = s8 + pltpu.roll(s8, 8, 0)
    win = jnp.where(g == 0, s2, jnp.where(g == 1, s4, jnp.where(g == 2, s8, s16)))[HALO:]
    return win * _pool_inv_count(g, row0, tm, cur.shape[1]) - cur


def _pool_inv_count(g, row0, rows, cols):
    tpos = row0 + lax.broadcasted_iota(jnp.int32, (rows, cols), 0)
    width = jnp.left_shift(2, g)
    return 1.0 / jnp.minimum(tpos + 1, width).astype(F32)


def _pool_fwd(proj, pool_w, pool_scale, *, d, tm, name):
    t = proj.shape[0]
    pw = d // 2
    gd = pw // N_POOL_GROUPS
    za_col = pw // gd

    def body(xa_ref, xp_ref, za_ref, w_ref, ps_ref, ya_ref):
        g = pl.program_id(0)
        i = pl.program_id(1)
        cur = xa_ref[...].astype(F32)
        mixed = _pool_mixed(_with_prev(xp_ref, cur, i == 0), cur, g, i * tm, tm)
        ypre = jnp.dot(mixed.astype(BF16), w_ref[...], preferred_element_type=F32)
        za = za_ref[...].astype(F32)
        ya_ref[...] = (ypre * ps_ref[...] * (za * _sigmoid(za))).astype(BF16)

    return pl.pallas_call(
        body, grid=(N_POOL_GROUPS, t // tm),
        in_specs=[_cur(tm, gd, 0), _prev(tm, gd, 0), _cur(tm, gd, za_col),
                  pl.BlockSpec((None, gd, gd), lambda g, i: (g, 0, 0)),
                  pl.BlockSpec((1, gd), lambda g, i: (0, g))],
        out_specs=pl.BlockSpec((tm, gd), lambda g, i: (i, g)),
        out_shape=jax.ShapeDtypeStruct((t, pw), BF16), name=name,
        compiler_params=_cparams(("arbitrary", "arbitrary"), 40),
    )(proj, proj, proj, pool_w, pool_scale)


def _pool_bwd(dya, proj, pool_w, pool_scale, *, d, tm, name):
    t = proj.shape[0]
    pw = d // 2
    gd = pw // N_POOL_GROUPS
    za_col = pw // gd
    n_i = t // tm

    def body(xa_ref, xp_ref, za_ref, zn_ref, dy_ref, dyn_ref, w_ref, ps_ref,
             dxa_ref, dza_ref, dw_ref, dps_ref):
        g = pl.program_id(0)
        i = pl.program_id(1)
        last = i == n_i - 1
        cur = xa_ref[...].astype(F32)
        mixed = _pool_mixed(_with_prev(xp_ref, cur, i == 0), cur, g, i * tm, tm)
        mixed16 = mixed.astype(BF16)
        ypre = jnp.dot(mixed16, w_ref[...], preferred_element_type=F32)
        za = za_ref[...].astype(F32)
        sg = _sigmoid(za)
        dy = dy_ref[...].astype(F32)
        dza_ref[...] = (dy * (ypre * ps_ref[...]) * (sg * (1.0 + za * (1.0 - sg)))).astype(BF16)
        dysc = dy * (za * sg)
        za_e = _with_next(za, zn_ref, last)
        dy_e = _with_next(dy, dyn_ref, last)
        dypre_e = (dy_e * (za_e * _sigmoid(za_e)) * ps_ref[...]).astype(BF16)
        dm_e = lax.dot_general(dypre_e, w_ref[...], (((1,), (1,)), ((), ())), preferred_element_type=F32)
        rows = tm + HALO
        q = dm_e * _pool_inv_count(g, i * tm, rows, gd)
        f2 = q + pltpu.roll(q, rows - 1, 0)
        f4 = f2 + pltpu.roll(f2, rows - 2, 0)
        f8 = f4 + pltpu.roll(f4, rows - 4, 0)
        f16 = f8 + pltpu.roll(f8, rows - 8, 0)
        ahead = jnp.where(g == 0, f2, jnp.where(g == 1, f4, jnp.where(g == 2, f8, f16)))
        dxa_ref[...] = (ahead[:tm] - dm_e[:tm]).astype(BF16)
        dw = lax.dot_general(mixed16, dypre_e[:tm], (((0,), (0,)), ((), ())), preferred_element_type=F32)
        dps = jnp.sum(dysc * ypre, axis=0, keepdims=True)

        @pl.when(i == 0)
        def _():
            dw_ref[...] = dw
            dps_ref[...] = dps

        @pl.when(i > 0)
        def _():
            dw_ref[...] += dw
            dps_ref[...] += dps

    out_tile = pl.BlockSpec((tm, gd), lambda g, i: (i, g))
    return pl.pallas_call(
        body, grid=(N_POOL_GROUPS, n_i),
        in_specs=[_cur(tm, gd, 0), _prev(tm, gd, 0), _cur(tm, gd, za_col), _next(tm, gd, za_col, t),
                  _cur(tm, gd, 0), _next(tm, gd, 0, t),
                  pl.BlockSpec((None, gd, gd), lambda g, i: (g, 0, 0)),
                  pl.BlockSpec((1, gd), lambda g, i: (0, g))],
        out_specs=[out_tile, out_tile, pl.BlockSpec((None, gd, gd), lambda g, i: (g, 0, 0)),
                   pl.BlockSpec((1, gd), lambda g, i: (0, g))],
        out_shape=[jax.ShapeDtypeStruct((t, pw), BF16), jax.ShapeDtypeStruct((t, pw), BF16),
                   jax.ShapeDtypeStruct((N_POOL_GROUPS, gd, gd), F32), jax.ShapeDtypeStruct((1, pw), F32)],
        name=name, compiler_params=_cparams(("arbitrary", "arbitrary"), 48),
    )(proj, proj, proj, proj, dya, dya, pool_w, pool_scale)


def _conv_cols(d, cw):
    cc = d // 2
    base = (d + 4 * d) // cw
    return base, base + cc // cw, base + 2 * (cc // cw), base + 3 * (cc // cw)


def _conv_fwd(proj, conv_w, *, d, tm, cw, name):
    t = proj.shape[0]
    cc = d // 2
    u_col, b_col, c_col, z_col = _conv_cols(d, cw)

    def body(u_ref, up_ref, c_ref, cp_ref, b_ref, z_ref, w_ref, y_ref):
        i = pl.program_id(1)
        v = u_ref[...].astype(F32) * c_ref[...].astype(F32)
        vprev = up_ref[...].astype(F32) * cp_ref[...].astype(F32)
        ve = jnp.concatenate([jnp.where(i == 0, 0.0, vprev), v], axis=0)
        w = w_ref[...]
        y = w[0:1] * pltpu.roll(ve, 2, 0)[HALO:] + w[1:2] * pltpu.roll(ve, 1, 0)[HALO:] + w[2:3] * v
        z = z_ref[...].astype(F32)
        y_ref[...] = (b_ref[...].astype(F32) * y * (z * _sigmoid(z))).astype(BF16)

    return pl.pallas_call(
        body, grid=(cc // cw, t // tm),
        in_specs=[_cur(tm, cw, u_col), _prev(tm, cw, u_col), _cur(tm, cw, c_col), _prev(tm, cw, c_col),
                  _cur(tm, cw, b_col), _cur(tm, cw, z_col),
                  pl.BlockSpec((8, cw), lambda cb, i: (0, cb))],
        out_specs=pl.BlockSpec((tm, cw), lambda cb, i: (i, cb)),
        out_shape=jax.ShapeDtypeStruct((t, cc), BF16), name=name,
        compiler_params=_cparams(("arbitrary", "arbitrary"), 40),
    )(proj, proj, proj, proj, proj, proj, conv_w)


def _conv_bwd(dyc, proj, conv_w, *, d, tm, cw, name):
    t = proj.shape[0]
    cc = d // 2
    n_i = t // tm
    u_col, b_col, c_col, z_col = _conv_cols(d, cw)

    def body(u_ref, up_ref, c_ref, cp_ref, b_ref, bn_ref, z_ref, zn_ref, dy_ref, dyn_ref, w_ref,
             du_ref, db_ref, dc_ref, dz_ref, dw_ref):
        i = pl.program_id(1)
        last = i == n_i - 1
        u = u_ref[...].astype(F32)
        cg = c_ref[...].astype(F32)
        v = u * cg
        vprev = up_ref[...].astype(F32) * cp_ref[...].astype(F32)
        ve = jnp.concatenate([jnp.where(i == 0, 0.0, vprev), v], axis=0)
        v2 = pltpu.roll(ve, 2, 0)[HALO:]
        v1 = pltpu.roll(ve, 1, 0)[HALO:]
        w = w_ref[...]
        y = w[0:1] * v2 + w[1:2] * v1 + w[2:3] * v
        z = z_ref[...].astype(F32)
        sg = _sigmoid(z)
        bg = b_ref[...].astype(F32)
        dyc_v = dy_ref[...].astype(F32)
        dz_ref[...] = (dyc_v * bg * y * (sg * (1.0 + z * (1.0 - sg)))).astype(BF16)
        db_ref[...] = (dyc_v * y * (z * sg)).astype(BF16)
        ze = _with_next(z, zn_ref, last)
        dy_e = (_with_next(dyc_v, dyn_ref, last) * _with_next(bg, bn_ref, last) * (ze * _sigmoid(ze)))
        rows = tm + HALO
        dy0 = dy_e[:tm]
        dv = (w[2:3] * dy0 + w[1:2] * pltpu.roll(dy_e, rows - 1, 0)[:tm]
              + w[0:1] * pltpu.roll(dy_e, rows - 2, 0)[:tm])
        du_ref[...] = (dv * cg).astype(BF16)
        dc_ref[...] = (dv * u).astype(BF16)
        s0 = jnp.sum(dy0 * v2, axis=0, keepdims=True)
        s1 = jnp.sum(dy0 * v1, axis=0, keepdims=True)
        s2 = jnp.sum(dy0 * v, axis=0, keepdims=True)
        r = lax.broadcasted_iota(jnp.int32, (8, cw), 0)
        dw = jnp.where(r == 0, s0, jnp.where(r == 1, s1, jnp.where(r == 2, s2, 0.0)))

        @pl.when(i == 0)
        def _():
            dw_ref[...] = dw

        @pl.when(i > 0)
        def _():
            dw_ref[...] += dw

    out_tile = pl.BlockSpec((tm, cw), lambda cb, i: (i, cb))
    act = jax.ShapeDtypeStruct((t, cc), BF16)
    return pl.pallas_call(
        body, grid=(cc // cw, n_i),
        in_specs=[_cur(tm, cw, u_col), _prev(tm, cw, u_col), _cur(tm, cw, c_col), _prev(tm, cw, c_col),
                  _cur(tm, cw, b_col), _next(tm, cw, b_col, t), _cur(tm, cw, z_col), _next(tm, cw, z_col, t),
                  _cur(tm, cw, 0), _next(tm, cw, 0, t),
                  pl.BlockSpec((8, cw), lambda cb, i: (0, cb))],
        out_specs=[out_tile, out_tile, out_tile, out_tile, pl.BlockSpec((8, cw), lambda cb, i: (0, cb))],
        out_shape=[act, act, act, act, jax.ShapeDtypeStruct((8, cc), F32)], name=name,
        compiler_params=_cparams(("arbitrary", "arbitrary"), 48),
    )(proj, proj, proj, proj, proj, proj, proj, proj, dyc, dyc, conv_w)


def _sum_matrix(prefix):
    r = lax.broadcasted_iota(jnp.int32, (KEY_CHUNK, 2 * KEY_CHUNK), 0)
    c = lax.broadcasted_iota(jnp.int32, (KEY_CHUNK, 2 * KEY_CHUNK), 1)
    inside = (r <= c) if prefix else (r > c)
    return jnp.where((c >= KEY_CHUNK) | inside, 1.0, 0.0).astype(BF16)


def _chunk_sums(val, mat, two_pass):
    hi = val.astype(BF16)
    both = jnp.dot(hi, mat, preferred_element_type=F32)
    if two_pass:
        lo = (val - hi.astype(F32)).astype(BF16)
        both = both + jnp.dot(lo, mat, preferred_element_type=F32)
    return both[:, :KEY_CHUNK], both[:, KEY_CHUNK:]


def _stick_logits(qs, kb, strict):
    z = lax.dot_general(qs, kb, (((1,), (1,)), ((), ())), preferred_element_type=F32)
    sp = jnp.maximum(z, 0.0) + jnp.log(1.0 + jnp.exp(-jnp.abs(z)))
    log_beta = z - sp
    if strict is not None:
        sp = jnp.where(strict, sp, 0.0)
    return sp, log_beta


def _running(val, mat, carry, upwards):
    n = val.shape[1] // KEY_CHUNK
    out = [None] * n
    for c in (range(n) if upwards else reversed(range(n))):
        part, total = _chunk_sums(val[:, c * KEY_CHUNK:(c + 1) * KEY_CHUNK], mat, False)
        out[c] = part + carry
        carry = carry + total
    return jnp.concatenate(out, axis=1), carry


def _attn_cols(d):
    h = d // HEAD_DIM
    return h, 2 * h, 3 * h, 4 * h


def _sweep(i, per, block, carry, upwards):
    if upwards:
        carry = lax.fori_loop(0, i * per, lambda it, cr: block(it, cr, False), carry)
        for dgl in range(per):
            carry = block(i * per + dgl, carry, True)
        return carry
    for dgl in range(per - 1, -1, -1):
        carry = block(i * per + dgl, carry, True)
    return lax.fori_loop(0, i * per, lambda it, cr: block(i * per - 1 - it, cr, False), carry)


def _strict_mask(i, tq, kb, width):
    t_idx = i * tq + lax.broadcasted_iota(jnp.int32, (tq, width), 0)
    s_idx = kb * width + lax.broadcasted_iota(jnp.int32, (tq, width), 1)
    return s_idx < t_idx


def _attn_fwd(proj, *, d, tq, name):
    t = proj.shape[0]
    heads = d // HEAD_DIM
    q_col, k_col, v_col, z_col = _attn_cols(d)

    def body(q_ref, k_ref, v_ref, z_ref, o_ref, y_ref, tot_ref):
        i = pl.program_id(1)
        qs = (q_ref[...].astype(F32) * (HEAD_DIM ** -0.5)).astype(BF16)
        mat = _sum_matrix(prefix=False)

        def block(kb, carry, masked):
            acc, after = carry
            off = pl.multiple_of(kb * KEY_BLOCK, KEY_BLOCK)
            strict = _strict_mask(i, tq, kb, KEY_BLOCK) if masked else None
            sp, log_beta = _stick_logits(qs, k_ref[pl.ds(off, KEY_BLOCK), :], strict)
            between, after = _running(sp, mat, after, upwards=False)
            a = jnp.exp(log_beta - between)
            if masked:
                a = jnp.where(strict, a, 0.0)
            acc = acc + jnp.dot(a.astype(BF16), v_ref[pl.ds(off, KEY_BLOCK), :], preferred_element_type=F32)
            return acc, after

        zero = jnp.zeros((tq, HEAD_DIM), F32)
        o, total = _sweep(i, tq // KEY_BLOCK, block, (zero, zero), upwards=False)
        o_ref[...] = o.astype(BF16)
        tot_ref[...] = total
        z = z_ref[...].astype(F32)
        y_ref[...] = (o * (z * _sigmoid(z))).astype(BF16)

    tile = lambda col: pl.BlockSpec((tq, HEAD_DIM), lambda h, i: (i, col + h))
    full = lambda col: pl.BlockSpec((t, HEAD_DIM), lambda h, i: (0, col + h))
    act = jax.ShapeDtypeStruct((t, d), BF16)
    return pl.pallas_call(
        body, grid=(heads, t // tq),
        in_specs=[tile(q_col), full(k_col), full(v_col), tile(z_col)],
        out_specs=[tile(0), tile(0), tile(0)], out_shape=[act, act, jax.ShapeDtypeStruct((t, d), F32)],
        name=name, compiler_params=_cparams(("arbitrary", "arbitrary"), 40),
    )(proj, proj, proj, proj)


def _attn_bwd(dyb, o, tot, proj, *, d, tq, name):
    t = proj.shape[0]
    heads = d // HEAD_DIM
    n_i = t // tq
    q_col, k_col, v_col, z_col = _attn_cols(d)
    scale = HEAD_DIM ** -0.5

    def body(q_ref, k_ref, v_ref, z_ref, o_ref, tot_ref, dy_ref, dq_ref, dk_ref, dv_ref, dz_ref, dk_acc, dv_acc):
        i = pl.program_id(1)

        @pl.when(i == 0)
        def _():
            dk_acc[...] = jnp.zeros_like(dk_acc)
            dv_acc[...] = jnp.zeros_like(dv_acc)

        qs = (q_ref[...].astype(F32) * scale).astype(BF16)
        z = z_ref[...].astype(F32)
        sg = _sigmoid(z)
        dy = dy_ref[...].astype(F32)
        of = o_ref[...].astype(F32)
        dz_ref[...] = (dy * of * (sg * (1.0 + z * (1.0 - sg)))).astype(BF16)
        do16 = (dy * (z * sg)).astype(BF16)
        all_keep = tot_ref[...]
        all_keep = jnp.concatenate([all_keep] * (KEY_BLOCK // HEAD_DIM), axis=1)
        mat = _sum_matrix(prefix=True)

        def block(kb, carry, masked):
            dq, sp_seen, g_seen = carry
            off = pl.multiple_of(kb * KEY_BLOCK, KEY_BLOCK)
            kblk = k_ref[pl.ds(off, KEY_BLOCK), :]
            vblk = v_ref[pl.ds(off, KEY_BLOCK), :]
            strict = _strict_mask(i, tq, kb, KEY_BLOCK) if masked else None
            sp, log_beta = _stick_logits(qs, kblk, strict)
            sp_upto, sp_seen = _running(sp, mat, sp_seen, upwards=True)
            a = jnp.exp(log_beta - (all_keep - sp_upto))
            if masked:
                a = jnp.where(strict, a, 0.0)
            g = a * lax.dot_general(do16, vblk, (((1,), (1,)), ((), ())), preferred_element_type=F32)
            g_upto, g_seen = _running(g, mat, g_seen, upwards=True)
            dz_ts = g - jnp.exp(log_beta) * g_upto
            if masked:
                dz_ts = jnp.where(strict, dz_ts, 0.0)
            dz16 = dz_ts.astype(BF16)
            dq = dq + jnp.dot(dz16, kblk, preferred_element_type=F32)
            dk_acc[pl.ds(off, KEY_BLOCK), :] += lax.dot_general(
                dz16, qs, (((0,), (0,)), ((), ())), preferred_element_type=F32)
            dv_acc[pl.ds(off, KEY_BLOCK), :] += lax.dot_general(
                a.astype(BF16), do16, (((0,), (0,)), ((), ())), preferred_element_type=F32)
            return dq, sp_seen, g_seen

        zero = jnp.zeros((tq, HEAD_DIM), F32)
        dq, _, _ = _sweep(i, tq // KEY_BLOCK, block, (zero, zero, zero), upwards=True)
        dq_ref[...] = (dq * scale).astype(BF16)

        @pl.when(i == n_i - 1)
        def _():
            dk_ref[...] = dk_acc[...].astype(BF16)
            dv_ref[...] = dv_acc[...].astype(BF16)

    tile = lambda col: pl.BlockSpec((tq, HEAD_DIM), lambda h, i: (i, col + h))
    full = lambda col: pl.BlockSpec((t, HEAD_DIM), lambda h, i: (0, col + h))
    act = jax.ShapeDtypeStruct((t, d), BF16)
    return pl.pallas_call(
        body, grid=(heads, n_i),
        in_specs=[tile(q_col), full(k_col), full(v_col), tile(z_col), tile(0), tile(0), tile(0)],
        out_specs=[tile(0), full(0), full(0), tile(0)], out_shape=[act, act, act, act],
        scratch_shapes=[pltpu.VMEM((t, HEAD_DIM), F32), pltpu.VMEM((t, HEAD_DIM), F32)], name=name,
        compiler_params=_cparams(("arbitrary", "arbitrary"), 48),
    )(proj, proj, proj, proj, o, tot, dyb)


def _merge_fwd(gl, bst, *, d, tm, tn, name):
    t = gl.shape[0]
    nk = d // tn

    def body(g0, g1, g2, b0, b1, b2, m_ref):
        acc = _sigmoid(g0[...].astype(F32)) * b0[...].astype(F32)
        acc += _sigmoid(g1[...].astype(F32)) * b1[...].astype(F32)
        acc += _sigmoid(g2[...].astype(F32)) * b2[...].astype(F32)
        m_ref[...] = acc.astype(BF16)

    spec = lambda b: pl.BlockSpec((tm, tn), lambda i, kb: (i, b * nk + kb))
    return pl.pallas_call(
        body, grid=(t // tm, nk), in_specs=[spec(0), spec(1), spec(2)] * 2,
        out_specs=pl.BlockSpec((tm, tn), lambda i, kb: (i, kb)),
        out_shape=jax.ShapeDtypeStruct((t, d), BF16), name=name,
        compiler_params=_cparams(("arbitrary", "arbitrary"), 40),
    )(gl, gl, gl, bst, bst, bst)


def _merge_bwd(dmerged, gl, bst, *, d, tm, tn, name):
    t = gl.shape[0]
    nk = d // tn

    def body(dm_ref, g_ref, b_ref, db_ref, dg_ref, dbias_ref):
        i = pl.program_id(2)
        dm = dm_ref[...].astype(F32)
        sg = _sigmoid(g_ref[...].astype(F32))
        db_ref[...] = (dm * sg).astype(BF16)
        dgl = dm * b_ref[...].astype(F32) * (sg * (1.0 - sg))
        dg_ref[...] = dgl.astype(BF16)
        dbias = jnp.sum(dgl, axis=0, keepdims=True)

        @pl.when(i == 0)
        def _():
            dbias_ref[...] = dbias

        @pl.when(i > 0)
        def _():
            dbias_ref[...] += dbias

    wide = pl.BlockSpec((tm, tn), lambda b, kb, i: (i, b * nk + kb))
    act = jax.ShapeDtypeStruct((t, 3 * d), BF16)
    return pl.pallas_call(
        body, grid=(3, nk, t // tm),
        in_specs=[pl.BlockSpec((tm, tn), lambda b, kb, i: (i, kb)), wide, wide],
        out_specs=[wide, wide, pl.BlockSpec((1, tn), lambda b, kb, i: (0, b * nk + kb))],
        out_shape=[act, act, jax.ShapeDtypeStruct((1, 3 * d), F32)], name=name,
        compiler_params=_cparams(("arbitrary", "arbitrary", "arbitrary"), 40),
    )(dmerged, gl, bst)


def _adamw(w, m, v, parts, layer, prev, *, tr, name):
    depth, r, c = w.shape
    n = parts.shape[0]

    def body(*refs):
        w_ref, m_ref, v_ref, p_ref = refs[:4]
        g_ref, d_ref, nm_ref, nv_ref = refs[-4:]
        g = p_ref[0].astype(F32)
        for s in range(1, n):
            g = g + p_ref[s].astype(F32)
        wv = w_ref[...]
        nm = ADAM_B1 * m_ref[...] + (1.0 - ADAM_B1) * g
        nv = ADAM_B2 * v_ref[...] + (1.0 - ADAM_B2) * (g * g)
        m_hat = nm / (1.0 - ADAM_B1 ** ADAM_STEP)
        v_hat = nv / (1.0 - ADAM_B2 ** ADAM_STEP)
        g_ref[...] = g
        nm_ref[...] = nm
        nv_ref[...] = nv
        d_ref[...] = -ADAM_LR * (m_hat / (jnp.sqrt(v_hat) + ADAM_EPS) + ADAM_WD * wv)

    tile = pl.BlockSpec((None, tr, c), lambda i: (layer, i, 0))
    in_specs = [tile, tile, tile, pl.BlockSpec((n, tr, c), lambda i: (0, i, 0))]
    args = [w, m, v, parts]
    aliases = {}
    if prev is not None:
        in_specs += [pl.BlockSpec(memory_space=pl.ANY)] * 4
        aliases = {4 + k: k for k in range(4)}
        args += list(prev)
    full = jax.ShapeDtypeStruct((depth, r, c), F32)
    return pl.pallas_call(
        body, grid=(r // tr,), in_specs=in_specs, out_specs=[tile] * 4, out_shape=[full] * 4,
        input_output_aliases=aliases, name=name, compiler_params=_cparams(("arbitrary",), 48),
    )(*args)


def _adamw_layers(w, m, v, parts_by_layer, *, tr, name):
    lead = w.shape[0]
    flat = lambda a: a.reshape(lead, -1, a.shape[-1])
    w, m, v = flat(w), flat(m), flat(v)
    out = None
    for layer in reversed(range(lead)):
        parts = parts_by_layer[layer]
        parts = parts.reshape(parts.shape[0], -1, parts.shape[-1])
        out = _adamw(w, m, v, parts, layer, out, tr=min(tr, w.shape[1]), name=f"{name}_l{layer}")
    return out


def _place():
    return lax.axis_index("x"), lax.axis_index("y"), lax.axis_index("c")


def _index(px, py, pc):
    return 4 * px + 2 * py + pc


def _whole(ref, idx):
    return ref.at[idx]


def _row_block(rows):
    return lambda ref, idx: ref.at[:, pl.ds(idx * rows, rows), :]


def _lane_block(cols):
    return lambda ref, idx: ref.at[:, pl.ds(idx * cols, cols)]


def _all_gather(shards, out_shapes, placers, *, name):
    n = len(shards)

    def body(*refs):
        src, out = refs[:n], refs[n:2 * n]
        send_sems, recv_sems, local_sems = refs[2 * n:]
        x, y, c = _place()
        me, sibling = (x, y, c), (x, y, 1 - c)
        chips = [(1 - x, y), (x, 1 - y), (1 - x, 1 - y)]

        def copy(a, k, block, to, from_shard=False):
            window = placers[a](out[a], _index(*block))
            return pltpu.make_async_remote_copy(
                src_ref=src[a] if from_shard else window, dst_ref=window,
                send_sem=send_sems.at[a * 7 + k], recv_sem=recv_sems.at[a * 7 + k],
                device_id=to, device_id_type=MESH)

        mine = [pltpu.make_async_copy(src[a], placers[a](out[a], _index(*me)), local_sems.at[a])
                for a in range(n)]
        started = []
        for a in range(n):
            mine[a].start()
            started.append(copy(a, 0, me, sibling, True))
            started += [copy(a, 1 + j, me, (*chip, c), True) for j, chip in enumerate(chips)]
        for cp in started:
            cp.start()
        for j, chip in enumerate(chips):
            for a in range(n):
                copy(a, 1 + j, (*chip, c), me).wait_recv()
                passed = copy(a, 4 + j, (*chip, c), sibling)
                passed.start()
                started.append(passed)
        for a in range(n):
            copy(a, 0, sibling, me).wait_recv()
            for j, chip in enumerate(chips):
                copy(a, 4 + j, (*chip, 1 - c), me).wait_recv()
        for cp in started:
            cp.wait_send()
        for cp in mine:
            cp.wait()

    hbm = pl.BlockSpec(memory_space=pl.ANY)
    return pl.pallas_call(
        body, in_specs=[hbm] * n, out_specs=[hbm] * n, out_shape=out_shapes,
        scratch_shapes=[pltpu.SemaphoreType.DMA((7 * n,)), pltpu.SemaphoreType.DMA((7 * n,)),
                        pltpu.SemaphoreType.DMA((n,))],
        name=name,
    )(*shards)


def _scatter_parts(grads, pickers, part_shapes, *, name):
    n = len(grads)

    def body(*refs):
        src, out = refs[:n], refs[n:2 * n]
        send_sems, recv_sems, local_sems = refs[2 * n:]
        x, y, c = _place()
        my_idx = _index(x, y, c)
        peers = [(x ^ (k >> 2), y ^ ((k >> 1) & 1), c ^ (k & 1)) for k in range(1, N_DEV)]

        def copy(a, k, peer):
            return pltpu.make_async_remote_copy(
                src_ref=pickers[a](src[a], _index(*peer)), dst_ref=out[a].at[my_idx],
                send_sem=send_sems.at[a * 7 + k], recv_sem=recv_sems.at[a * 7 + k],
                device_id=peer, device_id_type=MESH)

        def arrival(a, k, peer):
            return pltpu.make_async_remote_copy(
                src_ref=pickers[a](src[a], my_idx), dst_ref=out[a].at[_index(*peer)],
                send_sem=send_sems.at[a * 7 + k], recv_sem=recv_sems.at[a * 7 + k],
                device_id=peer, device_id_type=MESH)

        mine = [pltpu.make_async_copy(pickers[a](src[a], my_idx), out[a].at[my_idx], local_sems.at[a])
                for a in range(n)]
        sent = [copy(a, k, peer) for a in range(n) for k, peer in enumerate(peers)]
        for cp in mine + sent:
            cp.start()
        for a in range(n):
            for k, peer in enumerate(peers):
                arrival(a, k, peer).wait_recv()
        for cp in sent:
            cp.wait_send()
        for cp in mine:
            cp.wait()

    hbm = pl.BlockSpec(memory_space=pl.ANY)
    return pl.pallas_call(
        body, in_specs=[hbm] * n, out_specs=[hbm] * n, out_shape=part_shapes,
        scratch_shapes=[pltpu.SemaphoreType.DMA((7 * n,)), pltpu.SemaphoreType.DMA((7 * n,)),
                        pltpu.SemaphoreType.DMA((n,))],
        name=name,
    )(*grads)


def _ada_fwd(c8, w_ada, b_ada8, *, name):
    depth, d, n = w_ada.shape

    def body(c_ref, w_ref, b_ref, mod_ref, sc_ref, c_all, prod, got, send_sems, recv_sems):
        x, y, c = _place()
        my_idx = _index(x, y, c)
        peers = [(x ^ (k >> 2), y ^ ((k >> 1) & 1), c ^ (k & 1)) for k in range(1, N_DEV)]

        def slab(ref, idx):
            return ref.at[pl.ds(pl.multiple_of(idx * 8, 8), 8)]

        def c_copy(k, peer, sender):
            return pltpu.make_async_remote_copy(
                src_ref=c_ref, dst_ref=slab(c_all, sender), send_sem=send_sems.at[k],
                recv_sem=recv_sems.at[k], device_id=peer, device_id_type=MESH)

        def m_copy(k, peer, sender):
            return pltpu.make_async_remote_copy(
                src_ref=prod.at[_index(*peer)], dst_ref=got.at[sender], send_sem=send_sems.at[7 + k],
                recv_sem=recv_sems.at[7 + k], device_id=peer, device_id_type=MESH)

        sends = [c_copy(k, peer, my_idx) for k, peer in enumerate(peers)]
        for cp in sends:
            cp.start()
        slab(c_all, my_idx)[...] = c_ref[...]
        for k, peer in enumerate(peers):
            c_copy(k, peer, _index(*peer)).wait_recv()
        cv = c_all[...]
        sc = cv * _sigmoid(cv)
        sc_ref[...] = sc
        for layer in range(depth):
            p = jnp.dot(sc, w_ref[layer], preferred_element_type=F32, precision=lax.Precision.HIGHEST)
            for s in range(N_DEV):
                prod[s, layer] = p[8 * s:8 * s + 8]
        sends2 = [m_copy(k, peer, my_idx) for k, peer in enumerate(peers)]
        for cp in sends2:
            cp.start()
        got[my_idx] = prod[my_idx]
        for k, peer in enumerate(peers):
            m_copy(k, peer, _index(*peer)).wait_recv()
        for layer in range(depth):
            for s in range(N_DEV):
                mod_ref[layer, :, s * n:(s + 1) * n] = got[s, layer] + b_ref[layer, :, s * n:(s + 1) * n]
        for cp in sends + sends2:
            cp.wait_send()

    vmem = pl.BlockSpec(memory_space=pltpu.VMEM)
    return pl.pallas_call(
        body, in_specs=[vmem] * 3, out_specs=[vmem] * 2,
        out_shape=[jax.ShapeDtypeStruct((depth, 8, N_DEV * n), F32),
                   jax.ShapeDtypeStruct((8 * N_DEV, d), F32)],
        scratch_shapes=[pltpu.VMEM((8 * N_DEV, d), F32), pltpu.VMEM((N_DEV, depth, 8, n), F32),
                        pltpu.VMEM((N_DEV, depth, 8, n), F32),
                        pltpu.SemaphoreType.DMA((14,)), pltpu.SemaphoreType.DMA((14,))],
        name=name, compiler_params=pltpu.CompilerParams(vmem_limit_bytes=56 << 20),
    )(c8, w_ada, b_ada8)


def _ada_bwd(sc8, dmod, *, name):
    depth, _, n = dmod.shape
    d = sc8.shape[1]

    def body(sc_ref, dm_ref, o_ref):
        for layer in range(depth):
            o_ref[layer] = lax.dot_general(sc_ref[...], dm_ref[layer], (((0,), (0,)), ((), ())),
                                           preferred_element_type=F32, precision=lax.Precision.HIGHEST)

    vmem = pl.BlockSpec(memory_space=pltpu.VMEM)
    return pl.pallas_call(
        body, in_specs=[vmem] * 2, out_specs=vmem, out_shape=jax.ShapeDtypeStruct((depth, d, n), F32),
        name=name, compiler_params=pltpu.CompilerParams(vmem_limit_bytes=40 << 20),
    )(sc8, dmod)


def _gather_small(vec, *, name):
    r = vec.shape[0]

    def body(v_ref, o_ref, send_sems, recv_sems):
        x, y, c = _place()
        my_idx = _index(x, y, c)
        peers = [(x ^ (k >> 2), y ^ ((k >> 1) & 1), c ^ (k & 1)) for k in range(1, N_DEV)]

        def copy(k, peer, sender):
            return pltpu.make_async_remote_copy(
                src_ref=v_ref, dst_ref=o_ref.at[sender], send_sem=send_sems.at[k],
                recv_sem=recv_sems.at[k], device_id=peer, device_id_type=MESH)

        sends = [copy(k, peer, my_idx) for k, peer in enumerate(peers)]
        for cp in sends:
            cp.start()
        o_ref[my_idx] = v_ref[...]
        for k, peer in enumerate(peers):
            copy(k, peer, _index(*peer)).wait_recv()
        for cp in sends:
            cp.wait_send()

    vmem = pl.BlockSpec(memory_space=pltpu.VMEM)
    return pl.pallas_call(
        body, in_specs=[vmem], out_specs=vmem, out_shape=jax.ShapeDtypeStruct((N_DEV, r, LANE), F32),
        scratch_shapes=[pltpu.SemaphoreType.DMA((7,)), pltpu.SemaphoreType.DMA((7,))], name=name,
    )(vec)


TM = 512
TQ = 512


def _layer_fwd(x, mod, small, gw, *, d, tag):
    shift, scale, res_gate = mod
    h = _prenorm(x, small["norm_g"], scale, shift, tm=TM, name=f"prenorm_{tag}")
    proj = _mm_nn(h, gw["w_in"], out_dtype=BF16, tm=TM, name=f"in_proj_{tag}")
    gl = _mm_nn(h, gw["w_gate"], out_dtype=BF16, tm=TM, bias=small["b_gate"], name=f"gate_proj_{tag}")
    ya = _pool_fwd(proj, gw["pool_w"], small["pool_scale"], d=d, tm=TM, name=f"pool_fwd_{tag}")
    o, yb, tot = _attn_fwd(proj, d=d, tq=TQ, name=f"attn_fwd_{tag}")
    yc = _conv_fwd(proj, gw["conv_w"], d=d, tm=TM, cw=LANE * 2, name=f"conv_fwd_{tag}")
    nblk = gw["w_br_a"].shape[0]
    bst = _mm_nn(ya, gw["w_br_a"], out_dtype=BF16, tm=TM, out_cols=3 * d, name=f"branch_a_{tag}")
    bst = _mm_nn(yb, gw["w_br_b"], out_dtype=BF16, tm=TM, out=bst, out_col_block=1, name=f"branch_b_{tag}")
    bst = _mm_nn(yc, gw["w_br_c"], out_dtype=BF16, tm=TM, out=bst, out_col_block=2 * nblk,
                 name=f"branch_c_{tag}")
    merged = _merge_fwd(gl, bst, d=d, tm=TM, tn=TM, name=f"merge_fwd_{tag}")
    x_new, mo = _out_fwd(merged, gw["w_out"][0], x, res_gate, tm=TM, name=f"out_fwd_{tag}")
    return x_new, dict(x=x, h=h, proj=proj, gl=gl, ya=ya, yb=yb, yc=yc, o=o, tot=tot, bst=bst, merged=merged,
                       mo=mo)


def _layer_bwd(dout, sv, mod, small, gw, *, d, tag):
    shift, scale, res_gate = mod
    nblk = gw["w_br_a"].shape[0]
    dmo, d_res_gate = _out_bwd(dout, sv["mo"], res_gate, tm=TM, name=f"out_bwd_{tag}")
    g_w_out = _mm_tn(sv["merged"], dmo, nb=1, n=d, tk=TM, tmm=TM, name=f"dw_out_{tag}")
    dmerged = _mm_nt(dmo, gw["w_out"], out_dtype=BF16, tm=TM, name=f"dmerged_{tag}")
    dbst, dgl, d_b_gate = _merge_bwd(dmerged, sv["gl"], sv["bst"], d=d, tm=TM, tn=TM, name=f"merge_bwd_{tag}")
    g_w_gate = _mm_tn(sv["h"], dgl, nb=nblk, n=3 * d // nblk, tk=d // 2, tmm=TM, name=f"dw_gate_{tag}")
    dh = _mm_nt(dgl, gw["w_gate"], out_dtype=F32, tm=TM, name=f"dh_gate_{tag}")
    g_w_br_a = _mm_tn(sv["ya"], dbst, nb=nblk, n=d // nblk, tk=d // 2, tmm=TM, name=f"dw_br_a_{tag}")
    g_w_br_b = _mm_tn(sv["yb"], dbst, nb=1, n=d, tk=TM, tmm=TM, col_block=1, name=f"dw_br_b_{tag}")
    g_w_br_c = _mm_tn(sv["yc"], dbst, nb=nblk, n=d // nblk, tk=d // 2, tmm=TM, col_block=2 * nblk,
                      name=f"dw_br_c_{tag}")
    dya = _mm_nt(dbst, gw["w_br_a"], out_dtype=BF16, tm=TM, name=f"dya_{tag}")
    dyb = _mm_nt(dbst, gw["w_br_b"], out_dtype=BF16, tm=TM, col_block=1, name=f"dyb_{tag}")
    dyc = _mm_nt(dbst, gw["w_br_c"], out_dtype=BF16, tm=TM, col_block=2 * nblk, name=f"dyc_{tag}")
    dxa, dza, g_pool_w, d_pool_scale = _pool_bwd(dya, sv["proj"], gw["pool_w"], small["pool_scale"], d=d, tm=TM,
                                                 name=f"pool_bwd_{tag}")
    dq, dk, dv, dzb = _attn_bwd(dyb, sv["o"], sv["tot"], sv["proj"], d=d, tq=TQ, name=f"attn_bwd_{tag}")
    du, dbg, dcg, dzc, d_conv_w = _conv_bwd(dyc, sv["proj"], gw["conv_w"], d=d, tm=TM, cw=LANE * 2,
                                            name=f"conv_bwd_{tag}")
    dproj = jnp.concatenate([dxa, dza, dq, dk, dv, dzb, du, dbg, dcg, dzc], axis=1)
    g_w_in = _mm_tn(sv["h"], dproj, nb=nblk, n=7 * d // nblk, tk=d // 2, tmm=TM, name=f"dw_in_{tag}")
    dh = _mm_nt(dproj, gw["w_in"], out_dtype=F32, tm=TM, addend=dh, name=f"dh_in_{tag}")
    dx, d_norm_g, d_scale, d_shift = _prenorm_bwd(dh, sv["x"], dout, small["norm_g"], scale, tm=TM,
                                                  name=f"prenorm_bwd_{tag}")
    big = dict(w_in=g_w_in, w_gate=g_w_gate, w_br_a=g_w_br_a, w_br_b=g_w_br_b, w_br_c=g_w_br_c,
               w_out=g_w_out, pool_w=g_pool_w.astype(BF16))
    part = dict(norm_g=d_norm_g, mod=jnp.concatenate([d_shift, d_scale, d_res_gate], axis=1),
                pool_scale=d_pool_scale, b_gate=d_b_gate, conv_w=d_conv_w[:3])
    return dx, big, part


BIG = ("w_in", "w_gate", "w_br_a", "w_br_b", "w_br_c", "w_out", "pool_w")
SMALL_PER_LAYER = ("norm_g", "mod", "pool_scale", "b_gate", "conv_w")


def _gather_layer(w, layer, *, d):
    nd = N_DEV
    gd = d // 2 // N_POOL_GROUPS
    cc = d // 2
    shards = [w[k][layer].astype(BF16) for k in BIG]
    shards.append(jnp.pad(w["conv_w"][layer], ((0, 5), (0, 0))))
    blocks = [jax.ShapeDtypeStruct((nd,) + s.shape, BF16) for s in shards[:6]]
    blocks.append(jax.ShapeDtypeStruct((N_POOL_GROUPS, gd, gd), BF16))
    blocks.append(jax.ShapeDtypeStruct((8, cc), F32))
    placers = [_whole] * 6 + [_row_block(gd // nd), _lane_block(cc // nd)]
    got = _all_gather(shards, blocks, placers, name=f"gather_weights_l{layer}")
    gw = dict(zip(BIG + ("conv_w",), got))
    gw["w_br_b"] = gw["w_br_b"].reshape(1, d, d)
    gw["w_out"] = gw["w_out"].reshape(1, d, d)
    return gw


def _scatter_layer(big, layer, *, d):
    nd = N_DEV
    gd = d // 2 // N_POOL_GROUPS
    grads = [big[k] for k in BIG]
    grads[3] = grads[3].reshape(nd, d // nd, d)
    grads[5] = grads[5].reshape(nd, d // nd, d)
    shapes = [jax.ShapeDtypeStruct(g.shape, BF16) for g in grads[:6]]
    shapes.append(jax.ShapeDtypeStruct((nd, N_POOL_GROUPS, gd // nd, gd), BF16))
    pickers = [_whole] * 6 + [_row_block(gd // nd)]
    return dict(zip(BIG, _scatter_parts(grads, pickers, shapes, name=f"scatter_grads_l{layer}")))


def _pack(pieces):
    return jnp.concatenate([p.reshape(-1, LANE) for p in pieces], axis=0)


def kernel(x, c, norm_g, w_ada, b_ada, w_in, pool_w, pool_scale, conv_w, w_br_a, w_br_b, w_br_c, w_gate, b_gate, w_out, final_g, loss_target, m_norm_g, m_w_ada, m_b_ada, m_w_in, m_pool_w, m_pool_scale, m_conv_w, m_w_br_a, m_w_br_b, m_w_br_c, m_w_gate, m_b_gate, m_w_out, m_final_g, v_norm_g, v_w_ada, v_b_ada, v_w_in, v_pool_w, v_pool_scale, v_conv_w, v_w_br_a, v_w_br_b, v_w_br_c, v_w_gate, v_b_gate, v_w_out, v_final_g):
    names = ("norm_g", "w_ada", "b_ada", "w_in", "pool_w", "pool_scale", "conv_w", "w_br_a", "w_br_b",
             "w_br_c", "w_gate", "b_gate", "w_out", "final_g")
    w = dict(zip(names, (norm_g, w_ada, b_ada, w_in, pool_w, pool_scale, conv_w, w_br_a, w_br_b, w_br_c,
                         w_gate, b_gate, w_out, final_g)))
    m = dict(zip(names, (m_norm_g, m_w_ada, m_b_ada, m_w_in, m_pool_w, m_pool_scale, m_conv_w, m_w_br_a,
                         m_w_br_b, m_w_br_c, m_w_gate, m_b_gate, m_w_out, m_final_g)))
    v = dict(zip(names, (v_norm_g, v_w_ada, v_b_ada, v_w_in, v_pool_w, v_pool_scale, v_conv_w, v_w_br_a,
                         v_w_br_b, v_w_br_c, v_w_gate, v_b_gate, v_w_out, v_final_g)))
    _, t, d = x.shape
    depth = norm_g.shape[0]
    cc = d // 2
    my_idx = _index(*_place())

    mod8, sc_all = _ada_fwd(jnp.broadcast_to(c, (8, d)), w_ada,
                            jnp.broadcast_to(b_ada[:, None, :], (depth, 8, 3 * d)), name="ada_fwd")
    mods = [tuple(mod8[l, 0:1, k * d:(k + 1) * d] for k in range(3)) for l in range(depth)]
    small = [dict(norm_g=norm_g[l][None], b_gate=b_gate[l][None], pool_scale=pool_scale[l][None])
             for l in range(depth)]

    gws = [_gather_layer(w, l, d=d) for l in range(depth)]

    xs = x[0]
    saved = []
    for l in range(depth):
        xs, sv = _layer_fwd(xs, mods[l], small[l], gws[l], d=d, tag=f"l{l}")
        saved.append(sv)
    dx, loss8, d_final_g = _loss_head(xs, loss_target[0], final_g[None], tm=TM, name="loss_head")
    loss = lax.psum(loss8[0, 0], ("x", "y", "c"))

    parts, small_parts = [None] * depth, [None] * depth
    for l in reversed(range(depth)):
        dx, big, small_parts[l] = _layer_bwd(dx, saved[l], mods[l], small[l], gws[l], d=d, tag=f"l{l}")
        parts[l] = _scatter_layer(big, l, d=d)

    pieces = [small_parts[l][k] for l in range(depth) for k in SMALL_PER_LAYER] + [d_final_g]
    sizes = [p.size for p in pieces]
    gathered = _gather_small(_pack(pieces), name="gather_small_grads")
    zero_conv = jnp.zeros((3, cc), F32)

    def packed(src):
        per_layer = lambda l: [src["norm_g"][l], src["b_ada"][l], src["pool_scale"][l], src["b_gate"][l],
                               zero_conv]
        return _pack([p for l in range(depth) for p in per_layer(l)] + [src["final_g"]])[None]

    small_out = _adamw(packed(w), packed(m), packed(v), gathered, 0, None, tr=gathered.shape[1],
                       name="adamw_small")

    def unpack(flat):
        flat = flat.reshape(-1)
        out, off = [], 0
        for n in sizes:
            out.append(flat[off:off + n])
            off += n
        return out

    small_res = [unpack(a) for a in small_out]
    k_per = len(SMALL_PER_LAYER)

    def small_stack(which, pos, shape):
        return jnp.stack([small_res[which][l * k_per + pos] for l in range(depth)]).reshape(shape)

    res = {}
    for pos, name in ((0, "norm_g"), (1, "b_ada"), (2, "pool_scale"), (3, "b_gate")):
        res[name] = tuple(small_stack(k, pos, w[name].shape) for k in range(4))
    res["final_g"] = tuple(small_res[k][-1].reshape(final_g.shape) for k in range(4))

    ncol = cc // N_DEV
    conv_total = small_stack(0, 4, (depth * 3, cc))
    conv_mine = lax.dynamic_slice_in_dim(conv_total, my_idx * ncol, ncol, axis=1)
    pad8 = lambda a: jnp.pad(a.reshape(depth * 3, ncol), ((0, 8 - depth * 3), (0, 0)))[None]
    conv_out = _adamw(pad8(conv_w), pad8(m["conv_w"]), pad8(v["conv_w"]), pad8(conv_mine), 0, None, tr=8,
                      name="adamw_conv_w")
    res["conv_w"] = tuple(a[0, :depth * 3].reshape(conv_w.shape) for a in conv_out)

    n_ada = 3 * d // N_DEV
    mod_off = [sum(sizes[:l * k_per + 1]) // LANE for l in range(depth)]
    dmod = jnp.stack([
        lax.dynamic_slice_in_dim(gathered[:, mod_off[l]:mod_off[l] + 3 * d // LANE].reshape(N_DEV, 3 * d),
                                 my_idx * n_ada, n_ada, axis=1) for l in range(depth)])
    g_w_ada = _ada_bwd(sc_all[::8], dmod, name="ada_bwd")
    res["w_ada"] = _adamw_layers(w_ada, m["w_ada"], v["w_ada"], [g_w_ada[l][None] for l in range(depth)],
                                 tr=128, name="adamw_w_ada")

    for name in BIG:
        out = _adamw_layers(w[name], m[name], v[name], [parts[l][name] for l in range(depth)], tr=128,
                            name=f"adamw_{name}")
        res[name] = tuple(a.reshape(w[name].shape) for a in out)

    outs = [loss, dx[None]]
    for k in range(4):
        outs += [res[name][k] for name in names]
    return tuple(outs)
```

```python
import functools

import jax
import jax.numpy as jnp
from jax import lax
from jax.experimental import pallas as pl
from jax.experimental.pallas import tpu as pltpu

F32 = jnp.float32
BF16 = jnp.bfloat16
MESH = pl.DeviceIdType.MESH

N_DEV = 8
DEPTH = 2
HEAD_DIM = 128
N_POOL_GROUPS = 4
POOL_WINDOWS = (2, 4, 8, 16)
RMS_EPS = 1e-6
LANE = 128
HALO = 16
KEY_CHUNK = 128
KEY_BLOCK = 512
ADAM_LR = 0.001
ADAM_B1 = 0.9
ADAM_B2 = 0.999
ADAM_EPS = 1e-08
ADAM_WD = 0.01
ADAM_STEP = 10


def _cparams(semantics, vmem_mb):
    return pltpu.CompilerParams(dimension_semantics=semantics, vmem_limit_bytes=vmem_mb << 20)


def _sigmoid(z):
    return 1.0 / (1.0 + jnp.exp(-z))


def _mm_nn(a, w, *, out_dtype, tm, name, bias=None, out=None, out_cols=None, out_col_block=0):
    m, k = a.shape
    nb, _, n = w.shape
    total_cols = out.shape[1] if out is not None else (nb * n if out_cols is None else out_cols)

    def body(*refs):
        if out is not None:
            refs = refs[:-2] + refs[-1:]
        if bias is not None:
            a_ref, w_ref, b_ref, o_ref = refs
        else:
            a_ref, w_ref, o_ref = refs
        acc = jnp.dot(a_ref[...], w_ref[...], preferred_element_type=F32)
        if bias is not None:
            acc = acc + b_ref[...]
        o_ref[...] = acc.astype(out_dtype)

    in_specs = [pl.BlockSpec((tm, k), lambda j, i: (i, 0)),
                pl.BlockSpec((None, k, n), lambda j, i: (j, 0, 0))]
    args = [a, w]
    if bias is not None:
        in_specs.append(pl.BlockSpec((1, n), lambda j, i: (0, j)))
        args.append(bias)
    aliases = {}
    if out is not None:
        in_specs.append(pl.BlockSpec(memory_space=pl.ANY))
        aliases = {len(args): 0}
        args.append(out)
    return pl.pallas_call(
        body, grid=(nb, m // tm), in_specs=in_specs,
        out_specs=pl.BlockSpec((tm, n), lambda j, i: (i, out_col_block + j)),
        out_shape=jax.ShapeDtypeStruct((m, total_cols), out_dtype),
        input_output_aliases=aliases, name=name,
        compiler_params=_cparams(("arbitrary", "arbitrary"), 56),
    )(*args)


def _mm_nt(dy, w, *, out_dtype, tm, name, addend=None, col_block=0):
    m = dy.shape[0]
    nb, k, n = w.shape

    def body(*refs):
        if addend is not None:
            dy_ref, w_ref, add_ref, o_ref, acc_ref = refs
        else:
            dy_ref, w_ref, o_ref, acc_ref = refs
        j = pl.program_id(1)
        part = lax.dot_general(dy_ref[...], w_ref[...], (((1,), (1,)), ((), ())),
                               preferred_element_type=F32)

        @pl.when(j == 0)
        def _():
            acc_ref[...] = part if addend is None else part + add_ref[...]

        @pl.when(j > 0)
        def _():
            acc_ref[...] += part

        @pl.when(j == nb - 1)
        def _():
            o_ref[...] = acc_ref[...].astype(out_dtype)

    in_specs = [pl.BlockSpec((tm, n), lambda i, j: (i, col_block + j)),
                pl.BlockSpec((None, k, n), lambda i, j: (j, 0, 0))]
    args = [dy, w]
    if addend is not None:
        in_specs.append(pl.BlockSpec((tm, k), lambda i, j: (i, 0)))
        args.append(addend)
    return pl.pallas_call(
        body, grid=(m // tm, nb), in_specs=in_specs,
        out_specs=pl.BlockSpec((tm, k), lambda i, j: (i, 0)),
        out_shape=jax.ShapeDtypeStruct((m, k), out_dtype),
        scratch_shapes=[pltpu.VMEM((tm, k), F32)], name=name,
        compiler_params=_cparams(("arbitrary", "arbitrary"), 56),
    )(*args)


def _mm_tn(a, dy, *, nb, n, tk, tmm, name, col_block=0, out_dtype=BF16):
    m, k = a.shape

    def body(a_ref, dy_ref, o_ref, acc_ref):
        t = pl.program_id(2)
        part = lax.dot_general(a_ref[...], dy_ref[...], (((0,), (0,)), ((), ())),
                               preferred_element_type=F32)

        @pl.when(t == 0)
        def _():
            acc_ref[...] = part

        @pl.when(t > 0)
        def _():
            acc_ref[...] += part

        @pl.when(t == pl.num_programs(2) - 1)
        def _():
            o_ref[...] = acc_ref[...].astype(out_dtype)

    return pl.pallas_call(
        body, grid=(nb, k // tk, m // tmm),
        in_specs=[pl.BlockSpec((tmm, tk), lambda j, kb, t: (t, kb)),
                  pl.BlockSpec((tmm, n), lambda j, kb, t: (t, col_block + j))],
        out_specs=pl.BlockSpec((None, tk, n), lambda j, kb, t: (j, kb, 0)),
        out_shape=jax.ShapeDtypeStruct((nb, k, n), out_dtype),
        scratch_shapes=[pltpu.VMEM((tk, n), F32)], name=name,
        compiler_params=_cparams(("arbitrary", "arbitrary", "arbitrary"), 56),
    )(a, dy)


def _prenorm(x, g, scale, shift, *, tm, name):
    t, d = x.shape

    def body(x_ref, g_ref, sc_ref, sh_ref, h_ref):
        xv = x_ref[...]
        r = lax.rsqrt(jnp.mean(xv * xv, axis=-1, keepdims=True) + RMS_EPS)
        h_ref[...] = ((xv * r) * g_ref[...] * (1.0 + sc_ref[...]) + sh_ref[...]).astype(BF16)

    row = pl.BlockSpec((1, d), lambda i: (0, 0))
    tile = pl.BlockSpec((tm, d), lambda i: (i, 0))
    return pl.pallas_call(
        body, grid=(t // tm,), in_specs=[tile, row, row, row], out_specs=tile,
        out_shape=jax.ShapeDtypeStruct((t, d), BF16), name=name,
        compiler_params=_cparams(("arbitrary",), 40),
    )(x, g, scale, shift)


def _prenorm_bwd(dh, x, dout, g, scale, *, tm, name):
    t, d = x.shape

    def body(dh_ref, x_ref, do_ref, g_ref, sc_ref, dx_ref, dg_ref, dsc_ref, dsh_ref):
        i = pl.program_id(0)
        xv = x_ref[...]
        dhv = dh_ref[...]
        r = lax.rsqrt(jnp.mean(xv * xv, axis=-1, keepdims=True) + RMS_EPS)
        xn = xv * r
        gain = g_ref[...] * (1.0 + sc_ref[...])
        dxn = dhv * gain
        dx_ref[...] = do_ref[...] + r * (dxn - xn * jnp.mean(dxn * xn, axis=-1, keepdims=True))
        dhxn = dhv * xn
        dg = jnp.sum(dhxn * (1.0 + sc_ref[...]), axis=0, keepdims=True)
        dsc = jnp.sum(dhxn * g_ref[...], axis=0, keepdims=True)
        dsh = jnp.sum(dhv, axis=0, keepdims=True)

        @pl.when(i == 0)
        def _():
            dg_ref[...] = dg
            dsc_ref[...] = dsc
            dsh_ref[...] = dsh

        @pl.when(i > 0)
        def _():
            dg_ref[...] += dg
            dsc_ref[...] += dsc
            dsh_ref[...] += dsh

    row = pl.BlockSpec((1, d), lambda i: (0, 0))
    tile = pl.BlockSpec((tm, d), lambda i: (i, 0))
    vec = jax.ShapeDtypeStruct((1, d), F32)
    return pl.pallas_call(
        body, grid=(t // tm,), in_specs=[tile, tile, tile, row, row],
        out_specs=[tile, row, row, row],
        out_shape=[jax.ShapeDtypeStruct((t, d), F32), vec, vec, vec], name=name,
        compiler_params=_cparams(("arbitrary",), 48),
    )(dh, x, dout, g, scale)


def _out_fwd(merged, w_out, x, res_gate, *, tm, name):
    t, d = x.shape
    k = merged.shape[1]

    def body(m_ref, w_ref, x_ref, rg_ref, xo_ref, mo_ref):
        mo = jnp.dot(m_ref[...], w_ref[...], preferred_element_type=F32)
        mo_ref[...] = mo.astype(BF16)
        xo_ref[...] = x_ref[...] + rg_ref[...] * mo

    tile = pl.BlockSpec((tm, d), lambda i: (i, 0))
    return pl.pallas_call(
        body, grid=(t // tm,),
        in_specs=[pl.BlockSpec((tm, k), lambda i: (i, 0)), pl.BlockSpec((k, d), lambda i: (0, 0)),
                  tile, pl.BlockSpec((1, d), lambda i: (0, 0))],
        out_specs=[tile, tile],
        out_shape=[jax.ShapeDtypeStruct((t, d), F32), jax.ShapeDtypeStruct((t, d), BF16)], name=name,
        compiler_params=_cparams(("arbitrary",), 56),
    )(merged, w_out, x, res_gate)


def _out_bwd(dout, mo, res_gate, *, tm, name):
    t, d = dout.shape

    def body(do_ref, mo_ref, rg_ref, dmo_ref, drg_ref):
        i = pl.program_id(0)
        dov = do_ref[...]
        dmo_ref[...] = (dov * rg_ref[...]).astype(BF16)
        drg = jnp.sum(dov * mo_ref[...].astype(F32), axis=0, keepdims=True)

        @pl.when(i == 0)
        def _():
            drg_ref[...] = drg

        @pl.when(i > 0)
        def _():
            drg_ref[...] += drg

    row = pl.BlockSpec((1, d), lambda i: (0, 0))
    tile = pl.BlockSpec((tm, d), lambda i: (i, 0))
    return pl.pallas_call(
        body, grid=(t // tm,), in_specs=[tile, tile, row], out_specs=[tile, row],
        out_shape=[jax.ShapeDtypeStruct((t, d), BF16), jax.ShapeDtypeStruct((1, d), F32)], name=name,
        compiler_params=_cparams(("arbitrary",), 40),
    )(dout, mo, res_gate)


def _loss_head(x, target, final_g, *, tm, name):
    t, d = x.shape

    def body(x_ref, t_ref, g_ref, dx_ref, loss_ref, dg_ref):
        i = pl.program_id(0)
        xv = x_ref[...]
        r = lax.rsqrt(jnp.mean(xv * xv, axis=-1, keepdims=True) + RMS_EPS)
        xn = xv * r
        err = xn * g_ref[...] - t_ref[...]
        part = (0.5 / d) * jnp.sum(jnp.sum(err * err, axis=1, keepdims=True), axis=0, keepdims=True)
        dy = err * (1.0 / d)
        dxn = dy * g_ref[...]
        dx_ref[...] = r * (dxn - xn * jnp.mean(dxn * xn, axis=-1, keepdims=True))
        dg = jnp.sum(dy * xn, axis=0, keepdims=True)

        @pl.when(i == 0)
        def _():
            loss_ref[...] = jnp.zeros_like(loss_ref) + part
            dg_ref[...] = dg

        @pl.when(i > 0)
        def _():
            loss_ref[...] += part
            dg_ref[...] += dg

    row = pl.BlockSpec((1, d), lambda i: (0, 0))
    tile = pl.BlockSpec((tm, d), lambda i: (i, 0))
    return pl.pallas_call(
        body, grid=(t // tm,), in_specs=[tile, tile, row],
        out_specs=[tile, pl.BlockSpec((8, LANE), lambda i: (0, 0)), row],
        out_shape=[jax.ShapeDtypeStruct((t, d), F32), jax.ShapeDtypeStruct((8, LANE), F32),
                   jax.ShapeDtypeStruct((1, d), F32)], name=name,
        compiler_params=_cparams(("arbitrary",), 40),
    )(x, target, final_g)


def _cur(tm, cw, col):
    return pl.BlockSpec((tm, cw), lambda cb, i: (i, col + cb))


def _prev(tm, cw, col):
    return pl.BlockSpec((HALO, cw), lambda cb, i: (jnp.maximum(i * (tm // HALO) - 1, 0), col + cb))


def _next(tm, cw, col, t):
    last = t // HALO - 1
    return pl.BlockSpec((HALO, cw), lambda cb, i: (jnp.minimum((i + 1) * (tm // HALO), last), col + cb))


def _with_prev(prev_ref, cur, first):
    prev = jnp.where(first, 0.0, prev_ref[...].astype(F32))
    return jnp.concatenate([prev, cur], axis=0)


def _with_next(cur, next_ref, last):
    nxt = jnp.where(last, 0.0, next_ref[...].astype(F32))
    return jnp.concatenate([cur, nxt], axis=0)


def _pool_mixed(xe, cur, g, row0, tm):
    s2 = xe + pltpu.roll(xe, 1, 0)
    s4 = s2 + pltpu.roll(s2, 2, 0)
    s8 = s4 + pltpu.roll(s4, 4, 0)
    s16 = s8 + pltpu.roll(s8, 8, 0)
    win = jnp.where(g == 0, s2, jnp.where(g == 1, s4, jnp.where(g == 2, s8, s16)))[HALO:]
    return win * _pool_inv_count(g, row0, tm, cur.shape[1]) - cur


def _pool_inv_count(g, row0, rows, cols):
    tpos = row0 + lax.broadcasted_iota(jnp.int32, (rows, cols), 0)
    width = jnp.left_shift(2, g)
    return 1.0 / jnp.minimum(tpos + 1, width).astype(F32)


def _pool_fwd(proj, pool_w, pool_scale, *, d, tm, name):
    t = proj.shape[0]
    pw = d // 2
    gd = pw // N_POOL_GROUPS
    za_col = pw // gd

    def body(xa_ref, xp_ref, za_ref, w_ref, ps_ref, ya_ref):
        g = pl.program_id(0)
        i = pl.program_id(1)
        cur = xa_ref[...].astype(F32)
        mixed = _pool_mixed(_with_prev(xp_ref, cur, i == 0), cur, g, i * tm, tm)
        ypre = jnp.dot(mixed.astype(BF16), w_ref[...], preferred_element_type=F32)
        za = za_ref[...].astype(F32)
        ya_ref[...] = (ypre * ps_ref[...] * (za * _sigmoid(za))).astype(BF16)

    return pl.pallas_call(
        body, grid=(N_POOL_GROUPS, t // tm),
        in_specs=[_cur(tm, gd, 0), _prev(tm, gd, 0), _cur(tm, gd, za_col),
                  pl.BlockSpec((None, gd, gd), lambda g, i: (g, 0, 0)),
                  pl.BlockSpec((1, gd), lambda g, i: (0, g))],
        out_specs=pl.BlockSpec((tm, gd), lambda g, i: (i, g)),
        out_shape=jax.ShapeDtypeStruct((t, pw), BF16), name=name,
        compiler_params=_cparams(("arbitrary", "arbitrary"), 40),
    )(proj, proj, proj, pool_w, pool_scale)


def _pool_bwd(dya, proj, pool_w, pool_scale, *, d, tm, name):
    t = proj.shape[0]
    pw = d // 2
    gd = pw // N_POOL_GROUPS
    za_col = pw // gd
    n_i = t // tm

    def body(xa_ref, xp_ref, za_ref, zn_ref, dy_ref, dyn_ref, w_ref, ps_ref,
             dxa_ref, dza_ref, dw_ref, dps_ref):
        g = pl.program_id(0)
        i = pl.program_id(1)
        last = i == n_i - 1
        cur = xa_ref[...].astype(F32)
        mixed = _pool_mixed(_with_prev(xp_ref, cur, i == 0), cur, g, i * tm, tm)
        mixed16 = mixed.astype(BF16)
        ypre = jnp.dot(mixed16, w_ref[...], preferred_element_type=F32)
        za = za_ref[...].astype(F32)
        sg = _sigmoid(za)
        dy = dy_ref[...].astype(F32)
        dza_ref[...] = (dy * (ypre * ps_ref[...]) * (sg * (1.0 + za * (1.0 - sg)))).astype(BF16)
        dysc = dy * (za * sg)
        za_e = _with_next(za, zn_ref, last)
        dy_e = _with_next(dy, dyn_ref, last)
        dypre_e = (dy_e * (za_e * _sigmoid(za_e)) * ps_ref[...]).astype(BF16)
        dm_e = lax.dot_general(dypre_e, w_ref[...], (((1,), (1,)), ((), ())), preferred_element_type=F32)
        rows = tm + HALO
        q = dm_e * _pool_inv_count(g, i * tm, rows, gd)
        f2 = q + pltpu.roll(q, rows - 1, 0)
        f4 = f2 + pltpu.roll(f2, rows - 2, 0)
        f8 = f4 + pltpu.roll(f4, rows - 4, 0)
        f16 = f8 + pltpu.roll(f8, rows - 8, 0)
        ahead = jnp.where(g == 0, f2, jnp.where(g == 1, f4, jnp.where(g == 2, f8, f16)))
        dxa_ref[...] = (ahead[:tm] - dm_e[:tm]).astype(BF16)
        dw = lax.dot_general(mixed16, dypre_e[:tm], (((0,), (0,)), ((), ())), preferred_element_type=F32)
        dps = jnp.sum(dysc * ypre, axis=0, keepdims=True)

        @pl.when(i == 0)
        def _():
            dw_ref[...] = dw
            dps_ref[...] = dps

        @pl.when(i > 0)
        def _():
            dw_ref[...] += dw
            dps_ref[...] += dps

    out_tile = pl.BlockSpec((tm, gd), lambda g, i: (i, g))
    return pl.pallas_call(
        body, grid=(N_POOL_GROUPS, n_i),
        in_specs=[_cur(tm, gd, 0), _prev(tm, gd, 0), _cur(tm, gd, za_col), _next(tm, gd, za_col, t),
                  _cur(tm, gd, 0), _next(tm, gd, 0, t),
                  pl.BlockSpec((None, gd, gd), lambda g, i: (g, 0, 0)),
                  pl.BlockSpec((1, gd), lambda g, i: (0, g))],
        out_specs=[out_tile, out_tile, pl.BlockSpec((None, gd, gd), lambda g, i: (g, 0, 0)),
                   pl.BlockSpec((1, gd), lambda g, i: (0, g))],
        out_shape=[jax.ShapeDtypeStruct((t, pw), BF16), jax.ShapeDtypeStruct((t, pw), BF16),
                   jax.ShapeDtypeStruct((N_POOL_GROUPS, gd, gd), F32), jax.ShapeDtypeStruct((1, pw), F32)],
        name=name, compiler_params=_cparams(("arbitrary", "arbitrary"), 48),
    )(proj, proj, proj, proj, dya, dya, pool_w, pool_scale)


def _conv_cols(d, cw):
    cc = d // 2
    base = (d + 4 * d) // cw
    return base, base + cc // cw, base + 2 * (cc // cw), base + 3 * (cc // cw)


def _conv_fwd(proj, conv_w, *, d, tm, cw, name):
    t = proj.shape[0]
    cc = d // 2
    u_col, b_col, c_col, z_col = _conv_cols(d, cw)

    def body(u_ref, up_ref, c_ref, cp_ref, b_ref, z_ref, w_ref, y_ref):
        i = pl.program_id(1)
        v = u_ref[...].astype(F32) * c_ref[...].astype(F32)
        vprev = up_ref[...].astype(F32) * cp_ref[...].astype(F32)
        ve = jnp.concatenate([jnp.where(i == 0, 0.0, vprev), v], axis=0)
        w = w_ref[...]
        y = w[0:1] * pltpu.roll(ve, 2, 0)[HALO:] + w[1:2] * pltpu.roll(ve, 1, 0)[HALO:] + w[2:3] * v
        z = z_ref[...].astype(F32)
        y_ref[...] = (b_ref[...].astype(F32) * y * (z * _sigmoid(z))).astype(BF16)

    return pl.pallas_call(
        body, grid=(cc // cw, t // tm),
        in_specs=[_cur(tm, cw, u_col), _prev(tm, cw, u_col), _cur(tm, cw, c_col), _prev(tm, cw, c_col),
                  _cur(tm, cw, b_col), _cur(tm, cw, z_col),
                  pl.BlockSpec((8, cw), lambda cb, i: (0, cb))],
        out_specs=pl.BlockSpec((tm, cw), lambda cb, i: (i, cb)),
        out_shape=jax.ShapeDtypeStruct((t, cc), BF16), name=name,
        compiler_params=_cparams(("arbitrary", "arbitrary"), 40),
    )(proj, proj, proj, proj, proj, proj, conv_w)


def _conv_bwd(dyc, proj, conv_w, *, d, tm, cw, name):
    t = proj.shape[0]
    cc = d // 2
    n_i = t // tm
    u_col, b_col, c_col, z_col = _conv_cols(d, cw)

    def body(u_ref, up_ref, c_ref, cp_ref, b_ref, bn_ref, z_ref, zn_ref, dy_ref, dyn_ref, w_ref,
             du_ref, db_ref, dc_ref, dz_ref, dw_ref):
        i = pl.program_id(1)
        last = i == n_i - 1
        u = u_ref[...].astype(F32)
        cg = c_ref[...].astype(F32)
        v = u * cg
        vprev = up_ref[...].astype(F32) * cp_ref[...].astype(F32)
        ve = jnp.concatenate([jnp.where(i == 0, 0.0, vprev), v], axis=0)
        v2 = pltpu.roll(ve, 2, 0)[HALO:]
        v1 = pltpu.roll(ve, 1, 0)[HALO:]
        w = w_ref[...]
        y = w[0:1] * v2 + w[1:2] * v1 + w[2:3] * v
        z = z_ref[...].astype(F32)
        sg = _sigmoid(z)
        bg = b_ref[...].astype(F32)
        dyc_v = dy_ref[...].astype(F32)
        dz_ref[...] = (dyc_v * bg * y * (sg * (1.0 + z * (1.0 - sg)))).astype(BF16)
        db_ref[...] = (dyc_v * y * (z * sg)).astype(BF16)
        ze = _with_next(z, zn_ref, last)
        dy_e = (_with_next(dyc_v, dyn_ref, last) * _with_next(bg, bn_ref, last) * (ze * _sigmoid(ze)))
        rows = tm + HALO
        dy0 = dy_e[:tm]
        dv = (w[2:3] * dy0 + w[1:2] * pltpu.roll(dy_e, rows - 1, 0)[:tm]
              + w[0:1] * pltpu.roll(dy_e, rows - 2, 0)[:tm])
        du_ref[...] = (dv * cg).astype(BF16)
        dc_ref[...] = (dv * u).astype(BF16)
        s0 = jnp.sum(dy0 * v2, axis=0, keepdims=True)
        s1 = jnp.sum(dy0 * v1, axis=0, keepdims=True)
        s2 = jnp.sum(dy0 * v, axis=0, keepdims=True)
        r = lax.broadcasted_iota(jnp.int32, (8, cw), 0)
        dw = jnp.where(r == 0, s0, jnp.where(r == 1, s1, jnp.where(r == 2, s2, 0.0)))

        @pl.when(i == 0)
        def _():
            dw_ref[...] = dw

        @pl.when(i > 0)
        def _():
            dw_ref[...] += dw

    out_tile = pl.BlockSpec((tm, cw), lambda cb, i: (i, cb))
    act = jax.ShapeDtypeStruct((t, cc), BF16)
    return pl.pallas_call(
        body, grid=(cc // cw, n_i),
        in_specs=[_cur(tm, cw, u_col), _prev(tm, cw, u_col), _cur(tm, cw, c_col), _prev(tm, cw, c_col),
                  _cur(tm, cw, b_col), _next(tm, cw, b_col, t), _cur(tm, cw, z_col), _next(tm, cw, z_col, t),
                  _cur(tm, cw, 0), _next(tm, cw, 0, t),
                  pl.BlockSpec((8, cw), lambda cb, i: (0, cb))],
        out_specs=[out_tile, out_tile, out_tile, out_tile, pl.BlockSpec((8, cw), lambda cb, i: (0, cb))],
        out_shape=[act, act, act, act, jax.ShapeDtypeStruct((8, cc), F32)], name=name,
        compiler_params=_cparams(("arbitrary", "arbitrary"), 48),
    )(proj, proj, proj, proj, proj, proj, proj, proj, dyc, dyc, conv_w)


def _sum_matrix(prefix):
    r = lax.broadcasted_iota(jnp.int32, (KEY_CHUNK, 2 * KEY_CHUNK), 0)
    c = lax.broadcasted_iota(jnp.int32, (KEY_CHUNK, 2 * KEY_CHUNK), 1)
    inside = (r <= c) if prefix else (r > c)
    return jnp.where((c >= KEY_CHUNK) | inside, 1.0, 0.0).astype(BF16)


def _chunk_sums(val, mat, two_pass):
    hi = val.astype(BF16)
    both = jnp.dot(hi, mat, preferred_element_type=F32)
    if two_pass:
        lo = (val - hi.astype(F32)).astype(BF16)
        both = both + jnp.dot(lo, mat, preferred_element_type=F32)
    return both[:, :KEY_CHUNK], both[:, KEY_CHUNK:]


def _stick_logits(qs, kb, strict):
    z = lax.dot_general(qs, kb, (((1,), (1,)), ((), ())), preferred_element_type=F32)
    sp = jnp.maximum(z, 0.0) + jnp.log(1.0 + jnp.exp(-jnp.abs(z)))
    log_beta = z - sp
    if strict is not None:
        sp = jnp.where(strict, sp, 0.0)
    return sp, log_beta


def _running(val, mat, carry, upwards):
    n = val.shape[1] // KEY_CHUNK
    out = [None] * n
    for c in (range(n) if upwards else reversed(range(n))):
        part, total = _chunk_sums(val[:, c * KEY_CHUNK:(c + 1) * KEY_CHUNK], mat, False)
        out[c] = part + carry
        carry = carry + total
    return jnp.concatenate(out, axis=1), carry


def _attn_cols(d):
    h = d // HEAD_DIM
    return h, 2 * h, 3 * h, 4 * h


def _sweep(i, per, block, carry, upwards):
    if upwards:
        carry = lax.fori_loop(0, i * per, lambda it, cr: block(it, cr, False), carry)
        for dgl in range(per):
            carry = block(i * per + dgl, carry, True)
        return carry
    for dgl in range(per - 1, -1, -1):
        carry = block(i * per + dgl, carry, True)
    return lax.fori_loop(0, i * per, lambda it, cr: block(i * per - 1 - it, cr, False), carry)


def _strict_mask(i, tq, kb, width):
    t_idx = i * tq + lax.broadcasted_iota(jnp.int32, (tq, width), 0)
    s_idx = kb * width + lax.broadcasted_iota(jnp.int32, (tq, width), 1)
    return s_idx < t_idx


def _attn_fwd(proj, *, d, tq, name):
    t = proj.shape[0]
    heads = d // HEAD_DIM
    q_col, k_col, v_col, z_col = _attn_cols(d)

    def body(q_ref, k_ref, v_ref, z_ref, o_ref, y_ref, tot_ref):
        i = pl.program_id(1)
        qs = (q_ref[...].astype(F32) * (HEAD_DIM ** -0.5)).astype(BF16)
        mat = _sum_matrix(prefix=False)

        def block(kb, carry, masked):
            acc, after = carry
            off = pl.multiple_of(kb * KEY_BLOCK, KEY_BLOCK)
            strict = _strict_mask(i, tq, kb, KEY_BLOCK) if masked else None
            sp, log_beta = _stick_logits(qs, k_ref[pl.ds(off, KEY_BLOCK), :], strict)
            between, after = _running(sp, mat, after, upwards=False)
            a = jnp.exp(log_beta - between)
            if masked:
                a = jnp.where(strict, a, 0.0)
            acc = acc + jnp.dot(a.astype(BF16), v_ref[pl.ds(off, KEY_BLOCK), :], preferred_element_type=F32)
            return acc, after

        zero = jnp.zeros((tq, HEAD_DIM), F32)
        o, total = _sweep(i, tq // KEY_BLOCK, block, (zero, zero), upwards=False)
        o_ref[...] = o.astype(BF16)
        tot_ref[...] = total
        z = z_ref[...].astype(F32)
        y_ref[...] = (o * (z * _sigmoid(z))).astype(BF16)

    tile = lambda col: pl.BlockSpec((tq, HEAD_DIM), lambda h, i: (i, col + h))
    full = lambda col: pl.BlockSpec((t, HEAD_DIM), lambda h, i: (0, col + h))
    act = jax.ShapeDtypeStruct((t, d), BF16)
    return pl.pallas_call(
        body, grid=(heads, t // tq),
        in_specs=[tile(q_col), full(k_col), full(v_col), tile(z_col)],
        out_specs=[tile(0), tile(0), tile(0)], out_shape=[act, act, jax.ShapeDtypeStruct((t, d), F32)],
        name=name, compiler_params=_cparams(("arbitrary", "arbitrary"), 40),
    )(proj, proj, proj, proj)


def _attn_bwd(dyb, o, tot, proj, *, d, tq, name):
    t = proj.shape[0]
    heads = d // HEAD_DIM
    n_i = t // tq
    q_col, k_col, v_col, z_col = _attn_cols(d)
    scale = HEAD_DIM ** -0.5

    def body(q_ref, k_ref, v_ref, z_ref, o_ref, tot_ref, dy_ref, dq_ref, dk_ref, dv_ref, dz_ref, dk_acc, dv_acc):
        i = pl.program_id(1)

        @pl.when(i == 0)
        def _():
            dk_acc[...] = jnp.zeros_like(dk_acc)
            dv_acc[...] = jnp.zeros_like(dv_acc)

        qs = (q_ref[...].astype(F32) * scale).astype(BF16)
        z = z_ref[...].astype(F32)
        sg = _sigmoid(z)
        dy = dy_ref[...].astype(F32)
        of = o_ref[...].astype(F32)
        dz_ref[...] = (dy * of * (sg * (1.0 + z * (1.0 - sg)))).astype(BF16)
        do16 = (dy * (z * sg)).astype(BF16)
        all_keep = tot_ref[...]
        all_keep = jnp.concatenate([all_keep] * (KEY_BLOCK // HEAD_DIM), axis=1)
        mat = _sum_matrix(prefix=True)

        def block(kb, carry, masked):
            dq, sp_seen, g_seen = carry
            off = pl.multiple_of(kb * KEY_BLOCK, KEY_BLOCK)
            kblk = k_ref[pl.ds(off, KEY_BLOCK), :]
            vblk = v_ref[pl.ds(off, KEY_BLOCK), :]
            strict = _strict_mask(i, tq, kb, KEY_BLOCK) if masked else None
            sp, log_beta = _stick_logits(qs, kblk, strict)
            sp_upto, sp_seen = _running(sp, mat, sp_seen, upwards=True)
            a = jnp.exp(log_beta - (all_keep - sp_upto))
            if masked:
                a = jnp.where(strict, a, 0.0)
            g = a * lax.dot_general(do16, vblk, (((1,), (1,)), ((), ())), preferred_element_type=F32)
            g_upto, g_seen = _running(g, mat, g_seen, upwards=True)
            dz_ts = g - jnp.exp(log_beta) * g_upto
            if masked:
                dz_ts = jnp.where(strict, dz_ts, 0.0)
            dz16 = dz_ts.astype(BF16)
            dq = dq + jnp.dot(dz16, kblk, preferred_element_type=F32)
            dk_acc[pl.ds(off, KEY_BLOCK), :] += lax.dot_general(
                dz16, qs, (((0,), (0,)), ((), ())), preferred_element_type=F32)
            dv_acc[pl.ds(off, KEY_BLOCK), :] += lax.dot_general(
                a.astype(BF16), do16, (((0,), (0,)), ((), ())), preferred_element_type=F32)
            return dq, sp_seen, g_seen

        zero = jnp.zeros((tq, HEAD_DIM), F32)
        dq, _, _ = _sweep(i, tq // KEY_BLOCK, block, (zero, zero, zero), upwards=True)
        dq_ref[...] = (dq * scale).astype(BF16)

        @pl.when(i == n_i - 1)
        def _():
            dk_ref[...] = dk_acc[...].astype(BF16)
            dv_ref[...] = dv_acc[...].astype(BF16)

    tile = lambda col: pl.BlockSpec((tq, HEAD_DIM), lambda h, i: (i, col + h))
    full = lambda col: pl.BlockSpec((t, HEAD_DIM), lambda h, i: (0, col + h))
    act = jax.ShapeDtypeStruct((t, d), BF16)
    return pl.pallas_call(
        body, grid=(heads, n_i),
        in_specs=[tile(q_col), full(k_col), full(v_col), tile(z_col), tile(0), tile(0), tile(0)],
        out_specs=[tile(0), full(0), full(0), tile(0)], out_shape=[act, act, act, act],
        scratch_shapes=[pltpu.VMEM((t, HEAD_DIM), F32), pltpu.VMEM((t, HEAD_DIM), F32)], name=name,
        compiler_params=_cparams(("arbitrary", "arbitrary"), 48),
    )(proj, proj, proj, proj, o, tot, dyb)


def _merge_fwd(gl, bst, *, d, tm, tn, name):
    t = gl.shape[0]
    nk = d // tn

    def body(g0, g1, g2, b0, b1, b2, m_ref):
        acc = _sigmoid(g0[...].astype(F32)) * b0[...].astype(F32)
        acc += _sigmoid(g1[...].astype(F32)) * b1[...].astype(F32)
        acc += _sigmoid(g2[...].astype(F32)) * b2[...].astype(F32)
        m_ref[...] = acc.astype(BF16)

    spec = lambda b: pl.BlockSpec((tm, tn), lambda i, kb: (i, b * nk + kb))
    return pl.pallas_call(
        body, grid=(t // tm, nk), in_specs=[spec(0), spec(1), spec(2)] * 2,
        out_specs=pl.BlockSpec((tm, tn), lambda i, kb: (i, kb)),
        out_shape=jax.ShapeDtypeStruct((t, d), BF16), name=name,
        compiler_params=_cparams(("arbitrary", "arbitrary"), 40),
    )(gl, gl, gl, bst, bst, bst)


def _merge_bwd(dmerged, gl, bst, *, d, tm, tn, name):
    t = gl.shape[0]
    nk = d // tn

    def body(dm_ref, g_ref, b_ref, db_ref, dg_ref, dbias_ref):
        i = pl.program_id(2)
        dm = dm_ref[...].astype(F32)
        sg = _sigmoid(g_ref[...].astype(F32))
        db_ref[...] = (dm * sg).astype(BF16)
        dgl = dm * b_ref[...].astype(F32) * (sg * (1.0 - sg))
        dg_ref[...] = dgl.astype(BF16)
        dbias = jnp.sum(dgl, axis=0, keepdims=True)

        @pl.when(i == 0)
        def _():
            dbias_ref[...] = dbias

        @pl.when(i > 0)
        def _():
            dbias_ref[...] += dbias

    wide = pl.BlockSpec((tm, tn), lambda b, kb, i: (i, b * nk + kb))
    act = jax.ShapeDtypeStruct((t, 3 * d), BF16)
    return pl.pallas_call(
        body, grid=(3, nk, t // tm),
        in_specs=[pl.BlockSpec((tm, tn), lambda b, kb, i: (i, kb)), wide, wide],
        out_specs=[wide, wide, pl.BlockSpec((1, tn), lambda b, kb, i: (0, b * nk + kb))],
        out_shape=[act, act, jax.ShapeDtypeStruct((1, 3 * d), F32)], name=name,
        compiler_params=_cparams(("arbitrary", "arbitrary", "arbitrary"), 40),
    )(dmerged, gl, bst)


def _adamw(w, m, v, parts, layer, prev, *, tr, name):
    depth, r, c = w.shape
    n = parts.shape[0]

    def body(*refs):
        w_ref, m_ref, v_ref, p_ref = refs[:4]
        g_ref, d_ref, nm_ref, nv_ref = refs[-4:]
        g = p_ref[0].astype(F32)
        for s in range(1, n):
            g = g + p_ref[s].astype(F32)
        wv = w_ref[...]
        nm = ADAM_B1 * m_ref[...] + (1.0 - ADAM_B1) * g
        nv = ADAM_B2 * v_ref[...] + (1.0 - ADAM_B2) * (g * g)
        m_hat = nm / (1.0 - ADAM_B1 ** ADAM_STEP)
        v_hat = nv / (1.0 - ADAM_B2 ** ADAM_STEP)
        g_ref[...] = g
        nm_ref[...] = nm
        nv_ref[...] = nv
        d_ref[...] = -ADAM_LR * (m_hat / (jnp.sqrt(v_hat) + ADAM_EPS) + ADAM_WD * wv)

    tile = pl.BlockSpec((None, tr, c), lambda i: (layer, i, 0))
    in_specs = [tile, tile, tile, pl.BlockSpec((n, tr, c), lambda i: (0, i, 0))]
    args = [w, m, v, parts]
    aliases = {}
    if prev is not None:
        in_specs += [pl.BlockSpec(memory_space=pl.ANY)] * 4
        aliases = {4 + k: k for k in range(4)}
        args += list(prev)
    full = jax.ShapeDtypeStruct((depth, r, c), F32)
    return pl.pallas_call(
        body, grid=(r // tr,), in_specs=in_specs, out_specs=[tile] * 4, out_shape=[full] * 4,
        input_output_aliases=aliases, name=name, compiler_params=_cparams(("arbitrary",), 48),
    )(*args)


def _adamw_layers(w, m, v, parts_by_layer, *, tr, name):
    lead = w.shape[0]
    flat = lambda a: a.reshape(lead, -1, a.shape[-1])
    w, m, v = flat(w), flat(m), flat(v)
    out = None
    for layer in reversed(range(lead)):
        parts = parts_by_layer[layer]
        parts = parts.reshape(parts.shape[0], -1, parts.shape[-1])
        out = _adamw(w, m, v, parts, layer, out, tr=min(tr, w.shape[1]), name=f"{name}_l{layer}")
    return out


def _place():
    return lax.axis_index("x"), lax.axis_index("y"), lax.axis_index("c")


def _index(px, py, pc):
    return 4 * px + 2 * py + pc


def _whole(ref, idx):
    return ref.at[idx]


def _row_block(rows):
    return lambda ref, idx: ref.at[:, pl.ds(idx * rows, rows), :]


def _lane_block(cols):
    return lambda ref, idx: ref.at[:, pl.ds(idx * cols, cols)]


def _all_gather(shards, out_shapes, placers, *, name):
    n = len(shards)

    def body(*refs):
        src, out = refs[:n], refs[n:2 * n]
        send_sems, recv_sems, local_sems = refs[2 * n:]
        x, y, c = _place()
        me, sibling = (x, y, c), (x, y, 1 - c)
        chips = [(1 - x, y), (x, 1 - y), (1 - x, 1 - y)]

        def copy(a, k, block, to, from_shard=False):
            window = placers[a](out[a], _index(*block))
            return pltpu.make_async_remote_copy(
                src_ref=src[a] if from_shard else window, dst_ref=window,
                send_sem=send_sems.at[a * 7 + k], recv_sem=recv_sems.at[a * 7 + k],
                device_id=to, device_id_type=MESH)

        mine = [pltpu.make_async_copy(src[a], placers[a](out[a], _index(*me)), local_sems.at[a])
                for a in range(n)]
        started = []
        for a in range(n):
            mine[a].start()
            started.append(copy(a, 0, me, sibling, True))
            started += [copy(a, 1 + j, me, (*chip, c), True) for j, chip in enumerate(chips)]
        for cp in started:
            cp.start()
        for j, chip in enumerate(chips):
            for a in range(n):
                copy(a, 1 + j, (*chip, c), me).wait_recv()
                passed = copy(a, 4 + j, (*chip, c), sibling)
                passed.start()
                started.append(passed)
        for a in range(n):
            copy(a, 0, sibling, me).wait_recv()
            for j, chip in enumerate(chips):
                copy(a, 4 + j, (*chip, 1 - c), me).wait_recv()
        for cp in started:
            cp.wait_send()
        for cp in mine:
            cp.wait()

    hbm = pl.BlockSpec(memory_space=pl.ANY)
    return pl.pallas_call(
        body, in_specs=[hbm] * n, out_specs=[hbm] * n, out_shape=out_shapes,
        scratch_shapes=[pltpu.SemaphoreType.DMA((7 * n,)), pltpu.SemaphoreType.DMA((7 * n,)),
                        pltpu.SemaphoreType.DMA((n,))],
        name=name,
    )(*shards)


def _scatter_parts(grads, pickers, part_shapes, *, name):
    n = len(grads)

    def body(*refs):
        src, out = refs[:n], refs[n:2 * n]
        send_sems, recv_sems, local_sems = refs[2 * n:]
        x, y, c = _place()
        my_idx = _index(x, y, c)
        peers = [(x ^ (k >> 2), y ^ ((k >> 1) & 1), c ^ (k & 1)) for k in range(1, N_DEV)]

        def copy(a, k, peer):
            return pltpu.make_async_remote_copy(
                src_ref=pickers[a](src[a], _index(*peer)), dst_ref=out[a].at[my_idx],
                send_sem=send_sems.at[a * 7 + k], recv_sem=recv_sems.at[a * 7 + k],
                device_id=peer, device_id_type=MESH)

        def arrival(a, k, peer):
            return pltpu.make_async_remote_copy(
                src_ref=pickers[a](src[a], my_idx), dst_ref=out[a].at[_index(*peer)],
                send_sem=send_sems.at[a * 7 + k], recv_sem=recv_sems.at[a * 7 + k],
                device_id=peer, device_id_type=MESH)

        mine = [pltpu.make_async_copy(pickers[a](src[a], my_idx), out[a].at[my_idx], local_sems.at[a])
                for a in range(n)]
        sent = [copy(a, k, peer) for a in range(n) for k, peer in enumerate(peers)]
        for cp in mine + sent:
            cp.start()
        for a in range(n):
            for k, peer in enumerate(peers):
                arrival(a, k, peer).wait_recv()
        for cp in sent:
            cp.wait_send()
        for cp in mine:
            cp.wait()

    hbm = pl.BlockSpec(memory_space=pl.ANY)
    return pl.pallas_call(
        body, in_specs=[hbm] * n, out_specs=[hbm] * n, out_shape=part_shapes,
        scratch_shapes=[pltpu.SemaphoreType.DMA((7 * n,)), pltpu.SemaphoreType.DMA((7 * n,)),
                        pltpu.SemaphoreType.DMA((n,))],
        name=name,
    )(*grads)


def _peers(x, y, c):
    return [(x ^ (k >> 2), y ^ ((k >> 1) & 1), c ^ (k & 1)) for k in range(1, N_DEV)]


_HBM = pl.BlockSpec(memory_space=pltpu.HBM)
_SEM = pl.BlockSpec(memory_space=pltpu.SEMAPHORE)
_EFFECT = pltpu.SideEffectType.DATAFLOW_SIDE_EFFECTING


def _exchange_start(srcs, land_shapes, src_win, dst_win, after, *, name):
    n = len(srcs)

    def body(*refs):
        src, land = refs[:n], refs[n:2 * n]
        send_sems, recv_sems, token = refs[2 * n + 1], refs[2 * n + 2], refs[-1]
        x, y, c = _place()
        my_idx = _index(x, y, c)
        for a in range(n):
            for k, peer in enumerate(_peers(x, y, c)):
                pltpu.make_async_remote_copy(
                    src_ref=src_win[a](src[a], _index(*peer)), dst_ref=dst_win[a](land[a], my_idx),
                    send_sem=send_sems.at[a * 7 + k], recv_sem=recv_sems.at[a * 7 + k],
                    device_id=peer, device_id_type=MESH).start()
        token[...] = jnp.zeros_like(token)

    hbm = lambda a: pltpu.with_memory_space_constraint(a, pltpu.HBM)
    lands = [hbm(lax.empty(s.shape, s.dtype)) for s in land_shapes]
    args = [hbm(s) for s in srcs] + lands
    kept = [pltpu.HBM(a.shape, a.dtype) for a in args]
    out = pl.pallas_call(
        body, name=name,
        out_shape=(pltpu.SemaphoreType.DMA((7 * n,)), pltpu.SemaphoreType.DMA((7 * n,)), *kept,
                   jax.ShapeDtypeStruct((8, LANE), F32)),
        in_specs=[_HBM] * (2 * n) + [pl.BlockSpec(memory_space=pl.ANY)],
        out_specs=(_SEM, _SEM, *([_HBM] * (2 * n)), pl.BlockSpec(memory_space=pltpu.VMEM)),
        input_output_aliases={i: 2 + i for i in range(2 * n)},
        compiler_params=pltpu.CompilerParams(has_side_effects=_EFFECT),
    )(*args, after)
    return out[0], out[1], list(out[2:2 + n]), list(out[2 + n:2 + 2 * n]), out[-1][0, 0]


def _exchange_wait(send_sems, recv_sems, srcs, lands, src_win, dst_win, after, *, name):
    n = len(srcs)

    def body(*refs):
        src, land = refs[:n], refs[n:2 * n]
        send_sems, recv_sems = refs[2 * n], refs[2 * n + 1]
        x, y, c = _place()
        for a in range(n):
            for k, peer in enumerate(_peers(x, y, c)):
                sender = _index(*peer)
                copy = pltpu.make_async_remote_copy(
                    src_ref=src_win[a](src[a], sender), dst_ref=dst_win[a](land[a], sender),
                    send_sem=send_sems.at[a * 7 + k], recv_sem=recv_sems.at[a * 7 + k],
                    device_id=peer, device_id_type=MESH)
                copy.wait_send()
                copy.wait_recv()

    kept = [pltpu.HBM(a.shape, a.dtype) for a in list(srcs) + list(lands)]
    out = pl.pallas_call(
        body, name=name, out_shape=tuple(kept),
        in_specs=[_HBM] * (2 * n) + [_SEM, _SEM, pl.BlockSpec(memory_space=pl.ANY)],
        out_specs=tuple([_HBM] * (2 * n)),
        input_output_aliases={i: i for i in range(2 * n)},
        compiler_params=pltpu.CompilerParams(has_side_effects=_EFFECT),
    )(*srcs, *lands, send_sems, recv_sems, after)
    return list(out[:n]), list(out[n:])


def _place_own(srcs, lands, src_win, dst_win, *, name):
    n = len(srcs)

    def body(*refs):
        src, land = refs[:n], refs[n:2 * n]
        sems = refs[-1]
        my_idx = _index(*_place())
        copies = [pltpu.make_async_copy(src_win[a](src[a], my_idx), dst_win[a](land[a], my_idx), sems.at[a])
                  for a in range(n)]
        for cp in copies:
            cp.start()
        for cp in copies:
            cp.wait()

    hbm = pl.BlockSpec(memory_space=pl.ANY)
    return list(pl.pallas_call(
        body, name=name, in_specs=[hbm] * (2 * n), out_specs=[hbm] * n,
        out_shape=[jax.ShapeDtypeStruct(a.shape, a.dtype) for a in lands],
        input_output_aliases={n + i: i for i in range(n)},
        scratch_shapes=[pltpu.SemaphoreType.DMA((n,))],
    )(*srcs, *lands))


def _ada_fwd(c8, w_ada, b_ada8, *, name):
    depth, d, n = w_ada.shape

    def body(c_ref, w_ref, b_ref, mod_ref, sc_ref, c_all, prod, got, send_sems, recv_sems):
        x, y, c = _place()
        my_idx = _index(x, y, c)
        peers = [(x ^ (k >> 2), y ^ ((k >> 1) & 1), c ^ (k & 1)) for k in range(1, N_DEV)]

        def slab(ref, idx):
            return ref.at[pl.ds(pl.multiple_of(idx * 8, 8), 8)]

        def c_copy(k, peer, sender):
            return pltpu.make_async_remote_copy(
                src_ref=c_ref, dst_ref=slab(c_all, sender), send_sem=send_sems.at[k],
                recv_sem=recv_sems.at[k], device_id=peer, device_id_type=MESH)

        def m_copy(k, peer, sender):
            return pltpu.make_async_remote_copy(
                src_ref=prod.at[_index(*peer)], dst_ref=got.at[sender], send_sem=send_sems.at[7 + k],
                recv_sem=recv_sems.at[7 + k], device_id=peer, device_id_type=MESH)

        sends = [c_copy(k, peer, my_idx) for k, peer in enumerate(peers)]
        for cp in sends:
            cp.start()
        slab(c_all, my_idx)[...] = c_ref[...]
        for k, peer in enumerate(peers):
            c_copy(k, peer, _index(*peer)).wait_recv()
        cv = c_all[...]
        sc = cv * _sigmoid(cv)
        sc_ref[...] = sc
        for layer in range(depth):
            p = jnp.dot(sc, w_ref[layer], preferred_element_type=F32, precision=lax.Precision.HIGHEST)
            for s in range(N_DEV):
                prod[s, layer] = p[8 * s:8 * s + 8]
        sends2 = [m_copy(k, peer, my_idx) for k, peer in enumerate(peers)]
        for cp in sends2:
            cp.start()
        got[my_idx] = prod[my_idx]
        for k, peer in enumerate(peers):
            m_copy(k, peer, _index(*peer)).wait_recv()
        for layer in range(depth):
            for s in range(N_DEV):
                mod_ref[layer, :, s * n:(s + 1) * n] = got[s, layer] + b_ref[layer, :, s * n:(s + 1) * n]
        for cp in sends + sends2:
            cp.wait_send()

    vmem = pl.BlockSpec(memory_space=pltpu.VMEM)
    return pl.pallas_call(
        body, in_specs=[vmem] * 3, out_specs=[vmem] * 2,
        out_shape=[jax.ShapeDtypeStruct((depth, 8, N_DEV * n), F32),
                   jax.ShapeDtypeStruct((8 * N_DEV, d), F32)],
        scratch_shapes=[pltpu.VMEM((8 * N_DEV, d), F32), pltpu.VMEM((N_DEV, depth, 8, n), F32),
                        pltpu.VMEM((N_DEV, depth, 8, n), F32),
                        pltpu.SemaphoreType.DMA((14,)), pltpu.SemaphoreType.DMA((14,))],
        name=name, compiler_params=pltpu.CompilerParams(vmem_limit_bytes=56 << 20),
    )(c8, w_ada, b_ada8)


def _ada_bwd(sc8, dmod, *, name):
    depth, _, n = dmod.shape
    d = sc8.shape[1]

    def body(sc_ref, dm_ref, o_ref):
        for layer in range(depth):
            o_ref[layer] = lax.dot_general(sc_ref[...], dm_ref[layer], (((0,), (0,)), ((), ())),
                                           preferred_element_type=F32, precision=lax.Precision.HIGHEST)

    vmem = pl.BlockSpec(memory_space=pltpu.VMEM)
    return pl.pallas_call(
        body, in_specs=[vmem] * 2, out_specs=vmem, out_shape=jax.ShapeDtypeStruct((depth, d, n), F32),
        name=name, compiler_params=pltpu.CompilerParams(vmem_limit_bytes=40 << 20),
    )(sc8, dmod)


def _gather_small(vec, *, name):
    r = vec.shape[0]

    def body(v_ref, o_ref, send_sems, recv_sems):
        x, y, c = _place()
        my_idx = _index(x, y, c)
        peers = [(x ^ (k >> 2), y ^ ((k >> 1) & 1), c ^ (k & 1)) for k in range(1, N_DEV)]

        def copy(k, peer, sender):
            return pltpu.make_async_remote_copy(
                src_ref=v_ref, dst_ref=o_ref.at[sender], send_sem=send_sems.at[k],
                recv_sem=recv_sems.at[k], device_id=peer, device_id_type=MESH)

        sends = [copy(k, peer, my_idx) for k, peer in enumerate(peers)]
        for cp in sends:
            cp.start()
        o_ref[my_idx] = v_ref[...]
        for k, peer in enumerate(peers):
            copy(k, peer, _index(*peer)).wait_recv()
        for cp in sends:
            cp.wait_send()

    vmem = pl.BlockSpec(memory_space=pltpu.VMEM)
    return pl.pallas_call(
        body, in_specs=[vmem], out_specs=vmem, out_shape=jax.ShapeDtypeStruct((N_DEV, r, LANE), F32),
        scratch_shapes=[pltpu.SemaphoreType.DMA((7,)), pltpu.SemaphoreType.DMA((7,))], name=name,
    )(vec)


TM = 512
TQ = 512


def _layer_fwd(x, mod, small, gw, *, d, tag):
    shift, scale, res_gate = mod
    h = _prenorm(x, small["norm_g"], scale, shift, tm=TM, name=f"prenorm_{tag}")
    proj = _mm_nn(h, gw["w_in"], out_dtype=BF16, tm=TM, name=f"in_proj_{tag}")
    gl = _mm_nn(h, gw["w_gate"], out_dtype=BF16, tm=TM, bias=small["b_gate"], name=f"gate_proj_{tag}")
    ya = _pool_fwd(proj, gw["pool_w"], small["pool_scale"], d=d, tm=TM, name=f"pool_fwd_{tag}")
    o, yb, tot = _attn_fwd(proj, d=d, tq=TQ, name=f"attn_fwd_{tag}")
    yc = _conv_fwd(proj, gw["conv_w"], d=d, tm=TM, cw=LANE * 2, name=f"conv_fwd_{tag}")
    nblk = gw["w_br_a"].shape[0]
    bst = _mm_nn(ya, gw["w_br_a"], out_dtype=BF16, tm=TM, out_cols=3 * d, name=f"branch_a_{tag}")
    bst = _mm_nn(yb, gw["w_br_b"], out_dtype=BF16, tm=TM, out=bst, out_col_block=1, name=f"branch_b_{tag}")
    bst = _mm_nn(yc, gw["w_br_c"], out_dtype=BF16, tm=TM, out=bst, out_col_block=2 * nblk,
                 name=f"branch_c_{tag}")
    merged = _merge_fwd(gl, bst, d=d, tm=TM, tn=TM, name=f"merge_fwd_{tag}")
    x_new, mo = _out_fwd(merged, gw["w_out"][0], x, res_gate, tm=TM, name=f"out_fwd_{tag}")
    return x_new, dict(x=x, h=h, proj=proj, gl=gl, ya=ya, yb=yb, yc=yc, o=o, tot=tot, bst=bst, merged=merged,
                       mo=mo)


EARLY = ("w_gate", "w_br_a", "w_br_b", "w_br_c", "w_out")
LATE = ("w_in", "pool_w")


def _layer_bwd(dout, sv, mod, small, gw, *, d, tag, send_early=None):
    shift, scale, res_gate = mod
    nblk = gw["w_br_a"].shape[0]
    dmo, d_res_gate = _out_bwd(dout, sv["mo"], res_gate, tm=TM, name=f"out_bwd_{tag}")
    g_w_out = _mm_tn(sv["merged"], dmo, nb=1, n=d, tk=TM, tmm=TM, name=f"dw_out_{tag}")
    dmerged = _mm_nt(dmo, gw["w_out"], out_dtype=BF16, tm=TM, name=f"dmerged_{tag}")
    dbst, dgl, d_b_gate = _merge_bwd(dmerged, sv["gl"], sv["bst"], d=d, tm=TM, tn=TM, name=f"merge_bwd_{tag}")
    g_w_gate = _mm_tn(sv["h"], dgl, nb=nblk, n=3 * d // nblk, tk=d // 2, tmm=TM, name=f"dw_gate_{tag}")
    dh = _mm_nt(dgl, gw["w_gate"], out_dtype=F32, tm=TM, name=f"dh_gate_{tag}")
    g_w_br_a = _mm_tn(sv["ya"], dbst, nb=nblk, n=d // nblk, tk=d // 2, tmm=TM, name=f"dw_br_a_{tag}")
    g_w_br_b = _mm_tn(sv["yb"], dbst, nb=1, n=d, tk=TM, tmm=TM, col_block=1, name=f"dw_br_b_{tag}")
    g_w_br_c = _mm_tn(sv["yc"], dbst, nb=nblk, n=d // nblk, tk=d // 2, tmm=TM, col_block=2 * nblk,
                      name=f"dw_br_c_{tag}")
    dya = _mm_nt(dbst, gw["w_br_a"], out_dtype=BF16, tm=TM, name=f"dya_{tag}")
    dyb = _mm_nt(dbst, gw["w_br_b"], out_dtype=BF16, tm=TM, col_block=1, name=f"dyb_{tag}")
    dyc = _mm_nt(dbst, gw["w_br_c"], out_dtype=BF16, tm=TM, col_block=2 * nblk, name=f"dyc_{tag}")
    big = dict(w_gate=g_w_gate, w_br_a=g_w_br_a, w_br_b=g_w_br_b, w_br_c=g_w_br_c, w_out=g_w_out)
    if send_early is not None:
        token = send_early(big)
        small = dict(small, pool_scale=small["pool_scale"] + token, norm_g=small["norm_g"] + token)
        dyb = dyb + token.astype(BF16)
    dxa, dza, g_pool_w, d_pool_scale = _pool_bwd(dya, sv["proj"], gw["pool_w"], small["pool_scale"], d=d, tm=TM,
                                                 name=f"pool_bwd_{tag}")
    dq, dk, dv, dzb = _attn_bwd(dyb, sv["o"], sv["tot"], sv["proj"], d=d, tq=TQ, name=f"attn_bwd_{tag}")
    du, dbg, dcg, dzc, d_conv_w = _conv_bwd(dyc, sv["proj"], gw["conv_w"], d=d, tm=TM, cw=LANE * 2,
                                            name=f"conv_bwd_{tag}")
    dproj = jnp.concatenate([dxa, dza, dq, dk, dv, dzb, du, dbg, dcg, dzc], axis=1)
    g_w_in = _mm_tn(sv["h"], dproj, nb=nblk, n=7 * d // nblk, tk=d // 2, tmm=TM, name=f"dw_in_{tag}")
    dh = _mm_nt(dproj, gw["w_in"], out_dtype=F32, tm=TM, addend=dh, name=f"dh_in_{tag}")
    dx, d_norm_g, d_scale, d_shift = _prenorm_bwd(dh, sv["x"], dout, small["norm_g"], scale, tm=TM,
                                                  name=f"prenorm_bwd_{tag}")
    big.update(w_in=g_w_in, pool_w=g_pool_w.astype(BF16))
    part = dict(norm_g=d_norm_g, mod=jnp.concatenate([d_shift, d_scale, d_res_gate], axis=1),
                pool_scale=d_pool_scale, b_gate=d_b_gate, conv_w=d_conv_w[:3])
    return dx, big, part


BIG = ("w_in", "w_gate", "w_br_a", "w_br_b", "w_br_c", "w_out", "pool_w")
SMALL_PER_LAYER = ("norm_g", "mod", "pool_scale", "b_gate", "conv_w")


def _gather_layer(w, layer, *, d):
    nd = N_DEV
    gd = d // 2 // N_POOL_GROUPS
    cc = d // 2
    shards = [w[k][layer].astype(BF16) for k in BIG]
    shards.append(jnp.pad(w["conv_w"][layer], ((0, 5), (0, 0))))
    blocks = [jax.ShapeDtypeStruct((nd,) + s.shape, BF16) for s in shards[:6]]
    blocks.append(jax.ShapeDtypeStruct((N_POOL_GROUPS, gd, gd), BF16))
    blocks.append(jax.ShapeDtypeStruct((8, cc), F32))
    placers = [_whole] * 6 + [_row_block(gd // nd), _lane_block(cc // nd)]
    return shards, blocks, placers


def _as_weights(got, d):
    gw = dict(zip(BIG + ("conv_w",), got))
    gw["w_br_b"] = gw["w_br_b"].reshape(1, d, d)
    gw["w_out"] = gw["w_out"].reshape(1, d, d)
    return gw


def _shard_itself(ref, idx):
    return ref


def _scatter_spec(big, keys, *, d):
    nd = N_DEV
    gd = d // 2 // N_POOL_GROUPS
    grads, pickers, shapes = [], [], []
    for k in keys:
        g = big[k]
        if k == "pool_w":
            pickers.append(_row_block(gd // nd))
            shapes.append(jax.ShapeDtypeStruct((nd, N_POOL_GROUPS, gd // nd, gd), BF16))
        else:
            if g.shape[0] == 1:
                g = g.reshape(nd, g.shape[1] // nd, g.shape[2])
            pickers.append(_whole)
            shapes.append(jax.ShapeDtypeStruct(g.shape, BF16))
        grads.append(g)
    return grads, pickers, shapes


class _Behind:
    def __init__(self, srcs, land_shapes, src_win, dst_win, after, name):
        self.src_win, self.dst_win, self.name = src_win, dst_win, name
        self.send, self.recv, self.srcs, self.lands, self.token = _exchange_start(
            srcs, land_shapes, src_win, dst_win, after, name=f"{name}_start")

    def finish(self, after):
        srcs, lands = _exchange_wait(self.send, self.recv, self.srcs, self.lands, self.src_win, self.dst_win,
                                     after, name=f"{self.name}_wait")
        return _place_own(srcs, lands, self.src_win, self.dst_win, name=f"{self.name}_own")


def _pack(pieces):
    return jnp.concatenate([p.reshape(-1, LANE) for p in pieces], axis=0)


def kernel(x, c, norm_g, w_ada, b_ada, w_in, pool_w, pool_scale, conv_w, w_br_a, w_br_b, w_br_c, w_gate, b_gate, w_out, final_g, loss_target, m_norm_g, m_w_ada, m_b_ada, m_w_in, m_pool_w, m_pool_scale, m_conv_w, m_w_br_a, m_w_br_b, m_w_br_c, m_w_gate, m_b_gate, m_w_out, m_final_g, v_norm_g, v_w_ada, v_b_ada, v_w_in, v_pool_w, v_pool_scale, v_conv_w, v_w_br_a, v_w_br_b, v_w_br_c, v_w_gate, v_b_gate, v_w_out, v_final_g):
    names = ("norm_g", "w_ada", "b_ada", "w_in", "pool_w", "pool_scale", "conv_w", "w_br_a", "w_br_b",
             "w_br_c", "w_gate", "b_gate", "w_out", "final_g")
    w = dict(zip(names, (norm_g, w_ada, b_ada, w_in, pool_w, pool_scale, conv_w, w_br_a, w_br_b, w_br_c,
                         w_gate, b_gate, w_out, final_g)))
    m = dict(zip(names, (m_norm_g, m_w_ada, m_b_ada, m_w_in, m_pool_w, m_pool_scale, m_conv_w, m_w_br_a,
                         m_w_br_b, m_w_br_c, m_w_gate, m_b_gate, m_w_out, m_final_g)))
    v = dict(zip(names, (v_norm_g, v_w_ada, v_b_ada, v_w_in, v_pool_w, v_pool_scale, v_conv_w, v_w_br_a,
                         v_w_br_b, v_w_br_c, v_w_gate, v_b_gate, v_w_out, v_final_g)))
    _, t, d = x.shape
    depth = norm_g.shape[0]
    cc = d // 2
    my_idx = _index(*_place())

    mod8, sc_all = _ada_fwd(jnp.broadcast_to(c, (8, d)), w_ada,
                            jnp.broadcast_to(b_ada[:, None, :], (depth, 8, 3 * d)), name="ada_fwd")
    mods = [tuple(mod8[l, 0:1, k * d:(k + 1) * d] for k in range(3)) for l in range(depth)]
    small = [dict(norm_g=norm_g[l][None], b_gate=b_gate[l][None], pool_scale=pool_scale[l][None])
             for l in range(depth)]

    shards, blocks, placers = _gather_layer(w, 0, d=d)
    gws = [_as_weights(_all_gather(shards, blocks, placers, name="gather_weights_l0"), d)]
    xs = x[0]
    saved = []
    for l in range(depth):
        coming = None
        if l + 1 < depth:
            shards, blocks, placers = _gather_layer(w, l + 1, d=d)
            coming = _Behind(shards, blocks, [_shard_itself] * len(shards), placers, gws[l]["conv_w"],
                             f"gather_weights_l{l + 1}")
            shift, scale, res_gate = mods[l]
            mods[l] = (shift + coming.token, scale, res_gate)
        xs, sv = _layer_fwd(xs, mods[l], small[l], gws[l], d=d, tag=f"l{l}")
        saved.append(sv)
        if coming is not None:
            gws.append(_as_weights(coming.finish(xs), d))
    dx, loss8, d_final_g = _loss_head(xs, loss_target[0], final_g[None], tm=TM, name="loss_head")
    loss = lax.psum(loss8[0, 0], ("x", "y", "c"))

    parts, small_parts = [dict() for _ in range(depth)], [None] * depth
    going = []
    for l in reversed(range(depth)):
        if going:
            shift, scale, res_gate = mods[l]
            mods[l] = (shift, scale, res_gate + going[-1][2].token)
        send_early = None
        if l == 0:
            def send_early(early, l=l):
                grads, pickers, shapes = _scatter_spec(early, EARLY, d=d)
                going.append((l, EARLY, _Behind(grads, shapes, pickers, [_whole] * len(grads), grads[0],
                                                f"scatter_early_l{l}")))
                return going[-1][2].token
        dx, big, small_parts[l] = _layer_bwd(dx, saved[l], mods[l], small[l], gws[l], d=d, tag=f"l{l}",
                                             send_early=send_early)
        if l > 0:
            grads, pickers, shapes = _scatter_spec(big, BIG, d=d)
            going.append((l, BIG, _Behind(grads, shapes, pickers, [_whole] * len(grads), dx,
                                          f"scatter_grads_l{l}")))
        else:
            grads, pickers, shapes = _scatter_spec(big, LATE, d=d)
            parts[l].update(zip(LATE, _scatter_parts(grads, pickers, shapes, name=f"scatter_late_l{l}")))
    for l, keys, behind in going:
        parts[l].update(zip(keys, behind.finish(dx)))

    pieces = [small_parts[l][k] for l in range(depth) for k in SMALL_PER_LAYER] + [d_final_g]
    sizes = [p.size for p in pieces]
    gathered = _gather_small(_pack(pieces), name="gather_small_grads")
    zero_conv = jnp.zeros((3, cc), F32)

    def packed(src):
        per_layer = lambda l: [src["norm_g"][l], src["b_ada"][l], src["pool_scale"][l], src["b_gate"][l],
                               zero_conv]
        return _pack([p for l in range(depth) for p in per_layer(l)] + [src["final_g"]])[None]

    small_out = _adamw(packed(w), packed(m), packed(v), gathered, 0, None, tr=gathered.shape[1],
                       name="adamw_small")

    def unpack(flat):
        flat = flat.reshape(-1)
        out, off = [], 0
        for n in sizes:
            out.append(flat[off:off + n])
            off += n
        return out

    small_res = [unpack(a) for a in small_out]
    k_per = len(SMALL_PER_LAYER)

    def small_stack(which, pos, shape):
        return jnp.stack([small_res[which][l * k_per + pos] for l in range(depth)]).reshape(shape)

    res = {}
    for pos, name in ((0, "norm_g"), (1, "b_ada"), (2, "pool_scale"), (3, "b_gate")):
        res[name] = tuple(small_stack(k, pos, w[name].shape) for k in range(4))
    res["final_g"] = tuple(small_res[k][-1].reshape(final_g.shape) for k in range(4))

    ncol = cc // N_DEV
    conv_total = small_stack(0, 4, (depth * 3, cc))
    conv_mine = lax.dynamic_slice_in_dim(conv_total, my_idx * ncol, ncol, axis=1)
    pad8 = lambda a: jnp.pad(a.reshape(depth * 3, ncol), ((0, 8 - depth * 3), (0, 0)))[None]
    conv_out = _adamw(pad8(conv_w), pad8(m["conv_w"]), pad8(v["conv_w"]), pad8(conv_mine), 0, None, tr=8,
                      name="adamw_conv_w")
    res["conv_w"] = tuple(a[0, :depth * 3].reshape(conv_w.shape) for a in conv_out)

    n_ada = 3 * d // N_DEV
    mod_off = [sum(sizes[:l * k_per + 1]) // LANE for l in range(depth)]
    dmod = jnp.stack([
        lax.dynamic_slice_in_dim(gathered[:, mod_off[l]:mod_off[l] + 3 * d // LANE].reshape(N_DEV, 3 * d),
                                 my_idx * n_ada, n_ada, axis=1) for l in range(depth)])
    g_w_ada = _ada_bwd(sc_all[::8], dmod, name="ada_bwd")
    res["w_ada"] = _adamw_layers(w_ada, m["w_ada"], v["w_ada"], [g_w_ada[l][None] for l in range(depth)],
                                 tr=128, name="adamw_w_ada")

    for name in BIG:
        out = _adamw_layers(w[name], m[name], v[name], [parts[l][name] for l in range(depth)], tr=128,
                            name=f"adamw_{name}")
        res[name] = tuple(a.reshape(w[name].shape) for a in out)

    outs = [loss, dx[None]]
    for k in range(4):
        outs += [res[name][k] for name in names]
    return tuple(outs)
```

```python
import functools

import jax
import jax.numpy as jnp
from jax import lax
from jax.experimental import pallas as pl
from jax.experimental.pallas import tpu as pltpu

F32 = jnp.float32
BF16 = jnp.bfloat16
MESH = pl.DeviceIdType.MESH

N_DEV = 8
DEPTH = 2
HEAD_DIM = 128
N_POOL_GROUPS = 4
POOL_WINDOWS = (2, 4, 8, 16)
RMS_EPS = 1e-6
LANE = 128
HALO = 16
KEY_CHUNK = 128
KEY_BLOCK = 512
ADAM_LR = 0.001
ADAM_B1 = 0.9
ADAM_B2 = 0.999
ADAM_EPS = 1e-08
ADAM_WD = 0.01
ADAM_STEP = 10


def _cparams(semantics, vmem_mb):
    return pltpu.CompilerParams(dimension_semantics=semantics, vmem_limit_bytes=vmem_mb << 20)


def _sigmoid(z):
    return 1.0 / (1.0 + jnp.exp(-z))


def _mm_nn(a, w, *, out_dtype, tm, name, bias=None, out=None, out_cols=None, out_col_block=0):
    m, k = a.shape
    nb, _, n = w.shape
    tm = min(tm, m)
    total_cols =out.shape[1] if out is not None else (nb * n if out_cols is None else out_cols)

    def body(*refs):
        if out is not None:
            refs = refs[:-2] + refs[-1:]
        if bias is not None:
            a_ref, w_ref, b_ref, o_ref = refs
        else:
            a_ref, w_ref, o_ref = refs
        acc = jnp.dot(a_ref[...], w_ref[...], preferred_element_type=F32)
        if bias is not None:
            acc = acc + b_ref[...]
        o_ref[...] = acc.astype(out_dtype)

    in_specs = [pl.BlockSpec((tm, k), lambda j, i: (i, 0)),
                pl.BlockSpec((None, k, n), lambda j, i: (j, 0, 0))]
    args = [a, w]
    if bias is not None:
        in_specs.append(pl.BlockSpec((1, n), lambda j, i: (0, j)))
        args.append(bias)
    aliases = {}
    if out is not None:
        in_specs.append(pl.BlockSpec(memory_space=pl.ANY))
        aliases = {len(args): 0}
        args.append(out)
    return pl.pallas_call(
        body, grid=(nb, m // tm), in_specs=in_specs,
        out_specs=pl.BlockSpec((tm, n), lambda j, i: (i, out_col_block + j)),
        out_shape=jax.ShapeDtypeStruct((m, total_cols), out_dtype),
        input_output_aliases=aliases, name=name,
        compiler_params=_cparams(("arbitrary", "arbitrary"), 56),
    )(*args)


def _mm_nt(dy, w, *, out_dtype, tm, name, addend=None, col_block=0):
    m = dy.shape[0]
    nb, k, n = w.shape
    tm = min(tm, m)

    def body(*refs):
        if addend is not None:
            dy_ref, w_ref, add_ref, o_ref, acc_ref = refs
        else:
            dy_ref, w_ref, o_ref, acc_ref = refs
        j = pl.program_id(1)
        part = lax.dot_general(dy_ref[...], w_ref[...], (((1,), (1,)), ((), ())),
                               preferred_element_type=F32)

        @pl.when(j == 0)
        def _():
            acc_ref[...] = part if addend is None else part + add_ref[...]

        @pl.when(j > 0)
        def _():
            acc_ref[...] += part

        @pl.when(j == nb - 1)
        def _():
            o_ref[...] = acc_ref[...].astype(out_dtype)

    in_specs = [pl.BlockSpec((tm, n), lambda i, j: (i, col_block + j)),
                pl.BlockSpec((None, k, n), lambda i, j: (j, 0, 0))]
    args = [dy, w]
    if addend is not None:
        in_specs.append(pl.BlockSpec((tm, k), lambda i, j: (i, 0)))
        args.append(addend)
    return pl.pallas_call(
        body, grid=(m // tm, nb), in_specs=in_specs,
        out_specs=pl.BlockSpec((tm, k), lambda i, j: (i, 0)),
        out_shape=jax.ShapeDtypeStruct((m, k), out_dtype),
        scratch_shapes=[pltpu.VMEM((tm, k), F32)], name=name,
        compiler_params=_cparams(("arbitrary", "arbitrary"), 56),
    )(*args)


def _mm_tn(a, dy, *, nb, n, tk, tmm, name, col_block=0, out_dtype=BF16):
    m, k = a.shape
    tmm = min(tmm, m)

    def body(a_ref, dy_ref, o_ref, acc_ref):
        t = pl.program_id(2)
        part = lax.dot_general(a_ref[...], dy_ref[...], (((0,), (0,)), ((), ())),
                               preferred_element_type=F32)

        @pl.when(t == 0)
        def _():
            acc_ref[...] = part

        @pl.when(t > 0)
        def _():
            acc_ref[...] += part

        @pl.when(t == pl.num_programs(2) - 1)
        def _():
            o_ref[...] = acc_ref[...].astype(out_dtype)

    return pl.pallas_call(
        body, grid=(nb, k // tk, m // tmm),
        in_specs=[pl.BlockSpec((tmm, tk), lambda j, kb, t: (t, kb)),
                  pl.BlockSpec((tmm, n), lambda j, kb, t: (t, col_block + j))],
        out_specs=pl.BlockSpec((None, tk, n), lambda j, kb, t: (j, kb, 0)),
        out_shape=jax.ShapeDtypeStruct((nb, k, n), out_dtype),
        scratch_shapes=[pltpu.VMEM((tk, n), F32)], name=name,
        compiler_params=_cparams(("arbitrary", "arbitrary", "arbitrary"), 56),
    )(a, dy)


def _prenorm(x, g, scale, shift, *, tm, name):
    t, d = x.shape

    def body(x_ref, g_ref, sc_ref, sh_ref, h_ref):
        xv = x_ref[...]
        r = lax.rsqrt(jnp.mean(xv * xv, axis=-1, keepdims=True) + RMS_EPS)
        h_ref[...] = ((xv * r) * g_ref[...] * (1.0 + sc_ref[...]) + sh_ref[...]).astype(BF16)

    row = pl.BlockSpec((1, d), lambda i: (0, 0))
    tile = pl.BlockSpec((tm, d), lambda i: (i, 0))
    return pl.pallas_call(
        body, grid=(t // tm,), in_specs=[tile, row, row, row], out_specs=tile,
        out_shape=jax.ShapeDtypeStruct((t, d), BF16), name=name,
        compiler_params=_cparams(("arbitrary",), 40),
    )(x, g, scale, shift)


def _prenorm_bwd(dh, x, dout, g, scale, *, tm, name):
    t, d = x.shape

    def body(dh_ref, x_ref, do_ref, g_ref, sc_ref, dx_ref, dg_ref, dsc_ref, dsh_ref):
        i = pl.program_id(0)
        xv = x_ref[...]
        dhv = dh_ref[...]
        r = lax.rsqrt(jnp.mean(xv * xv, axis=-1, keepdims=True) + RMS_EPS)
        xn = xv * r
        gain = g_ref[...] * (1.0 + sc_ref[...])
        dxn = dhv * gain
        dx_ref[...] = do_ref[...] + r * (dxn - xn * jnp.mean(dxn * xn, axis=-1, keepdims=True))
        dhxn = dhv * xn
        dg = jnp.sum(dhxn * (1.0 + sc_ref[...]), axis=0, keepdims=True)
        dsc = jnp.sum(dhxn * g_ref[...], axis=0, keepdims=True)
        dsh = jnp.sum(dhv, axis=0, keepdims=True)

        @pl.when(i == 0)
        def _():
            dg_ref[...] = dg
            dsc_ref[...] = dsc
            dsh_ref[...] = dsh

        @pl.when(i > 0)
        def _():
            dg_ref[...] += dg
            dsc_ref[...] += dsc
            dsh_ref[...] += dsh

    row = pl.BlockSpec((1, d), lambda i: (0, 0))
    tile = pl.BlockSpec((tm, d), lambda i: (i, 0))
    vec = jax.ShapeDtypeStruct((1, d), F32)
    return pl.pallas_call(
        body, grid=(t // tm,), in_specs=[tile, tile, tile, row, row],
        out_specs=[tile, row, row, row],
        out_shape=[jax.ShapeDtypeStruct((t, d), F32), vec, vec, vec], name=name,
        compiler_params=_cparams(("arbitrary",), 48),
    )(dh, x, dout, g, scale)


def _out_fwd(merged, w_out, x, res_gate, *, tm, name):
    t, d = x.shape
    k = merged.shape[1]

    def body(m_ref, w_ref, x_ref, rg_ref, xo_ref, mo_ref):
        mo = jnp.dot(m_ref[...], w_ref[...], preferred_element_type=F32)
        mo_ref[...] = mo.astype(BF16)
        xo_ref[...] = x_ref[...] + rg_ref[...] * mo

    tile = pl.BlockSpec((tm, d), lambda i: (i, 0))
    return pl.pallas_call(
        body, grid=(t // tm,),
        in_specs=[pl.BlockSpec((tm, k), lambda i: (i, 0)), pl.BlockSpec((k, d), lambda i: (0, 0)),
                  tile, pl.BlockSpec((1, d), lambda i: (0, 0))],
        out_specs=[tile, tile],
        out_shape=[jax.ShapeDtypeStruct((t, d), F32), jax.ShapeDtypeStruct((t, d), BF16)], name=name,
        compiler_params=_cparams(("arbitrary",), 56),
    )(merged, w_out, x, res_gate)


def _out_bwd(dout, mo, res_gate, *, tm, name):
    t, d = dout.shape

    def body(do_ref, mo_ref, rg_ref, dmo_ref, drg_ref):
        i = pl.program_id(0)
        dov = do_ref[...]
        dmo_ref[...] = (dov * rg_ref[...]).astype(BF16)
        drg = jnp.sum(dov * mo_ref[...].astype(F32), axis=0, keepdims=True)

        @pl.when(i == 0)
        def _():
            drg_ref[...] = drg

        @pl.when(i > 0)
        def _():
            drg_ref[...] += drg

    row = pl.BlockSpec((1, d), lambda i: (0, 0))
    tile = pl.BlockSpec((tm, d), lambda i: (i, 0))
    return pl.pallas_call(
        body, grid=(t // tm,), in_specs=[tile, tile, row], out_specs=[tile, row],
        out_shape=[jax.ShapeDtypeStruct((t, d), BF16), jax.ShapeDtypeStruct((1, d), F32)], name=name,
        compiler_params=_cparams(("arbitrary",), 40),
    )(dout, mo, res_gate)


def _loss_head(x, target, final_g, *, tm, name):
    t, d = x.shape

    def body(x_ref, t_ref, g_ref, dx_ref, loss_ref, dg_ref):
        i = pl.program_id(0)
        xv = x_ref[...]
        r = lax.rsqrt(jnp.mean(xv * xv, axis=-1, keepdims=True) + RMS_EPS)
        xn = xv * r
        err = xn * g_ref[...] - t_ref[...]
        part = (0.5 / d) * jnp.sum(jnp.sum(err * err, axis=1, keepdims=True), axis=0, keepdims=True)
        dy = err * (1.0 / d)
        dxn = dy * g_ref[...]
        dx_ref[...] = r * (dxn - xn * jnp.mean(dxn * xn, axis=-1, keepdims=True))
        dg = jnp.sum(dy * xn, axis=0, keepdims=True)

        @pl.when(i == 0)
        def _():
            loss_ref[...] = jnp.zeros_like(loss_ref) + part
            dg_ref[...] = dg

        @pl.when(i > 0)
        def _():
            loss_ref[...] += part
            dg_ref[...] += dg

    row = pl.BlockSpec((1, d), lambda i: (0, 0))
    tile = pl.BlockSpec((tm, d), lambda i: (i, 0))
    return pl.pallas_call(
        body, grid=(t // tm,), in_specs=[tile, tile, row],
        out_specs=[tile, pl.BlockSpec((8, LANE), lambda i: (0, 0)), row],
        out_shape=[jax.ShapeDtypeStruct((t, d), F32), jax.ShapeDtypeStruct((8, LANE), F32),
                   jax.ShapeDtypeStruct((1, d), F32)], name=name,
        compiler_params=_cparams(("arbitrary",), 40),
    )(x, target, final_g)


def _cur(tm, cw, col):
    return pl.BlockSpec((tm, cw), lambda cb, i: (i, col + cb))


def _prev(tm, cw, col):
    return pl.BlockSpec((HALO, cw), lambda cb, i: (jnp.maximum(i * (tm // HALO) - 1, 0), col + cb))


def _next(tm, cw, col, t):
    last = t // HALO - 1
    return pl.BlockSpec((HALO, cw), lambda cb, i: (jnp.minimum((i + 1) * (tm // HALO), last), col + cb))


def _with_prev(prev_ref, cur, first):
    prev = jnp.where(first, 0.0, prev_ref[...].astype(F32))
    return jnp.concatenate([prev, cur], axis=0)


def _with_next(cur, next_ref, last):
    nxt = jnp.where(last, 0.0, next_ref[...].astype(F32))
    return jnp.concatenate([cur, nxt], axis=0)


def _pool_mixed(xe, cur, g, row0, tm):
    s2 = xe + pltpu.roll(xe, 1, 0)
    s4 = s2 + pltpu.roll(s2, 2, 0)
    s8 = s4 + pltpu.roll(s4, 4, 0)
    s16 = s8 + pltpu.roll(s8, 8, 0)
    win = jnp.where(g == 0, s2, jnp.where(g == 1, s4, jnp.where(g == 2, s8, s16)))[HALO:]
    return win * _pool_inv_count(g, row0, tm, cur.shape[1]) - cur


def _pool_inv_count(g, row0, rows, cols):
    tpos = row0 + lax.broadcasted_iota(jnp.int32, (rows, cols), 0)
    width = jnp.left_shift(2, g)
    return 1.0 / jnp.minimum(tpos + 1, width).astype(F32)


def _pool_fwd(proj, pool_w, pool_scale, *, d, tm, name):
    t = proj.shape[0]
    pw = d // 2
    gd = pw // N_POOL_GROUPS
    za_col = pw // gd

    def body(xa_ref, xp_ref, za_ref, w_ref, ps_ref, ya_ref):
        g = pl.program_id(0)
        i = pl.program_id(1)
        cur = xa_ref[...].astype(F32)
        mixed = _pool_mixed(_with_prev(xp_ref, cur, i == 0), cur, g, i * tm, tm)
        ypre = jnp.dot(mixed.astype(BF16), w_ref[...], preferred_element_type=F32)
        za = za_ref[...].astype(F32)
        ya_ref[...] = (ypre * ps_ref[...] * (za * _sigmoid(za))).astype(BF16)

    return pl.pallas_call(
        body, grid=(N_POOL_GROUPS, t // tm),
        in_specs=[_cur(tm, gd, 0), _prev(tm, gd, 0), _cur(tm, gd, za_col),
                  pl.BlockSpec((None, gd, gd), lambda g, i: (g, 0, 0)),
                  pl.BlockSpec((1, gd), lambda g, i: (0, g))],
        out_specs=pl.BlockSpec((tm, gd), lambda g, i: (i, g)),
        out_shape=jax.ShapeDtypeStruct((t, pw), BF16), name=name,
        compiler_params=_cparams(("arbitrary", "arbitrary"), 40),
    )(proj, proj, proj, pool_w, pool_scale)


def _pool_bwd(dya, proj, pool_w, pool_scale, *, d, tm, name):
    t = proj.shape[0]
    pw = d // 2
    gd = pw // N_POOL_GROUPS
    za_col = pw // gd
    n_i = t // tm

    def body(xa_ref, xp_ref, za_ref, zn_ref, dy_ref, dyn_ref, w_ref, ps_ref,
             dxa_ref, dza_ref, dw_ref, dps_ref):
        g = pl.program_id(0)
        i = pl.program_id(1)
        last = i == n_i - 1
        cur = xa_ref[...].astype(F32)
        mixed = _pool_mixed(_with_prev(xp_ref, cur, i == 0), cur, g, i * tm, tm)
        mixed16 = mixed.astype(BF16)
        ypre = jnp.dot(mixed16, w_ref[...], preferred_element_type=F32)
        za = za_ref[...].astype(F32)
        sg = _sigmoid(za)
        dy = dy_ref[...].astype(F32)
        dza_ref[...] = (dy * (ypre * ps_ref[...]) * (sg * (1.0 + za * (1.0 - sg)))).astype(BF16)
        dysc = dy * (za * sg)
        za_e = _with_next(za, zn_ref, last)
        dy_e = _with_next(dy, dyn_ref, last)
        dypre_e = (dy_e * (za_e * _sigmoid(za_e)) * ps_ref[...]).astype(BF16)
        dm_e = lax.dot_general(dypre_e, w_ref[...], (((1,), (1,)), ((), ())), preferred_element_type=F32)
        rows = tm + HALO
        q = dm_e * _pool_inv_count(g, i * tm, rows, gd)
        f2 = q + pltpu.roll(q, rows - 1, 0)
        f4 = f2 + pltpu.roll(f2, rows - 2, 0)
        f8 = f4 + pltpu.roll(f4, rows - 4, 0)
        f16 = f8 + pltpu.roll(f8, rows - 8, 0)
        ahead = jnp.where(g == 0, f2, jnp.where(g == 1, f4, jnp.where(g == 2, f8, f16)))
        dxa_ref[...] = (ahead[:tm] - dm_e[:tm]).astype(BF16)
        dw = lax.dot_general(mixed16, dypre_e[:tm], (((0,), (0,)), ((), ())), preferred_element_type=F32)
        dps = jnp.sum(dysc * ypre, axis=0, keepdims=True)

        @pl.when(i == 0)
        def _():
            dw_ref[...] = dw
            dps_ref[...] = dps

        @pl.when(i > 0)
        def _():
            dw_ref[...] += dw
            dps_ref[...] += dps

    out_tile = pl.BlockSpec((tm, gd), lambda g, i: (i, g))
    return pl.pallas_call(
        body, grid=(N_POOL_GROUPS, n_i),
        in_specs=[_cur(tm, gd, 0), _prev(tm, gd, 0), _cur(tm, gd, za_col), _next(tm, gd, za_col, t),
                  _cur(tm, gd, 0), _next(tm, gd, 0, t),
                  pl.BlockSpec((None, gd, gd), lambda g, i: (g, 0, 0)),
                  pl.BlockSpec((1, gd), lambda g, i: (0, g))],
        out_specs=[out_tile, out_tile, pl.BlockSpec((None, gd, gd), lambda g, i: (g, 0, 0)),
                   pl.BlockSpec((1, gd), lambda g, i: (0, g))],
        out_shape=[jax.ShapeDtypeStruct((t, pw), BF16), jax.ShapeDtypeStruct((t, pw), BF16),
                   jax.ShapeDtypeStruct((N_POOL_GROUPS, gd, gd), F32), jax.ShapeDtypeStruct((1, pw), F32)],
        name=name, compiler_params=_cparams(("arbitrary", "arbitrary"), 48),
    )(proj, proj, proj, proj, dya, dya, pool_w, pool_scale)


def _conv_cols(d, cw):
    cc = d // 2
    base = (d + 4 * d) // cw
    return base, base + cc // cw, base + 2 * (cc // cw), base + 3 * (cc // cw)


def _conv_fwd(proj, conv_w, *, d, tm, cw, name):
    t = proj.shape[0]
    cc = d // 2
    u_col, b_col, c_col, z_col = _conv_cols(d, cw)

    def body(u_ref, up_ref, c_ref, cp_ref, b_ref, z_ref, w_ref, y_ref):
        i = pl.program_id(1)
        v = u_ref[...].astype(F32) * c_ref[...].astype(F32)
        vprev = up_ref[...].astype(F32) * cp_ref[...].astype(F32)
        ve = jnp.concatenate([jnp.where(i == 0, 0.0, vprev), v], axis=0)
        w = w_ref[...]
        y = w[0:1] * pltpu.roll(ve, 2, 0)[HALO:] + w[1:2] * pltpu.roll(ve, 1, 0)[HALO:] + w[2:3] * v
        z = z_ref[...].astype(F32)
        y_ref[...] = (b_ref[...].astype(F32) * y * (z * _sigmoid(z))).astype(BF16)

    return pl.pallas_call(
        body, grid=(cc // cw, t // tm),
        in_specs=[_cur(tm, cw, u_col), _prev(tm, cw, u_col), _cur(tm, cw, c_col), _prev(tm, cw, c_col),
                  _cur(tm, cw, b_col), _cur(tm, cw, z_col),
                  pl.BlockSpec((8, cw), lambda cb, i: (0, cb))],
        out_specs=pl.BlockSpec((tm, cw), lambda cb, i: (i, cb)),
        out_shape=jax.ShapeDtypeStruct((t, cc), BF16), name=name,
        compiler_params=_cparams(("arbitrary", "arbitrary"), 40),
    )(proj, proj, proj, proj, proj, proj, conv_w)


def _conv_bwd(dyc, proj, conv_w, *, d, tm, cw, name):
    t = proj.shape[0]
    cc = d // 2
    n_i = t // tm
    u_col, b_col, c_col, z_col = _conv_cols(d, cw)

    def body(u_ref, up_ref, c_ref, cp_ref, b_ref, bn_ref, z_ref, zn_ref, dy_ref, dyn_ref, w_ref,
             du_ref, db_ref, dc_ref, dz_ref, dw_ref):
        i = pl.program_id(1)
        last = i == n_i - 1
        u = u_ref[...].astype(F32)
        cg = c_ref[...].astype(F32)
        v = u * cg
        vprev = up_ref[...].astype(F32) * cp_ref[...].astype(F32)
        ve = jnp.concatenate([jnp.where(i == 0, 0.0, vprev), v], axis=0)
        v2 = pltpu.roll(ve, 2, 0)[HALO:]
        v1 = pltpu.roll(ve, 1, 0)[HALO:]
        w = w_ref[...]
        y = w[0:1] * v2 + w[1:2] * v1 + w[2:3] * v
        z = z_ref[...].astype(F32)
        sg = _sigmoid(z)
        bg = b_ref[...].astype(F32)
        dyc_v = dy_ref[...].astype(F32)
        dz_ref[...] = (dyc_v * bg * y * (sg * (1.0 + z * (1.0 - sg)))).astype(BF16)
        db_ref[...] = (dyc_v * y * (z * sg)).astype(BF16)
        ze = _with_next(z, zn_ref, last)
        dy_e = (_with_next(dyc_v, dyn_ref, last) * _with_next(bg, bn_ref, last) * (ze * _sigmoid(ze)))
        rows = tm + HALO
        dy0 = dy_e[:tm]
        dv = (w[2:3] * dy0 + w[1:2] * pltpu.roll(dy_e, rows - 1, 0)[:tm]
              + w[0:1] * pltpu.roll(dy_e, rows - 2, 0)[:tm])
        du_ref[...] = (dv * cg).astype(BF16)
        dc_ref[...] = (dv * u).astype(BF16)
        s0 = jnp.sum(dy0 * v2, axis=0, keepdims=True)
        s1 = jnp.sum(dy0 * v1, axis=0, keepdims=True)
        s2 = jnp.sum(dy0 * v, axis=0, keepdims=True)
        r = lax.broadcasted_iota(jnp.int32, (8, cw), 0)
        dw = jnp.where(r == 0, s0, jnp.where(r == 1, s1, jnp.where(r == 2, s2, 0.0)))

        @pl.when(i == 0)
        def _():
            dw_ref[...] = dw

        @pl.when(i > 0)
        def _():
            dw_ref[...] += dw

    out_tile = pl.BlockSpec((tm, cw), lambda cb, i: (i, cb))
    act = jax.ShapeDtypeStruct((t, cc), BF16)
    return pl.pallas_call(
        body, grid=(cc // cw, n_i),
        in_specs=[_cur(tm, cw, u_col), _prev(tm, cw, u_col), _cur(tm, cw, c_col), _prev(tm, cw, c_col),
                  _cur(tm, cw, b_col), _next(tm, cw, b_col, t), _cur(tm, cw, z_col), _next(tm, cw, z_col, t),
                  _cur(tm, cw, 0), _next(tm, cw, 0, t),
                  pl.BlockSpec((8, cw), lambda cb, i: (0, cb))],
        out_specs=[out_tile, out_tile, out_tile, out_tile, pl.BlockSpec((8, cw), lambda cb, i: (0, cb))],
        out_shape=[act, act, act, act, jax.ShapeDtypeStruct((8, cc), F32)], name=name,
        compiler_params=_cparams(("arbitrary", "arbitrary"), 48),
    )(proj, proj, proj, proj, proj, proj, proj, proj, dyc, dyc, conv_w)


def _sum_matrix(prefix):
    r = lax.broadcasted_iota(jnp.int32, (KEY_CHUNK, 2 * KEY_CHUNK), 0)
    c = lax.broadcasted_iota(jnp.int32, (KEY_CHUNK, 2 * KEY_CHUNK), 1)
    inside = (r <= c) if prefix else (r > c)
    return jnp.where((c >= KEY_CHUNK) | inside, 1.0, 0.0).astype(BF16)


def _chunk_sums(val, mat, two_pass):
    hi = val.astype(BF16)
    both = jnp.dot(hi, mat, preferred_element_type=F32)
    if two_pass:
        lo = (val - hi.astype(F32)).astype(BF16)
        both = both + jnp.dot(lo, mat, preferred_element_type=F32)
    return both[:, :KEY_CHUNK], both[:, KEY_CHUNK:]


def _stick_logits(qs, kb, strict):
    z = lax.dot_general(qs, kb, (((1,), (1,)), ((), ())), preferred_element_type=F32)
    sp = jnp.maximum(z, 0.0) + jnp.log(1.0 + jnp.exp(-jnp.abs(z)))
    log_beta = z - sp
    if strict is not None:
        sp = jnp.where(strict, sp, 0.0)
    return sp, log_beta


def _running(val, mat, carry, upwards):
    n = val.shape[1] // KEY_CHUNK
    out = [None] * n
    for c in (range(n) if upwards else reversed(range(n))):
        part, total = _chunk_sums(val[:, c * KEY_CHUNK:(c + 1) * KEY_CHUNK], mat, False)
        out[c] = part + carry
        carry = carry + total
    return jnp.concatenate(out, axis=1), carry


def _attn_cols(d):
    h = d // HEAD_DIM
    return h, 2 * h, 3 * h, 4 * h


def _sweep(i, per, block, carry, upwards):
    if upwards:
        carry = lax.fori_loop(0, i * per, lambda it, cr: block(it, cr, False), carry)
        for dgl in range(per):
            carry = block(i * per + dgl, carry, True)
        return carry
    for dgl in range(per - 1, -1, -1):
        carry = block(i * per + dgl, carry, True)
    return lax.fori_loop(0, i * per, lambda it, cr: block(i * per - 1 - it, cr, False), carry)


def _strict_mask(i, tq, kb, width):
    t_idx = i * tq + lax.broadcasted_iota(jnp.int32, (tq, width), 0)
    s_idx = kb * width + lax.broadcasted_iota(jnp.int32, (tq, width), 1)
    return s_idx < t_idx


def _attn_fwd(proj, *, d, tq, name):
    t = proj.shape[0]
    heads = d // HEAD_DIM
    q_col, k_col, v_col, z_col = _attn_cols(d)

    def body(q_ref, k_ref, v_ref, z_ref, o_ref, y_ref, tot_ref):
        i = pl.program_id(1)
        qs = (q_ref[...].astype(F32) * (HEAD_DIM ** -0.5)).astype(BF16)
        mat = _sum_matrix(prefix=False)

        def block(kb, carry, masked):
            acc, after = carry
            off = pl.multiple_of(kb * KEY_BLOCK, KEY_BLOCK)
            strict = _strict_mask(i, tq, kb, KEY_BLOCK) if masked else None
            sp, log_beta = _stick_logits(qs, k_ref[pl.ds(off, KEY_BLOCK), :], strict)
            between, after = _running(sp, mat, after, upwards=False)
            a = jnp.exp(log_beta - between)
            if masked:
                a = jnp.where(strict, a, 0.0)
            acc = acc + jnp.dot(a.astype(BF16), v_ref[pl.ds(off, KEY_BLOCK), :], preferred_element_type=F32)
            return acc, after

        zero = jnp.zeros((tq, HEAD_DIM), F32)
        o, total = _sweep(i, tq // KEY_BLOCK, block, (zero, zero), upwards=False)
        o_ref[...] = o.astype(BF16)
        tot_ref[...] = total
        z = z_ref[...].astype(F32)
        y_ref[...] = (o * (z * _sigmoid(z))).astype(BF16)

    tile = lambda col: pl.BlockSpec((tq, HEAD_DIM), lambda h, i: (i, col + h))
    full = lambda col: pl.BlockSpec((t, HEAD_DIM), lambda h, i: (0, col + h))
    act = jax.ShapeDtypeStruct((t, d), BF16)
    return pl.pallas_call(
        body, grid=(heads, t // tq),
        in_specs=[tile(q_col), full(k_col), full(v_col), tile(z_col)],
        out_specs=[tile(0), tile(0), tile(0)], out_shape=[act, act, jax.ShapeDtypeStruct((t, d), F32)],
        name=name, compiler_params=_cparams(("arbitrary", "arbitrary"), 40),
    )(proj, proj, proj, proj)


def _attn_bwd(dyb, o, tot, proj, *, d, tq, name):
    t = proj.shape[0]
    heads = d // HEAD_DIM
    n_i = t // tq
    q_col, k_col, v_col, z_col = _attn_cols(d)
    scale = HEAD_DIM ** -0.5

    def body(q_ref, k_ref, v_ref, z_ref, o_ref, tot_ref, dy_ref, dq_ref, dk_ref, dv_ref, dz_ref, dk_acc, dv_acc):
        i = pl.program_id(1)

        @pl.when(i == 0)
        def _():
            dk_acc[...] = jnp.zeros_like(dk_acc)
            dv_acc[...] = jnp.zeros_like(dv_acc)

        qs = (q_ref[...].astype(F32) * scale).astype(BF16)
        z = z_ref[...].astype(F32)
        sg = _sigmoid(z)
        dy = dy_ref[...].astype(F32)
        of = o_ref[...].astype(F32)
        dz_ref[...] = (dy * of * (sg * (1.0 + z * (1.0 - sg)))).astype(BF16)
        do16 = (dy * (z * sg)).astype(BF16)
        all_keep = tot_ref[...]
        all_keep = jnp.concatenate([all_keep] * (KEY_BLOCK // HEAD_DIM), axis=1)
        mat = _sum_matrix(prefix=True)

        def block(kb, carry, masked):
            dq, sp_seen, g_seen = carry
            off = pl.multiple_of(kb * KEY_BLOCK, KEY_BLOCK)
            kblk = k_ref[pl.ds(off, KEY_BLOCK), :]
            vblk = v_ref[pl.ds(off, KEY_BLOCK), :]
            strict = _strict_mask(i, tq, kb, KEY_BLOCK) if masked else None
            sp, log_beta = _stick_logits(qs, kblk, strict)
            sp_upto, sp_seen = _running(sp, mat, sp_seen, upwards=True)
            a = jnp.exp(log_beta - (all_keep - sp_upto))
            if masked:
                a = jnp.where(strict, a, 0.0)
            g = a * lax.dot_general(do16, vblk, (((1,), (1,)), ((), ())), preferred_element_type=F32)
            g_upto, g_seen = _running(g, mat, g_seen, upwards=True)
            dz_ts = g - jnp.exp(log_beta) * g_upto
            if masked:
                dz_ts = jnp.where(strict, dz_ts, 0.0)
            dz16 = dz_ts.astype(BF16)
            dq = dq + jnp.dot(dz16, kblk, preferred_element_type=F32)
            dk_acc[pl.ds(off, KEY_BLOCK), :] += lax.dot_general(
                dz16, qs, (((0,), (0,)), ((), ())), preferred_element_type=F32)
            dv_acc[pl.ds(off, KEY_BLOCK), :] += lax.dot_general(
                a.astype(BF16), do16, (((0,), (0,)), ((), ())), preferred_element_type=F32)
            return dq, sp_seen, g_seen

        zero = jnp.zeros((tq, HEAD_DIM), F32)
        dq, _, _ = _sweep(i, tq // KEY_BLOCK, block, (zero, zero, zero), upwards=True)
        dq_ref[...] = (dq * scale).astype(BF16)

        @pl.when(i == n_i - 1)
        def _():
            dk_ref[...] = dk_acc[...].astype(BF16)
            dv_ref[...] = dv_acc[...].astype(BF16)

    tile = lambda col: pl.BlockSpec((tq, HEAD_DIM), lambda h, i: (i, col + h))
    full = lambda col: pl.BlockSpec((t, HEAD_DIM), lambda h, i: (0, col + h))
    act = jax.ShapeDtypeStruct((t, d), BF16)
    return pl.pallas_call(
        body, grid=(heads, n_i),
        in_specs=[tile(q_col), full(k_col), full(v_col), tile(z_col), tile(0), tile(0), tile(0)],
        out_specs=[tile(0), full(0), full(0), tile(0)], out_shape=[act, act, act, act],
        scratch_shapes=[pltpu.VMEM((t, HEAD_DIM), F32), pltpu.VMEM((t, HEAD_DIM), F32)], name=name,
        compiler_params=_cparams(("arbitrary", "arbitrary"), 48),
    )(proj, proj, proj, proj, o, tot, dyb)


def _merge_fwd(gl, bst, *, d, tm, tn, name):
    t = gl.shape[0]
    nk = d // tn

    def body(g0, g1, g2, b0, b1, b2, m_ref):
        acc = _sigmoid(g0[...].astype(F32)) * b0[...].astype(F32)
        acc += _sigmoid(g1[...].astype(F32)) * b1[...].astype(F32)
        acc += _sigmoid(g2[...].astype(F32)) * b2[...].astype(F32)
        m_ref[...] = acc.astype(BF16)

    spec = lambda b: pl.BlockSpec((tm, tn), lambda i, kb: (i, b * nk + kb))
    return pl.pallas_call(
        body, grid=(t // tm, nk), in_specs=[spec(0), spec(1), spec(2)] * 2,
        out_specs=pl.BlockSpec((tm, tn), lambda i, kb: (i, kb)),
        out_shape=jax.ShapeDtypeStruct((t, d), BF16), name=name,
        compiler_params=_cparams(("arbitrary", "arbitrary"), 40),
    )(gl, gl, gl, bst, bst, bst)


def _merge_bwd(dmerged, gl, bst, *, d, tm, tn, name):
    t = gl.shape[0]
    nk = d // tn

    def body(dm_ref, g_ref, b_ref, db_ref, dg_ref, dbias_ref):
        i = pl.program_id(2)
        dm = dm_ref[...].astype(F32)
        sg = _sigmoid(g_ref[...].astype(F32))
        db_ref[...] = (dm * sg).astype(BF16)
        dgl = dm * b_ref[...].astype(F32) * (sg * (1.0 - sg))
        dg_ref[...] = dgl.astype(BF16)
        dbias = jnp.sum(dgl, axis=0, keepdims=True)

        @pl.when(i == 0)
        def _():
            dbias_ref[...] = dbias

        @pl.when(i > 0)
        def _():
            dbias_ref[...] += dbias

    wide = pl.BlockSpec((tm, tn), lambda b, kb, i: (i, b * nk + kb))
    act = jax.ShapeDtypeStruct((t, 3 * d), BF16)
    return pl.pallas_call(
        body, grid=(3, nk, t // tm),
        in_specs=[pl.BlockSpec((tm, tn), lambda b, kb, i: (i, kb)), wide, wide],
        out_specs=[wide, wide, pl.BlockSpec((1, tn), lambda b, kb, i: (0, b * nk + kb))],
        out_shape=[act, act, jax.ShapeDtypeStruct((1, 3 * d), F32)], name=name,
        compiler_params=_cparams(("arbitrary", "arbitrary", "arbitrary"), 40),
    )(dmerged, gl, bst)


def _adamw(w, m, v, parts, layer, prev, *, tr, name):
    depth, r, c = w.shape
    n = parts.shape[0]

    def body(*refs):
        w_ref, m_ref, v_ref, p_ref = refs[:4]
        g_ref, d_ref, nm_ref, nv_ref = refs[-4:]
        g = p_ref[0].astype(F32)
        for s in range(1, n):
            g = g + p_ref[s].astype(F32)
        wv = w_ref[...]
        nm = ADAM_B1 * m_ref[...] + (1.0 - ADAM_B1) * g
        nv = ADAM_B2 * v_ref[...] + (1.0 - ADAM_B2) * (g * g)
        m_hat = nm / (1.0 - ADAM_B1 ** ADAM_STEP)
        v_hat = nv / (1.0 - ADAM_B2 ** ADAM_STEP)
        g_ref[...] = g
        nm_ref[...] = nm
        nv_ref[...] = nv
        d_ref[...] = -ADAM_LR * (m_hat / (jnp.sqrt(v_hat) + ADAM_EPS) + ADAM_WD * wv)

    tile = pl.BlockSpec((None, tr, c), lambda i: (layer, i, 0))
    in_specs = [tile, tile, tile, pl.BlockSpec((n, tr, c), lambda i: (0, i, 0))]
    args = [w, m, v, parts]
    aliases = {}
    if prev is not None:
        in_specs += [pl.BlockSpec(memory_space=pl.ANY)] * 4
        aliases = {4 + k: k for k in range(4)}
        args += list(prev)
    full = jax.ShapeDtypeStruct((depth, r, c), F32)
    return pl.pallas_call(
        body, grid=(r // tr,), in_specs=in_specs, out_specs=[tile] * 4, out_shape=[full] * 4,
        input_output_aliases=aliases, name=name, compiler_params=_cparams(("arbitrary",), 48),
    )(*args)


def _adamw_layers(w, m, v, parts_by_layer, *, tr, name):
    lead = w.shape[0]
    flat = lambda a: a.reshape(lead, -1, a.shape[-1])
    w, m, v = flat(w), flat(m), flat(v)
    out = None
    for layer in reversed(range(lead)):
        parts = parts_by_layer[layer]
        parts = parts.reshape(parts.shape[0], -1, parts.shape[-1])
        out = _adamw(w, m, v, parts, layer, out, tr=min(tr, w.shape[1]), name=f"{name}_l{layer}")
    return out


def _place():
    return lax.axis_index("x"), lax.axis_index("y"), lax.axis_index("c")


def _index(px, py, pc):
    return 4 * px + 2 * py + pc


def _whole(ref, idx):
    return ref.at[idx]


def _row_block(rows):
    return lambda ref, idx: ref.at[:, pl.ds(idx * rows, rows), :]


def _lane_block(cols):
    return lambda ref, idx: ref.at[:, pl.ds(idx * cols, cols)]


def _all_gather(shards, out_shapes, placers, *, name):
    n = len(shards)

    def body(*refs):
        src, out = refs[:n], refs[n:2 * n]
        send_sems, recv_sems, local_sems = refs[2 * n:]
        x, y, c = _place()
        me, sibling = (x, y, c), (x, y, 1 - c)
        chips = [(1 - x, y), (x, 1 - y), (1 - x, 1 - y)]

        def copy(a, k, block, to, from_shard=False):
            window = placers[a](out[a], _index(*block))
            return pltpu.make_async_remote_copy(
                src_ref=src[a] if from_shard else window, dst_ref=window,
                send_sem=send_sems.at[a * 7 + k], recv_sem=recv_sems.at[a * 7 + k],
                device_id=to, device_id_type=MESH)

        mine = [pltpu.make_async_copy(src[a], placers[a](out[a], _index(*me)), local_sems.at[a])
                for a in range(n)]
        started = []
        for a in range(n):
            mine[a].start()
            started.append(copy(a, 0, me, sibling, True))
            started += [copy(a, 1 + j, me, (*chip, c), True) for j, chip in enumerate(chips)]
        for cp in started:
            cp.start()
        for j, chip in enumerate(chips):
            for a in range(n):
                copy(a, 1 + j, (*chip, c), me).wait_recv()
                passed = copy(a, 4 + j, (*chip, c), sibling)
                passed.start()
                started.append(passed)
        for a in range(n):
            copy(a, 0, sibling, me).wait_recv()
            for j, chip in enumerate(chips):
                copy(a, 4 + j, (*chip, 1 - c), me).wait_recv()
        for cp in started:
            cp.wait_send()
        for cp in mine:
            cp.wait()

    hbm = pl.BlockSpec(memory_space=pl.ANY)
    return pl.pallas_call(
        body, in_specs=[hbm] * n, out_specs=[hbm] * n, out_shape=out_shapes,
        scratch_shapes=[pltpu.SemaphoreType.DMA((7 * n,)), pltpu.SemaphoreType.DMA((7 * n,)),
                        pltpu.SemaphoreType.DMA((n,))],
        name=name,
    )(*shards)


def _scatter_parts(grads, pickers, part_shapes, *, name):
    n = len(grads)

    def body(*refs):
        src, out = refs[:n], refs[n:2 * n]
        send_sems, recv_sems, local_sems = refs[2 * n:]
        x, y, c = _place()
        my_idx = _index(x, y, c)
        peers = [(x ^ (k >> 2), y ^ ((k >> 1) & 1), c ^ (k & 1)) for k in range(1, N_DEV)]

        def copy(a, k, peer):
            return pltpu.make_async_remote_copy(
                src_ref=pickers[a](src[a], _index(*peer)), dst_ref=out[a].at[my_idx],
                send_sem=send_sems.at[a * 7 + k], recv_sem=recv_sems.at[a * 7 + k],
                device_id=peer, device_id_type=MESH)

        def arrival(a, k, peer):
            return pltpu.make_async_remote_copy(
                src_ref=pickers[a](src[a], my_idx), dst_ref=out[a].at[_index(*peer)],
                send_sem=send_sems.at[a * 7 + k], recv_sem=recv_sems.at[a * 7 + k],
                device_id=peer, device_id_type=MESH)

        mine = [pltpu.make_async_copy(pickers[a](src[a], my_idx), out[a].at[my_idx], local_sems.at[a])
                for a in range(n)]
        sent = [copy(a, k, peer) for a in range(n) for k, peer in enumerate(peers)]
        for cp in mine + sent:
            cp.start()
        for a in range(n):
            for k, peer in enumerate(peers):
                arrival(a, k, peer).wait_recv()
        for cp in sent:
            cp.wait_send()
        for cp in mine:
            cp.wait()

    hbm = pl.BlockSpec(memory_space=pl.ANY)
    return pl.pallas_call(
        body, in_specs=[hbm] * n, out_specs=[hbm] * n, out_shape=part_shapes,
        scratch_shapes=[pltpu.SemaphoreType.DMA((7 * n,)), pltpu.SemaphoreType.DMA((7 * n,)),
                        pltpu.SemaphoreType.DMA((n,))],
        name=name,
    )(*grads)


def _peers(x, y, c):
    return [(x ^ (k >> 2), y ^ ((k >> 1) & 1), c ^ (k & 1)) for k in range(1, N_DEV)]


_HBM = pl.BlockSpec(memory_space=pltpu.HBM)
_SEM = pl.BlockSpec(memory_space=pltpu.SEMAPHORE)
_EFFECT = pltpu.SideEffectType.DATAFLOW_SIDE_EFFECTING


def _exchange_start(srcs, land_shapes, src_win, dst_win, after, *, name):
    n = len(srcs)

    def body(*refs):
        src, land = refs[:n], refs[n:2 * n]
        send_sems, recv_sems, token = refs[2 * n + len(after)], refs[2 * n + len(after) + 1], refs[-1]
        x, y, c = _place()
        my_idx = _index(x, y, c)
        for a in range(n):
            for k, peer in enumerate(_peers(x, y, c)):
                pltpu.make_async_remote_copy(
                    src_ref=src_win[a](src[a], _index(*peer)), dst_ref=dst_win[a](land[a], my_idx),
                    send_sem=send_sems.at[a * 7 + k], recv_sem=recv_sems.at[a * 7 + k],
                    device_id=peer, device_id_type=MESH).start()
        token[...] = jnp.zeros_like(token)

    hbm = lambda a: pltpu.with_memory_space_constraint(a, pltpu.HBM)
    lands = [hbm(lax.empty(s.shape, s.dtype)) for s in land_shapes]
    args = [hbm(s) for s in srcs] + lands
    kept = [pltpu.HBM(a.shape, a.dtype) for a in args]
    out = pl.pallas_call(
        body, name=name,
        out_shape=(pltpu.SemaphoreType.DMA((7 * n,)), pltpu.SemaphoreType.DMA((7 * n,)), *kept,
                   jax.ShapeDtypeStruct((8, LANE), F32)),
        in_specs=[_HBM] * (2 * n) + [pl.BlockSpec(memory_space=pl.ANY)] * len(after),
        out_specs=(_SEM, _SEM, *([_HBM] * (2 * n)), pl.BlockSpec(memory_space=pltpu.VMEM)),
        input_output_aliases={i: 2 + i for i in range(2 * n)},
        compiler_params=pltpu.CompilerParams(has_side_effects=_EFFECT),
    )(*args, *after)
    return out[0], out[1], list(out[2:2 + n]), list(out[2 + n:2 + 2 * n]), out[-1][0, 0]


def _exchange_wait(send_sems, recv_sems, srcs, lands, src_win, dst_win, after, *, name):
    n = len(srcs)

    def body(*refs):
        src, land = refs[:n], refs[n:2 * n]
        send_sems, recv_sems = refs[2 * n], refs[2 * n + 1]
        x, y, c = _place()
        for a in range(n):
            for k, peer in enumerate(_peers(x, y, c)):
                sender = _index(*peer)
                copy = pltpu.make_async_remote_copy(
                    src_ref=src_win[a](src[a], sender), dst_ref=dst_win[a](land[a], sender),
                    send_sem=send_sems.at[a * 7 + k], recv_sem=recv_sems.at[a * 7 + k],
                    device_id=peer, device_id_type=MESH)
                copy.wait_send()
                copy.wait_recv()

    kept = [pltpu.HBM(a.shape, a.dtype) for a in list(srcs) + list(lands)]
    out = pl.pallas_call(
        body, name=name, out_shape=tuple(kept),
        in_specs=[_HBM] * (2 * n) + [_SEM, _SEM] + [pl.BlockSpec(memory_space=pl.ANY)] * len(after),
        out_specs=tuple([_HBM] * (2 * n)),
        input_output_aliases={i: i for i in range(2 * n)},
        compiler_params=pltpu.CompilerParams(has_side_effects=_EFFECT),
    )(*srcs, *lands, send_sems, recv_sems, *after)
    return list(out[:n]), list(out[n:])


COPY_TILE_BYTES = 2 << 20


def _copy_own_block(src, land, idx, *, from_stack, name):
    k, n = land.shape[1:]
    tk = k
    while tk * n * land.dtype.itemsize > COPY_TILE_BYTES and tk % 32 == 0:
        tk //= 2

    def body(idx_ref, src_ref, land_ref, out_ref):
        out_ref[...] = src_ref[...]

    own = pl.BlockSpec((None, tk, n), lambda i, idx_ref: (idx_ref[0], i, 0))
    grid_spec = pltpu.PrefetchScalarGridSpec(
        num_scalar_prefetch=1, grid=(k // tk,),
        in_specs=[own if from_stack else pl.BlockSpec((tk, n), lambda i, idx_ref: (i, 0)),
                  pl.BlockSpec(memory_space=pl.ANY)],
        out_specs=own)
    return pl.pallas_call(
        body, grid_spec=grid_spec, out_shape=jax.ShapeDtypeStruct(land.shape, land.dtype),
        input_output_aliases={2: 0}, name=name,
    )(idx, src, land)


def _place_own(srcs, lands, src_win, dst_win, *, name):
    lands = list(lands)
    idx = _index(*_place()).astype(jnp.int32).reshape(1)
    rest = []
    for a in range(len(srcs)):
        if dst_win[a] is _whole and src_win[a] in (_whole, _shard_itself) and lands[a].ndim == 3:
            lands[a] = _copy_own_block(srcs[a], lands[a], idx, from_stack=src_win[a] is _whole,
                                       name=f"{name}_{a}")
        else:
            rest.append(a)
    if rest:
        placed = _dma_own([srcs[a] for a in rest], [lands[a] for a in rest], [src_win[a] for a in rest],
                          [dst_win[a] for a in rest], name=f"{name}_dma")
        for a, land in zip(rest, placed):
            lands[a] = land
    return lands


def _dma_own(srcs, lands, src_win, dst_win, *, name):
    n = len(srcs)

    def body(*refs):
        src, land = refs[:n], refs[n:2 * n]
        sems = refs[-1]
        my_idx = _index(*_place())
        copies = [pltpu.make_async_copy(src_win[a](src[a], my_idx), dst_win[a](land[a], my_idx), sems.at[a])
                  for a in range(n)]
        for cp in copies:
            cp.start()
        for cp in copies:
            cp.wait()

    hbm = pl.BlockSpec(memory_space=pl.ANY)
    return list(pl.pallas_call(
        body, name=name, in_specs=[hbm] * (2 * n), out_specs=[hbm] * n,
        out_shape=[jax.ShapeDtypeStruct(a.shape, a.dtype) for a in lands],
        input_output_aliases={n + i: i for i in range(n)},
        scratch_shapes=[pltpu.SemaphoreType.DMA((n,))],
    )(*srcs, *lands))


def _ada_fwd(c8, w_ada, b_ada8, *, name):
    depth, d, n = w_ada.shape

    def body(c_ref, w_ref, b_ref, mod_ref, sc_ref, c_all, prod, got, send_sems, recv_sems):
        x, y, c = _place()
        my_idx = _index(x, y, c)
        peers = [(x ^ (k >> 2), y ^ ((k >> 1) & 1), c ^ (k & 1)) for k in range(1, N_DEV)]

        def slab(ref, idx):
            return ref.at[pl.ds(pl.multiple_of(idx * 8, 8), 8)]

        def c_copy(k, peer, sender):
            return pltpu.make_async_remote_copy(
                src_ref=c_ref, dst_ref=slab(c_all, sender), send_sem=send_sems.at[k],
                recv_sem=recv_sems.at[k], device_id=peer, device_id_type=MESH)

        def m_copy(k, peer, sender):
            return pltpu.make_async_remote_copy(
                src_ref=prod.at[_index(*peer)], dst_ref=got.at[sender], send_sem=send_sems.at[7 + k],
                recv_sem=recv_sems.at[7 + k], device_id=peer, device_id_type=MESH)

        sends = [c_copy(k, peer, my_idx) for k, peer in enumerate(peers)]
        for cp in sends:
            cp.start()
        slab(c_all, my_idx)[...] = c_ref[...]
        for k, peer in enumerate(peers):
            c_copy(k, peer, _index(*peer)).wait_recv()
        cv = c_all[...]
        sc = cv * _sigmoid(cv)
        sc_ref[...] = sc
        for layer in range(depth):
            p = jnp.dot(sc, w_ref[layer], preferred_element_type=F32, precision=lax.Precision.HIGHEST)
            for s in range(N_DEV):
                prod[s, layer] = p[8 * s:8 * s + 8]
        sends2 = [m_copy(k, peer, my_idx) for k, peer in enumerate(peers)]
        for cp in sends2:
            cp.start()
        got[my_idx] = prod[my_idx]
        for k, peer in enumerate(peers):
            m_copy(k, peer, _index(*peer)).wait_recv()
        for layer in range(depth):
            for s in range(N_DEV):
                mod_ref[layer, :, s * n:(s + 1) * n] = got[s, layer] + b_ref[layer, :, s * n:(s + 1) * n]
        for cp in sends + sends2:
            cp.wait_send()

    vmem = pl.BlockSpec(memory_space=pltpu.VMEM)
    return pl.pallas_call(
        body, in_specs=[vmem] * 3, out_specs=[vmem] * 2,
        out_shape=[jax.ShapeDtypeStruct((depth, 8, N_DEV * n), F32),
                   jax.ShapeDtypeStruct((8 * N_DEV, d), F32)],
        scratch_shapes=[pltpu.VMEM((8 * N_DEV, d), F32), pltpu.VMEM((N_DEV, depth, 8, n), F32),
                        pltpu.VMEM((N_DEV, depth, 8, n), F32),
                        pltpu.SemaphoreType.DMA((14,)), pltpu.SemaphoreType.DMA((14,))],
        name=name, compiler_params=pltpu.CompilerParams(vmem_limit_bytes=56 << 20),
    )(c8, w_ada, b_ada8)


def _ada_bwd(sc8, dmod, *, name):
    depth, _, n = dmod.shape
    d = sc8.shape[1]

    def body(sc_ref, dm_ref, o_ref):
        for layer in range(depth):
            o_ref[layer] = lax.dot_general(sc_ref[...], dm_ref[layer], (((0,), (0,)), ((), ())),
                                           preferred_element_type=F32, precision=lax.Precision.HIGHEST)

    vmem = pl.BlockSpec(memory_space=pltpu.VMEM)
    return pl.pallas_call(
        body, in_specs=[vmem] * 2, out_specs=vmem, out_shape=jax.ShapeDtypeStruct((depth, d, n), F32),
        name=name, compiler_params=pltpu.CompilerParams(vmem_limit_bytes=40 << 20),
    )(sc8, dmod)


def _gather_small(vec, *, name):
    r = vec.shape[0]

    def body(v_ref, o_ref, send_sems, recv_sems):
        x, y, c = _place()
        my_idx = _index(x, y, c)
        peers = [(x ^ (k >> 2), y ^ ((k >> 1) & 1), c ^ (k & 1)) for k in range(1, N_DEV)]

        def copy(k, peer, sender):
            return pltpu.make_async_remote_copy(
                src_ref=v_ref, dst_ref=o_ref.at[sender], send_sem=send_sems.at[k],
                recv_sem=recv_sems.at[k], device_id=peer, device_id_type=MESH)

        sends = [copy(k, peer, my_idx) for k, peer in enumerate(peers)]
        for cp in sends:
            cp.start()
        o_ref[my_idx] = v_ref[...]
        for k, peer in enumerate(peers):
            copy(k, peer, _index(*peer)).wait_recv()
        for cp in sends:
            cp.wait_send()

    vmem = pl.BlockSpec(memory_space=pltpu.VMEM)
    return pl.pallas_call(
        body, in_specs=[vmem], out_specs=vmem, out_shape=jax.ShapeDtypeStruct((N_DEV, r, LANE), F32),
        scratch_shapes=[pltpu.SemaphoreType.DMA((7,)), pltpu.SemaphoreType.DMA((7,))], name=name,
    )(vec)


TM = 512
TM_WIDE = 2048
TMM_DEEP = 1024
TQ = 512


def _layer_fwd(x, mod, small, gw, *, d, tag):
    shift, scale, res_gate = mod
    h = _prenorm(x, small["norm_g"], scale, shift, tm=TM, name=f"prenorm_{tag}")
    proj = _mm_nn(h, gw["w_in"], out_dtype=BF16, tm=TM, name=f"in_proj_{tag}")
    gl = _mm_nn(h, gw["w_gate"], out_dtype=BF16, tm=TM, bias=small["b_gate"], name=f"gate_proj_{tag}")
    ya = _pool_fwd(proj, gw["pool_w"], small["pool_scale"], d=d, tm=TM, name=f"pool_fwd_{tag}")
    o, yb, tot = _attn_fwd(proj, d=d, tq=TQ, name=f"attn_fwd_{tag}")
    yc = _conv_fwd(proj, gw["conv_w"], d=d, tm=TM, cw=LANE * 2, name=f"conv_fwd_{tag}")
    nblk = gw["w_br_a"].shape[0]
    bst = _mm_nn(ya, gw["w_br_a"], out_dtype=BF16, tm=TM_WIDE, out_cols=3 * d, name=f"branch_a_{tag}")
    bst = _mm_nn(yb, gw["w_br_b"], out_dtype=BF16, tm=TM, out=bst, out_col_block=1, name=f"branch_b_{tag}")
    bst = _mm_nn(yc, gw["w_br_c"], out_dtype=BF16, tm=TM_WIDE, out=bst, out_col_block=2 * nblk,
                 name=f"branch_c_{tag}")
    merged = _merge_fwd(gl, bst, d=d, tm=TM, tn=TM, name=f"merge_fwd_{tag}")
    x_new, mo = _out_fwd(merged, gw["w_out"][0], x, res_gate, tm=TM, name=f"out_fwd_{tag}")
    return x_new, dict(x=x, h=h, proj=proj, gl=gl, ya=ya, yb=yb, yc=yc, o=o, tot=tot, bst=bst, merged=merged,
                       mo=mo)


EARLY = ("w_gate", "w_br_a", "w_br_b", "w_br_c", "w_out")
LATE = ("w_in", "pool_w")


def _layer_bwd(dout, sv, mod, small, gw, *, d, tag, send_early=None, send_late=None):
    shift, scale, res_gate = mod
    nblk = gw["w_br_a"].shape[0]
    dmo, d_res_gate = _out_bwd(dout, sv["mo"], res_gate, tm=TM, name=f"out_bwd_{tag}")
    g_w_out = _mm_tn(sv["merged"], dmo, nb=1, n=d, tk=TM, tmm=TMM_DEEP, name=f"dw_out_{tag}")
    dmerged = _mm_nt(dmo, gw["w_out"], out_dtype=BF16, tm=TM, name=f"dmerged_{tag}")
    dbst, dgl, d_b_gate = _merge_bwd(dmerged, sv["gl"], sv["bst"], d=d, tm=TM, tn=TM, name=f"merge_bwd_{tag}")
    g_w_gate = _mm_tn(sv["h"], dgl, nb=nblk, n=3 * d // nblk, tk=d // 2, tmm=TMM_DEEP,
                      name=f"dw_gate_{tag}")
    dh = _mm_nt(dgl, gw["w_gate"], out_dtype=F32, tm=TM, name=f"dh_gate_{tag}")
    g_w_br_a = _mm_tn(sv["ya"], dbst, nb=nblk, n=d // nblk, tk=d // 2, tmm=TM_WIDE,
                      name=f"dw_br_a_{tag}")
    g_w_br_b = _mm_tn(sv["yb"], dbst, nb=1, n=d, tk=TM, tmm=TMM_DEEP, col_block=1, name=f"dw_br_b_{tag}")
    g_w_br_c = _mm_tn(sv["yc"], dbst, nb=nblk, n=d // nblk, tk=d // 2, tmm=TM_WIDE, col_block=2 * nblk,
                      name=f"dw_br_c_{tag}")
    dya = _mm_nt(dbst, gw["w_br_a"], out_dtype=BF16, tm=TM_WIDE, name=f"dya_{tag}")
    dyb = _mm_nt(dbst, gw["w_br_b"], out_dtype=BF16, tm=TM, col_block=1, name=f"dyb_{tag}")
    dyc = _mm_nt(dbst, gw["w_br_c"], out_dtype=BF16, tm=TM_WIDE, col_block=2 * nblk, name=f"dyc_{tag}")
    big = dict(w_gate=g_w_gate, w_br_a=g_w_br_a, w_br_b=g_w_br_b, w_br_c=g_w_br_c, w_out=g_w_out)
    if send_early is not None:
        token = send_early(big)
        small = dict(small, pool_scale=small["pool_scale"] + token, norm_g=small["norm_g"] + token)
        dyb = dyb + token.astype(BF16)
    dxa, dza, g_pool_w, d_pool_scale = _pool_bwd(dya, sv["proj"], gw["pool_w"], small["pool_scale"], d=d, tm=TM,
                                                 name=f"pool_bwd_{tag}")
    dq, dk, dv, dzb = _attn_bwd(dyb, sv["o"], sv["tot"], sv["proj"], d=d, tq=TQ, name=f"attn_bwd_{tag}")
    du, dbg, dcg, dzc, d_conv_w = _conv_bwd(dyc, sv["proj"], gw["conv_w"], d=d, tm=TM, cw=LANE * 2,
                                            name=f"conv_bwd_{tag}")
    dproj = jnp.concatenate([dxa, dza, dq, dk, dv, dzb, du, dbg, dcg, dzc], axis=1)
    g_w_in = _mm_tn(sv["h"], dproj, nb=nblk, n=7 * d // nblk, tk=d // 2, tmm=TMM_DEEP,
                    name=f"dw_in_{tag}")
    big.update(w_in=g_w_in, pool_w=g_pool_w.astype(BF16))
    if send_late is not None:
        dh = dh + send_late(big)
    dh = _mm_nt(dproj, gw["w_in"], out_dtype=F32, tm=TM, addend=dh, name=f"dh_in_{tag}")
    dx, d_norm_g, d_scale, d_shift = _prenorm_bwd(dh, sv["x"], dout, small["norm_g"], scale, tm=TM,
                                                  name=f"prenorm_bwd_{tag}")
    part = dict(norm_g=d_norm_g, mod=jnp.concatenate([d_shift, d_scale, d_res_gate], axis=1),
                pool_scale=d_pool_scale, b_gate=d_b_gate, conv_w=d_conv_w[:3])
    return dx, big, part


BIG = ("w_in", "w_gate", "w_br_a", "w_br_b", "w_br_c", "w_out", "pool_w")
SMALL_PER_LAYER = ("norm_g", "mod", "pool_scale", "b_gate", "conv_w")


def _gather_layer(w, layer, *, d):
    nd = N_DEV
    gd = d // 2 // N_POOL_GROUPS
    cc = d // 2
    shards = [w[k][layer].astype(BF16) for k in BIG]
    shards.append(jnp.pad(w["conv_w"][layer], ((0, 5), (0, 0))))
    blocks = [jax.ShapeDtypeStruct((nd,) + s.shape, BF16) for s in shards[:6]]
    blocks.append(jax.ShapeDtypeStruct((N_POOL_GROUPS, gd, gd), BF16))
    blocks.append(jax.ShapeDtypeStruct((8, cc), F32))
    placers = [_whole] * 6 + [_row_block(gd // nd), _lane_block(cc // nd)]
    return shards, blocks, placers


def _as_weights(got, d):
    gw = dict(zip(BIG + ("conv_w",), got))
    gw["w_br_b"] = gw["w_br_b"].reshape(1, d, d)
    gw["w_out"] = gw["w_out"].reshape(1, d, d)
    return gw


def _shard_itself(ref, idx):
    return ref


def _scatter_spec(big, keys, *, d):
    nd = N_DEV
    gd = d // 2 // N_POOL_GROUPS
    grads, pickers, shapes = [], [], []
    for k in keys:
        g = big[k]
        if k == "pool_w":
            pickers.append(_row_block(gd // nd))
            shapes.append(jax.ShapeDtypeStruct((nd, N_POOL_GROUPS, gd // nd, gd), BF16))
        else:
            if g.shape[0] == 1:
                g = g.reshape(nd, g.shape[1] // nd, g.shape[2])
            pickers.append(_whole)
            shapes.append(jax.ShapeDtypeStruct(g.shape, BF16))
        grads.append(g)
    return grads, pickers, shapes


class _Behind:
    def __init__(self, srcs, land_shapes, src_win, dst_win, after, name):
        self.src_win, self.dst_win, self.name = src_win, dst_win, name
        self.send, self.recv, self.srcs, self.lands, self.token = _exchange_start(
            srcs, land_shapes, src_win, dst_win, after, name=f"{name}_start")

    def finish(self, after):
        srcs, lands = _exchange_wait(self.send, self.recv, self.srcs, self.lands, self.src_win, self.dst_win,
                                     after, name=f"{self.name}_wait")
        return _place_own(srcs, lands, self.src_win, self.dst_win, name=f"{self.name}_own")


def _pack(pieces):
    return jnp.concatenate([p.reshape(-1, LANE) for p in pieces], axis=0)


def kernel(x, c, norm_g, w_ada, b_ada, w_in, pool_w, pool_scale, conv_w, w_br_a, w_br_b, w_br_c, w_gate, b_gate, w_out, final_g, loss_target, m_norm_g, m_w_ada, m_b_ada, m_w_in, m_pool_w, m_pool_scale, m_conv_w, m_w_br_a, m_w_br_b, m_w_br_c, m_w_gate, m_b_gate, m_w_out, m_final_g, v_norm_g, v_w_ada, v_b_ada, v_w_in, v_pool_w, v_pool_scale, v_conv_w, v_w_br_a, v_w_br_b, v_w_br_c, v_w_gate, v_b_gate, v_w_out, v_final_g):
    names = ("norm_g", "w_ada", "b_ada", "w_in", "pool_w", "pool_scale", "conv_w", "w_br_a", "w_br_b",
             "w_br_c", "w_gate", "b_gate", "w_out", "final_g")
    w = dict(zip(names, (norm_g, w_ada, b_ada, w_in, pool_w, pool_scale, conv_w, w_br_a, w_br_b, w_br_c,
                         w_gate, b_gate, w_out, final_g)))
    m = dict(zip(names, (m_norm_g, m_w_ada, m_b_ada, m_w_in, m_pool_w, m_pool_scale, m_conv_w, m_w_br_a,
                         m_w_br_b, m_w_br_c, m_w_gate, m_b_gate, m_w_out, m_final_g)))
    v = dict(zip(names, (v_norm_g, v_w_ada, v_b_ada, v_w_in, v_pool_w, v_pool_scale, v_conv_w, v_w_br_a,
                         v_w_br_b, v_w_br_c, v_w_gate, v_b_gate, v_w_out, v_final_g)))
    _, t, d = x.shape
    depth = norm_g.shape[0]
    cc = d // 2
    my_idx = _index(*_place())

    mod8, sc_all = _ada_fwd(jnp.broadcast_to(c, (8, d)), w_ada,
                            jnp.broadcast_to(b_ada[:, None, :], (depth, 8, 3 * d)), name="ada_fwd")
    mods = [tuple(mod8[l, 0:1, k * d:(k + 1) * d] for k in range(3)) for l in range(depth)]
    small = [dict(norm_g=norm_g[l][None], b_gate=b_gate[l][None], pool_scale=pool_scale[l][None])
             for l in range(depth)]

    shards, blocks, placers = _gather_layer(w, 0, d=d)
    gws = [_as_weights(_all_gather(shards, blocks, placers, name="gather_weights_l0"), d)]
    xs = x[0]
    saved = []
    for l in range(depth):
        coming = None
        if l + 1 < depth:
            shards, blocks, placers = _gather_layer(w, l + 1, d=d)
            coming = _Behind(shards, blocks, [_shard_itself] * len(shards), placers, (mod8, gws[l]["conv_w"]),
                             f"gather_weights_l{l + 1}")
            shift, scale, res_gate = mods[l]
            mods[l] = (shift + coming.token, scale, res_gate)
        xs, sv = _layer_fwd(xs, mods[l], small[l], gws[l], d=d, tag=f"l{l}")
        saved.append(sv)
        if coming is not None:
            gws.append(_as_weights(coming.finish((xs,)), d))
    dx, loss8, d_final_g = _loss_head(xs, loss_target[0], final_g[None], tm=TM, name="loss_head")
    loss = lax.psum(loss8[0, 0], ("x", "y", "c"))

    parts, small_parts = [dict() for _ in range(depth)], [None] * depth
    going, last = [], []

    def send(queue, keys, l, name):
        def start(grads_so_far):
            grads, pickers, shapes = _scatter_spec(grads_so_far, keys, d=d)
            queue.append((l, keys, _Behind(grads, shapes, pickers, [_whole] * len(grads), (grads[0],), name)))
            return queue[-1][2].token
        return start

    for l in reversed(range(depth)):
        if going:
            shift, scale, res_gate = mods[l]
            mods[l] = (shift, scale, res_gate + going[-1][2].token)
        lowest = l == 0
        dx, big, small_parts[l] = _layer_bwd(
            dx, saved[l], mods[l], small[l], gws[l], d=d, tag=f"l{l}",
            send_early=send(going, EARLY, l, f"scatter_early_l{l}") if lowest else None,
            send_late=send(last, LATE, l, f"scatter_late_l{l}") if lowest else None)
        if not lowest:
            send(going, BIG, l, f"scatter_grads_l{l}")(big)
    for l, keys, behind in going:
        parts[l].update(zip(keys, behind.finish((dx,))))

    pieces = [small_parts[l][k] for l in range(depth) for k in SMALL_PER_LAYER] + [d_final_g]
    sizes = [p.size for p in pieces]
    gathered = _gather_small(_pack(pieces), name="gather_small_grads")
    zero_conv = jnp.zeros((3, cc), F32)

    def packed(src):
        per_layer = lambda l: [src["norm_g"][l], src["b_ada"][l], src["pool_scale"][l], src["b_gate"][l],
                               zero_conv]
        return _pack([p for l in range(depth) for p in per_layer(l)] + [src["final_g"]])[None]

    small_out = _adamw(packed(w), packed(m), packed(v), gathered, 0, None, tr=gathered.shape[1],
                       name="adamw_small")

    def unpack(flat):
        flat = flat.reshape(-1)
        out, off = [], 0
        for n in sizes:
            out.append(flat[off:off + n])
            off += n
        return out

    small_res = [unpack(a) for a in small_out]
    k_per = len(SMALL_PER_LAYER)

    def small_stack(which, pos, shape):
        return jnp.stack([small_res[which][l * k_per + pos] for l in range(depth)]).reshape(shape)

    res = {}
    for pos, name in ((0, "norm_g"), (1, "b_ada"), (2, "pool_scale"), (3, "b_gate")):
        res[name] = tuple(small_stack(k, pos, w[name].shape) for k in range(4))
    res["final_g"] = tuple(small_res[k][-1].reshape(final_g.shape) for k in range(4))

    ncol = cc // N_DEV
    conv_total = small_stack(0, 4, (depth * 3, cc))
    conv_mine = lax.dynamic_slice_in_dim(conv_total, my_idx * ncol, ncol, axis=1)
    pad8 = lambda a: jnp.pad(a.reshape(depth * 3, ncol), ((0, 8 - depth * 3), (0, 0)))[None]
    conv_out = _adamw(pad8(conv_w), pad8(m["conv_w"]), pad8(v["conv_w"]), pad8(conv_mine), 0, None, tr=8,
                      name="adamw_conv_w")
    res["conv_w"] = tuple(a[0, :depth * 3].reshape(conv_w.shape) for a in conv_out)

    n_ada = 3 * d // N_DEV
    mod_off = [sum(sizes[:l * k_per + 1]) // LANE for l in range(depth)]
    dmod = jnp.stack([
        lax.dynamic_slice_in_dim(gathered[:, mod_off[l]:mod_off[l] + 3 * d // LANE].reshape(N_DEV, 3 * d),
                                 my_idx * n_ada, n_ada, axis=1) for l in range(depth)])
    g_w_ada = _ada_bwd(sc_all[::8], dmod, name="ada_bwd")
    res["w_ada"] = _adamw_layers(w_ada, m["w_ada"], v["w_ada"], [g_w_ada[l][None] for l in range(depth)],
                                 tr=128, name="adamw_w_ada")

    def big_adamw(name):
        out = _adamw_layers(w[name], m[name], v[name], [parts[l][name] for l in range(depth)], tr=128,
                            name=f"adamw_{name}")
        res[name] = tuple(a.reshape(w[name].shape) for a in out)

    for name in EARLY:
        big_adamw(name)
    for l, keys, behind in last:
        parts[l].update(zip(keys, behind.finish(tuple(res[name][1] for name in ("w_ada",) + EARLY))))
    for name in LATE:
        big_adamw(name)

    outs = [loss, dx[None]]
    for k in range(4):
        outs += [res[name][k] for name in names]
    return tuple(outs)
```

```python
import functools

import jax
import jax.numpy as jnp
from jax import lax
from jax.experimental import pallas as pl
from jax.experimental.pallas import tpu as pltpu

F32 = jnp.float32
BF16 = jnp.bfloat16
MESH = pl.DeviceIdType.MESH

N_DEV = 8
DEPTH = 2
HEAD_DIM = 128
N_POOL_GROUPS = 4
POOL_WINDOWS = (2, 4, 8, 16)
RMS_EPS = 1e-6
LANE = 128
HALO = 16
KEY_CHUNK = 128
KEY_BLOCK = 512
ADAM_LR = 0.001
ADAM_B1 = 0.9
ADAM_B2 = 0.999
ADAM_EPS = 1e-08
ADAM_WD = 0.01
ADAM_STEP = 10


def _cparams(semantics, vmem_mb):
    return pltpu.CompilerParams(dimension_semantics=semantics, vmem_limit_bytes=vmem_mb << 20)


def _sigmoid(z):
    return 1.0 / (1.0 + jnp.exp(-z))


def _mm_nn(a, w, *, out_dtype, tm, name, bias=None, out=None, out_cols=None, out_col_block=0):
    m, k = a.shape
    nb, _, n = w.shape
    tm = min(tm, m)
    total_cols =out.shape[1] if out is not None else (nb * n if out_cols is None else out_cols)

    def body(*refs):
        if out is not None:
            refs = refs[:-2] + refs[-1:]
        if bias is not None:
            a_ref, w_ref, b_ref, o_ref = refs
        else:
            a_ref, w_ref, o_ref = refs
        acc = jnp.dot(a_ref[...], w_ref[...], preferred_element_type=F32)
        if bias is not None:
            acc = acc + b_ref[...]
        o_ref[...] = acc.astype(out_dtype)

    in_specs = [pl.BlockSpec((tm, k), lambda j, i: (i, 0)),
                pl.BlockSpec((None, k, n), lambda j, i: (j, 0, 0))]
    args = [a, w]
    if bias is not None:
        in_specs.append(pl.BlockSpec((1, n), lambda j, i: (0, j)))
        args.append(bias)
    aliases = {}
    if out is not None:
        in_specs.append(pl.BlockSpec(memory_space=pl.ANY))
        aliases = {len(args): 0}
        args.append(out)
    return pl.pallas_call(
        body, grid=(nb, m // tm), in_specs=in_specs,
        out_specs=pl.BlockSpec((tm, n), lambda j, i: (i, out_col_block + j)),
        out_shape=jax.ShapeDtypeStruct((m, total_cols), out_dtype),
        input_output_aliases=aliases, name=name,
        compiler_params=_cparams(("arbitrary", "arbitrary"), 56),
    )(*args)


def _mm_nt(dy, w, *, out_dtype, tm, name, addend=None, col_block=0):
    m = dy.shape[0]
    nb, k, n = w.shape
    tm = min(tm, m)

    def body(*refs):
        if addend is not None:
            dy_ref, w_ref, add_ref, o_ref, acc_ref = refs
        else:
            dy_ref, w_ref, o_ref, acc_ref = refs
        j = pl.program_id(1)
        part = lax.dot_general(dy_ref[...], w_ref[...], (((1,), (1,)), ((), ())),
                               preferred_element_type=F32)

        @pl.when(j == 0)
        def _():
            acc_ref[...] = part if addend is None else part + add_ref[...]

        @pl.when(j > 0)
        def _():
            acc_ref[...] += part

        @pl.when(j == nb - 1)
        def _():
            o_ref[...] = acc_ref[...].astype(out_dtype)

    in_specs = [pl.BlockSpec((tm, n), lambda i, j: (i, col_block + j)),
                pl.BlockSpec((None, k, n), lambda i, j: (j, 0, 0))]
    args = [dy, w]
    if addend is not None:
        in_specs.append(pl.BlockSpec((tm, k), lambda i, j: (i, 0)))
        args.append(addend)
    return pl.pallas_call(
        body, grid=(m // tm, nb), in_specs=in_specs,
        out_specs=pl.BlockSpec((tm, k), lambda i, j: (i, 0)),
        out_shape=jax.ShapeDtypeStruct((m, k), out_dtype),
        scratch_shapes=[pltpu.VMEM((tm, k), F32)], name=name,
        compiler_params=_cparams(("arbitrary", "arbitrary"), 56),
    )(*args)


def _mm_tn(a, dy, *, nb, n, tk, tmm, name, col_block=0, out_dtype=BF16):
    m, k = a.shape
    tmm = min(tmm, m)

    def body(a_ref, dy_ref, o_ref, acc_ref):
        t = pl.program_id(2)
        part = lax.dot_general(a_ref[...], dy_ref[...], (((0,), (0,)), ((), ())),
                               preferred_element_type=F32)

        @pl.when(t == 0)
        def _():
            acc_ref[...] = part

        @pl.when(t > 0)
        def _():
            acc_ref[...] += part

        @pl.when(t == pl.num_programs(2) - 1)
        def _():
            o_ref[...] = acc_ref[...].astype(out_dtype)

    return pl.pallas_call(
        body, grid=(nb, k // tk, m // tmm),
        in_specs=[pl.BlockSpec((tmm, tk), lambda j, kb, t: (t, kb)),
                  pl.BlockSpec((tmm, n), lambda j, kb, t: (t, col_block + j))],
        out_specs=pl.BlockSpec((None, tk, n), lambda j, kb, t: (j, kb, 0)),
        out_shape=jax.ShapeDtypeStruct((nb, k, n), out_dtype),
        scratch_shapes=[pltpu.VMEM((tk, n), F32)], name=name,
        compiler_params=_cparams(("arbitrary", "arbitrary", "arbitrary"), 56),
    )(a, dy)


def _prenorm(x, g, scale, shift, *, tm, name):
    t, d = x.shape

    def body(x_ref, g_ref, sc_ref, sh_ref, h_ref):
        xv = x_ref[...]
        r = lax.rsqrt(jnp.mean(xv * xv, axis=-1, keepdims=True) + RMS_EPS)
        h_ref[...] = ((xv * r) * g_ref[...] * (1.0 + sc_ref[...]) + sh_ref[...]).astype(BF16)

    row = pl.BlockSpec((1, d), lambda i: (0, 0))
    tile = pl.BlockSpec((tm, d), lambda i: (i, 0))
    return pl.pallas_call(
        body, grid=(t // tm,), in_specs=[tile, row, row, row], out_specs=tile,
        out_shape=jax.ShapeDtypeStruct((t, d), BF16), name=name,
        compiler_params=_cparams(("arbitrary",), 40),
    )(x, g, scale, shift)


def _prenorm_bwd(dh, x, dout, g, scale, *, tm, name):
    t, d = x.shape

    def body(dh_ref, x_ref, do_ref, g_ref, sc_ref, dx_ref, dg_ref, dsc_ref, dsh_ref):
        i = pl.program_id(0)
        xv = x_ref[...]
        dhv = dh_ref[...]
        r = lax.rsqrt(jnp.mean(xv * xv, axis=-1, keepdims=True) + RMS_EPS)
        xn = xv * r
        gain = g_ref[...] * (1.0 + sc_ref[...])
        dxn = dhv * gain
        dx_ref[...] = do_ref[...] + r * (dxn - xn * jnp.mean(dxn * xn, axis=-1, keepdims=True))
        dhxn = dhv * xn
        dg = jnp.sum(dhxn * (1.0 + sc_ref[...]), axis=0, keepdims=True)
        dsc = jnp.sum(dhxn * g_ref[...], axis=0, keepdims=True)
        dsh = jnp.sum(dhv, axis=0, keepdims=True)

        @pl.when(i == 0)
        def _():
            dg_ref[...] = dg
            dsc_ref[...] = dsc
            dsh_ref[...] = dsh

        @pl.when(i > 0)
        def _():
            dg_ref[...] += dg
            dsc_ref[...] += dsc
            dsh_ref[...] += dsh

    row = pl.BlockSpec((1, d), lambda i: (0, 0))
    tile = pl.BlockSpec((tm, d), lambda i: (i, 0))
    vec = jax.ShapeDtypeStruct((1, d), F32)
    return pl.pallas_call(
        body, grid=(t // tm,), in_specs=[tile, tile, tile, row, row],
        out_specs=[tile, row, row, row],
        out_shape=[jax.ShapeDtypeStruct((t, d), F32), vec, vec, vec], name=name,
        compiler_params=_cparams(("arbitrary",), 48),
    )(dh, x, dout, g, scale)


def _out_fwd(merged, w_out, x, res_gate, *, tm, name):
    t, d = x.shape
    k = merged.shape[1]

    def body(m_ref, w_ref, x_ref, rg_ref, xo_ref, mo_ref):
        mo = jnp.dot(m_ref[...], w_ref[...], preferred_element_type=F32)
        mo_ref[...] = mo.astype(BF16)
        xo_ref[...] = x_ref[...] + rg_ref[...] * mo

    tile = pl.BlockSpec((tm, d), lambda i: (i, 0))
    return pl.pallas_call(
        body, grid=(t // tm,),
        in_specs=[pl.BlockSpec((tm, k), lambda i: (i, 0)), pl.BlockSpec((k, d), lambda i: (0, 0)),
                  tile, pl.BlockSpec((1, d), lambda i: (0, 0))],
        out_specs=[tile, tile],
        out_shape=[jax.ShapeDtypeStruct((t, d), F32), jax.ShapeDtypeStruct((t, d), BF16)], name=name,
        compiler_params=_cparams(("arbitrary",), 56),
    )(merged, w_out, x, res_gate)


def _out_bwd(dout, mo, res_gate, *, tm, name):
    t, d = dout.shape

    def body(do_ref, mo_ref, rg_ref, dmo_ref, drg_ref):
        i = pl.program_id(0)
        dov = do_ref[...]
        dmo_ref[...] = (dov * rg_ref[...]).astype(BF16)
        drg = jnp.sum(dov * mo_ref[...].astype(F32), axis=0, keepdims=True)

        @pl.when(i == 0)
        def _():
            drg_ref[...] = drg

        @pl.when(i > 0)
        def _():
            drg_ref[...] += drg

    row = pl.BlockSpec((1, d), lambda i: (0, 0))
    tile = pl.BlockSpec((tm, d), lambda i: (i, 0))
    return pl.pallas_call(
        body, grid=(t // tm,), in_specs=[tile, tile, row], out_specs=[tile, row],
        out_shape=[jax.ShapeDtypeStruct((t, d), BF16), jax.ShapeDtypeStruct((1, d), F32)], name=name,
        compiler_params=_cparams(("arbitrary",), 40),
    )(dout, mo, res_gate)


def _loss_head(x, target, final_g, *, tm, name):
    t, d = x.shape

    def body(x_ref, t_ref, g_ref, dx_ref, loss_ref, dg_ref):
        i = pl.program_id(0)
        xv = x_ref[...]
        r = lax.rsqrt(jnp.mean(xv * xv, axis=-1, keepdims=True) + RMS_EPS)
        xn = xv * r
        err = xn * g_ref[...] - t_ref[...]
        part = (0.5 / d) * jnp.sum(jnp.sum(err * err, axis=1, keepdims=True), axis=0, keepdims=True)
        dy = err * (1.0 / d)
        dxn = dy * g_ref[...]
        dx_ref[...] = r * (dxn - xn * jnp.mean(dxn * xn, axis=-1, keepdims=True))
        dg = jnp.sum(dy * xn, axis=0, keepdims=True)

        @pl.when(i == 0)
        def _():
            loss_ref[...] = jnp.zeros_like(loss_ref) + part
            dg_ref[...] = dg

        @pl.when(i > 0)
        def _():
            loss_ref[...] += part
            dg_ref[...] += dg

    row = pl.BlockSpec((1, d), lambda i: (0, 0))
    tile = pl.BlockSpec((tm, d), lambda i: (i, 0))
    return pl.pallas_call(
        body, grid=(t // tm,), in_specs=[tile, tile, row],
        out_specs=[tile, pl.BlockSpec((8, LANE), lambda i: (0, 0)), row],
        out_shape=[jax.ShapeDtypeStruct((t, d), F32), jax.ShapeDtypeStruct((8, LANE), F32),
                   jax.ShapeDtypeStruct((1, d), F32)], name=name,
        compiler_params=_cparams(("arbitrary",), 40),
    )(x, target, final_g)


def _cur(tm, cw, col):
    return pl.BlockSpec((tm, cw), lambda cb, i: (i, col + cb))


def _prev(tm, cw, col):
    return pl.BlockSpec((HALO, cw), lambda cb, i: (jnp.maximum(i * (tm // HALO) - 1, 0), col + cb))


def _next(tm, cw, col, t):
    last = t // HALO - 1
    return pl.BlockSpec((HALO, cw), lambda cb, i: (jnp.minimum((i + 1) * (tm // HALO), last), col + cb))


def _with_prev(prev_ref, cur, first):
    prev = jnp.where(first, 0.0, prev_ref[...].astype(F32))
    return jnp.concatenate([prev, cur], axis=0)


def _with_next(cur, next_ref, last):
    nxt = jnp.where(last, 0.0, next_ref[...].astype(F32))
    return jnp.concatenate([cur, nxt], axis=0)


def _pool_mixed(xe, cur, g, row0, tm):
    s2 = xe + pltpu.roll(xe, 1, 0)
    s4 = s2 + pltpu.roll(s2, 2, 0)
    s8 = s4 + pltpu.roll(s4, 4, 0)
    s16 = s8 + pltpu.roll(s8, 8, 0)
    win = jnp.where(g == 0, s2, jnp.where(g == 1, s4, jnp.where(g == 2, s8, s16)))[HALO:]
    return win * _pool_inv_count(g, row0, tm, cur.shape[1]) - cur


def _pool_inv_count(g, row0, rows, cols):
    tpos = row0 + lax.broadcasted_iota(jnp.int32, (rows, cols), 0)
    width = jnp.left_shift(2, g)
    return 1.0 / jnp.minimum(tpos + 1, width).astype(F32)


def _pool_fwd(proj, pool_w, pool_scale, *, d, tm, name):
    t = proj.shape[0]
    pw = d // 2
    gd = pw // N_POOL_GROUPS
    za_col = pw // gd

    def body(xa_ref, xp_ref, za_ref, w_ref, ps_ref, ya_ref):
        g = pl.program_id(0)
        i = pl.program_id(1)
        cur = xa_ref[...].astype(F32)
        mixed = _pool_mixed(_with_prev(xp_ref, cur, i == 0), cur, g, i * tm, tm)
        ypre = jnp.dot(mixed.astype(BF16), w_ref[...], preferred_element_type=F32)
        za = za_ref[...].astype(F32)
        ya_ref[...] = (ypre * ps_ref[...] * (za * _sigmoid(za))).astype(BF16)

    return pl.pallas_call(
        body, grid=(N_POOL_GROUPS, t // tm),
        in_specs=[_cur(tm, gd, 0), _prev(tm, gd, 0), _cur(tm, gd, za_col),
                  pl.BlockSpec((None, gd, gd), lambda g, i: (g, 0, 0)),
                  pl.BlockSpec((1, gd), lambda g, i: (0, g))],
        out_specs=pl.BlockSpec((tm, gd), lambda g, i: (i, g)),
        out_shape=jax.ShapeDtypeStruct((t, pw), BF16), name=name,
        compiler_params=_cparams(("arbitrary", "arbitrary"), 40),
    )(proj, proj, proj, pool_w, pool_scale)


def _pool_bwd(dya, proj, pool_w, pool_scale, *, d, tm, name):
    t = proj.shape[0]
    pw = d // 2
    gd = pw // N_POOL_GROUPS
    za_col = pw // gd
    n_i = t // tm

    def body(xa_ref, xp_ref, za_ref, zn_ref, dy_ref, dyn_ref, w_ref, ps_ref,
             dxa_ref, dza_ref, dw_ref, dps_ref):
        g = pl.program_id(0)
        i = pl.program_id(1)
        last = i == n_i - 1
        cur = xa_ref[...].astype(F32)
        mixed = _pool_mixed(_with_prev(xp_ref, cur, i == 0), cur, g, i * tm, tm)
        mixed16 = mixed.astype(BF16)
        ypre = jnp.dot(mixed16, w_ref[...], preferred_element_type=F32)
        za = za_ref[...].astype(F32)
        sg = _sigmoid(za)
        dy = dy_ref[...].astype(F32)
        dza_ref[...] = (dy * (ypre * ps_ref[...]) * (sg * (1.0 + za * (1.0 - sg)))).astype(BF16)
        dysc = dy * (za * sg)
        za_e = _with_next(za, zn_ref, last)
        dy_e = _with_next(dy, dyn_ref, last)
        dypre_e = (dy_e * (za_e * _sigmoid(za_e)) * ps_ref[...]).astype(BF16)
        dm_e = lax.dot_general(dypre_e, w_ref[...], (((1,), (1,)), ((), ())), preferred_element_type=F32)
        rows = tm + HALO
        q = dm_e * _pool_inv_count(g, i * tm, rows, gd)
        f2 = q + pltpu.roll(q, rows - 1, 0)
        f4 = f2 + pltpu.roll(f2, rows - 2, 0)
        f8 = f4 + pltpu.roll(f4, rows - 4, 0)
        f16 = f8 + pltpu.roll(f8, rows - 8, 0)
        ahead = jnp.where(g == 0, f2, jnp.where(g == 1, f4, jnp.where(g == 2, f8, f16)))
        dxa_ref[...] = (ahead[:tm] - dm_e[:tm]).astype(BF16)
        dw = lax.dot_general(mixed16, dypre_e[:tm], (((0,), (0,)), ((), ())), preferred_element_type=F32)
        dps = jnp.sum(dysc * ypre, axis=0, keepdims=True)

        @pl.when(i == 0)
        def _():
            dw_ref[...] = dw
            dps_ref[...] = dps

        @pl.when(i > 0)
        def _():
            dw_ref[...] += dw
            dps_ref[...] += dps

    out_tile = pl.BlockSpec((tm, gd), lambda g, i: (i, g))
    return pl.pallas_call(
        body, grid=(N_POOL_GROUPS, n_i),
        in_specs=[_cur(tm, gd, 0), _prev(tm, gd, 0), _cur(tm, gd, za_col), _next(tm, gd, za_col, t),
                  _cur(tm, gd, 0), _next(tm, gd, 0, t),
                  pl.BlockSpec((None, gd, gd), lambda g, i: (g, 0, 0)),
                  pl.BlockSpec((1, gd), lambda g, i: (0, g))],
        out_specs=[out_tile, out_tile, pl.BlockSpec((None, gd, gd), lambda g, i: (g, 0, 0)),
                   pl.BlockSpec((1, gd), lambda g, i: (0, g))],
        out_shape=[jax.ShapeDtypeStruct((t, pw), BF16), jax.ShapeDtypeStruct((t, pw), BF16),
                   jax.ShapeDtypeStruct((N_POOL_GROUPS, gd, gd), F32), jax.ShapeDtypeStruct((1, pw), F32)],
        name=name, compiler_params=_cparams(("arbitrary", "arbitrary"), 48),
    )(proj, proj, proj, proj, dya, dya, pool_w, pool_scale)


def _conv_cols(d, cw):
    cc = d // 2
    base = (d + 4 * d) // cw
    return base, base + cc // cw, base + 2 * (cc // cw), base + 3 * (cc // cw)


def _conv_fwd(proj, conv_w, *, d, tm, cw, name):
    t = proj.shape[0]
    cc = d // 2
    u_col, b_col, c_col, z_col = _conv_cols(d, cw)

    def body(u_ref, up_ref, c_ref, cp_ref, b_ref, z_ref, w_ref, y_ref):
        i = pl.program_id(1)
        v = u_ref[...].astype(F32) * c_ref[...].astype(F32)
        vprev = up_ref[...].astype(F32) * cp_ref[...].astype(F32)
        ve = jnp.concatenate([jnp.where(i == 0, 0.0, vprev), v], axis=0)
        w = w_ref[...]
        y = w[0:1] * pltpu.roll(ve, 2, 0)[HALO:] + w[1:2] * pltpu.roll(ve, 1, 0)[HALO:] + w[2:3] * v
        z = z_ref[...].astype(F32)
        y_ref[...] = (b_ref[...].astype(F32) * y * (z * _sigmoid(z))).astype(BF16)

    return pl.pallas_call(
        body, grid=(cc // cw, t // tm),
        in_specs=[_cur(tm, cw, u_col), _prev(tm, cw, u_col), _cur(tm, cw, c_col), _prev(tm, cw, c_col),
                  _cur(tm, cw, b_col), _cur(tm, cw, z_col),
                  pl.BlockSpec((8, cw), lambda cb, i: (0, cb))],
        out_specs=pl.BlockSpec((tm, cw), lambda cb, i: (i, cb)),
        out_shape=jax.ShapeDtypeStruct((t, cc), BF16), name=name,
        compiler_params=_cparams(("arbitrary", "arbitrary"), 40),
    )(proj, proj, proj, proj, proj, proj, conv_w)


def _conv_bwd(dyc, proj, conv_w, *, d, tm, cw, name):
    t = proj.shape[0]
    cc = d // 2
    n_i = t // tm
    u_col, b_col, c_col, z_col = _conv_cols(d, cw)

    def body(u_ref, up_ref, c_ref, cp_ref, b_ref, bn_ref, z_ref, zn_ref, dy_ref, dyn_ref, w_ref,
             du_ref, db_ref, dc_ref, dz_ref, dw_ref):
        i = pl.program_id(1)
        last = i == n_i - 1
        u = u_ref[...].astype(F32)
        cg = c_ref[...].astype(F32)
        v = u * cg
        vprev = up_ref[...].astype(F32) * cp_ref[...].astype(F32)
        ve = jnp.concatenate([jnp.where(i == 0, 0.0, vprev), v], axis=0)
        v2 = pltpu.roll(ve, 2, 0)[HALO:]
        v1 = pltpu.roll(ve, 1, 0)[HALO:]
        w = w_ref[...]
        y = w[0:1] * v2 + w[1:2] * v1 + w[2:3] * v
        z = z_ref[...].astype(F32)
        sg = _sigmoid(z)
        bg = b_ref[...].astype(F32)
        dyc_v = dy_ref[...].astype(F32)
        dz_ref[...] = (dyc_v * bg * y * (sg * (1.0 + z * (1.0 - sg)))).astype(BF16)
        db_ref[...] = (dyc_v * y * (z * sg)).astype(BF16)
        ze = _with_next(z, zn_ref, last)
        dy_e = (_with_next(dyc_v, dyn_ref, last) * _with_next(bg, bn_ref, last) * (ze * _sigmoid(ze)))
        rows = tm + HALO
        dy0 = dy_e[:tm]
        dv = (w[2:3] * dy0 + w[1:2] * pltpu.roll(dy_e, rows - 1, 0)[:tm]
              + w[0:1] * pltpu.roll(dy_e, rows - 2, 0)[:tm])
        du_ref[...] = (dv * cg).astype(BF16)
        dc_ref[...] = (dv * u).astype(BF16)
        s0 = jnp.sum(dy0 * v2, axis=0, keepdims=True)
        s1 = jnp.sum(dy0 * v1, axis=0, keepdims=True)
        s2 = jnp.sum(dy0 * v, axis=0, keepdims=True)
        r = lax.broadcasted_iota(jnp.int32, (8, cw), 0)
        dw = jnp.where(r == 0, s0, jnp.where(r == 1, s1, jnp.where(r == 2, s2, 0.0)))

        @pl.when(i == 0)
        def _():
            dw_ref[...] = dw

        @pl.when(i > 0)
        def _():
            dw_ref[...] += dw

    out_tile = pl.BlockSpec((tm, cw), lambda cb, i: (i, cb))
    act = jax.ShapeDtypeStruct((t, cc), BF16)
    return pl.pallas_call(
        body, grid=(cc // cw, n_i),
        in_specs=[_cur(tm, cw, u_col), _prev(tm, cw, u_col), _cur(tm, cw, c_col), _prev(tm, cw, c_col),
                  _cur(tm, cw, b_col), _next(tm, cw, b_col, t), _cur(tm, cw, z_col), _next(tm, cw, z_col, t),
                  _cur(tm, cw, 0), _next(tm, cw, 0, t),
                  pl.BlockSpec((8, cw), lambda cb, i: (0, cb))],
        out_specs=[out_tile, out_tile, out_tile, out_tile, pl.BlockSpec((8, cw), lambda cb, i: (0, cb))],
        out_shape=[act, act, act, act, jax.ShapeDtypeStruct((8, cc), F32)], name=name,
        compiler_params=_cparams(("arbitrary", "arbitrary"), 48),
    )(proj, proj, proj, proj, proj, proj, proj, proj, dyc, dyc, conv_w)


def _sum_matrix(prefix):
    r = lax.broadcasted_iota(jnp.int32, (KEY_CHUNK, 2 * KEY_CHUNK), 0)
    c = lax.broadcasted_iota(jnp.int32, (KEY_CHUNK, 2 * KEY_CHUNK), 1)
    inside = (r <= c) if prefix else (r > c)
    return jnp.where((c >= KEY_CHUNK) | inside, 1.0, 0.0).astype(BF16)


def _chunk_sums(val, mat, two_pass):
    hi = val.astype(BF16)
    both = jnp.dot(hi, mat, preferred_element_type=F32)
    if two_pass:
        lo = (val - hi.astype(F32)).astype(BF16)
        both = both + jnp.dot(lo, mat, preferred_element_type=F32)
    return both[:, :KEY_CHUNK], both[:, KEY_CHUNK:]


def _stick_logits(qs, kb, strict):
    z = lax.dot_general(qs, kb, (((1,), (1,)), ((), ())), preferred_element_type=F32)
    sp = jnp.maximum(z, 0.0) + jnp.log(1.0 + jnp.exp(-jnp.abs(z)))
    log_beta = z - sp
    if strict is not None:
        sp = jnp.where(strict, sp, 0.0)
    return sp, log_beta


def _running(val, mat, carry, upwards):
    n = val.shape[1] // KEY_CHUNK
    out = [None] * n
    for c in (range(n) if upwards else reversed(range(n))):
        part, total = _chunk_sums(val[:, c * KEY_CHUNK:(c + 1) * KEY_CHUNK], mat, False)
        out[c] = part + carry
        carry = carry + total
    return jnp.concatenate(out, axis=1), carry


def _attn_cols(d):
    h = d // HEAD_DIM
    return h, 2 * h, 3 * h, 4 * h


def _sweep(i, per, block, carry, upwards):
    if upwards:
        carry = lax.fori_loop(0, i * per, lambda it, cr: block(it, cr, False), carry)
        for dgl in range(per):
            carry = block(i * per + dgl, carry, True)
        return carry
    for dgl in range(per - 1, -1, -1):
        carry = block(i * per + dgl, carry, True)
    return lax.fori_loop(0, i * per, lambda it, cr: block(i * per - 1 - it, cr, False), carry)


def _strict_mask(i, tq, kb, width):
    t_idx = i * tq + lax.broadcasted_iota(jnp.int32, (tq, width), 0)
    s_idx = kb * width + lax.broadcasted_iota(jnp.int32, (tq, width), 1)
    return s_idx < t_idx


def _attn_fwd(proj, *, d, tq, name):
    t = proj.shape[0]
    heads = d // HEAD_DIM
    q_col, k_col, v_col, z_col = _attn_cols(d)

    def body(q_ref, k_ref, v_ref, z_ref, o_ref, y_ref, tot_ref):
        i = pl.program_id(1)
        qs = (q_ref[...].astype(F32) * (HEAD_DIM ** -0.5)).astype(BF16)
        mat = _sum_matrix(prefix=False)

        def block(kb, carry, masked):
            acc, after = carry
            off = pl.multiple_of(kb * KEY_BLOCK, KEY_BLOCK)
            strict = _strict_mask(i, tq, kb, KEY_BLOCK) if masked else None
            sp, log_beta = _stick_logits(qs, k_ref[pl.ds(off, KEY_BLOCK), :], strict)
            between, after = _running(sp, mat, after, upwards=False)
            a = jnp.exp(log_beta - between)
            if masked:
                a = jnp.where(strict, a, 0.0)
            acc = acc + jnp.dot(a.astype(BF16), v_ref[pl.ds(off, KEY_BLOCK), :], preferred_element_type=F32)
            return acc, after

        zero = jnp.zeros((tq, HEAD_DIM), F32)
        o, total = _sweep(i, tq // KEY_BLOCK, block, (zero, zero), upwards=False)
        o_ref[...] = o.astype(BF16)
        tot_ref[...] = total
        z = z_ref[...].astype(F32)
        y_ref[...] = (o * (z * _sigmoid(z))).astype(BF16)

    tile = lambda col: pl.BlockSpec((tq, HEAD_DIM), lambda h, i: (i, col + h))
    full = lambda col: pl.BlockSpec((t, HEAD_DIM), lambda h, i: (0, col + h))
    act = jax.ShapeDtypeStruct((t, d), BF16)
    return pl.pallas_call(
        body, grid=(heads, t // tq),
        in_specs=[tile(q_col), full(k_col), full(v_col), tile(z_col)],
        out_specs=[tile(0), tile(0), tile(0)], out_shape=[act, act, jax.ShapeDtypeStruct((t, d), F32)],
        name=name, compiler_params=_cparams(("arbitrary", "arbitrary"), 40),
    )(proj, proj, proj, proj)


def _attn_bwd(dyb, o, tot, proj, *, d, tq, name):
    t = proj.shape[0]
    heads = d // HEAD_DIM
    n_i = t // tq
    q_col, k_col, v_col, z_col = _attn_cols(d)
    scale = HEAD_DIM ** -0.5

    def body(q_ref, k_ref, v_ref, z_ref, o_ref, tot_ref, dy_ref, dq_ref, dk_ref, dv_ref, dz_ref, dk_acc, dv_acc):
        i = pl.program_id(1)

        @pl.when(i == 0)
        def _():
            dk_acc[...] = jnp.zeros_like(dk_acc)
            dv_acc[...] = jnp.zeros_like(dv_acc)

        qs = (q_ref[...].astype(F32) * scale).astype(BF16)
        z = z_ref[...].astype(F32)
        sg = _sigmoid(z)
        dy = dy_ref[...].astype(F32)
        of = o_ref[...].astype(F32)
        dz_ref[...] = (dy * of * (sg * (1.0 + z * (1.0 - sg)))).astype(BF16)
        do16 = (dy * (z * sg)).astype(BF16)
        all_keep = tot_ref[...]
        all_keep = jnp.concatenate([all_keep] * (KEY_BLOCK // HEAD_DIM), axis=1)
        mat = _sum_matrix(prefix=True)

        def block(kb, carry, masked):
            dq, sp_seen, g_seen = carry
            off = pl.multiple_of(kb * KEY_BLOCK, KEY_BLOCK)
            kblk = k_ref[pl.ds(off, KEY_BLOCK), :]
            vblk = v_ref[pl.ds(off, KEY_BLOCK), :]
            strict = _strict_mask(i, tq, kb, KEY_BLOCK) if masked else None
            sp, log_beta = _stick_logits(qs, kblk, strict)
            sp_upto, sp_seen = _running(sp, mat, sp_seen, upwards=True)
            a = jnp.exp(log_beta - (all_keep - sp_upto))
            if masked:
                a = jnp.where(strict, a, 0.0)
            g = a * lax.dot_general(do16, vblk, (((1,), (1,)), ((), ())), preferred_element_type=F32)
            g_upto, g_seen = _running(g, mat, g_seen, upwards=True)
            dz_ts = g - jnp.exp(log_beta) * g_upto
            if masked:
                dz_ts = jnp.where(strict, dz_ts, 0.0)
            dz16 = dz_ts.astype(BF16)
            dq = dq + jnp.dot(dz16, kblk, preferred_element_type=F32)
            dk_acc[pl.ds(off, KEY_BLOCK), :] += lax.dot_general(
                dz16, qs, (((0,), (0,)), ((), ())), preferred_element_type=F32)
            dv_acc[pl.ds(off, KEY_BLOCK), :] += lax.dot_general(
                a.astype(BF16), do16, (((0,), (0,)), ((), ())), preferred_element_type=F32)
            return dq, sp_seen, g_seen

        zero = jnp.zeros((tq, HEAD_DIM), F32)
        dq, _, _ = _sweep(i, tq // KEY_BLOCK, block, (zero, zero, zero), upwards=True)
        dq_ref[...] = (dq * scale).astype(BF16)

        @pl.when(i == n_i - 1)
        def _():
            dk_ref[...] = dk_acc[...].astype(BF16)
            dv_ref[...] = dv_acc[...].astype(BF16)

    tile = lambda col: pl.BlockSpec((tq, HEAD_DIM), lambda h, i: (i, col + h))
    full = lambda col: pl.BlockSpec((t, HEAD_DIM), lambda h, i: (0, col + h))
    act = jax.ShapeDtypeStruct((t, d), BF16)
    return pl.pallas_call(
        body, grid=(heads, n_i),
        in_specs=[tile(q_col), full(k_col), full(v_col), tile(z_col), tile(0), tile(0), tile(0)],
        out_specs=[tile(0), full(0), full(0), tile(0)], out_shape=[act, act, act, act],
        scratch_shapes=[pltpu.VMEM((t, HEAD_DIM), F32), pltpu.VMEM((t, HEAD_DIM), F32)], name=name,
        compiler_params=_cparams(("arbitrary", "arbitrary"), 48),
    )(proj, proj, proj, proj, o, tot, dyb)


def _merge_fwd(gl, bst, *, d, tm, tn, name):
    t = gl.shape[0]
    nk = d // tn

    def body(g0, g1, g2, b0, b1, b2, m_ref):
        acc = _sigmoid(g0[...].astype(F32)) * b0[...].astype(F32)
        acc += _sigmoid(g1[...].astype(F32)) * b1[...].astype(F32)
        acc += _sigmoid(g2[...].astype(F32)) * b2[...].astype(F32)
        m_ref[...] = acc.astype(BF16)

    spec = lambda b: pl.BlockSpec((tm, tn), lambda i, kb: (i, b * nk + kb))
    return pl.pallas_call(
        body, grid=(t // tm, nk), in_specs=[spec(0), spec(1), spec(2)] * 2,
        out_specs=pl.BlockSpec((tm, tn), lambda i, kb: (i, kb)),
        out_shape=jax.ShapeDtypeStruct((t, d), BF16), name=name,
        compiler_params=_cparams(("arbitrary", "arbitrary"), 40),
    )(gl, gl, gl, bst, bst, bst)


def _merge_bwd(dmerged, gl, bst, *, d, tm, tn, name):
    t = gl.shape[0]
    nk = d // tn

    def body(dm_ref, g_ref, b_ref, db_ref, dg_ref, dbias_ref):
        i = pl.program_id(2)
        dm = dm_ref[...].astype(F32)
        sg = _sigmoid(g_ref[...].astype(F32))
        db_ref[...] = (dm * sg).astype(BF16)
        dgl = dm * b_ref[...].astype(F32) * (sg * (1.0 - sg))
        dg_ref[...] = dgl.astype(BF16)
        dbias = jnp.sum(dgl, axis=0, keepdims=True)

        @pl.when(i == 0)
        def _():
            dbias_ref[...] = dbias

        @pl.when(i > 0)
        def _():
            dbias_ref[...] += dbias

    wide = pl.BlockSpec((tm, tn), lambda b, kb, i: (i, b * nk + kb))
    act = jax.ShapeDtypeStruct((t, 3 * d), BF16)
    return pl.pallas_call(
        body, grid=(3, nk, t // tm),
        in_specs=[pl.BlockSpec((tm, tn), lambda b, kb, i: (i, kb)), wide, wide],
        out_specs=[wide, wide, pl.BlockSpec((1, tn), lambda b, kb, i: (0, b * nk + kb))],
        out_shape=[act, act, jax.ShapeDtypeStruct((1, 3 * d), F32)], name=name,
        compiler_params=_cparams(("arbitrary", "arbitrary", "arbitrary"), 40),
    )(dmerged, gl, bst)


def _adamw(w, m, v, parts, layer, prev, *, tr, name):
    depth, r, c = w.shape
    n = parts.shape[0]

    def body(*refs):
        w_ref, m_ref, v_ref, p_ref = refs[:4]
        g_ref, d_ref, nm_ref, nv_ref = refs[-4:]
        g = p_ref[0].astype(F32)
        for s in range(1, n):
            g = g + p_ref[s].astype(F32)
        wv = w_ref[...]
        nm = ADAM_B1 * m_ref[...] + (1.0 - ADAM_B1) * g
        nv = ADAM_B2 * v_ref[...] + (1.0 - ADAM_B2) * (g * g)
        m_hat = nm / (1.0 - ADAM_B1 ** ADAM_STEP)
        v_hat = nv / (1.0 - ADAM_B2 ** ADAM_STEP)
        g_ref[...] = g
        nm_ref[...] = nm
        nv_ref[...] = nv
        d_ref[...] = -ADAM_LR * (m_hat / (jnp.sqrt(v_hat) + ADAM_EPS) + ADAM_WD * wv)

    tile = pl.BlockSpec((None, tr, c), lambda i: (layer, i, 0))
    in_specs = [tile, tile, tile, pl.BlockSpec((n, tr, c), lambda i: (0, i, 0))]
    args = [w, m, v, parts]
    aliases = {}
    if prev is not None:
        in_specs += [pl.BlockSpec(memory_space=pl.ANY)] * 4
        aliases = {4 + k: k for k in range(4)}
        args += list(prev)
    full = jax.ShapeDtypeStruct((depth, r, c), F32)
    return pl.pallas_call(
        body, grid=(r // tr,), in_specs=in_specs, out_specs=[tile] * 4, out_shape=[full] * 4,
        input_output_aliases=aliases, name=name, compiler_params=_cparams(("arbitrary",), 48),
    )(*args)


def _adamw_layers(w, m, v, parts_by_layer, *, tr, name):
    lead = w.shape[0]
    flat = lambda a: a.reshape(lead, -1, a.shape[-1])
    w, m, v = flat(w), flat(m), flat(v)
    out = None
    for layer in reversed(range(lead)):
        parts = parts_by_layer[layer]
        parts = parts.reshape(parts.shape[0], -1, parts.shape[-1])
        out = _adamw(w, m, v, parts, layer, out, tr=min(tr, w.shape[1]), name=f"{name}_l{layer}")
    return out


def _place():
    return lax.axis_index("x"), lax.axis_index("y"), lax.axis_index("c")


def _index(px, py, pc):
    return 4 * px + 2 * py + pc


def _whole(ref, idx):
    return ref.at[idx]


def _row_block(rows):
    return lambda ref, idx: ref.at[:, pl.ds(idx * rows, rows), :]


def _lane_block(cols):
    return lambda ref, idx: ref.at[:, pl.ds(idx * cols, cols)]


def _all_gather(shards, out_shapes, placers, *, name):
    n = len(shards)

    def body(*refs):
        src, out = refs[:n], refs[n:2 * n]
        send_sems, recv_sems, local_sems = refs[2 * n:]
        x, y, c = _place()
        me, sibling = (x, y, c), (x, y, 1 - c)
        chips = [(1 - x, y), (x, 1 - y), (1 - x, 1 - y)]

        def copy(a, k, block, to, from_shard=False):
            window = placers[a](out[a], _index(*block))
            return pltpu.make_async_remote_copy(
                src_ref=src[a] if from_shard else window, dst_ref=window,
                send_sem=send_sems.at[a * 7 + k], recv_sem=recv_sems.at[a * 7 + k],
                device_id=to, device_id_type=MESH)

        mine = [pltpu.make_async_copy(src[a], placers[a](out[a], _index(*me)), local_sems.at[a])
                for a in range(n)]
        started = []
        for a in range(n):
            mine[a].start()
            started.append(copy(a, 0, me, sibling, True))
            started += [copy(a, 1 + j, me, (*chip, c), True) for j, chip in enumerate(chips)]
        for cp in started:
            cp.start()
        for j, chip in enumerate(chips):
            for a in range(n):
                copy(a, 1 + j, (*chip, c), me).wait_recv()
                passed = copy(a, 4 + j, (*chip, c), sibling)
                passed.start()
                started.append(passed)
        for a in range(n):
            copy(a, 0, sibling, me).wait_recv()
            for j, chip in enumerate(chips):
                copy(a, 4 + j, (*chip, 1 - c), me).wait_recv()
        for cp in started:
            cp.wait_send()
        for cp in mine:
            cp.wait()

    hbm = pl.BlockSpec(memory_space=pl.ANY)
    return pl.pallas_call(
        body, in_specs=[hbm] * n, out_specs=[hbm] * n, out_shape=out_shapes,
        scratch_shapes=[pltpu.SemaphoreType.DMA((7 * n,)), pltpu.SemaphoreType.DMA((7 * n,)),
                        pltpu.SemaphoreType.DMA((n,))],
        name=name,
    )(*shards)


def _scatter_parts(grads, pickers, part_shapes, *, name):
    n = len(grads)

    def body(*refs):
        src, out = refs[:n], refs[n:2 * n]
        send_sems, recv_sems, local_sems = refs[2 * n:]
        x, y, c = _place()
        my_idx = _index(x, y, c)
        peers = [(x ^ (k >> 2), y ^ ((k >> 1) & 1), c ^ (k & 1)) for k in range(1, N_DEV)]

        def copy(a, k, peer):
            return pltpu.make_async_remote_copy(
                src_ref=pickers[a](src[a], _index(*peer)), dst_ref=out[a].at[my_idx],
                send_sem=send_sems.at[a * 7 + k], recv_sem=recv_sems.at[a * 7 + k],
                device_id=peer, device_id_type=MESH)

        def arrival(a, k, peer):
            return pltpu.make_async_remote_copy(
                src_ref=pickers[a](src[a], my_idx), dst_ref=out[a].at[_index(*peer)],
                send_sem=send_sems.at[a * 7 + k], recv_sem=recv_sems.at[a * 7 + k],
                device_id=peer, device_id_type=MESH)

        mine = [pltpu.make_async_copy(pickers[a](src[a], my_idx), out[a].at[my_idx], local_sems.at[a])
                for a in range(n)]
        sent = [copy(a, k, peer) for a in range(n) for k, peer in enumerate(peers)]
        for cp in mine + sent:
            cp.start()
        for a in range(n):
            for k, peer in enumerate(peers):
                arrival(a, k, peer).wait_recv()
        for cp in sent:
            cp.wait_send()
        for cp in mine:
            cp.wait()

    hbm = pl.BlockSpec(memory_space=pl.ANY)
    return pl.pallas_call(
        body, in_specs=[hbm] * n, out_specs=[hbm] * n, out_shape=part_shapes,
        scratch_shapes=[pltpu.SemaphoreType.DMA((7 * n,)), pltpu.SemaphoreType.DMA((7 * n,)),
                        pltpu.SemaphoreType.DMA((n,))],
        name=name,
    )(*grads)


def _peers(x, y, c):
    return [(x ^ (k >> 2), y ^ ((k >> 1) & 1), c ^ (k & 1)) for k in range(1, N_DEV)]


_HBM = pl.BlockSpec(memory_space=pltpu.HBM)
_SEM = pl.BlockSpec(memory_space=pltpu.SEMAPHORE)
_EFFECT = pltpu.SideEffectType.DATAFLOW_SIDE_EFFECTING


def _exchange_start(srcs, land_shapes, src_win, dst_win, after, *, name):
    n = len(srcs)

    def body(*refs):
        src, land = refs[:n], refs[n:2 * n]
        send_sems, recv_sems, token = refs[2 * n + len(after)], refs[2 * n + len(after) + 1], refs[-1]
        x, y, c = _place()
        my_idx = _index(x, y, c)
        for a in range(n):
            for k, peer in enumerate(_peers(x, y, c)):
                pltpu.make_async_remote_copy(
                    src_ref=src_win[a](src[a], _index(*peer)), dst_ref=dst_win[a](land[a], my_idx),
                    send_sem=send_sems.at[a * 7 + k], recv_sem=recv_sems.at[a * 7 + k],
                    device_id=peer, device_id_type=MESH).start()
        token[...] = jnp.zeros_like(token)

    hbm = lambda a: pltpu.with_memory_space_constraint(a, pltpu.HBM)
    lands = [hbm(lax.empty(s.shape, s.dtype)) for s in land_shapes]
    args = [hbm(s) for s in srcs] + lands
    kept = [pltpu.HBM(a.shape, a.dtype) for a in args]
    out = pl.pallas_call(
        body, name=name,
        out_shape=(pltpu.SemaphoreType.DMA((7 * n,)), pltpu.SemaphoreType.DMA((7 * n,)), *kept,
                   jax.ShapeDtypeStruct((8, LANE), F32)),
        in_specs=[_HBM] * (2 * n) + [pl.BlockSpec(memory_space=pl.ANY)] * len(after),
        out_specs=(_SEM, _SEM, *([_HBM] * (2 * n)), pl.BlockSpec(memory_space=pltpu.VMEM)),
        input_output_aliases={i: 2 + i for i in range(2 * n)},
        compiler_params=pltpu.CompilerParams(has_side_effects=_EFFECT),
    )(*args, *after)
    return out[0], out[1], list(out[2:2 + n]), list(out[2 + n:2 + 2 * n]), out[-1][0, 0]


def _exchange_wait(send_sems, recv_sems, srcs, lands, src_win, dst_win, after, *, name):
    n = len(srcs)

    def body(*refs):
        src, land = refs[:n], refs[n:2 * n]
        send_sems, recv_sems = refs[2 * n], refs[2 * n + 1]
        x, y, c = _place()
        for a in range(n):
            for k, peer in enumerate(_peers(x, y, c)):
                sender = _index(*peer)
                copy = pltpu.make_async_remote_copy(
                    src_ref=src_win[a](src[a], sender), dst_ref=dst_win[a](land[a], sender),
                    send_sem=send_sems.at[a * 7 + k], recv_sem=recv_sems.at[a * 7 + k],
                    device_id=peer, device_id_type=MESH)
                copy.wait_send()
                copy.wait_recv()

    kept = [pltpu.HBM(a.shape, a.dtype) for a in list(srcs) + list(lands)]
    out = pl.pallas_call(
        body, name=name, out_shape=tuple(kept),
        in_specs=[_HBM] * (2 * n) + [_SEM, _SEM] + [pl.BlockSpec(memory_space=pl.ANY)] * len(after),
        out_specs=tuple([_HBM] * (2 * n)),
        input_output_aliases={i: i for i in range(2 * n)},
        compiler_params=pltpu.CompilerParams(has_side_effects=_EFFECT),
    )(*srcs, *lands, send_sems, recv_sems, *after)
    return list(out[:n]), list(out[n:])


COPY_TILE_BYTES = 2 << 20


def _copy_own_block(src, land, idx, *, from_stack, name):
    k, n = land.shape[1:]
    tk = k
    while tk * n * land.dtype.itemsize > COPY_TILE_BYTES and tk % 32 == 0:
        tk //= 2

    def body(idx_ref, src_ref, land_ref, out_ref):
        out_ref[...] = src_ref[...]

    own = pl.BlockSpec((None, tk, n), lambda i, idx_ref: (idx_ref[0], i, 0))
    grid_spec = pltpu.PrefetchScalarGridSpec(
        num_scalar_prefetch=1, grid=(k // tk,),
        in_specs=[own if from_stack else pl.BlockSpec((tk, n), lambda i, idx_ref: (i, 0)),
                  pl.BlockSpec(memory_space=pl.ANY)],
        out_specs=own)
    return pl.pallas_call(
        body, grid_spec=grid_spec, out_shape=jax.ShapeDtypeStruct(land.shape, land.dtype),
        input_output_aliases={2: 0}, name=name,
    )(idx, src, land)


def _place_own(srcs, lands, src_win, dst_win, *, name):
    lands = list(lands)
    idx = _index(*_place()).astype(jnp.int32).reshape(1)
    rest = []
    for a in range(len(srcs)):
        if dst_win[a] is _whole and src_win[a] in (_whole, _shard_itself) and lands[a].ndim == 3:
            lands[a] = _copy_own_block(srcs[a], lands[a], idx, from_stack=src_win[a] is _whole,
                                       name=f"{name}_{a}")
        else:
            rest.append(a)
    if rest:
        placed = _dma_own([srcs[a] for a in rest], [lands[a] for a in rest], [src_win[a] for a in rest],
                          [dst_win[a] for a in rest], name=f"{name}_dma")
        for a, land in zip(rest, placed):
            lands[a] = land
    return lands


def _dma_own(srcs, lands, src_win, dst_win, *, name):
    n = len(srcs)

    def body(*refs):
        src, land = refs[:n], refs[n:2 * n]
        sems = refs[-1]
        my_idx = _index(*_place())
        copies = [pltpu.make_async_copy(src_win[a](src[a], my_idx), dst_win[a](land[a], my_idx), sems.at[a])
                  for a in range(n)]
        for cp in copies:
            cp.start()
        for cp in copies:
            cp.wait()

    hbm = pl.BlockSpec(memory_space=pl.ANY)
    return list(pl.pallas_call(
        body, name=name, in_specs=[hbm] * (2 * n), out_specs=[hbm] * n,
        out_shape=[jax.ShapeDtypeStruct(a.shape, a.dtype) for a in lands],
        input_output_aliases={n + i: i for i in range(n)},
        scratch_shapes=[pltpu.SemaphoreType.DMA((n,))],
    )(*srcs, *lands))


def _ada_fwd(c8, w_ada, b_ada8, *, name):
    depth, d, n = w_ada.shape

    def body(c_ref, w_ref, b_ref, mod_ref, sc_ref, c_all, prod, got, send_sems, recv_sems):
        x, y, c = _place()
        my_idx = _index(x, y, c)
        peers = [(x ^ (k >> 2), y ^ ((k >> 1) & 1), c ^ (k & 1)) for k in range(1, N_DEV)]

        def slab(ref, idx):
            return ref.at[pl.ds(pl.multiple_of(idx * 8, 8), 8)]

        def c_copy(k, peer, sender):
            return pltpu.make_async_remote_copy(
                src_ref=c_ref, dst_ref=slab(c_all, sender), send_sem=send_sems.at[k],
                recv_sem=recv_sems.at[k], device_id=peer, device_id_type=MESH)

        def m_copy(k, peer, sender):
            return pltpu.make_async_remote_copy(
                src_ref=prod.at[_index(*peer)], dst_ref=got.at[sender], send_sem=send_sems.at[7 + k],
                recv_sem=recv_sems.at[7 + k], device_id=peer, device_id_type=MESH)

        sends = [c_copy(k, peer, my_idx) for k, peer in enumerate(peers)]
        for cp in sends:
            cp.start()
        slab(c_all, my_idx)[...] = c_ref[...]
        for k, peer in enumerate(peers):
            c_copy(k, peer, _index(*peer)).wait_recv()
        cv = c_all[...]
        sc = cv * _sigmoid(cv)
        sc_ref[...] = sc
        for layer in range(depth):
            p = jnp.dot(sc, w_ref[layer], preferred_element_type=F32, precision=lax.Precision.HIGHEST)
            for s in range(N_DEV):
                prod[s, layer] = p[8 * s:8 * s + 8]
        sends2 = [m_copy(k, peer, my_idx) for k, peer in enumerate(peers)]
        for cp in sends2:
            cp.start()
        got[my_idx] = prod[my_idx]
        for k, peer in enumerate(peers):
            m_copy(k, peer, _index(*peer)).wait_recv()
        for layer in range(depth):
            for s in range(N_DEV):
                mod_ref[layer, :, s * n:(s + 1) * n] = got[s, layer] + b_ref[layer, :, s * n:(s + 1) * n]
        for cp in sends + sends2:
            cp.wait_send()

    vmem = pl.BlockSpec(memory_space=pltpu.VMEM)
    return pl.pallas_call(
        body, in_specs=[vmem] * 3, out_specs=[vmem] * 2,
        out_shape=[jax.ShapeDtypeStruct((depth, 8, N_DEV * n), F32),
                   jax.ShapeDtypeStruct((8 * N_DEV, d), F32)],
        scratch_shapes=[pltpu.VMEM((8 * N_DEV, d), F32), pltpu.VMEM((N_DEV, depth, 8, n), F32),
                        pltpu.VMEM((N_DEV, depth, 8, n), F32),
                        pltpu.SemaphoreType.DMA((14,)), pltpu.SemaphoreType.DMA((14,))],
        name=name, compiler_params=pltpu.CompilerParams(vmem_limit_bytes=56 << 20),
    )(c8, w_ada, b_ada8)


def _ada_bwd(sc8, dmod, *, name):
    depth, _, n = dmod.shape
    d = sc8.shape[1]

    def body(sc_ref, dm_ref, o_ref):
        for layer in range(depth):
            o_ref[layer] = lax.dot_general(sc_ref[...], dm_ref[layer], (((0,), (0,)), ((), ())),
                                           preferred_element_type=F32, precision=lax.Precision.HIGHEST)

    vmem = pl.BlockSpec(memory_space=pltpu.VMEM)
    return pl.pallas_call(
        body, in_specs=[vmem] * 2, out_specs=vmem, out_shape=jax.ShapeDtypeStruct((depth, d, n), F32),
        name=name, compiler_params=pltpu.CompilerParams(vmem_limit_bytes=40 << 20),
    )(sc8, dmod)


def _gather_small(vec, *, name):
    r = vec.shape[0]

    def body(v_ref, o_ref, send_sems, recv_sems):
        x, y, c = _place()
        my_idx = _index(x, y, c)
        peers = [(x ^ (k >> 2), y ^ ((k >> 1) & 1), c ^ (k & 1)) for k in range(1, N_DEV)]

        def copy(k, peer, sender):
            return pltpu.make_async_remote_copy(
                src_ref=v_ref, dst_ref=o_ref.at[sender], send_sem=send_sems.at[k],
                recv_sem=recv_sems.at[k], device_id=peer, device_id_type=MESH)

        sends = [copy(k, peer, my_idx) for k, peer in enumerate(peers)]
        for cp in sends:
            cp.start()
        o_ref[my_idx] = v_ref[...]
        for k, peer in enumerate(peers):
            copy(k, peer, _index(*peer)).wait_recv()
        for cp in sends:
            cp.wait_send()

    vmem = pl.BlockSpec(memory_space=pltpu.VMEM)
    return pl.pallas_call(
        body, in_specs=[vmem], out_specs=vmem, out_shape=jax.ShapeDtypeStruct((N_DEV, r, LANE), F32),
        scratch_shapes=[pltpu.SemaphoreType.DMA((7,)), pltpu.SemaphoreType.DMA((7,))], name=name,
    )(vec)


TM = 512
TM_WIDE = 2048
TMM_DEEP = 1024
TQ = 512


def _layer_fwd(x, mod, small, gw, *, d, tag, more_weights=None):
    shift, scale, res_gate = mod
    h = _prenorm(x, small["norm_g"], scale, shift, tm=TM, name=f"prenorm_{tag}")
    proj = _mm_nn(h, gw["w_in"], out_dtype=BF16, tm=TM, name=f"in_proj_{tag}")
    ya = _pool_fwd(proj, gw["pool_w"], small["pool_scale"], d=d, tm=TM, name=f"pool_fwd_{tag}")
    o, yb, tot = _attn_fwd(proj, d=d, tq=TQ, name=f"attn_fwd_{tag}")
    yc = _conv_fwd(proj, gw["conv_w"], d=d, tm=TM, cw=LANE * 2, name=f"conv_fwd_{tag}")
    if more_weights is not None:
        gw = dict(gw, **more_weights(o))
    gl = _mm_nn(h, gw["w_gate"], out_dtype=BF16, tm=TM, bias=small["b_gate"], name=f"gate_proj_{tag}")
    nblk = gw["w_br_a"].shape[0]
    bst = _mm_nn(ya, gw["w_br_a"], out_dtype=BF16, tm=TM_WIDE, out_cols=3 * d, name=f"branch_a_{tag}")
    bst = _mm_nn(yb, gw["w_br_b"], out_dtype=BF16, tm=TM, out=bst, out_col_block=1, name=f"branch_b_{tag}")
    bst = _mm_nn(yc, gw["w_br_c"], out_dtype=BF16, tm=TM_WIDE, out=bst, out_col_block=2 * nblk,
                 name=f"branch_c_{tag}")
    merged = _merge_fwd(gl, bst, d=d, tm=TM, tn=TM, name=f"merge_fwd_{tag}")
    x_new, mo = _out_fwd(merged, gw["w_out"][0], x, res_gate, tm=TM, name=f"out_fwd_{tag}")
    return x_new, dict(x=x, h=h, proj=proj, gl=gl, ya=ya, yb=yb, yc=yc, o=o, tot=tot, bst=bst, merged=merged,
                       mo=mo), gw


EARLY = ("w_gate", "w_br_a", "w_br_b", "w_br_c", "w_out")
LATE = ("w_in", "pool_w")


def _layer_bwd(dout, sv, mod, small, gw, *, d, tag, send_early=None, send_late=None):
    shift, scale, res_gate = mod
    nblk = gw["w_br_a"].shape[0]
    dmo, d_res_gate = _out_bwd(dout, sv["mo"], res_gate, tm=TM, name=f"out_bwd_{tag}")
    g_w_out = _mm_tn(sv["merged"], dmo, nb=1, n=d, tk=TM, tmm=TMM_DEEP, name=f"dw_out_{tag}")
    dmerged = _mm_nt(dmo, gw["w_out"], out_dtype=BF16, tm=TM, name=f"dmerged_{tag}")
    dbst, dgl, d_b_gate = _merge_bwd(dmerged, sv["gl"], sv["bst"], d=d, tm=TM, tn=TM, name=f"merge_bwd_{tag}")
    g_w_gate = _mm_tn(sv["h"], dgl, nb=nblk, n=3 * d // nblk, tk=d // 2, tmm=TMM_DEEP,
                      name=f"dw_gate_{tag}")
    dh = _mm_nt(dgl, gw["w_gate"], out_dtype=F32, tm=TM, name=f"dh_gate_{tag}")
    g_w_br_a = _mm_tn(sv["ya"], dbst, nb=nblk, n=d // nblk, tk=d // 2, tmm=TM_WIDE,
                      name=f"dw_br_a_{tag}")
    g_w_br_b = _mm_tn(sv["yb"], dbst, nb=1, n=d, tk=TM, tmm=TMM_DEEP, col_block=1, name=f"dw_br_b_{tag}")
    g_w_br_c = _mm_tn(sv["yc"], dbst, nb=nblk, n=d // nblk, tk=d // 2, tmm=TM_WIDE, col_block=2 * nblk,
                      name=f"dw_br_c_{tag}")
    dya = _mm_nt(dbst, gw["w_br_a"], out_dtype=BF16, tm=TM_WIDE, name=f"dya_{tag}")
    dyb = _mm_nt(dbst, gw["w_br_b"], out_dtype=BF16, tm=TM, col_block=1, name=f"dyb_{tag}")
    dyc = _mm_nt(dbst, gw["w_br_c"], out_dtype=BF16, tm=TM_WIDE, col_block=2 * nblk, name=f"dyc_{tag}")
    big = dict(w_gate=g_w_gate, w_br_a=g_w_br_a, w_br_b=g_w_br_b, w_br_c=g_w_br_c, w_out=g_w_out)
    if send_early is not None:
        token = send_early(big)
        small = dict(small, pool_scale=small["pool_scale"] + token, norm_g=small["norm_g"] + token)
        dyb = dyb + token.astype(BF16)
    dxa, dza, g_pool_w, d_pool_scale = _pool_bwd(dya, sv["proj"], gw["pool_w"], small["pool_scale"], d=d, tm=TM,
                                                 name=f"pool_bwd_{tag}")
    dq, dk, dv, dzb = _attn_bwd(dyb, sv["o"], sv["tot"], sv["proj"], d=d, tq=TQ, name=f"attn_bwd_{tag}")
    du, dbg, dcg, dzc, d_conv_w = _conv_bwd(dyc, sv["proj"], gw["conv_w"], d=d, tm=TM, cw=LANE * 2,
                                            name=f"conv_bwd_{tag}")
    dproj = jnp.concatenate([dxa, dza, dq, dk, dv, dzb, du, dbg, dcg, dzc], axis=1)
    g_w_in = _mm_tn(sv["h"], dproj, nb=nblk, n=7 * d // nblk, tk=d // 2, tmm=TMM_DEEP,
                    name=f"dw_in_{tag}")
    big.update(w_in=g_w_in, pool_w=g_pool_w.astype(BF16))
    if send_late is not None:
        dh = dh + send_late(big)
    dh = _mm_nt(dproj, gw["w_in"], out_dtype=F32, tm=TM, addend=dh, name=f"dh_in_{tag}")
    dx, d_norm_g, d_scale, d_shift = _prenorm_bwd(dh, sv["x"], dout, small["norm_g"], scale, tm=TM,
                                                  name=f"prenorm_bwd_{tag}")
    part = dict(norm_g=d_norm_g, mod=jnp.concatenate([d_shift, d_scale, d_res_gate], axis=1),
                pool_scale=d_pool_scale, b_gate=d_b_gate, conv_w=d_conv_w[:3])
    return dx, big, part


BIG = ("w_in", "w_gate", "w_br_a", "w_br_b", "w_br_c", "w_out", "pool_w")
SMALL_PER_LAYER = ("norm_g", "mod", "pool_scale", "b_gate", "conv_w")


def _gather_layer(w, layer, *, d):
    nd = N_DEV
    gd = d // 2 // N_POOL_GROUPS
    cc = d // 2
    shards = [w[k][layer].astype(BF16) for k in BIG]
    shards.append(jnp.pad(w["conv_w"][layer], ((0, 5), (0, 0))))
    blocks = [jax.ShapeDtypeStruct((nd,) + s.shape, BF16) for s in shards[:6]]
    blocks.append(jax.ShapeDtypeStruct((N_POOL_GROUPS, gd, gd), BF16))
    blocks.append(jax.ShapeDtypeStruct((8, cc), F32))
    placers = [_whole] * 6 + [_row_block(gd // nd), _lane_block(cc // nd)]
    return shards, blocks, placers


GATHERED = BIG + ("conv_w",)
FIRST = (0, 6, 7)
REST = (1, 2, 3, 4, 5)


def _as_weights(got, d, which=tuple(range(len(GATHERED)))):
    gw = dict(zip([GATHERED[i] for i in which], got))
    for k in ("w_br_b", "w_out"):
        if k in gw:
            gw[k] = gw[k].reshape(1, d, d)
    return gw


def _shard_itself(ref, idx):
    return ref


def _scatter_spec(big, keys, *, d):
    nd = N_DEV
    gd = d // 2 // N_POOL_GROUPS
    grads, pickers, shapes = [], [], []
    for k in keys:
        g = big[k]
        if k == "pool_w":
            pickers.append(_row_block(gd // nd))
            shapes.append(jax.ShapeDtypeStruct((nd, N_POOL_GROUPS, gd // nd, gd), BF16))
        else:
            if g.shape[0] == 1:
                g = g.reshape(nd, g.shape[1] // nd, g.shape[2])
            pickers.append(_whole)
            shapes.append(jax.ShapeDtypeStruct(g.shape, BF16))
        grads.append(g)
    return grads, pickers, shapes


class _Behind:
    def __init__(self, srcs, land_shapes, src_win, dst_win, after, name):
        self.src_win, self.dst_win, self.name = src_win, dst_win, name
        self.send, self.recv, self.srcs, self.lands, self.token = _exchange_start(
            srcs, land_shapes, src_win, dst_win, after, name=f"{name}_start")

    def finish(self, after):
        srcs, lands = _exchange_wait(self.send, self.recv, self.srcs, self.lands, self.src_win, self.dst_win,
                                     after, name=f"{self.name}_wait")
        return _place_own(srcs, lands, self.src_win, self.dst_win, name=f"{self.name}_own")


def _pack(pieces):
    return jnp.concatenate([p.reshape(-1, LANE) for p in pieces], axis=0)


def kernel(x, c, norm_g, w_ada, b_ada, w_in, pool_w, pool_scale, conv_w, w_br_a, w_br_b, w_br_c, w_gate, b_gate, w_out, final_g, loss_target, m_norm_g, m_w_ada, m_b_ada, m_w_in, m_pool_w, m_pool_scale, m_conv_w, m_w_br_a, m_w_br_b, m_w_br_c, m_w_gate, m_b_gate, m_w_out, m_final_g, v_norm_g, v_w_ada, v_b_ada, v_w_in, v_pool_w, v_pool_scale, v_conv_w, v_w_br_a, v_w_br_b, v_w_br_c, v_w_gate, v_b_gate, v_w_out, v_final_g):
    names = ("norm_g", "w_ada", "b_ada", "w_in", "pool_w", "pool_scale", "conv_w", "w_br_a", "w_br_b",
             "w_br_c", "w_gate", "b_gate", "w_out", "final_g")
    w = dict(zip(names, (norm_g, w_ada, b_ada, w_in, pool_w, pool_scale, conv_w, w_br_a, w_br_b, w_br_c,
                         w_gate, b_gate, w_out, final_g)))
    m = dict(zip(names, (m_norm_g, m_w_ada, m_b_ada, m_w_in, m_pool_w, m_pool_scale, m_conv_w, m_w_br_a,
                         m_w_br_b, m_w_br_c, m_w_gate, m_b_gate, m_w_out, m_final_g)))
    v = dict(zip(names, (v_norm_g, v_w_ada, v_b_ada, v_w_in, v_pool_w, v_pool_scale, v_conv_w, v_w_br_a,
                         v_w_br_b, v_w_br_c, v_w_gate, v_b_gate, v_w_out, v_final_g)))
    _, t, d = x.shape
    depth = norm_g.shape[0]
    cc = d // 2
    my_idx = _index(*_place())

    mod8, sc_all = _ada_fwd(jnp.broadcast_to(c, (8, d)), w_ada,
                            jnp.broadcast_to(b_ada[:, None, :], (depth, 8, 3 * d)), name="ada_fwd")
    mods = [tuple(mod8[l, 0:1, k * d:(k + 1) * d] for k in range(3)) for l in range(depth)]
    small = [dict(norm_g=norm_g[l][None], b_gate=b_gate[l][None], pool_scale=pool_scale[l][None])
             for l in range(depth)]

    shards, blocks, placers = _gather_layer(w, 0, d=d)
    pick = lambda seq, which: [seq[i] for i in which]
    first = _all_gather(pick(shards, FIRST), pick(blocks, FIRST), pick(placers, FIRST), name="gather_first_l0")
    rest = _Behind(pick(shards, REST), pick(blocks, REST), [_shard_itself] * len(REST), pick(placers, REST),
                   (mod8, first[-1]), "gather_rest_l0")
    gws = [_as_weights(first, d, FIRST)]
    order = rest.token
    xs = x[0]
    saved = []
    for l in range(depth):
        coming = None
        if l + 1 < depth:
            shards, blocks, placers = _gather_layer(w, l + 1, d=d)
            coming = _Behind(shards, blocks, [_shard_itself] * len(shards), placers,
                             (mod8, gws[l]["conv_w"], order.reshape(1, 1)), f"gather_weights_l{l + 1}")
            order = order + coming.token
        shift, scale, res_gate = mods[l]
        mods[l] = (shift + order, scale, res_gate)
        more = (lambda o: _as_weights(rest.finish((o,)), d, REST)) if l == 0 else None
        xs, sv, gws[l] = _layer_fwd(xs, mods[l], small[l], gws[l], d=d, tag=f"l{l}", more_weights=more)
        saved.append(sv)
        if coming is not None:
            gws.append(_as_weights(coming.finish((xs,)), d))
    dx, loss8, d_final_g = _loss_head(xs, loss_target[0], final_g[None], tm=TM, name="loss_head")
    loss = lax.psum(loss8[0, 0], ("x", "y", "c"))

    parts, small_parts = [dict() for _ in range(depth)], [None] * depth
    going, last = [], []

    def send(queue, keys, l, name):
        def start(grads_so_far):
            grads, pickers, shapes = _scatter_spec(grads_so_far, keys, d=d)
            queue.append((l, keys, _Behind(grads, shapes, pickers, [_whole] * len(grads), (grads[0],), name)))
            return queue[-1][2].token
        return start

    for l in reversed(range(depth)):
        if going:
            shift, scale, res_gate = mods[l]
            mods[l] = (shift, scale, res_gate + going[-1][2].token)
        lowest = l == 0
        dx, big, small_parts[l] = _layer_bwd(
            dx, saved[l], mods[l], small[l], gws[l], d=d, tag=f"l{l}",
            send_early=send(going, EARLY, l, f"scatter_early_l{l}") if lowest else None,
            send_late=send(last, LATE, l, f"scatter_late_l{l}") if lowest else None)
        if not lowest:
            send(going, BIG, l, f"scatter_grads_l{l}")(big)
    for l, keys, behind in going:
        parts[l].update(zip(keys, behind.finish((dx,))))

    pieces = [small_parts[l][k] for l in range(depth) for k in SMALL_PER_LAYER] + [d_final_g]
    sizes = [p.size for p in pieces]
    gathered = _gather_small(_pack(pieces), name="gather_small_grads")
    zero_conv = jnp.zeros((3, cc), F32)

    def packed(src):
        per_layer = lambda l: [src["norm_g"][l], src["b_ada"][l], src["pool_scale"][l], src["b_gate"][l],
                               zero_conv]
        return _pack([p for l in range(depth) for p in per_layer(l)] + [src["final_g"]])[None]

    small_out = _adamw(packed(w), packed(m), packed(v), gathered, 0, None, tr=gathered.shape[1],
                       name="adamw_small")

    def unpack(flat):
        flat = flat.reshape(-1)
        out, off = [], 0
        for n in sizes:
            out.append(flat[off:off + n])
            off += n
        return out

    small_res = [unpack(a) for a in small_out]
    k_per = len(SMALL_PER_LAYER)

    def small_stack(which, pos, shape):
        return jnp.stack([small_res[which][l * k_per + pos] for l in range(depth)]).reshape(shape)

    res = {}
    for pos, name in ((0, "norm_g"), (1, "b_ada"), (2, "pool_scale"), (3, "b_gate")):
        res[name] = tuple(small_stack(k, pos, w[name].shape) for k in range(4))
    res["final_g"] = tuple(small_res[k][-1].reshape(final_g.shape) for k in range(4))

    ncol = cc // N_DEV
    conv_total = small_stack(0, 4, (depth * 3, cc))
    conv_mine = lax.dynamic_slice_in_dim(conv_total, my_idx * ncol, ncol, axis=1)
    pad8 = lambda a: jnp.pad(a.reshape(depth * 3, ncol), ((0, 8 - depth * 3), (0, 0)))[None]
    conv_out = _adamw(pad8(conv_w), pad8(m["conv_w"]), pad8(v["conv_w"]), pad8(conv_mine), 0, None, tr=8,
                      name="adamw_conv_w")
    res["conv_w"] = tuple(a[0, :depth * 3].reshape(conv_w.shape) for a in conv_out)

    n_ada = 3 * d // N_DEV
    mod_off = [sum(sizes[:l * k_per + 1]) // LANE for l in range(depth)]
    dmod = jnp.stack([
        lax.dynamic_slice_in_dim(gathered[:, mod_off[l]:mod_off[l] + 3 * d // LANE].reshape(N_DEV, 3 * d),
                                 my_idx * n_ada, n_ada, axis=1) for l in range(depth)])
    g_w_ada = _ada_bwd(sc_all[::8], dmod, name="ada_bwd")
    res["w_ada"] = _adamw_layers(w_ada, m["w_ada"], v["w_ada"], [g_w_ada[l][None] for l in range(depth)],
                                 tr=128, name="adamw_w_ada")

    def big_adamw(name):
        out = _adamw_layers(w[name], m[name], v[name], [parts[l][name] for l in range(depth)], tr=128,
                            name=f"adamw_{name}")
        res[name] = tuple(a.reshape(w[name].shape) for a in out)

    for name in EARLY:
        big_adamw(name)
    for l, keys, behind in last:
        parts[l].update(zip(keys, behind.finish(tuple(res[name][1] for name in ("w_ada",) + EARLY))))
    for name in LATE:
        big_adamw(name)

    outs = [loss, dx[None]]
    for k in range(4):
        outs += [res[name][k] for name in names]
    return tuple(outs)
```

```python
import functools

import jax
import jax.numpy as jnp
from jax import lax
from jax.experimental import pallas as pl
from jax.experimental.pallas import tpu as pltpu

F32 = jnp.float32
BF16 = jnp.bfloat16
MESH = pl.DeviceIdType.MESH

N_DEV = 8
DEPTH = 2
HEAD_DIM = 128
N_POOL_GROUPS = 4
POOL_WINDOWS = (2, 4, 8, 16)
RMS_EPS = 1e-6
LANE = 128
HALO = 16
KEY_CHUNK = 256
KEY_BLOCK = 512
ADAM_LR = 0.001
ADAM_B1 = 0.9
ADAM_B2 = 0.999
ADAM_EPS = 1e-08
ADAM_WD = 0.01
ADAM_STEP = 10


def _cparams(semantics, vmem_mb):
    return pltpu.CompilerParams(dimension_semantics=semantics, vmem_limit_bytes=vmem_mb << 20)


def _sigmoid(z):
    return 1.0 / (1.0 + jnp.exp(-z))


def _mm_nn(a, w, *, out_dtype, tm, name, bias=None, out=None, out_cols=None, out_col_block=0):
    m, k = a.shape
    nb, _, n = w.shape
    tm = min(tm, m)
    total_cols =out.shape[1] if out is not None else (nb * n if out_cols is None else out_cols)

    def body(*refs):
        if out is not None:
            refs = refs[:-2] + refs[-1:]
        if bias is not None:
            a_ref, w_ref, b_ref, o_ref = refs
        else:
            a_ref, w_ref, o_ref = refs
        acc = jnp.dot(a_ref[...], w_ref[...], preferred_element_type=F32)
        if bias is not None:
            acc = acc + b_ref[...]
        o_ref[...] = acc.astype(out_dtype)

    in_specs = [pl.BlockSpec((tm, k), lambda j, i: (i, 0)),
                pl.BlockSpec((None, k, n), lambda j, i: (j, 0, 0))]
    args = [a, w]
    if bias is not None:
        in_specs.append(pl.BlockSpec((1, n), lambda j, i: (0, j)))
        args.append(bias)
    aliases = {}
    if out is not None:
        in_specs.append(pl.BlockSpec(memory_space=pl.ANY))
        aliases = {len(args): 0}
        args.append(out)
    return pl.pallas_call(
        body, grid=(nb, m // tm), in_specs=in_specs,
        out_specs=pl.BlockSpec((tm, n), lambda j, i: (i, out_col_block + j)),
        out_shape=jax.ShapeDtypeStruct((m, total_cols), out_dtype),
        input_output_aliases=aliases, name=name,
        compiler_params=_cparams(("arbitrary", "arbitrary"), 56),
    )(*args)


def _mm_nt(dy, w, *, out_dtype, tm, name, addend=None, col_block=0):
    m = dy.shape[0]
    nb, k, n = w.shape
    tm = min(tm, m)

    def body(*refs):
        if addend is not None:
            dy_ref, w_ref, add_ref, o_ref, acc_ref = refs
        else:
            dy_ref, w_ref, o_ref, acc_ref = refs
        j = pl.program_id(1)
        part = lax.dot_general(dy_ref[...], w_ref[...], (((1,), (1,)), ((), ())),
                               preferred_element_type=F32)

        @pl.when(j == 0)
        def _():
            acc_ref[...] = part if addend is None else part + add_ref[...]

        @pl.when(j > 0)
        def _():
            acc_ref[...] += part

        @pl.when(j == nb - 1)
        def _():
            o_ref[...] = acc_ref[...].astype(out_dtype)

    in_specs = [pl.BlockSpec((tm, n), lambda i, j: (i, col_block + j)),
                pl.BlockSpec((None, k, n), lambda i, j: (j, 0, 0))]
    args = [dy, w]
    if addend is not None:
        in_specs.append(pl.BlockSpec((tm, k), lambda i, j: (i, 0)))
        args.append(addend)
    return pl.pallas_call(
        body, grid=(m // tm, nb), in_specs=in_specs,
        out_specs=pl.BlockSpec((tm, k), lambda i, j: (i, 0)),
        out_shape=jax.ShapeDtypeStruct((m, k), out_dtype),
        scratch_shapes=[pltpu.VMEM((tm, k), F32)], name=name,
        compiler_params=_cparams(("arbitrary", "arbitrary"), 56),
    )(*args)


def _mm_tn(a, dy, *, nb, n, tk, tmm, name, col_block=0, out_dtype=BF16):
    m, k = a.shape
    tmm = min(tmm, m)

    def body(a_ref, dy_ref, o_ref, acc_ref):
        t = pl.program_id(2)
        part = lax.dot_general(a_ref[...], dy_ref[...], (((0,), (0,)), ((), ())),
                               preferred_element_type=F32)

        @pl.when(t == 0)
        def _():
            acc_ref[...] = part

        @pl.when(t > 0)
        def _():
            acc_ref[...] += part

        @pl.when(t == pl.num_programs(2) - 1)
        def _():
            o_ref[...] = acc_ref[...].astype(out_dtype)

    return pl.pallas_call(
        body, grid=(nb, k // tk, m // tmm),
        in_specs=[pl.BlockSpec((tmm, tk), lambda j, kb, t: (t, kb)),
                  pl.BlockSpec((tmm, n), lambda j, kb, t: (t, col_block + j))],
        out_specs=pl.BlockSpec((None, tk, n), lambda j, kb, t: (j, kb, 0)),
        out_shape=jax.ShapeDtypeStruct((nb, k, n), out_dtype),
        scratch_shapes=[pltpu.VMEM((tk, n), F32)], name=name,
        compiler_params=_cparams(("arbitrary", "arbitrary", "arbitrary"), 56),
    )(a, dy)


def _prenorm(x, g, scale, shift, *, tm, name):
    t, d = x.shape

    def body(x_ref, g_ref, sc_ref, sh_ref, h_ref):
        xv = x_ref[...]
        r = lax.rsqrt(jnp.mean(xv * xv, axis=-1, keepdims=True) + RMS_EPS)
        h_ref[...] = ((xv * r) * g_ref[...] * (1.0 + sc_ref[...]) + sh_ref[...]).astype(BF16)

    row = pl.BlockSpec((1, d), lambda i: (0, 0))
    tile = pl.BlockSpec((tm, d), lambda i: (i, 0))
    return pl.pallas_call(
        body, grid=(t // tm,), in_specs=[tile, row, row, row], out_specs=tile,
        out_shape=jax.ShapeDtypeStruct((t, d), BF16), name=name,
        compiler_params=_cparams(("arbitrary",), 40),
    )(x, g, scale, shift)


def _prenorm_bwd(dh, x, dout, g, scale, *, tm, name):
    t, d = x.shape

    def body(dh_ref, x_ref, do_ref, g_ref, sc_ref, dx_ref, dg_ref, dsc_ref, dsh_ref):
        i = pl.program_id(0)
        xv = x_ref[...]
        dhv = dh_ref[...]
        r = lax.rsqrt(jnp.mean(xv * xv, axis=-1, keepdims=True) + RMS_EPS)
        xn = xv * r
        gain = g_ref[...] * (1.0 + sc_ref[...])
        dxn = dhv * gain
        dx_ref[...] = do_ref[...] + r * (dxn - xn * jnp.mean(dxn * xn, axis=-1, keepdims=True))
        dhxn = dhv * xn
        dg = jnp.sum(dhxn * (1.0 + sc_ref[...]), axis=0, keepdims=True)
        dsc = jnp.sum(dhxn * g_ref[...], axis=0, keepdims=True)
        dsh = jnp.sum(dhv, axis=0, keepdims=True)

        @pl.when(i == 0)
        def _():
            dg_ref[...] = dg
            dsc_ref[...] = dsc
            dsh_ref[...] = dsh

        @pl.when(i > 0)
        def _():
            dg_ref[...] += dg
            dsc_ref[...] += dsc
            dsh_ref[...] += dsh

    row = pl.BlockSpec((1, d), lambda i: (0, 0))
    tile = pl.BlockSpec((tm, d), lambda i: (i, 0))
    vec = jax.ShapeDtypeStruct((1, d), F32)
    return pl.pallas_call(
        body, grid=(t // tm,), in_specs=[tile, tile, tile, row, row],
        out_specs=[tile, row, row, row],
        out_shape=[jax.ShapeDtypeStruct((t, d), F32), vec, vec, vec], name=name,
        compiler_params=_cparams(("arbitrary",), 48),
    )(dh, x, dout, g, scale)


def _out_fwd(merged, w_out, x, res_gate, *, tm, name):
    t, d = x.shape
    k = merged.shape[1]

    def body(m_ref, w_ref, x_ref, rg_ref, xo_ref, mo_ref):
        mo = jnp.dot(m_ref[...], w_ref[...], preferred_element_type=F32)
        mo_ref[...] = mo.astype(BF16)
        xo_ref[...] = x_ref[...] + rg_ref[...] * mo

    tile = pl.BlockSpec((tm, d), lambda i: (i, 0))
    return pl.pallas_call(
        body, grid=(t // tm,),
        in_specs=[pl.BlockSpec((tm, k), lambda i: (i, 0)), pl.BlockSpec((k, d), lambda i: (0, 0)),
                  tile, pl.BlockSpec((1, d), lambda i: (0, 0))],
        out_specs=[tile, tile],
        out_shape=[jax.ShapeDtypeStruct((t, d), F32), jax.ShapeDtypeStruct((t, d), BF16)], name=name,
        compiler_params=_cparams(("arbitrary",), 56),
    )(merged, w_out, x, res_gate)


def _out_bwd(dout, mo, res_gate, *, tm, name):
    t, d = dout.shape

    def body(do_ref, mo_ref, rg_ref, dmo_ref, drg_ref):
        i = pl.program_id(0)
        dov = do_ref[...]
        dmo_ref[...] = (dov * rg_ref[...]).astype(BF16)
        drg = jnp.sum(dov * mo_ref[...].astype(F32), axis=0, keepdims=True)

        @pl.when(i == 0)
        def _():
            drg_ref[...] = drg

        @pl.when(i > 0)
        def _():
            drg_ref[...] += drg

    row = pl.BlockSpec((1, d), lambda i: (0, 0))
    tile = pl.BlockSpec((tm, d), lambda i: (i, 0))
    return pl.pallas_call(
        body, grid=(t // tm,), in_specs=[tile, tile, row], out_specs=[tile, row],
        out_shape=[jax.ShapeDtypeStruct((t, d), BF16), jax.ShapeDtypeStruct((1, d), F32)], name=name,
        compiler_params=_cparams(("arbitrary",), 40),
    )(dout, mo, res_gate)


def _loss_head(x, target, final_g, *, tm, name):
    t, d = x.shape

    def body(x_ref, t_ref, g_ref, dx_ref, loss_ref, dg_ref):
        i = pl.program_id(0)
        xv = x_ref[...]
        r = lax.rsqrt(jnp.mean(xv * xv, axis=-1, keepdims=True) + RMS_EPS)
        xn = xv * r
        err = xn * g_ref[...] - t_ref[...]
        part = (0.5 / d) * jnp.sum(jnp.sum(err * err, axis=1, keepdims=True), axis=0, keepdims=True)
        dy = err * (1.0 / d)
        dxn = dy * g_ref[...]
        dx_ref[...] = r * (dxn - xn * jnp.mean(dxn * xn, axis=-1, keepdims=True))
        dg = jnp.sum(dy * xn, axis=0, keepdims=True)

        @pl.when(i == 0)
        def _():
            loss_ref[...] = jnp.zeros_like(loss_ref) + part
            dg_ref[...] = dg

        @pl.when(i > 0)
        def _():
            loss_ref[...] += part
            dg_ref[...] += dg

    row = pl.BlockSpec((1, d), lambda i: (0, 0))
    tile = pl.BlockSpec((tm, d), lambda i: (i, 0))
    return pl.pallas_call(
        body, grid=(t // tm,), in_specs=[tile, tile, row],
        out_specs=[tile, pl.BlockSpec((8, LANE), lambda i: (0, 0)), row],
        out_shape=[jax.ShapeDtypeStruct((t, d), F32), jax.ShapeDtypeStruct((8, LANE), F32),
                   jax.ShapeDtypeStruct((1, d), F32)], name=name,
        compiler_params=_cparams(("arbitrary",), 40),
    )(x, target, final_g)


def _cur(tm, cw, col):
    return pl.BlockSpec((tm, cw), lambda cb, i: (i, col + cb))


def _prev(tm, cw, col):
    return pl.BlockSpec((HALO, cw), lambda cb, i: (jnp.maximum(i * (tm // HALO) - 1, 0), col + cb))


def _next(tm, cw, col, t):
    last = t // HALO - 1
    return pl.BlockSpec((HALO, cw), lambda cb, i: (jnp.minimum((i + 1) * (tm // HALO), last), col + cb))


def _with_prev(prev_ref, cur, first):
    prev = jnp.where(first, 0.0, prev_ref[...].astype(F32))
    return jnp.concatenate([prev, cur], axis=0)


def _with_next(cur, next_ref, last):
    nxt = jnp.where(last, 0.0, next_ref[...].astype(F32))
    return jnp.concatenate([cur, nxt], axis=0)


def _pool_mixed(xe, cur, g, row0, tm):
    s2 = xe + pltpu.roll(xe, 1, 0)
    s4 = s2 + pltpu.roll(s2, 2, 0)
    s8 = s4 + pltpu.roll(s4, 4, 0)
    s16 = s8 + pltpu.roll(s8, 8, 0)
    win = jnp.where(g == 0, s2, jnp.where(g == 1, s4, jnp.where(g == 2, s8, s16)))[HALO:]
    return win * _pool_inv_count(g, row0, tm, cur.shape[1]) - cur


def _pool_inv_count(g, row0, rows, cols):
    tpos = row0 + lax.broadcasted_iota(jnp.int32, (rows, cols), 0)
    width = jnp.left_shift(2, g)
    return 1.0 / jnp.minimum(tpos + 1, width).astype(F32)


def _pool_fwd(proj, pool_w, pool_scale, *, d, tm, name):
    t = proj.shape[0]
    pw = d // 2
    gd = pw // N_POOL_GROUPS
    za_col = pw // gd

    def body(xa_ref, xp_ref, za_ref, w_ref, ps_ref, ya_ref):
        g = pl.program_id(0)
        i = pl.program_id(1)
        cur = xa_ref[...].astype(F32)
        mixed = _pool_mixed(_with_prev(xp_ref, cur, i == 0), cur, g, i * tm, tm)
        ypre = jnp.dot(mixed.astype(BF16), w_ref[...], preferred_element_type=F32)
        za = za_ref[...].astype(F32)
        ya_ref[...] = (ypre * ps_ref[...] * (za * _sigmoid(za))).astype(BF16)

    return pl.pallas_call(
        body, grid=(N_POOL_GROUPS, t // tm),
        in_specs=[_cur(tm, gd, 0), _prev(tm, gd, 0), _cur(tm, gd, za_col),
                  pl.BlockSpec((None, gd, gd), lambda g, i: (g, 0, 0)),
                  pl.BlockSpec((1, gd), lambda g, i: (0, g))],
        out_specs=pl.BlockSpec((tm, gd), lambda g, i: (i, g)),
        out_shape=jax.ShapeDtypeStruct((t, pw), BF16), name=name,
        compiler_params=_cparams(("arbitrary", "arbitrary"), 40),
    )(proj, proj, proj, pool_w, pool_scale)


def _pool_bwd(dya, proj, pool_w, pool_scale, *, d, tm, name):
    t = proj.shape[0]
    pw = d // 2
    gd = pw // N_POOL_GROUPS
    za_col = pw // gd
    n_i = t // tm

    def body(xa_ref, xp_ref, za_ref, zn_ref, dy_ref, dyn_ref, w_ref, ps_ref,
             dxa_ref, dza_ref, dw_ref, dps_ref):
        g = pl.program_id(0)
        i = pl.program_id(1)
        last = i == n_i - 1
        cur = xa_ref[...].astype(F32)
        mixed = _pool_mixed(_with_prev(xp_ref, cur, i == 0), cur, g, i * tm, tm)
        mixed16 = mixed.astype(BF16)
        ypre = jnp.dot(mixed16, w_ref[...], preferred_element_type=F32)
        za = za_ref[...].astype(F32)
        sg = _sigmoid(za)
        dy = dy_ref[...].astype(F32)
        dza_ref[...] = (dy * (ypre * ps_ref[...]) * (sg * (1.0 + za * (1.0 - sg)))).astype(BF16)
        dysc = dy * (za * sg)
        za_e = _with_next(za, zn_ref, last)
        dy_e = _with_next(dy, dyn_ref, last)
        dypre_e = (dy_e * (za_e * _sigmoid(za_e)) * ps_ref[...]).astype(BF16)
        dm_e = lax.dot_general(dypre_e, w_ref[...], (((1,), (1,)), ((), ())), preferred_element_type=F32)
        rows = tm + HALO
        q = dm_e * _pool_inv_count(g, i * tm, rows, gd)
        f2 = q + pltpu.roll(q, rows - 1, 0)
        f4 = f2 + pltpu.roll(f2, rows - 2, 0)
        f8 = f4 + pltpu.roll(f4, rows - 4, 0)
        f16 = f8 + pltpu.roll(f8, rows - 8, 0)
        ahead = jnp.where(g == 0, f2, jnp.where(g == 1, f4, jnp.where(g == 2, f8, f16)))
        dxa_ref[...] = (ahead[:tm] - dm_e[:tm]).astype(BF16)
        dw = lax.dot_general(mixed16, dypre_e[:tm], (((0,), (0,)), ((), ())), preferred_element_type=F32)
        dps = jnp.sum(dysc * ypre, axis=0, keepdims=True)

        @pl.when(i == 0)
        def _():
            dw_ref[...] = dw
            dps_ref[...] = dps

        @pl.when(i > 0)
        def _():
            dw_ref[...] += dw
            dps_ref[...] += dps

    out_tile = pl.BlockSpec((tm, gd), lambda g, i: (i, g))
    return pl.pallas_call(
        body, grid=(N_POOL_GROUPS, n_i),
        in_specs=[_cur(tm, gd, 0), _prev(tm, gd, 0), _cur(tm, gd, za_col), _next(tm, gd, za_col, t),
                  _cur(tm, gd, 0), _next(tm, gd, 0, t),
                  pl.BlockSpec((None, gd, gd), lambda g, i: (g, 0, 0)),
                  pl.BlockSpec((1, gd), lambda g, i: (0, g))],
        out_specs=[out_tile, out_tile, pl.BlockSpec((None, gd, gd), lambda g, i: (g, 0, 0)),
                   pl.BlockSpec((1, gd), lambda g, i: (0, g))],
        out_shape=[jax.ShapeDtypeStruct((t, pw), BF16), jax.ShapeDtypeStruct((t, pw), BF16),
                   jax.ShapeDtypeStruct((N_POOL_GROUPS, gd, gd), F32), jax.ShapeDtypeStruct((1, pw), F32)],
        name=name, compiler_params=_cparams(("arbitrary", "arbitrary"), 48),
    )(proj, proj, proj, proj, dya, dya, pool_w, pool_scale)


def _conv_cols(d, cw):
    cc = d // 2
    base = (d + 4 * d) // cw
    return base, base + cc // cw, base + 2 * (cc // cw), base + 3 * (cc // cw)


def _conv_fwd(proj, conv_w, *, d, tm, cw, name):
    t = proj.shape[0]
    cc = d // 2
    u_col, b_col, c_col, z_col = _conv_cols(d, cw)

    def body(u_ref, up_ref, c_ref, cp_ref, b_ref, z_ref, w_ref, y_ref):
        i = pl.program_id(1)
        v = u_ref[...].astype(F32) * c_ref[...].astype(F32)
        vprev = up_ref[...].astype(F32) * cp_ref[...].astype(F32)
        ve = jnp.concatenate([jnp.where(i == 0, 0.0, vprev), v], axis=0)
        w = w_ref[...]
        y = w[0:1] * pltpu.roll(ve, 2, 0)[HALO:] + w[1:2] * pltpu.roll(ve, 1, 0)[HALO:] + w[2:3] * v
        z = z_ref[...].astype(F32)
        y_ref[...] = (b_ref[...].astype(F32) * y * (z * _sigmoid(z))).astype(BF16)

    return pl.pallas_call(
        body, grid=(cc // cw, t // tm),
        in_specs=[_cur(tm, cw, u_col), _prev(tm, cw, u_col), _cur(tm, cw, c_col), _prev(tm, cw, c_col),
                  _cur(tm, cw, b_col), _cur(tm, cw, z_col),
                  pl.BlockSpec((8, cw), lambda cb, i: (0, cb))],
        out_specs=pl.BlockSpec((tm, cw), lambda cb, i: (i, cb)),
        out_shape=jax.ShapeDtypeStruct((t, cc), BF16), name=name,
        compiler_params=_cparams(("arbitrary", "arbitrary"), 40),
    )(proj, proj, proj, proj, proj, proj, conv_w)


def _conv_bwd(dyc, proj, conv_w, *, d, tm, cw, name):
    t = proj.shape[0]
    cc = d // 2
    n_i = t // tm
    u_col, b_col, c_col, z_col = _conv_cols(d, cw)

    def body(u_ref, up_ref, c_ref, cp_ref, b_ref, bn_ref, z_ref, zn_ref, dy_ref, dyn_ref, w_ref,
             du_ref, db_ref, dc_ref, dz_ref, dw_ref):
        i = pl.program_id(1)
        last = i == n_i - 1
        u = u_ref[...].astype(F32)
        cg = c_ref[...].astype(F32)
        v = u * cg
        vprev = up_ref[...].astype(F32) * cp_ref[...].astype(F32)
        ve = jnp.concatenate([jnp.where(i == 0, 0.0, vprev), v], axis=0)
        v2 = pltpu.roll(ve, 2, 0)[HALO:]
        v1 = pltpu.roll(ve, 1, 0)[HALO:]
        w = w_ref[...]
        y = w[0:1] * v2 + w[1:2] * v1 + w[2:3] * v
        z = z_ref[...].astype(F32)
        sg = _sigmoid(z)
        bg = b_ref[...].astype(F32)
        dyc_v = dy_ref[...].astype(F32)
        dz_ref[...] = (dyc_v * bg * y * (sg * (1.0 + z * (1.0 - sg)))).astype(BF16)
        db_ref[...] = (dyc_v * y * (z * sg)).astype(BF16)
        ze = _with_next(z, zn_ref, last)
        dy_e = (_with_next(dyc_v, dyn_ref, last) * _with_next(bg, bn_ref, last) * (ze * _sigmoid(ze)))
        rows = tm + HALO
        dy0 = dy_e[:tm]
        dv = (w[2:3] * dy0 + w[1:2] * pltpu.roll(dy_e, rows - 1, 0)[:tm]
              + w[0:1] * pltpu.roll(dy_e, rows - 2, 0)[:tm])
        du_ref[...] = (dv * cg).astype(BF16)
        dc_ref[...] = (dv * u).astype(BF16)
        s0 = jnp.sum(dy0 * v2, axis=0, keepdims=True)
        s1 = jnp.sum(dy0 * v1, axis=0, keepdims=True)
        s2 = jnp.sum(dy0 * v, axis=0, keepdims=True)
        r = lax.broadcasted_iota(jnp.int32, (8, cw), 0)
        dw = jnp.where(r == 0, s0, jnp.where(r == 1, s1, jnp.where(r == 2, s2, 0.0)))

        @pl.when(i == 0)
        def _():
            dw_ref[...] = dw

        @pl.when(i > 0)
        def _():
            dw_ref[...] += dw

    out_tile = pl.BlockSpec((tm, cw), lambda cb, i: (i, cb))
    act = jax.ShapeDtypeStruct((t, cc), BF16)
    return pl.pallas_call(
        body, grid=(cc // cw, n_i),
        in_specs=[_cur(tm, cw, u_col), _prev(tm, cw, u_col), _cur(tm, cw, c_col), _prev(tm, cw, c_col),
                  _cur(tm, cw, b_col), _next(tm, cw, b_col, t), _cur(tm, cw, z_col), _next(tm, cw, z_col, t),
                  _cur(tm, cw, 0), _next(tm, cw, 0, t),
                  pl.BlockSpec((8, cw), lambda cb, i: (0, cb))],
        out_specs=[out_tile, out_tile, out_tile, out_tile, pl.BlockSpec((8, cw), lambda cb, i: (0, cb))],
        out_shape=[act, act, act, act, jax.ShapeDtypeStruct((8, cc), F32)], name=name,
        compiler_params=_cparams(("arbitrary", "arbitrary"), 48),
    )(proj, proj, proj, proj, proj, proj, proj, proj, dyc, dyc, conv_w)


def _sum_matrix(prefix):
    r = lax.broadcasted_iota(jnp.int32, (KEY_CHUNK, KEY_CHUNK), 0)
    c = lax.broadcasted_iota(jnp.int32, (KEY_CHUNK, KEY_CHUNK), 1)
    return jnp.where((r <= c) if prefix else (r > c), 1.0, 0.0).astype(BF16)


def _chunk_sums(val, mat, prefix):
    hi = val.astype(BF16)
    part = jnp.dot(hi, mat, preferred_element_type=F32)
    if prefix:
        return part, part[:, KEY_CHUNK - 1:]
    return part, part[:, :1] + hi[:, :1].astype(F32)


def _stick_logits(qs, kb, strict):
    z = lax.dot_general(qs, kb, (((1,), (1,)), ((), ())), preferred_element_type=F32)
    minus_abs = pltpu.bitcast(pltpu.bitcast(z, jnp.int32) | jnp.int32(-2 ** 31), F32)
    sp = jnp.maximum(z, 0.0) + jnp.log(1.0 + jnp.exp(minus_abs))
    log_beta = z - sp
    if strict is not None:
        sp = jnp.where(strict, sp, 0.0)
    return sp, log_beta


def _running(val, mat, carry, upwards):
    n = val.shape[1] // KEY_CHUNK
    out = [None] * n
    for c in (range(n) if upwards else reversed(range(n))):
        part, total = _chunk_sums(val[:, c * KEY_CHUNK:(c + 1) * KEY_CHUNK], mat, upwards)
        out[c] = part + carry
        carry = carry + total
    return jnp.concatenate(out, axis=1), carry


def _attn_cols(d):
    h = d // HEAD_DIM
    return h, 2 * h, 3 * h, 4 * h


def _sweep(i, per, block, carry, upwards):
    if upwards:
        carry = lax.fori_loop(0, i * per, lambda it, cr: block(it, cr, False), carry)
        for dgl in range(per):
            carry = block(i * per + dgl, carry, True)
        return carry
    for dgl in range(per - 1, -1, -1):
        carry = block(i * per + dgl, carry, True)
    return lax.fori_loop(0, i * per, lambda it, cr: block(i * per - 1 - it, cr, False), carry)


def _strict_mask(i, tq, kb, width):
    t_idx = i * tq + lax.broadcasted_iota(jnp.int32, (tq, width), 0)
    s_idx = kb * width + lax.broadcasted_iota(jnp.int32, (tq, width), 1)
    return s_idx < t_idx


def _attn_fwd(proj, *, d, tq, name):
    t = proj.shape[0]
    heads = d // HEAD_DIM
    q_col, k_col, v_col, z_col = _attn_cols(d)

    def body(q_ref, k_ref, v_ref, z_ref, o_ref, y_ref, tot_ref):
        i = pl.program_id(1)
        qs = (q_ref[...].astype(F32) * (HEAD_DIM ** -0.5)).astype(BF16)
        mat = _sum_matrix(prefix=False)

        def block(kb, carry, masked):
            acc, after = carry
            off = pl.multiple_of(kb * KEY_BLOCK, KEY_BLOCK)
            strict = _strict_mask(i, tq, kb, KEY_BLOCK) if masked else None
            sp, log_beta = _stick_logits(qs, k_ref[pl.ds(off, KEY_BLOCK), :], strict)
            between, after = _running(sp, mat, after, upwards=False)
            a = jnp.exp(log_beta - between)
            if masked:
                a = jnp.where(strict, a, 0.0)
            acc = acc + jnp.dot(a.astype(BF16), v_ref[pl.ds(off, KEY_BLOCK), :], preferred_element_type=F32)
            return acc, after

        zero = jnp.zeros((tq, HEAD_DIM), F32)
        o, total = _sweep(i, tq // KEY_BLOCK, block, (zero, jnp.zeros((tq, 1), F32)), upwards=False)
        o_ref[...] = o.astype(BF16)
        tot_ref[...] = jnp.broadcast_to(total, (tq, HEAD_DIM))
        z = z_ref[...].astype(F32)
        y_ref[...] = (o * (z * _sigmoid(z))).astype(BF16)

    tile = lambda col: pl.BlockSpec((tq, HEAD_DIM), lambda h, i: (i, col + h))
    full = lambda col: pl.BlockSpec((t, HEAD_DIM), lambda h, i: (0, col + h))
    act = jax.ShapeDtypeStruct((t, d), BF16)
    return pl.pallas_call(
        body, grid=(heads, t // tq),
        in_specs=[tile(q_col), full(k_col), full(v_col), tile(z_col)],
        out_specs=[tile(0), tile(0), tile(0)], out_shape=[act, act, jax.ShapeDtypeStruct((t, d), F32)],
        name=name, compiler_params=_cparams(("arbitrary", "arbitrary"), 40),
    )(proj, proj, proj, proj)


def _attn_bwd(dyb, o, tot, proj, *, d, tq, name):
    t = proj.shape[0]
    heads = d // HEAD_DIM
    n_i = t // tq
    q_col, k_col, v_col, z_col = _attn_cols(d)
    scale = HEAD_DIM ** -0.5

    def body(q_ref, k_ref, v_ref, z_ref, o_ref, tot_ref, dy_ref, dq_ref, dk_ref, dv_ref, dz_ref, dk_acc, dv_acc):
        i = pl.program_id(1)

        @pl.when(i == 0)
        def _():
            dk_acc[...] = jnp.zeros_like(dk_acc)
            dv_acc[...] = jnp.zeros_like(dv_acc)

        qs = (q_ref[...].astype(F32) * scale).astype(BF16)
        z = z_ref[...].astype(F32)
        sg = _sigmoid(z)
        dy = dy_ref[...].astype(F32)
        of = o_ref[...].astype(F32)
        dz_ref[...] = (dy * of * (sg * (1.0 + z * (1.0 - sg)))).astype(BF16)
        do16 = (dy * (z * sg)).astype(BF16)
        all_keep = jnp.concatenate([tot_ref[...]] * (KEY_BLOCK // HEAD_DIM), axis=1)
        mat = _sum_matrix(prefix=True)

        def block(kb, carry, masked):
            dq, sp_seen, g_seen = carry
            off = pl.multiple_of(kb * KEY_BLOCK, KEY_BLOCK)
            kblk = k_ref[pl.ds(off, KEY_BLOCK), :]
            vblk = v_ref[pl.ds(off, KEY_BLOCK), :]
            strict = _strict_mask(i, tq, kb, KEY_BLOCK) if masked else None
            sp, log_beta = _stick_logits(qs, kblk, strict)
            sp_upto, sp_seen = _running(sp, mat, sp_seen, upwards=True)
            a = jnp.exp(log_beta - (all_keep - sp_upto))
            if masked:
                a = jnp.where(strict, a, 0.0)
            g = a * lax.dot_general(do16, vblk, (((1,), (1,)), ((), ())), preferred_element_type=F32)
            g_upto, g_seen = _running(g, mat, g_seen, upwards=True)
            dz_ts = g - jnp.exp(log_beta) * g_upto
            if masked:
                dz_ts = jnp.where(strict, dz_ts, 0.0)
            dz16 = dz_ts.astype(BF16)
            dq = dq + jnp.dot(dz16, kblk, preferred_element_type=F32)
            dk_acc[pl.ds(off, KEY_BLOCK), :] += lax.dot_general(
                dz16, qs, (((0,), (0,)), ((), ())), preferred_element_type=F32)
            dv_acc[pl.ds(off, KEY_BLOCK), :] += lax.dot_general(
                a.astype(BF16), do16, (((0,), (0,)), ((), ())), preferred_element_type=F32)
            return dq, sp_seen, g_seen

        column = jnp.zeros((tq, 1), F32)
        dq, _, _ = _sweep(i, tq // KEY_BLOCK, block, (jnp.zeros((tq, HEAD_DIM), F32), column, column),
                          upwards=True)
        dq_ref[...] = (dq * scale).astype(BF16)

        @pl.when(i == n_i - 1)
        def _():
            dk_ref[...] = dk_acc[...].astype(BF16)
            dv_ref[...] = dv_acc[...].astype(BF16)

    tile = lambda col: pl.BlockSpec((tq, HEAD_DIM), lambda h, i: (i, col + h))
    full = lambda col: pl.BlockSpec((t, HEAD_DIM), lambda h, i: (0, col + h))
    act = jax.ShapeDtypeStruct((t, d), BF16)
    return pl.pallas_call(
        body, grid=(heads, n_i),
        in_specs=[tile(q_col), full(k_col), full(v_col), tile(z_col), tile(0), tile(0), tile(0)],
        out_specs=[tile(0), full(0), full(0), tile(0)], out_shape=[act, act, act, act],
        scratch_shapes=[pltpu.VMEM((t, HEAD_DIM), F32), pltpu.VMEM((t, HEAD_DIM), F32)], name=name,
        compiler_params=_cparams(("arbitrary", "arbitrary"), 48),
    )(proj, proj, proj, proj, o, tot, dyb)


def _merge_fwd(gl, bst, *, d, tm, tn, name):
    t = gl.shape[0]
    nk = d // tn

    def body(g0, g1, g2, b0, b1, b2, m_ref):
        acc = _sigmoid(g0[...].astype(F32)) * b0[...].astype(F32)
        acc += _sigmoid(g1[...].astype(F32)) * b1[...].astype(F32)
        acc += _sigmoid(g2[...].astype(F32)) * b2[...].astype(F32)
        m_ref[...] = acc.astype(BF16)

    spec = lambda b: pl.BlockSpec((tm, tn), lambda i, kb: (i, b * nk + kb))
    return pl.pallas_call(
        body, grid=(t // tm, nk), in_specs=[spec(0), spec(1), spec(2)] * 2,
        out_specs=pl.BlockSpec((tm, tn), lambda i, kb: (i, kb)),
        out_shape=jax.ShapeDtypeStruct((t, d), BF16), name=name,
        compiler_params=_cparams(("arbitrary", "arbitrary"), 40),
    )(gl, gl, gl, bst, bst, bst)


def _merge_bwd(dmerged, gl, bst, *, d, tm, tn, name):
    t = gl.shape[0]
    nk = d // tn

    def body(dm_ref, g_ref, b_ref, db_ref, dg_ref, dbias_ref):
        i = pl.program_id(2)
        dm = dm_ref[...].astype(F32)
        sg = _sigmoid(g_ref[...].astype(F32))
        db_ref[...] = (dm * sg).astype(BF16)
        dgl = dm * b_ref[...].astype(F32) * (sg * (1.0 - sg))
        dg_ref[...] = dgl.astype(BF16)
        dbias = jnp.sum(dgl, axis=0, keepdims=True)

        @pl.when(i == 0)
        def _():
            dbias_ref[...] = dbias

        @pl.when(i > 0)
        def _():
            dbias_ref[...] += dbias

    wide = pl.BlockSpec((tm, tn), lambda b, kb, i: (i, b * nk + kb))
    act = jax.ShapeDtypeStruct((t, 3 * d), BF16)
    return pl.pallas_call(
        body, grid=(3, nk, t // tm),
        in_specs=[pl.BlockSpec((tm, tn), lambda b, kb, i: (i, kb)), wide, wide],
        out_specs=[wide, wide, pl.BlockSpec((1, tn), lambda b, kb, i: (0, b * nk + kb))],
        out_shape=[act, act, jax.ShapeDtypeStruct((1, 3 * d), F32)], name=name,
        compiler_params=_cparams(("arbitrary", "arbitrary", "arbitrary"), 40),
    )(dmerged, gl, bst)


def _adamw(w, m, v, parts, layer, prev, *, tr, name):
    depth, r, c = w.shape
    n = parts.shape[0]

    def body(*refs):
        w_ref, m_ref, v_ref, p_ref = refs[:4]
        g_ref, d_ref, nm_ref, nv_ref = refs[-4:]
        g = p_ref[0].astype(F32)
        for s in range(1, n):
            g = g + p_ref[s].astype(F32)
        wv = w_ref[...]
        nm = ADAM_B1 * m_ref[...] + (1.0 - ADAM_B1) * g
        nv = ADAM_B2 * v_ref[...] + (1.0 - ADAM_B2) * (g * g)
        m_hat = nm / (1.0 - ADAM_B1 ** ADAM_STEP)
        v_hat = nv / (1.0 - ADAM_B2 ** ADAM_STEP)
        g_ref[...] = g
        nm_ref[...] = nm
        nv_ref[...] = nv
        d_ref[...] = -ADAM_LR * (m_hat / (jnp.sqrt(v_hat) + ADAM_EPS) + ADAM_WD * wv)

    tile = pl.BlockSpec((None, tr, c), lambda i: (layer, i, 0))
    in_specs = [tile, tile, tile, pl.BlockSpec((n, tr, c), lambda i: (0, i, 0))]
    args = [w, m, v, parts]
    aliases = {}
    if prev is not None:
        in_specs += [pl.BlockSpec(memory_space=pl.ANY)] * 4
        aliases = {4 + k: k for k in range(4)}
        args += list(prev)
    full = jax.ShapeDtypeStruct((depth, r, c), F32)
    return pl.pallas_call(
        body, grid=(r // tr,), in_specs=in_specs, out_specs=[tile] * 4, out_shape=[full] * 4,
        input_output_aliases=aliases, name=name, compiler_params=_cparams(("arbitrary",), 48),
    )(*args)


def _adamw_layers(w, m, v, parts_by_layer, *, tr, name):
    lead = w.shape[0]
    flat = lambda a: a.reshape(lead, -1, a.shape[-1])
    w, m, v = flat(w), flat(m), flat(v)
    out = None
    for layer in reversed(range(lead)):
        parts = parts_by_layer[layer]
        parts = parts.reshape(parts.shape[0], -1, parts.shape[-1])
        out = _adamw(w, m, v, parts, layer, out, tr=min(tr, w.shape[1]), name=f"{name}_l{layer}")
    return out


def _place():
    return lax.axis_index("x"), lax.axis_index("y"), lax.axis_index("c")


def _index(px, py, pc):
    return 4 * px + 2 * py + pc


def _whole(ref, idx):
    return ref.at[idx]


def _row_block(rows):
    return lambda ref, idx: ref.at[:, pl.ds(idx * rows, rows), :]


def _lane_block(cols):
    return lambda ref, idx: ref.at[:, pl.ds(idx * cols, cols)]


def _all_gather(shards, out_shapes, placers, *, name):
    n = len(shards)

    def body(*refs):
        src, out = refs[:n], refs[n:2 * n]
        send_sems, recv_sems, local_sems = refs[2 * n:]
        x, y, c = _place()
        me, sibling = (x, y, c), (x, y, 1 - c)
        chips = [(1 - x, y), (x, 1 - y), (1 - x, 1 - y)]

        def copy(a, k, block, to, from_shard=False):
            window = placers[a](out[a], _index(*block))
            return pltpu.make_async_remote_copy(
                src_ref=src[a] if from_shard else window, dst_ref=window,
                send_sem=send_sems.at[a * 7 + k], recv_sem=recv_sems.at[a * 7 + k],
                device_id=to, device_id_type=MESH)

        mine = [pltpu.make_async_copy(src[a], placers[a](out[a], _index(*me)), local_sems.at[a])
                for a in range(n)]
        started = []
        for a in range(n):
            mine[a].start()
            started.append(copy(a, 0, me, sibling, True))
            started += [copy(a, 1 + j, me, (*chip, c), True) for j, chip in enumerate(chips)]
        for cp in started:
            cp.start()
        for j, chip in enumerate(chips):
            for a in range(n):
                copy(a, 1 + j, (*chip, c), me).wait_recv()
                passed = copy(a, 4 + j, (*chip, c), sibling)
                passed.start()
                started.append(passed)
        for a in range(n):
            copy(a, 0, sibling, me).wait_recv()
            for j, chip in enumerate(chips):
                copy(a, 4 + j, (*chip, 1 - c), me).wait_recv()
        for cp in started:
            cp.wait_send()
        for cp in mine:
            cp.wait()

    hbm = pl.BlockSpec(memory_space=pl.ANY)
    return pl.pallas_call(
        body, in_specs=[hbm] * n, out_specs=[hbm] * n, out_shape=out_shapes,
        scratch_shapes=[pltpu.SemaphoreType.DMA((7 * n,)), pltpu.SemaphoreType.DMA((7 * n,)),
                        pltpu.SemaphoreType.DMA((n,))],
        name=name,
    )(*shards)


def _scatter_parts(grads, pickers, part_shapes, *, name):
    n = len(grads)

    def body(*refs):
        src, out = refs[:n], refs[n:2 * n]
        send_sems, recv_sems, local_sems = refs[2 * n:]
        x, y, c = _place()
        my_idx = _index(x, y, c)
        peers = [(x ^ (k >> 2), y ^ ((k >> 1) & 1), c ^ (k & 1)) for k in range(1, N_DEV)]

        def copy(a, k, peer):
            return pltpu.make_async_remote_copy(
                src_ref=pickers[a](src[a], _index(*peer)), dst_ref=out[a].at[my_idx],
                send_sem=send_sems.at[a * 7 + k], recv_sem=recv_sems.at[a * 7 + k],
                device_id=peer, device_id_type=MESH)

        def arrival(a, k, peer):
            return pltpu.make_async_remote_copy(
                src_ref=pickers[a](src[a], my_idx), dst_ref=out[a].at[_index(*peer)],
                send_sem=send_sems.at[a * 7 + k], recv_sem=recv_sems.at[a * 7 + k],
                device_id=peer, device_id_type=MESH)

        mine = [pltpu.make_async_copy(pickers[a](src[a], my_idx), out[a].at[my_idx], local_sems.at[a])
                for a in range(n)]
        sent = [copy(a, k, peer) for a in range(n) for k, peer in enumerate(peers)]
        for cp in mine + sent:
            cp.start()
        for a in range(n):
            for k, peer in enumerate(peers):
                arrival(a, k, peer).wait_recv()
        for cp in sent:
            cp.wait_send()
        for cp in mine:
            cp.wait()

    hbm = pl.BlockSpec(memory_space=pl.ANY)
    return pl.pallas_call(
        body, in_specs=[hbm] * n, out_specs=[hbm] * n, out_shape=part_shapes,
        scratch_shapes=[pltpu.SemaphoreType.DMA((7 * n,)), pltpu.SemaphoreType.DMA((7 * n,)),
                        pltpu.SemaphoreType.DMA((n,))],
        name=name,
    )(*grads)


def _peers(x, y, c):
    return [(x ^ (k >> 2), y ^ ((k >> 1) & 1), c ^ (k & 1)) for k in range(1, N_DEV)]


_HBM = pl.BlockSpec(memory_space=pltpu.HBM)
_SEM = pl.BlockSpec(memory_space=pltpu.SEMAPHORE)
_EFFECT = pltpu.SideEffectType.DATAFLOW_SIDE_EFFECTING


def _exchange_start(srcs, land_shapes, src_win, dst_win, after, *, name):
    n = len(srcs)

    def body(*refs):
        src, land = refs[:n], refs[n:2 * n]
        send_sems, recv_sems, token = refs[2 * n + len(after)], refs[2 * n + len(after) + 1], refs[-1]
        x, y, c = _place()
        my_idx = _index(x, y, c)
        for a in range(n):
            for k, peer in enumerate(_peers(x, y, c)):
                pltpu.make_async_remote_copy(
                    src_ref=src_win[a](src[a], _index(*peer)), dst_ref=dst_win[a](land[a], my_idx),
                    send_sem=send_sems.at[a * 7 + k], recv_sem=recv_sems.at[a * 7 + k],
                    device_id=peer, device_id_type=MESH).start()
        token[...] = jnp.zeros_like(token)

    hbm = lambda a: pltpu.with_memory_space_constraint(a, pltpu.HBM)
    lands = [hbm(lax.empty(s.shape, s.dtype)) for s in land_shapes]
    args = [hbm(s) for s in srcs] + lands
    kept = [pltpu.HBM(a.shape, a.dtype) for a in args]
    out = pl.pallas_call(
        body, name=name,
        out_shape=(pltpu.SemaphoreType.DMA((7 * n,)), pltpu.SemaphoreType.DMA((7 * n,)), *kept,
                   jax.ShapeDtypeStruct((8, LANE), F32)),
        in_specs=[_HBM] * (2 * n) + [pl.BlockSpec(memory_space=pl.ANY)] * len(after),
        out_specs=(_SEM, _SEM, *([_HBM] * (2 * n)), pl.BlockSpec(memory_space=pltpu.VMEM)),
        input_output_aliases={i: 2 + i for i in range(2 * n)},
        compiler_params=pltpu.CompilerParams(has_side_effects=_EFFECT),
    )(*args, *after)
    return out[0], out[1], list(out[2:2 + n]), list(out[2 + n:2 + 2 * n]), out[-1][0, 0]


def _exchange_wait(send_sems, recv_sems, srcs, lands, src_win, dst_win, after, *, name):
    n = len(srcs)

    def body(*refs):
        src, land = refs[:n], refs[n:2 * n]
        send_sems, recv_sems = refs[2 * n], refs[2 * n + 1]
        x, y, c = _place()
        for a in range(n):
            for k, peer in enumerate(_peers(x, y, c)):
                sender = _index(*peer)
                copy = pltpu.make_async_remote_copy(
                    src_ref=src_win[a](src[a], sender), dst_ref=dst_win[a](land[a], sender),
                    send_sem=send_sems.at[a * 7 + k], recv_sem=recv_sems.at[a * 7 + k],
                    device_id=peer, device_id_type=MESH)
                copy.wait_send()
                copy.wait_recv()

    kept = [pltpu.HBM(a.shape, a.dtype) for a in list(srcs) + list(lands)]
    out = pl.pallas_call(
        body, name=name, out_shape=tuple(kept),
        in_specs=[_HBM] * (2 * n) + [_SEM, _SEM] + [pl.BlockSpec(memory_space=pl.ANY)] * len(after),
        out_specs=tuple([_HBM] * (2 * n)),
        input_output_aliases={i: i for i in range(2 * n)},
        compiler_params=pltpu.CompilerParams(has_side_effects=_EFFECT),
    )(*srcs, *lands, send_sems, recv_sems, *after)
    return list(out[:n]), list(out[n:])


COPY_TILE_BYTES = 2 << 20


def _copy_own_block(src, land, idx, *, from_stack, name):
    k, n = land.shape[1:]
    tk = k
    while tk * n * land.dtype.itemsize > COPY_TILE_BYTES and tk % 32 == 0:
        tk //= 2

    def body(idx_ref, src_ref, land_ref, out_ref):
        out_ref[...] = src_ref[...]

    own = pl.BlockSpec((None, tk, n), lambda i, idx_ref: (idx_ref[0], i, 0))
    grid_spec = pltpu.PrefetchScalarGridSpec(
        num_scalar_prefetch=1, grid=(k // tk,),
        in_specs=[own if from_stack else pl.BlockSpec((tk, n), lambda i, idx_ref: (i, 0)),
                  pl.BlockSpec(memory_space=pl.ANY)],
        out_specs=own)
    return pl.pallas_call(
        body, grid_spec=grid_spec, out_shape=jax.ShapeDtypeStruct(land.shape, land.dtype),
        input_output_aliases={2: 0}, name=name,
    )(idx, src, land)


def _place_own(srcs, lands, src_win, dst_win, *, name):
    lands = list(lands)
    idx = _index(*_place()).astype(jnp.int32).reshape(1)
    rest = []
    for a in range(len(srcs)):
        if dst_win[a] is _whole and src_win[a] in (_whole, _shard_itself) and lands[a].ndim == 3:
            lands[a] = _copy_own_block(srcs[a], lands[a], idx, from_stack=src_win[a] is _whole,
                                       name=f"{name}_{a}")
        else:
            rest.append(a)
    if rest:
        placed = _dma_own([srcs[a] for a in rest], [lands[a] for a in rest], [src_win[a] for a in rest],
                          [dst_win[a] for a in rest], name=f"{name}_dma")
        for a, land in zip(rest, placed):
            lands[a] = land
    return lands


def _dma_own(srcs, lands, src_win, dst_win, *, name):
    n = len(srcs)

    def body(*refs):
        src, land = refs[:n], refs[n:2 * n]
        sems = refs[-1]
        my_idx = _index(*_place())
        copies = [pltpu.make_async_copy(src_win[a](src[a], my_idx), dst_win[a](land[a], my_idx), sems.at[a])
                  for a in range(n)]
        for cp in copies:
            cp.start()
        for cp in copies:
            cp.wait()

    hbm = pl.BlockSpec(memory_space=pl.ANY)
    return list(pl.pallas_call(
        body, name=name, in_specs=[hbm] * (2 * n), out_specs=[hbm] * n,
        out_shape=[jax.ShapeDtypeStruct(a.shape, a.dtype) for a in lands],
        input_output_aliases={n + i: i for i in range(n)},
        scratch_shapes=[pltpu.SemaphoreType.DMA((n,))],
    )(*srcs, *lands))


def _ada_fwd(c8, w_ada, b_ada8, *, name):
    depth, d, n = w_ada.shape

    def body(c_ref, w_ref, b_ref, mod_ref, sc_ref, c_all, prod, got, send_sems, recv_sems):
        x, y, c = _place()
        my_idx = _index(x, y, c)
        peers = [(x ^ (k >> 2), y ^ ((k >> 1) & 1), c ^ (k & 1)) for k in range(1, N_DEV)]

        def slab(ref, idx):
            return ref.at[pl.ds(pl.multiple_of(idx * 8, 8), 8)]

        def c_copy(k, peer, sender):
            return pltpu.make_async_remote_copy(
                src_ref=c_ref, dst_ref=slab(c_all, sender), send_sem=send_sems.at[k],
                recv_sem=recv_sems.at[k], device_id=peer, device_id_type=MESH)

        def m_copy(k, peer, sender):
            return pltpu.make_async_remote_copy(
                src_ref=prod.at[_index(*peer)], dst_ref=got.at[sender], send_sem=send_sems.at[7 + k],
                recv_sem=recv_sems.at[7 + k], device_id=peer, device_id_type=MESH)

        sends = [c_copy(k, peer, my_idx) for k, peer in enumerate(peers)]
        for cp in sends:
            cp.start()
        slab(c_all, my_idx)[...] = c_ref[...]
        for k, peer in enumerate(peers):
            c_copy(k, peer, _index(*peer)).wait_recv()
        cv = c_all[...]
        sc = cv * _sigmoid(cv)
        sc_ref[...] = sc
        for layer in range(depth):
            p = jnp.dot(sc, w_ref[layer], preferred_element_type=F32, precision=lax.Precision.HIGHEST)
            for s in range(N_DEV):
                prod[s, layer] = p[8 * s:8 * s + 8]
        sends2 = [m_copy(k, peer, my_idx) for k, peer in enumerate(peers)]
        for cp in sends2:
            cp.start()
        got[my_idx] = prod[my_idx]
        for k, peer in enumerate(peers):
            m_copy(k, peer, _index(*peer)).wait_recv()
        for layer in range(depth):
            for s in range(N_DEV):
                mod_ref[layer, :, s * n:(s + 1) * n] = got[s, layer] + b_ref[layer, :, s * n:(s + 1) * n]
        for cp in sends + sends2:
            cp.wait_send()

    vmem = pl.BlockSpec(memory_space=pltpu.VMEM)
    return pl.pallas_call(
        body, in_specs=[vmem] * 3, out_specs=[vmem] * 2,
        out_shape=[jax.ShapeDtypeStruct((depth, 8, N_DEV * n), F32),
                   jax.ShapeDtypeStruct((8 * N_DEV, d), F32)],
        scratch_shapes=[pltpu.VMEM((8 * N_DEV, d), F32), pltpu.VMEM((N_DEV, depth, 8, n), F32),
                        pltpu.VMEM((N_DEV, depth, 8, n), F32),
                        pltpu.SemaphoreType.DMA((14,)), pltpu.SemaphoreType.DMA((14,))],
        name=name, compiler_params=pltpu.CompilerParams(vmem_limit_bytes=56 << 20),
    )(c8, w_ada, b_ada8)


def _ada_bwd(sc8, dmod, *, name):
    depth, _, n = dmod.shape
    d = sc8.shape[1]

    def body(sc_ref, dm_ref, o_ref):
        for layer in range(depth):
            o_ref[layer] = lax.dot_general(sc_ref[...], dm_ref[layer], (((0,), (0,)), ((), ())),
                                           preferred_element_type=F32, precision=lax.Precision.HIGHEST)

    vmem = pl.BlockSpec(memory_space=pltpu.VMEM)
    return pl.pallas_call(
        body, in_specs=[vmem] * 2, out_specs=vmem, out_shape=jax.ShapeDtypeStruct((depth, d, n), F32),
        name=name, compiler_params=pltpu.CompilerParams(vmem_limit_bytes=40 << 20),
    )(sc8, dmod)


def _gather_small(vec, *, name):
    r = vec.shape[0]

    def body(v_ref, o_ref, send_sems, recv_sems):
        x, y, c = _place()
        my_idx = _index(x, y, c)
        peers = [(x ^ (k >> 2), y ^ ((k >> 1) & 1), c ^ (k & 1)) for k in range(1, N_DEV)]

        def copy(k, peer, sender):
            return pltpu.make_async_remote_copy(
                src_ref=v_ref, dst_ref=o_ref.at[sender], send_sem=send_sems.at[k],
                recv_sem=recv_sems.at[k], device_id=peer, device_id_type=MESH)

        sends = [copy(k, peer, my_idx) for k, peer in enumerate(peers)]
        for cp in sends:
            cp.start()
        o_ref[my_idx] = v_ref[...]
        for k, peer in enumerate(peers):
            copy(k, peer, _index(*peer)).wait_recv()
        for cp in sends:
            cp.wait_send()

    vmem = pl.BlockSpec(memory_space=pltpu.VMEM)
    return pl.pallas_call(
        body, in_specs=[vmem], out_specs=vmem, out_shape=jax.ShapeDtypeStruct((N_DEV, r, LANE), F32),
        scratch_shapes=[pltpu.SemaphoreType.DMA((7,)), pltpu.SemaphoreType.DMA((7,))], name=name,
    )(vec)


TM = 512
TM_WIDE = 2048
TMM_DEEP = 1024
TQ = 512


def _layer_fwd(x, mod, small, gw, *, d, tag, more_weights=None):
    shift, scale, res_gate = mod
    h = _prenorm(x, small["norm_g"], scale, shift, tm=TM, name=f"prenorm_{tag}")
    proj = _mm_nn(h, gw["w_in"], out_dtype=BF16, tm=TM, name=f"in_proj_{tag}")
    ya = _pool_fwd(proj, gw["pool_w"], small["pool_scale"], d=d, tm=TM, name=f"pool_fwd_{tag}")
    o, yb, tot = _attn_fwd(proj, d=d, tq=TQ, name=f"attn_fwd_{tag}")
    yc = _conv_fwd(proj, gw["conv_w"], d=d, tm=TM, cw=LANE * 2, name=f"conv_fwd_{tag}")
    if more_weights is not None:
        gw = dict(gw, **more_weights(o))
    gl = _mm_nn(h, gw["w_gate"], out_dtype=BF16, tm=TM, bias=small["b_gate"], name=f"gate_proj_{tag}")
    nblk = gw["w_br_a"].shape[0]
    bst = _mm_nn(ya, gw["w_br_a"], out_dtype=BF16, tm=TM_WIDE, out_cols=3 * d, name=f"branch_a_{tag}")
    bst = _mm_nn(yb, gw["w_br_b"], out_dtype=BF16, tm=TM, out=bst, out_col_block=1, name=f"branch_b_{tag}")
    bst = _mm_nn(yc, gw["w_br_c"], out_dtype=BF16, tm=TM_WIDE, out=bst, out_col_block=2 * nblk,
                 name=f"branch_c_{tag}")
    merged = _merge_fwd(gl, bst, d=d, tm=TM, tn=TM, name=f"merge_fwd_{tag}")
    x_new, mo = _out_fwd(merged, gw["w_out"][0], x, res_gate, tm=TM, name=f"out_fwd_{tag}")
    return x_new, dict(x=x, h=h, proj=proj, gl=gl, ya=ya, yb=yb, yc=yc, o=o, tot=tot, bst=bst, merged=merged,
                       mo=mo), gw


EARLY = ("w_gate", "w_br_a", "w_br_b", "w_br_c", "w_out")
LATE = ("w_in", "pool_w")


def _layer_bwd(dout, sv, mod, small, gw, *, d, tag, send_early=None, send_late=None):
    shift, scale, res_gate = mod
    nblk = gw["w_br_a"].shape[0]
    dmo, d_res_gate = _out_bwd(dout, sv["mo"], res_gate, tm=TM, name=f"out_bwd_{tag}")
    g_w_out = _mm_tn(sv["merged"], dmo, nb=1, n=d, tk=TM, tmm=TMM_DEEP, name=f"dw_out_{tag}")
    dmerged = _mm_nt(dmo, gw["w_out"], out_dtype=BF16, tm=TM, name=f"dmerged_{tag}")
    dbst, dgl, d_b_gate = _merge_bwd(dmerged, sv["gl"], sv["bst"], d=d, tm=TM, tn=TM, name=f"merge_bwd_{tag}")
    g_w_gate = _mm_tn(sv["h"], dgl, nb=nblk, n=3 * d // nblk, tk=d // 2, tmm=TMM_DEEP,
                      name=f"dw_gate_{tag}")
    dh = _mm_nt(dgl, gw["w_gate"], out_dtype=F32, tm=TM, name=f"dh_gate_{tag}")
    g_w_br_a = _mm_tn(sv["ya"], dbst, nb=nblk, n=d // nblk, tk=d // 2, tmm=TM_WIDE,
                      name=f"dw_br_a_{tag}")
    g_w_br_b = _mm_tn(sv["yb"], dbst, nb=1, n=d, tk=TM, tmm=TMM_DEEP, col_block=1, name=f"dw_br_b_{tag}")
    g_w_br_c = _mm_tn(sv["yc"], dbst, nb=nblk, n=d // nblk, tk=d // 2, tmm=TM_WIDE, col_block=2 * nblk,
                      name=f"dw_br_c_{tag}")
    dya = _mm_nt(dbst, gw["w_br_a"], out_dtype=BF16, tm=TM_WIDE, name=f"dya_{tag}")
    dyb = _mm_nt(dbst, gw["w_br_b"], out_dtype=BF16, tm=TM, col_block=1, name=f"dyb_{tag}")
    dyc = _mm_nt(dbst, gw["w_br_c"], out_dtype=BF16, tm=TM_WIDE, col_block=2 * nblk, name=f"dyc_{tag}")
    big = dict(w_gate=g_w_gate, w_br_a=g_w_br_a, w_br_b=g_w_br_b, w_br_c=g_w_br_c, w_out=g_w_out)
    if send_early is not None:
        token = send_early(big)
        small = dict(small, pool_scale=small["pool_scale"] + token, norm_g=small["norm_g"] + token)
        dyb = dyb + token.astype(BF16)
    dxa, dza, g_pool_w, d_pool_scale = _pool_bwd(dya, sv["proj"], gw["pool_w"], small["pool_scale"], d=d, tm=TM,
                                                 name=f"pool_bwd_{tag}")
    dq, dk, dv, dzb = _attn_bwd(dyb, sv["o"], sv["tot"], sv["proj"], d=d, tq=TQ, name=f"attn_bwd_{tag}")
    du, dbg, dcg, dzc, d_conv_w = _conv_bwd(dyc, sv["proj"], gw["conv_w"], d=d, tm=TM, cw=LANE * 2,
                                            name=f"conv_bwd_{tag}")
    dproj = jnp.concatenate([dxa, dza, dq, dk, dv, dzb, du, dbg, dcg, dzc], axis=1)
    g_w_in = _mm_tn(sv["h"], dproj, nb=nblk, n=7 * d // nblk, tk=d // 2, tmm=TMM_DEEP,
                    name=f"dw_in_{tag}")
    big.update(w_in=g_w_in, pool_w=g_pool_w.astype(BF16))
    if send_late is not None:
        dh = dh + send_late(big)
    dh = _mm_nt(dproj, gw["w_in"], out_dtype=F32, tm=TM, addend=dh, name=f"dh_in_{tag}")
    dx, d_norm_g, d_scale, d_shift = _prenorm_bwd(dh, sv["x"], dout, small["norm_g"], scale, tm=TM,
                                                  name=f"prenorm_bwd_{tag}")
    part = dict(norm_g=d_norm_g, mod=jnp.concatenate([d_shift, d_scale, d_res_gate], axis=1),
                pool_scale=d_pool_scale, b_gate=d_b_gate, conv_w=d_conv_w[:3])
    return dx, big, part


BIG = ("w_in", "w_gate", "w_br_a", "w_br_b", "w_br_c", "w_out", "pool_w")
SMALL_PER_LAYER = ("norm_g", "mod", "pool_scale", "b_gate", "conv_w")


def _gather_layer(w, layer, *, d):
    nd = N_DEV
    gd = d // 2 // N_POOL_GROUPS
    cc = d // 2
    shards = [w[k][layer].astype(BF16) for k in BIG]
    shards.append(jnp.pad(w["conv_w"][layer], ((0, 5), (0, 0))))
    blocks = [jax.ShapeDtypeStruct((nd,) + s.shape, BF16) for s in shards[:6]]
    blocks.append(jax.ShapeDtypeStruct((N_POOL_GROUPS, gd, gd), BF16))
    blocks.append(jax.ShapeDtypeStruct((8, cc), F32))
    placers = [_whole] * 6 + [_row_block(gd // nd), _lane_block(cc // nd)]
    return shards, blocks, placers


GATHERED = BIG + ("conv_w",)
FIRST = (0, 6, 7)
REST = (1, 2, 3, 4, 5)


def _as_weights(got, d, which=tuple(range(len(GATHERED)))):
    gw = dict(zip([GATHERED[i] for i in which], got))
    for k in ("w_br_b", "w_out"):
        if k in gw:
            gw[k] = gw[k].reshape(1, d, d)
    return gw


def _shard_itself(ref, idx):
    return ref


def _scatter_spec(big, keys, *, d):
    nd = N_DEV
    gd = d // 2 // N_POOL_GROUPS
    grads, pickers, shapes = [], [], []
    for k in keys:
        g = big[k]
        if k == "pool_w":
            pickers.append(_row_block(gd // nd))
            shapes.append(jax.ShapeDtypeStruct((nd, N_POOL_GROUPS, gd // nd, gd), BF16))
        else:
            if g.shape[0] == 1:
                g = g.reshape(nd, g.shape[1] // nd, g.shape[2])
            pickers.append(_whole)
            shapes.append(jax.ShapeDtypeStruct(g.shape, BF16))
        grads.append(g)
    return grads, pickers, shapes


class _Behind:
    def __init__(self, srcs, land_shapes, src_win, dst_win, after, name):
        self.src_win, self.dst_win, self.name = src_win, dst_win, name
        self.send, self.recv, self.srcs, self.lands, self.token = _exchange_start(
            srcs, land_shapes, src_win, dst_win, after, name=f"{name}_start")

    def finish(self, after):
        srcs, lands = _exchange_wait(self.send, self.recv, self.srcs, self.lands, self.src_win, self.dst_win,
                                     after, name=f"{self.name}_wait")
        return _place_own(srcs, lands, self.src_win, self.dst_win, name=f"{self.name}_own")


def _pack(pieces):
    return jnp.concatenate([p.reshape(-1, LANE) for p in pieces], axis=0)


def kernel(x, c, norm_g, w_ada, b_ada, w_in, pool_w, pool_scale, conv_w, w_br_a, w_br_b, w_br_c, w_gate, b_gate, w_out, final_g, loss_target, m_norm_g, m_w_ada, m_b_ada, m_w_in, m_pool_w, m_pool_scale, m_conv_w, m_w_br_a, m_w_br_b, m_w_br_c, m_w_gate, m_b_gate, m_w_out, m_final_g, v_norm_g, v_w_ada, v_b_ada, v_w_in, v_pool_w, v_pool_scale, v_conv_w, v_w_br_a, v_w_br_b, v_w_br_c, v_w_gate, v_b_gate, v_w_out, v_final_g):
    names = ("norm_g", "w_ada", "b_ada", "w_in", "pool_w", "pool_scale", "conv_w", "w_br_a", "w_br_b",
             "w_br_c", "w_gate", "b_gate", "w_out", "final_g")
    w = dict(zip(names, (norm_g, w_ada, b_ada, w_in, pool_w, pool_scale, conv_w, w_br_a, w_br_b, w_br_c,
                         w_gate, b_gate, w_out, final_g)))
    m = dict(zip(names, (m_norm_g, m_w_ada, m_b_ada, m_w_in, m_pool_w, m_pool_scale, m_conv_w, m_w_br_a,
                         m_w_br_b, m_w_br_c, m_w_gate, m_b_gate, m_w_out, m_final_g)))
    v = dict(zip(names, (v_norm_g, v_w_ada, v_b_ada, v_w_in, v_pool_w, v_pool_scale, v_conv_w, v_w_br_a,
                         v_w_br_b, v_w_br_c, v_w_gate, v_b_gate, v_w_out, v_final_g)))
    _, t, d = x.shape
    depth = norm_g.shape[0]
    cc = d // 2
    my_idx = _index(*_place())

    mod8, sc_all = _ada_fwd(jnp.broadcast_to(c, (8, d)), w_ada,
                            jnp.broadcast_to(b_ada[:, None, :], (depth, 8, 3 * d)), name="ada_fwd")
    mods = [tuple(mod8[l, 0:1, k * d:(k + 1) * d] for k in range(3)) for l in range(depth)]
    small = [dict(norm_g=norm_g[l][None], b_gate=b_gate[l][None], pool_scale=pool_scale[l][None])
             for l in range(depth)]

    shards, blocks, placers = _gather_layer(w, 0, d=d)
    pick = lambda seq, which: [seq[i] for i in which]
    first = _all_gather(pick(shards, FIRST), pick(blocks, FIRST), pick(placers, FIRST), name="gather_first_l0")
    rest = _Behind(pick(shards, REST), pick(blocks, REST), [_shard_itself] * len(REST), pick(placers, REST),
                   (mod8, first[-1]), "gather_rest_l0")
    gws = [_as_weights(first, d, FIRST)]
    order = rest.token
    xs = x[0]
    saved = []
    for l in range(depth):
        coming = None
        if l + 1 < depth:
            shards, blocks, placers = _gather_layer(w, l + 1, d=d)
            coming = []
            for which, part in ((FIRST, "first"), (REST, "rest")):
                coming.append(_Behind(pick(shards, which), pick(blocks, which), [_shard_itself] * len(which),
                                      pick(placers, which), (mod8, gws[l]["conv_w"], order.reshape(1, 1)),
                                      f"gather_{part}_l{l + 1}"))
                order = order + coming[-1].token
        shift, scale, res_gate = mods[l]
        mods[l] = (shift + order, scale, res_gate)
        xs, sv, gws[l] = _layer_fwd(xs, mods[l], small[l], gws[l], d=d, tag=f"l{l}",
                                    more_weights=lambda o, rest=rest: _as_weights(rest.finish((o,)), d, REST))
        saved.append(sv)
        if coming is not None:
            gws.append(_as_weights(coming[0].finish((xs,)), d, FIRST))
            rest = coming[1]
    dx, loss8, d_final_g = _loss_head(xs, loss_target[0], final_g[None], tm=TM, name="loss_head")
    loss = lax.psum(loss8[0, 0], ("x", "y", "c"))

    parts, small_parts = [dict() for _ in range(depth)], [None] * depth
    going, last = [], []

    def send(queue, keys, l, name):
        def start(grads_so_far):
            grads, pickers, shapes = _scatter_spec(grads_so_far, keys, d=d)
            queue.append((l, keys, _Behind(grads, shapes, pickers, [_whole] * len(grads), (grads[0],), name)))
            return queue[-1][2].token
        return start

    for l in reversed(range(depth)):
        if going:
            shift, scale, res_gate = mods[l]
            mods[l] = (shift, scale, res_gate + going[-1][2].token)
        lowest = l == 0
        dx, big, small_parts[l] = _layer_bwd(
            dx, saved[l], mods[l], small[l], gws[l], d=d, tag=f"l{l}",
            send_early=send(going, EARLY, l, f"scatter_early_l{l}") if lowest else None,
            send_late=send(last, LATE, l, f"scatter_late_l{l}") if lowest else None)
        if not lowest:
            send(going, BIG, l, f"scatter_grads_l{l}")(big)
    for l, keys, behind in going:
        parts[l].update(zip(keys, behind.finish((dx,))))

    pieces = [small_parts[l][k] for l in range(depth) for k in SMALL_PER_LAYER] + [d_final_g]
    sizes = [p.size for p in pieces]
    gathered = _gather_small(_pack(pieces), name="gather_small_grads")
    zero_conv = jnp.zeros((3, cc), F32)

    def packed(src):
        per_layer = lambda l: [src["norm_g"][l], src["b_ada"][l], src["pool_scale"][l], src["b_gate"][l],
                               zero_conv]
        return _pack([p for l in range(depth) for p in per_layer(l)] + [src["final_g"]])[None]

    small_out = _adamw(packed(w), packed(m), packed(v), gathered, 0, None, tr=gathered.shape[1],
                       name="adamw_small")

    def unpack(flat):
        flat = flat.reshape(-1)
        out, off = [], 0
        for n in sizes:
            out.append(flat[off:off + n])
            off += n
        return out

    small_res = [unpack(a) for a in small_out]
    k_per = len(SMALL_PER_LAYER)

    def small_stack(which, pos, shape):
        return jnp.stack([small_res[which][l * k_per + pos] for l in range(depth)]).reshape(shape)

    res = {}
    for pos, name in ((0, "norm_g"), (1, "b_ada"), (2, "pool_scale"), (3, "b_gate")):
        res[name] = tuple(small_stack(k, pos, w[name].shape) for k in range(4))
    res["final_g"] = tuple(small_res[k][-1].reshape(final_g.shape) for k in range(4))

    ncol = cc // N_DEV
    conv_total = small_stack(0, 4, (depth * 3, cc))
    conv_mine = lax.dynamic_slice_in_dim(conv_total, my_idx * ncol, ncol, axis=1)
    pad8 = lambda a: jnp.pad(a.reshape(depth * 3, ncol), ((0, 8 - depth * 3), (0, 0)))[None]
    conv_out = _adamw(pad8(conv_w), pad8(m["conv_w"]), pad8(v["conv_w"]), pad8(conv_mine), 0, None, tr=8,
                      name="adamw_conv_w")
    res["conv_w"] = tuple(a[0, :depth * 3].reshape(conv_w.shape) for a in conv_out)

    n_ada = 3 * d // N_DEV
    mod_off = [sum(sizes[:l * k_per + 1]) // LANE for l in range(depth)]
    dmod = jnp.stack([
        lax.dynamic_slice_in_dim(gathered[:, mod_off[l]:mod_off[l] + 3 * d // LANE].reshape(N_DEV, 3 * d),
                                 my_idx * n_ada, n_ada, axis=1) for l in range(depth)])
    g_w_ada = _ada_bwd(sc_all[::8], dmod, name="ada_bwd")
    res["w_ada"] = _adamw_layers(w_ada, m["w_ada"], v["w_ada"], [g_w_ada[l][None] for l in range(depth)],
                                 tr=128, name="adamw_w_ada")

    def big_adamw(name):
        out = _adamw_layers(w[name], m[name], v[name], [parts[l][name] for l in range(depth)], tr=128,
                            name=f"adamw_{name}")
        res[name] = tuple(a.reshape(w[name].shape) for a in out)

    for name in EARLY:
        big_adamw(name)
    for l, keys, behind in last:
        parts[l].update(zip(keys, behind.finish(tuple(res[name][1] for name in ("w_ada",) + EARLY))))
    for name in LATE:
        big_adamw(name)

    outs = [loss, dx[None]]
    for k in range(4):
        outs += [res[name][k] for name in names]
    return tuple(outs)
```

```python
import functools

import jax
import jax.numpy as jnp
from jax import lax
from jax.experimental import pallas as pl
from jax.experimental.pallas import tpu as pltpu

F32 = jnp.float32
BF16 = jnp.bfloat16
MESH = pl.DeviceIdType.MESH

N_DEV = 8
DEPTH = 2
HEAD_DIM = 128
N_POOL_GROUPS = 4
POOL_WINDOWS = (2, 4, 8, 16)
RMS_EPS = 1e-6
LANE = 128
HALO = 16
KEY_CHUNK = 256
KEY_BLOCK = 512
ADAM_LR = 0.001
ADAM_B1 = 0.9
ADAM_B2 = 0.999
ADAM_EPS = 1e-08
ADAM_WD = 0.01
ADAM_STEP = 10


def _cparams(semantics, vmem_mb):
    return pltpu.CompilerParams(dimension_semantics=semantics, vmem_limit_bytes=vmem_mb << 20)


def _sigmoid(z):
    return 1.0 / (1.0 + jnp.exp(-z))


def _mm_nn(a, w, *, out_dtype, tm, name, bias=None, out=None, out_cols=None, out_col_block=0):
    m, k = a.shape
    nb, _, n = w.shape
    tm = min(tm, m)
    total_cols =out.shape[1] if out is not None else (nb * n if out_cols is None else out_cols)

    def body(*refs):
        if out is not None:
            refs = refs[:-2] + refs[-1:]
        if bias is not None:
            a_ref, w_ref, b_ref, o_ref = refs
        else:
            a_ref, w_ref, o_ref = refs
        acc = jnp.dot(a_ref[...], w_ref[...], preferred_element_type=F32)
        if bias is not None:
            acc = acc + b_ref[...]
        o_ref[...] = acc.astype(out_dtype)

    in_specs = [pl.BlockSpec((tm, k), lambda j, i: (i, 0)),
                pl.BlockSpec((None, k, n), lambda j, i: (j, 0, 0))]
    args = [a, w]
    if bias is not None:
        in_specs.append(pl.BlockSpec((1, n), lambda j, i: (0, j)))
        args.append(bias)
    aliases = {}
    if out is not None:
        in_specs.append(pl.BlockSpec(memory_space=pl.ANY))
        aliases = {len(args): 0}
        args.append(out)
    return pl.pallas_call(
        body, grid=(nb, m // tm), in_specs=in_specs,
        out_specs=pl.BlockSpec((tm, n), lambda j, i: (i, out_col_block + j)),
        out_shape=jax.ShapeDtypeStruct((m, total_cols), out_dtype),
        input_output_aliases=aliases, name=name,
        compiler_params=_cparams(("arbitrary", "arbitrary"), 56),
    )(*args)


def _mm_nt(dy, w, *, out_dtype, tm, name, addend=None, col_block=0):
    m = dy.shape[0]
    nb, k, n = w.shape
    tm = min(tm, m)

    def body(*refs):
        if addend is not None:
            dy_ref, w_ref, add_ref, o_ref, acc_ref = refs
        else:
            dy_ref, w_ref, o_ref, acc_ref = refs
        j = pl.program_id(1)
        part = lax.dot_general(dy_ref[...], w_ref[...], (((1,), (1,)), ((), ())),
                               preferred_element_type=F32)

        @pl.when(j == 0)
        def _():
            acc_ref[...] = part if addend is None else part + add_ref[...]

        @pl.when(j > 0)
        def _():
            acc_ref[...] += part

        @pl.when(j == nb - 1)
        def _():
            o_ref[...] = acc_ref[...].astype(out_dtype)

    in_specs = [pl.BlockSpec((tm, n), lambda i, j: (i, col_block + j)),
                pl.BlockSpec((None, k, n), lambda i, j: (j, 0, 0))]
    args = [dy, w]
    if addend is not None:
        in_specs.append(pl.BlockSpec((tm, k), lambda i, j: (i, 0)))
        args.append(addend)
    return pl.pallas_call(
        body, grid=(m // tm, nb), in_specs=in_specs,
        out_specs=pl.BlockSpec((tm, k), lambda i, j: (i, 0)),
        out_shape=jax.ShapeDtypeStruct((m, k), out_dtype),
        scratch_shapes=[pltpu.VMEM((tm, k), F32)], name=name,
        compiler_params=_cparams(("arbitrary", "arbitrary"), 56),
    )(*args)


def _mm_tn(a, dy, *, nb, n, tk, tmm, name, col_block=0, out_dtype=BF16):
    m, k = a.shape
    tmm = min(tmm, m)

    def body(a_ref, dy_ref, o_ref, acc_ref):
        t = pl.program_id(2)
        part = lax.dot_general(a_ref[...], dy_ref[...], (((0,), (0,)), ((), ())),
                               preferred_element_type=F32)

        @pl.when(t == 0)
        def _():
            acc_ref[...] = part

        @pl.when(t > 0)
        def _():
            acc_ref[...] += part

        @pl.when(t == pl.num_programs(2) - 1)
        def _():
            o_ref[...] = acc_ref[...].astype(out_dtype)

    return pl.pallas_call(
        body, grid=(nb, k // tk, m // tmm),
        in_specs=[pl.BlockSpec((tmm, tk), lambda j, kb, t: (t, kb)),
                  pl.BlockSpec((tmm, n), lambda j, kb, t: (t, col_block + j))],
        out_specs=pl.BlockSpec((None, tk, n), lambda j, kb, t: (j, kb, 0)),
        out_shape=jax.ShapeDtypeStruct((nb, k, n), out_dtype),
        scratch_shapes=[pltpu.VMEM((tk, n), F32)], name=name,
        compiler_params=_cparams(("arbitrary", "arbitrary", "arbitrary"), 56),
    )(a, dy)


def _prenorm(x, g, scale, shift, *, tm, name):
    t, d = x.shape

    def body(x_ref, g_ref, sc_ref, sh_ref, h_ref):
        xv = x_ref[...]
        r = lax.rsqrt(jnp.mean(xv * xv, axis=-1, keepdims=True) + RMS_EPS)
        h_ref[...] = ((xv * r) * g_ref[...] * (1.0 + sc_ref[...]) + sh_ref[...]).astype(BF16)

    row = pl.BlockSpec((1, d), lambda i: (0, 0))
    tile = pl.BlockSpec((tm, d), lambda i: (i, 0))
    return pl.pallas_call(
        body, grid=(t // tm,), in_specs=[tile, row, row, row], out_specs=tile,
        out_shape=jax.ShapeDtypeStruct((t, d), BF16), name=name,
        compiler_params=_cparams(("arbitrary",), 40),
    )(x, g, scale, shift)


def _prenorm_bwd(dh, x, dout, g, scale, *, tm, name):
    t, d = x.shape

    def body(dh_ref, x_ref, do_ref, g_ref, sc_ref, dx_ref, dg_ref, dsc_ref, dsh_ref):
        i = pl.program_id(0)
        xv = x_ref[...]
        dhv = dh_ref[...]
        r = lax.rsqrt(jnp.mean(xv * xv, axis=-1, keepdims=True) + RMS_EPS)
        xn = xv * r
        gain = g_ref[...] * (1.0 + sc_ref[...])
        dxn = dhv * gain
        dx_ref[...] = do_ref[...] + r * (dxn - xn * jnp.mean(dxn * xn, axis=-1, keepdims=True))
        dhxn = dhv * xn
        dg = jnp.sum(dhxn * (1.0 + sc_ref[...]), axis=0, keepdims=True)
        dsc = jnp.sum(dhxn * g_ref[...], axis=0, keepdims=True)
        dsh = jnp.sum(dhv, axis=0, keepdims=True)

        @pl.when(i == 0)
        def _():
            dg_ref[...] = dg
            dsc_ref[...] = dsc
            dsh_ref[...] = dsh

        @pl.when(i > 0)
        def _():
            dg_ref[...] += dg
            dsc_ref[...] += dsc
            dsh_ref[...] += dsh

    row = pl.BlockSpec((1, d), lambda i: (0, 0))
    tile = pl.BlockSpec((tm, d), lambda i: (i, 0))
    vec = jax.ShapeDtypeStruct((1, d), F32)
    return pl.pallas_call(
        body, grid=(t // tm,), in_specs=[tile, tile, tile, row, row],
        out_specs=[tile, row, row, row],
        out_shape=[jax.ShapeDtypeStruct((t, d), F32), vec, vec, vec], name=name,
        compiler_params=_cparams(("arbitrary",), 48),
    )(dh, x, dout, g, scale)


def _out_fwd(merged, w_out, x, res_gate, *, tm, name):
    t, d = x.shape
    k = merged.shape[1]

    def body(m_ref, w_ref, x_ref, rg_ref, xo_ref, mo_ref):
        mo = jnp.dot(m_ref[...], w_ref[...], preferred_element_type=F32)
        mo_ref[...] = mo.astype(BF16)
        xo_ref[...] = x_ref[...] + rg_ref[...] * mo

    tile = pl.BlockSpec((tm, d), lambda i: (i, 0))
    return pl.pallas_call(
        body, grid=(t // tm,),
        in_specs=[pl.BlockSpec((tm, k), lambda i: (i, 0)), pl.BlockSpec((k, d), lambda i: (0, 0)),
                  tile, pl.BlockSpec((1, d), lambda i: (0, 0))],
        out_specs=[tile, tile],
        out_shape=[jax.ShapeDtypeStruct((t, d), F32), jax.ShapeDtypeStruct((t, d), BF16)], name=name,
        compiler_params=_cparams(("arbitrary",), 56),
    )(merged, w_out, x, res_gate)


def _out_bwd(dout, mo, res_gate, *, tm, name):
    t, d = dout.shape

    def body(do_ref, mo_ref, rg_ref, dmo_ref, drg_ref):
        i = pl.program_id(0)
        dov = do_ref[...]
        dmo_ref[...] = (dov * rg_ref[...]).astype(BF16)
        drg = jnp.sum(dov * mo_ref[...].astype(F32), axis=0, keepdims=True)

        @pl.when(i == 0)
        def _():
            drg_ref[...] = drg

        @pl.when(i > 0)
        def _():
            drg_ref[...] += drg

    row = pl.BlockSpec((1, d), lambda i: (0, 0))
    tile = pl.BlockSpec((tm, d), lambda i: (i, 0))
    return pl.pallas_call(
        body, grid=(t // tm,), in_specs=[tile, tile, row], out_specs=[tile, row],
        out_shape=[jax.ShapeDtypeStruct((t, d), BF16), jax.ShapeDtypeStruct((1, d), F32)], name=name,
        compiler_params=_cparams(("arbitrary",), 40),
    )(dout, mo, res_gate)


def _loss_head(x, target, final_g, *, tm, name):
    t, d = x.shape

    def body(x_ref, t_ref, g_ref, dx_ref, loss_ref, dg_ref):
        i = pl.program_id(0)
        xv = x_ref[...]
        r = lax.rsqrt(jnp.mean(xv * xv, axis=-1, keepdims=True) + RMS_EPS)
        xn = xv * r
        err = xn * g_ref[...] - t_ref[...]
        part = (0.5 / d) * jnp.sum(jnp.sum(err * err, axis=1, keepdims=True), axis=0, keepdims=True)
        dy = err * (1.0 / d)
        dxn = dy * g_ref[...]
        dx_ref[...] = r * (dxn - xn * jnp.mean(dxn * xn, axis=-1, keepdims=True))
        dg = jnp.sum(dy * xn, axis=0, keepdims=True)

        @pl.when(i == 0)
        def _():
            loss_ref[...] = jnp.zeros_like(loss_ref) + part
            dg_ref[...] = dg

        @pl.when(i > 0)
        def _():
            loss_ref[...] += part
            dg_ref[...] += dg

    row = pl.BlockSpec((1, d), lambda i: (0, 0))
    tile = pl.BlockSpec((tm, d), lambda i: (i, 0))
    return pl.pallas_call(
        body, grid=(t // tm,), in_specs=[tile, tile, row],
        out_specs=[tile, pl.BlockSpec((8, LANE), lambda i: (0, 0)), row],
        out_shape=[jax.ShapeDtypeStruct((t, d), F32), jax.ShapeDtypeStruct((8, LANE), F32),
                   jax.ShapeDtypeStruct((1, d), F32)], name=name,
        compiler_params=_cparams(("arbitrary",), 40),
    )(x, target, final_g)


def _cur(tm, cw, col):
    return pl.BlockSpec((tm, cw), lambda cb, i: (i, col + cb))


def _prev(tm, cw, col):
    return pl.BlockSpec((HALO, cw), lambda cb, i: (jnp.maximum(i * (tm // HALO) - 1, 0), col + cb))


def _next(tm, cw, col, t):
    last = t // HALO - 1
    return pl.BlockSpec((HALO, cw), lambda cb, i: (jnp.minimum((i + 1) * (tm // HALO), last), col + cb))


def _with_prev(prev_ref, cur, first):
    prev = jnp.where(first, 0.0, prev_ref[...].astype(F32))
    return jnp.concatenate([prev, cur], axis=0)


def _with_next(cur, next_ref, last):
    nxt = jnp.where(last, 0.0, next_ref[...].astype(F32))
    return jnp.concatenate([cur, nxt], axis=0)


def _pool_mixed(xe, cur, g, row0, tm):
    s2 = xe + pltpu.roll(xe, 1, 0)
    s4 = s2 + pltpu.roll(s2, 2, 0)
    s8 = s4 + pltpu.roll(s4, 4, 0)
    s16 = s8 + pltpu.roll(s8, 8, 0)
    win = jnp.where(g == 0, s2, jnp.where(g == 1, s4, jnp.where(g == 2, s8, s16)))[HALO:]
    return win * _pool_inv_count(g, row0, tm, cur.shape[1]) - cur


def _pool_inv_count(g, row0, rows, cols):
    tpos = row0 + lax.broadcasted_iota(jnp.int32, (rows, cols), 0)
    width = jnp.left_shift(2, g)
    return 1.0 / jnp.minimum(tpos + 1, width).astype(F32)


def _pool_fwd(proj, pool_w, pool_scale, *, d, tm, name):
    t = proj.shape[0]
    pw = d // 2
    gd = pw // N_POOL_GROUPS
    za_col = pw // gd

    def body(xa_ref, xp_ref, za_ref, w_ref, ps_ref, ya_ref):
        g = pl.program_id(0)
        i = pl.program_id(1)
        cur = xa_ref[...].astype(F32)
        mixed = _pool_mixed(_with_prev(xp_ref, cur, i == 0), cur, g, i * tm, tm)
        ypre = jnp.dot(mixed.astype(BF16), w_ref[...], preferred_element_type=F32)
        za = za_ref[...].astype(F32)
        ya_ref[...] = (ypre * ps_ref[...] * (za * _sigmoid(za))).astype(BF16)

    return pl.pallas_call(
        body, grid=(N_POOL_GROUPS, t // tm),
        in_specs=[_cur(tm, gd, 0), _prev(tm, gd, 0), _cur(tm, gd, za_col),
                  pl.BlockSpec((None, gd, gd), lambda g, i: (g, 0, 0)),
                  pl.BlockSpec((1, gd), lambda g, i: (0, g))],
        out_specs=pl.BlockSpec((tm, gd), lambda g, i: (i, g)),
        out_shape=jax.ShapeDtypeStruct((t, pw), BF16), name=name,
        compiler_params=_cparams(("arbitrary", "arbitrary"), 40),
    )(proj, proj, proj, pool_w, pool_scale)


def _pool_bwd(dya, proj, pool_w, pool_scale, *, d, tm, name):
    t = proj.shape[0]
    pw = d // 2
    gd = pw // N_POOL_GROUPS
    za_col = pw // gd
    n_i = t // tm

    def body(xa_ref, xp_ref, za_ref, zn_ref, dy_ref, dyn_ref, w_ref, ps_ref,
             dxa_ref, dza_ref, dw_ref, dps_ref):
        g = pl.program_id(0)
        i = pl.program_id(1)
        last = i == n_i - 1
        cur = xa_ref[...].astype(F32)
        mixed = _pool_mixed(_with_prev(xp_ref, cur, i == 0), cur, g, i * tm, tm)
        mixed16 = mixed.astype(BF16)
        ypre = jnp.dot(mixed16, w_ref[...], preferred_element_type=F32)
        za = za_ref[...].astype(F32)
        sg = _sigmoid(za)
        dy = dy_ref[...].astype(F32)
        dza_ref[...] = (dy * (ypre * ps_ref[...]) * (sg * (1.0 + za * (1.0 - sg)))).astype(BF16)
        dysc = dy * (za * sg)
        za_e = _with_next(za, zn_ref, last)
        dy_e = _with_next(dy, dyn_ref, last)
        dypre_e = (dy_e * (za_e * _sigmoid(za_e)) * ps_ref[...]).astype(BF16)
        dm_e = lax.dot_general(dypre_e, w_ref[...], (((1,), (1,)), ((), ())), preferred_element_type=F32)
        rows = tm + HALO
        q = dm_e * _pool_inv_count(g, i * tm, rows, gd)
        f2 = q + pltpu.roll(q, rows - 1, 0)
        f4 = f2 + pltpu.roll(f2, rows - 2, 0)
        f8 = f4 + pltpu.roll(f4, rows - 4, 0)
        f16 = f8 + pltpu.roll(f8, rows - 8, 0)
        ahead = jnp.where(g == 0, f2, jnp.where(g == 1, f4, jnp.where(g == 2, f8, f16)))
        dxa_ref[...] = (ahead[:tm] - dm_e[:tm]).astype(BF16)
        dw = lax.dot_general(mixed16, dypre_e[:tm], (((0,), (0,)), ((), ())), preferred_element_type=F32)
        dps = jnp.sum(dysc * ypre, axis=0, keepdims=True)

        @pl.when(i == 0)
        def _():
            dw_ref[...] = dw
            dps_ref[...] = dps

        @pl.when(i > 0)
        def _():
            dw_ref[...] += dw
            dps_ref[...] += dps

    out_tile = pl.BlockSpec((tm, gd), lambda g, i: (i, g))
    return pl.pallas_call(
        body, grid=(N_POOL_GROUPS, n_i),
        in_specs=[_cur(tm, gd, 0), _prev(tm, gd, 0), _cur(tm, gd, za_col), _next(tm, gd, za_col, t),
                  _cur(tm, gd, 0), _next(tm, gd, 0, t),
                  pl.BlockSpec((None, gd, gd), lambda g, i: (g, 0, 0)),
                  pl.BlockSpec((1, gd), lambda g, i: (0, g))],
        out_specs=[out_tile, out_tile, pl.BlockSpec((None, gd, gd), lambda g, i: (g, 0, 0)),
                   pl.BlockSpec((1, gd), lambda g, i: (0, g))],
        out_shape=[jax.ShapeDtypeStruct((t, pw), BF16), jax.ShapeDtypeStruct((t, pw), BF16),
                   jax.ShapeDtypeStruct((N_POOL_GROUPS, gd, gd), F32), jax.ShapeDtypeStruct((1, pw), F32)],
        name=name, compiler_params=_cparams(("arbitrary", "arbitrary"), 48),
    )(proj, proj, proj, proj, dya, dya, pool_w, pool_scale)


def _conv_cols(d, cw):
    cc = d // 2
    base = (d + 4 * d) // cw
    return base, base + cc // cw, base + 2 * (cc // cw), base + 3 * (cc // cw)


def _conv_fwd(proj, conv_w, *, d, tm, cw, name):
    t = proj.shape[0]
    cc = d // 2
    u_col, b_col, c_col, z_col = _conv_cols(d, cw)

    def body(u_ref, up_ref, c_ref, cp_ref, b_ref, z_ref, w_ref, y_ref):
        i = pl.program_id(1)
        v = u_ref[...].astype(F32) * c_ref[...].astype(F32)
        vprev = up_ref[...].astype(F32) * cp_ref[...].astype(F32)
        ve = jnp.concatenate([jnp.where(i == 0, 0.0, vprev), v], axis=0)
        w = w_ref[...]
        y = w[0:1] * pltpu.roll(ve, 2, 0)[HALO:] + w[1:2] * pltpu.roll(ve, 1, 0)[HALO:] + w[2:3] * v
        z = z_ref[...].astype(F32)
        y_ref[...] = (b_ref[...].astype(F32) * y * (z * _sigmoid(z))).astype(BF16)

    return pl.pallas_call(
        body, grid=(cc // cw, t // tm),
        in_specs=[_cur(tm, cw, u_col), _prev(tm, cw, u_col), _cur(tm, cw, c_col), _prev(tm, cw, c_col),
                  _cur(tm, cw, b_col), _cur(tm, cw, z_col),
                  pl.BlockSpec((8, cw), lambda cb, i: (0, cb))],
        out_specs=pl.BlockSpec((tm, cw), lambda cb, i: (i, cb)),
        out_shape=jax.ShapeDtypeStruct((t, cc), BF16), name=name,
        compiler_params=_cparams(("arbitrary", "arbitrary"), 40),
    )(proj, proj, proj, proj, proj, proj, conv_w)


def _conv_bwd(dyc, proj, conv_w, *, d, tm, cw, name):
    t = proj.shape[0]
    cc = d // 2
    n_i = t // tm
    u_col, b_col, c_col, z_col = _conv_cols(d, cw)

    def body(u_ref, up_ref, c_ref, cp_ref, b_ref, bn_ref, z_ref, zn_ref, dy_ref, dyn_ref, w_ref,
             du_ref, db_ref, dc_ref, dz_ref, dw_ref):
        i = pl.program_id(1)
        last = i == n_i - 1
        u = u_ref[...].astype(F32)
        cg = c_ref[...].astype(F32)
        v = u * cg
        vprev = up_ref[...].astype(F32) * cp_ref[...].astype(F32)
        ve = jnp.concatenate([jnp.where(i == 0, 0.0, vprev), v], axis=0)
        v2 = pltpu.roll(ve, 2, 0)[HALO:]
        v1 = pltpu.roll(ve, 1, 0)[HALO:]
        w = w_ref[...]
        y = w[0:1] * v2 + w[1:2] * v1 + w[2:3] * v
        z = z_ref[...].astype(F32)
        sg = _sigmoid(z)
        bg = b_ref[...].astype(F32)
        dyc_v = dy_ref[...].astype(F32)
        dz_ref[...] = (dyc_v * bg * y * (sg * (1.0 + z * (1.0 - sg)))).astype(BF16)
        db_ref[...] = (dyc_v * y * (z * sg)).astype(BF16)
        ze = _with_next(z, zn_ref, last)
        dy_e = (_with_next(dyc_v, dyn_ref, last) * _with_next(bg, bn_ref, last) * (ze * _sigmoid(ze)))
        rows = tm + HALO
        dy0 = dy_e[:tm]
        dv = (w[2:3] * dy0 + w[1:2] * pltpu.roll(dy_e, rows - 1, 0)[:tm]
              + w[0:1] * pltpu.roll(dy_e, rows - 2, 0)[:tm])
        du_ref[...] = (dv * cg).astype(BF16)
        dc_ref[...] = (dv * u).astype(BF16)
        s0 = jnp.sum(dy0 * v2, axis=0, keepdims=True)
        s1 = jnp.sum(dy0 * v1, axis=0, keepdims=True)
        s2 = jnp.sum(dy0 * v, axis=0, keepdims=True)
        r = lax.broadcasted_iota(jnp.int32, (8, cw), 0)
        dw = jnp.where(r == 0, s0, jnp.where(r == 1, s1, jnp.where(r == 2, s2, 0.0)))

        @pl.when(i == 0)
        def _():
            dw_ref[...] = dw

        @pl.when(i > 0)
        def _():
            dw_ref[...] += dw

    out_tile = pl.BlockSpec((tm, cw), lambda cb, i: (i, cb))
    act = jax.ShapeDtypeStruct((t, cc), BF16)
    return pl.pallas_call(
        body, grid=(cc // cw, n_i),
        in_specs=[_cur(tm, cw, u_col), _prev(tm, cw, u_col), _cur(tm, cw, c_col), _prev(tm, cw, c_col),
                  _cur(tm, cw, b_col), _next(tm, cw, b_col, t), _cur(tm, cw, z_col), _next(tm, cw, z_col, t),
                  _cur(tm, cw, 0), _next(tm, cw, 0, t),
                  pl.BlockSpec((8, cw), lambda cb, i: (0, cb))],
        out_specs=[out_tile, out_tile, out_tile, out_tile, pl.BlockSpec((8, cw), lambda cb, i: (0, cb))],
        out_shape=[act, act, act, act, jax.ShapeDtypeStruct((8, cc), F32)], name=name,
        compiler_params=_cparams(("arbitrary", "arbitrary"), 48),
    )(proj, proj, proj, proj, proj, proj, proj, proj, dyc, dyc, conv_w)


def _sum_matrix(prefix):
    r = lax.broadcasted_iota(jnp.int32, (KEY_CHUNK, KEY_CHUNK), 0)
    c = lax.broadcasted_iota(jnp.int32, (KEY_CHUNK, KEY_CHUNK), 1)
    return jnp.where((r <= c) if prefix else (r > c), 1.0, 0.0).astype(BF16)


def _chunk_sums(val, mat, prefix):
    hi = val.astype(BF16)
    part = jnp.dot(hi, mat, preferred_element_type=F32)
    if prefix:
        return part, part[:, KEY_CHUNK - 1:]
    return part, part[:, :1] + hi[:, :1].astype(F32)


def _stick_logits(qs, kb, strict):
    z = lax.dot_general(qs, kb, (((1,), (1,)), ((), ())), preferred_element_type=F32)
    minus_abs = pltpu.bitcast(pltpu.bitcast(z, jnp.int32) | jnp.int32(-2 ** 31), F32)
    sp = jnp.maximum(z, 0.0) + jnp.log(1.0 + jnp.exp(minus_abs))
    log_beta = z - sp
    if strict is not None:
        sp = jnp.where(strict, sp, 0.0)
    return sp, log_beta


def _running(val, mat, carry, upwards):
    n = val.shape[1] // KEY_CHUNK
    out = [None] * n
    for c in (range(n) if upwards else reversed(range(n))):
        part, total = _chunk_sums(val[:, c * KEY_CHUNK:(c + 1) * KEY_CHUNK], mat, upwards)
        out[c] = part + carry
        carry = carry + total
    return jnp.concatenate(out, axis=1), carry


def _attn_cols(d):
    h = d // HEAD_DIM
    return h, 2 * h, 3 * h, 4 * h


def _sweep(i, per, block, carry, upwards):
    n = i * per
    odd = n % 2 == 1
    same = lambda cr: cr
    if upwards:
        carry = lax.fori_loop(0, n // 2, lambda it, cr: block(2 * it + 1, block(2 * it, cr, False), False), carry)
        carry = lax.cond(odd, lambda cr: block(n - 1, cr, False), same, carry)
        for dgl in range(per):
            carry = block(n + dgl, carry, True)
        return carry
    for dgl in range(per - 1, -1, -1):
        carry = block(n + dgl, carry, True)
    carry = lax.fori_loop(0, n // 2,
                          lambda it, cr: block(n - 2 - 2 * it, block(n - 1 - 2 * it, cr, False), False), carry)
    return lax.cond(odd, lambda cr: block(0, cr, False), same, carry)


def _strict_mask(i, tq, kb, width):
    t_idx = i * tq + lax.broadcasted_iota(jnp.int32, (tq, width), 0)
    s_idx = kb * width + lax.broadcasted_iota(jnp.int32, (tq, width), 1)
    return s_idx < t_idx


def _attn_fwd(proj, *, d, tq, name):
    t = proj.shape[0]
    heads = d // HEAD_DIM
    q_col, k_col, v_col, z_col = _attn_cols(d)

    def body(q_ref, k_ref, v_ref, z_ref, o_ref, y_ref, tot_ref):
        i = pl.program_id(1)
        qs = (q_ref[...].astype(F32) * (HEAD_DIM ** -0.5)).astype(BF16)
        mat = _sum_matrix(prefix=False)

        def block(kb, carry, masked):
            acc, after = carry
            off = pl.multiple_of(kb * KEY_BLOCK, KEY_BLOCK)
            strict = _strict_mask(i, tq, kb, KEY_BLOCK) if masked else None
            sp, log_beta = _stick_logits(qs, k_ref[pl.ds(off, KEY_BLOCK), :], strict)
            between, after = _running(sp, mat, after, upwards=False)
            a = jnp.exp(log_beta - between)
            if masked:
                a = jnp.where(strict, a, 0.0)
            acc = acc + jnp.dot(a.astype(BF16), v_ref[pl.ds(off, KEY_BLOCK), :], preferred_element_type=F32)
            return acc, after

        zero = jnp.zeros((tq, HEAD_DIM), F32)
        o, total = _sweep(i, tq // KEY_BLOCK, block, (zero, jnp.zeros((tq, 1), F32)), upwards=False)
        o_ref[...] = o.astype(BF16)
        tot_ref[...] = jnp.broadcast_to(total, (tq, HEAD_DIM))
        z = z_ref[...].astype(F32)
        y_ref[...] = (o * (z * _sigmoid(z))).astype(BF16)

    tile = lambda col: pl.BlockSpec((tq, HEAD_DIM), lambda h, i: (i, col + h))
    full = lambda col: pl.BlockSpec((t, HEAD_DIM), lambda h, i: (0, col + h))
    act = jax.ShapeDtypeStruct((t, d), BF16)
    return pl.pallas_call(
        body, grid=(heads, t // tq),
        in_specs=[tile(q_col), full(k_col), full(v_col), tile(z_col)],
        out_specs=[tile(0), tile(0), tile(0)], out_shape=[act, act, jax.ShapeDtypeStruct((t, d), F32)],
        name=name, compiler_params=_cparams(("arbitrary", "arbitrary"), 40),
    )(proj, proj, proj, proj)


def _attn_bwd(dyb, o, tot, proj, *, d, tq, name):
    t = proj.shape[0]
    heads = d // HEAD_DIM
    n_i = t // tq
    q_col, k_col, v_col, z_col = _attn_cols(d)
    scale = HEAD_DIM ** -0.5

    def body(q_ref, k_ref, v_ref, z_ref, o_ref, tot_ref, dy_ref, dq_ref, dk_ref, dv_ref, dz_ref, dk_acc, dv_acc):
        i = pl.program_id(1)

        @pl.when(i == 0)
        def _():
            dk_acc[...] = jnp.zeros_like(dk_acc)
            dv_acc[...] = jnp.zeros_like(dv_acc)

        qs = (q_ref[...].astype(F32) * scale).astype(BF16)
        z = z_ref[...].astype(F32)
        sg = _sigmoid(z)
        dy = dy_ref[...].astype(F32)
        of = o_ref[...].astype(F32)
        dz_ref[...] = (dy * of * (sg * (1.0 + z * (1.0 - sg)))).astype(BF16)
        do16 = (dy * (z * sg)).astype(BF16)
        all_keep = jnp.concatenate([tot_ref[...]] * (KEY_BLOCK // HEAD_DIM), axis=1)
        mat = _sum_matrix(prefix=True)

        def block(kb, carry, masked):
            dq, sp_seen, g_seen = carry
            off = pl.multiple_of(kb * KEY_BLOCK, KEY_BLOCK)
            kblk = k_ref[pl.ds(off, KEY_BLOCK), :]
            vblk = v_ref[pl.ds(off, KEY_BLOCK), :]
            strict = _strict_mask(i, tq, kb, KEY_BLOCK) if masked else None
            sp, log_beta = _stick_logits(qs, kblk, strict)
            sp_upto, sp_seen = _running(sp, mat, sp_seen, upwards=True)
            a = jnp.exp(log_beta - (all_keep - sp_upto))
            if masked:
                a = jnp.where(strict, a, 0.0)
            g = a * lax.dot_general(do16, vblk, (((1,), (1,)), ((), ())), preferred_element_type=F32)
            g_upto, g_seen = _running(g, mat, g_seen, upwards=True)
            dz_ts = g - jnp.exp(log_beta) * g_upto
            if masked:
                dz_ts = jnp.where(strict, dz_ts, 0.0)
            dz16 = dz_ts.astype(BF16)
            dq = dq + jnp.dot(dz16, kblk, preferred_element_type=F32)
            dk_acc[pl.ds(off, KEY_BLOCK), :] += lax.dot_general(
                dz16, qs, (((0,), (0,)), ((), ())), preferred_element_type=F32)
            dv_acc[pl.ds(off, KEY_BLOCK), :] += lax.dot_general(
                a.astype(BF16), do16, (((0,), (0,)), ((), ())), preferred_element_type=F32)
            return dq, sp_seen, g_seen

        column = jnp.zeros((tq, 1), F32)
        dq, _, _ = _sweep(i, tq // KEY_BLOCK, block, (jnp.zeros((tq, HEAD_DIM), F32), column, column),
                          upwards=True)
        dq_ref[...] = (dq * scale).astype(BF16)

        @pl.when(i == n_i - 1)
        def _():
            dk_ref[...] = dk_acc[...].astype(BF16)
            dv_ref[...] = dv_acc[...].astype(BF16)

    tile = lambda col: pl.BlockSpec((tq, HEAD_DIM), lambda h, i: (i, col + h))
    full = lambda col: pl.BlockSpec((t, HEAD_DIM), lambda h, i: (0, col + h))
    act = jax.ShapeDtypeStruct((t, d), BF16)
    return pl.pallas_call(
        body, grid=(heads, n_i),
        in_specs=[tile(q_col), full(k_col), full(v_col), tile(z_col), tile(0), tile(0), tile(0)],
        out_specs=[tile(0), full(0), full(0), tile(0)], out_shape=[act, act, act, act],
        scratch_shapes=[pltpu.VMEM((t, HEAD_DIM), F32), pltpu.VMEM((t, HEAD_DIM), F32)], name=name,
        compiler_params=_cparams(("arbitrary", "arbitrary"), 48),
    )(proj, proj, proj, proj, o, tot, dyb)


def _merge_fwd(gl, bst, *, d, tm, tn, name):
    t = gl.shape[0]
    nk = d // tn

    def body(g0, g1, g2, b0, b1, b2, m_ref):
        acc = _sigmoid(g0[...].astype(F32)) * b0[...].astype(F32)
        acc += _sigmoid(g1[...].astype(F32)) * b1[...].astype(F32)
        acc += _sigmoid(g2[...].astype(F32)) * b2[...].astype(F32)
        m_ref[...] = acc.astype(BF16)

    spec = lambda b: pl.BlockSpec((tm, tn), lambda i, kb: (i, b * nk + kb))
    return pl.pallas_call(
        body, grid=(t // tm, nk), in_specs=[spec(0), spec(1), spec(2)] * 2,
        out_specs=pl.BlockSpec((tm, tn), lambda i, kb: (i, kb)),
        out_shape=jax.ShapeDtypeStruct((t, d), BF16), name=name,
        compiler_params=_cparams(("arbitrary", "arbitrary"), 40),
    )(gl, gl, gl, bst, bst, bst)


def _merge_bwd(dmerged, gl, bst, *, d, tm, tn, name):
    t = gl.shape[0]
    nk = d // tn

    def body(dm_ref, g_ref, b_ref, db_ref, dg_ref, dbias_ref):
        i = pl.program_id(2)
        dm = dm_ref[...].astype(F32)
        sg = _sigmoid(g_ref[...].astype(F32))
        db_ref[...] = (dm * sg).astype(BF16)
        dgl = dm * b_ref[...].astype(F32) * (sg * (1.0 - sg))
        dg_ref[...] = dgl.astype(BF16)
        dbias = jnp.sum(dgl, axis=0, keepdims=True)

        @pl.when(i == 0)
        def _():
            dbias_ref[...] = dbias

        @pl.when(i > 0)
        def _():
            dbias_ref[...] += dbias

    wide = pl.BlockSpec((tm, tn), lambda b, kb, i: (i, b * nk + kb))
    act = jax.ShapeDtypeStruct((t, 3 * d), BF16)
    return pl.pallas_call(
        body, grid=(3, nk, t // tm),
        in_specs=[pl.BlockSpec((tm, tn), lambda b, kb, i: (i, kb)), wide, wide],
        out_specs=[wide, wide, pl.BlockSpec((1, tn), lambda b, kb, i: (0, b * nk + kb))],
        out_shape=[act, act, jax.ShapeDtypeStruct((1, 3 * d), F32)], name=name,
        compiler_params=_cparams(("arbitrary", "arbitrary", "arbitrary"), 40),
    )(dmerged, gl, bst)


def _adamw(w, m, v, parts, layer, prev, *, tr, name):
    depth, r, c = w.shape
    n = parts.shape[0]

    def body(*refs):
        w_ref, m_ref, v_ref, p_ref = refs[:4]
        g_ref, d_ref, nm_ref, nv_ref = refs[-4:]
        g = p_ref[0].astype(F32)
        for s in range(1, n):
            g = g + p_ref[s].astype(F32)
        wv = w_ref[...]
        nm = ADAM_B1 * m_ref[...] + (1.0 - ADAM_B1) * g
        nv = ADAM_B2 * v_ref[...] + (1.0 - ADAM_B2) * (g * g)
        m_hat = nm / (1.0 - ADAM_B1 ** ADAM_STEP)
        v_hat = nv / (1.0 - ADAM_B2 ** ADAM_STEP)
        g_ref[...] = g
        nm_ref[...] = nm
        nv_ref[...] = nv
        d_ref[...] = -ADAM_LR * (m_hat / (jnp.sqrt(v_hat) + ADAM_EPS) + ADAM_WD * wv)

    tile = pl.BlockSpec((None, tr, c), lambda i: (layer, i, 0))
    in_specs = [tile, tile, tile, pl.BlockSpec((n, tr, c), lambda i: (0, i, 0))]
    args = [w, m, v, parts]
    aliases = {}
    if prev is not None:
        in_specs += [pl.BlockSpec(memory_space=pl.ANY)] * 4
        aliases = {4 + k: k for k in range(4)}
        args += list(prev)
    full = jax.ShapeDtypeStruct((depth, r, c), F32)
    return pl.pallas_call(
        body, grid=(r // tr,), in_specs=in_specs, out_specs=[tile] * 4, out_shape=[full] * 4,
        input_output_aliases=aliases, name=name, compiler_params=_cparams(("arbitrary",), 48),
    )(*args)


def _adamw_layers(w, m, v, parts_by_layer, *, tr, name):
    lead = w.shape[0]
    flat = lambda a: a.reshape(lead, -1, a.shape[-1])
    w, m, v = flat(w), flat(m), flat(v)
    out = None
    for layer in reversed(range(lead)):
        parts = parts_by_layer[layer]
        parts = parts.reshape(parts.shape[0], -1, parts.shape[-1])
        out = _adamw(w, m, v, parts, layer, out, tr=min(tr, w.shape[1]), name=f"{name}_l{layer}")
    return out


def _place():
    return lax.axis_index("x"), lax.axis_index("y"), lax.axis_index("c")


def _index(px, py, pc):
    return 4 * px + 2 * py + pc


def _whole(ref, idx):
    return ref.at[idx]


def _row_block(rows):
    return lambda ref, idx: ref.at[:, pl.ds(idx * rows, rows), :]


def _lane_block(cols):
    return lambda ref, idx: ref.at[:, pl.ds(idx * cols, cols)]


def _all_gather(shards, out_shapes, placers, *, name):
    n = len(shards)

    def body(*refs):
        src, out = refs[:n], refs[n:2 * n]
        send_sems, recv_sems, local_sems = refs[2 * n:]
        x, y, c = _place()
        me, sibling = (x, y, c), (x, y, 1 - c)
        chips = [(1 - x, y), (x, 1 - y), (1 - x, 1 - y)]

        def copy(a, k, block, to, from_shard=False):
            window = placers[a](out[a], _index(*block))
            return pltpu.make_async_remote_copy(
                src_ref=src[a] if from_shard else window, dst_ref=window,
                send_sem=send_sems.at[a * 7 + k], recv_sem=recv_sems.at[a * 7 + k],
                device_id=to, device_id_type=MESH)

        mine = [pltpu.make_async_copy(src[a], placers[a](out[a], _index(*me)), local_sems.at[a])
                for a in range(n)]
        started = []
        for a in range(n):
            mine[a].start()
            started.append(copy(a, 0, me, sibling, True))
            started += [copy(a, 1 + j, me, (*chip, c), True) for j, chip in enumerate(chips)]
        for cp in started:
            cp.start()
        for j, chip in enumerate(chips):
            for a in range(n):
                copy(a, 1 + j, (*chip, c), me).wait_recv()
                passed = copy(a, 4 + j, (*chip, c), sibling)
                passed.start()
                started.append(passed)
        for a in range(n):
            copy(a, 0, sibling, me).wait_recv()
            for j, chip in enumerate(chips):
                copy(a, 4 + j, (*chip, 1 - c), me).wait_recv()
        for cp in started:
            cp.wait_send()
        for cp in mine:
            cp.wait()

    hbm = pl.BlockSpec(memory_space=pl.ANY)
    return pl.pallas_call(
        body, in_specs=[hbm] * n, out_specs=[hbm] * n, out_shape=out_shapes,
        scratch_shapes=[pltpu.SemaphoreType.DMA((7 * n,)), pltpu.SemaphoreType.DMA((7 * n,)),
                        pltpu.SemaphoreType.DMA((n,))],
        name=name,
    )(*shards)


def _scatter_parts(grads, pickers, part_shapes, *, name):
    n = len(grads)

    def body(*refs):
        src, out = refs[:n], refs[n:2 * n]
        send_sems, recv_sems, local_sems = refs[2 * n:]
        x, y, c = _place()
        my_idx = _index(x, y, c)
        peers = [(x ^ (k >> 2), y ^ ((k >> 1) & 1), c ^ (k & 1)) for k in range(1, N_DEV)]

        def copy(a, k, peer):
            return pltpu.make_async_remote_copy(
                src_ref=pickers[a](src[a], _index(*peer)), dst_ref=out[a].at[my_idx],
                send_sem=send_sems.at[a * 7 + k], recv_sem=recv_sems.at[a * 7 + k],
                device_id=peer, device_id_type=MESH)

        def arrival(a, k, peer):
            return pltpu.make_async_remote_copy(
                src_ref=pickers[a](src[a], my_idx), dst_ref=out[a].at[_index(*peer)],
                send_sem=send_sems.at[a * 7 + k], recv_sem=recv_sems.at[a * 7 + k],
                device_id=peer, device_id_type=MESH)

        mine = [pltpu.make_async_copy(pickers[a](src[a], my_idx), out[a].at[my_idx], local_sems.at[a])
                for a in range(n)]
        sent = [copy(a, k, peer) for a in range(n) for k, peer in enumerate(peers)]
        for cp in mine + sent:
            cp.start()
        for a in range(n):
            for k, peer in enumerate(peers):
                arrival(a, k, peer).wait_recv()
        for cp in sent:
            cp.wait_send()
        for cp in mine:
            cp.wait()

    hbm = pl.BlockSpec(memory_space=pl.ANY)
    return pl.pallas_call(
        body, in_specs=[hbm] * n, out_specs=[hbm] * n, out_shape=part_shapes,
        scratch_shapes=[pltpu.SemaphoreType.DMA((7 * n,)), pltpu.SemaphoreType.DMA((7 * n,)),
                        pltpu.SemaphoreType.DMA((n,))],
        name=name,
    )(*grads)


def _peers(x, y, c):
    return [(x ^ (k >> 2), y ^ ((k >> 1) & 1), c ^ (k & 1)) for k in range(1, N_DEV)]


_HBM = pl.BlockSpec(memory_space=pltpu.HBM)
_SEM = pl.BlockSpec(memory_space=pltpu.SEMAPHORE)
_EFFECT = pltpu.SideEffectType.DATAFLOW_SIDE_EFFECTING


def _exchange_start(srcs, land_shapes, src_win, dst_win, after, *, name):
    n = len(srcs)

    def body(*refs):
        src, land = refs[:n], refs[n:2 * n]
        send_sems, recv_sems, token = refs[2 * n + len(after)], refs[2 * n + len(after) + 1], refs[-1]
        x, y, c = _place()
        my_idx = _index(x, y, c)
        for a in range(n):
            for k, peer in enumerate(_peers(x, y, c)):
                pltpu.make_async_remote_copy(
                    src_ref=src_win[a](src[a], _index(*peer)), dst_ref=dst_win[a](land[a], my_idx),
                    send_sem=send_sems.at[a * 7 + k], recv_sem=recv_sems.at[a * 7 + k],
                    device_id=peer, device_id_type=MESH).start()
        token[...] = jnp.zeros_like(token)

    hbm = lambda a: pltpu.with_memory_space_constraint(a, pltpu.HBM)
    lands = [hbm(lax.empty(s.shape, s.dtype)) for s in land_shapes]
    out = pl.pallas_call(
        body, name=name,
        out_shape=(pltpu.SemaphoreType.DMA((7 * n,)), pltpu.SemaphoreType.DMA((7 * n,)),
                   *[pltpu.HBM(a.shape, a.dtype) for a in lands], jax.ShapeDtypeStruct((8, LANE), F32)),
        in_specs=[_HBM] * (2 * n) + [pl.BlockSpec(memory_space=pl.ANY)] * len(after),
        out_specs=(_SEM, _SEM, *([_HBM] * n), pl.BlockSpec(memory_space=pltpu.VMEM)),
        input_output_aliases={n + i: 2 + i for i in range(n)},
        compiler_params=pltpu.CompilerParams(has_side_effects=_EFFECT),
    )(*srcs, *lands, *after)
    return out[0], out[1], list(out[2:2 + n]), out[-1][0, 0]


def _exchange_wait(send_sems, recv_sems, srcs, lands, src_win, dst_win, after, *, name):
    n = len(srcs)

    def body(*refs):
        src, land = refs[:n], refs[n:2 * n]
        send_sems, recv_sems = refs[2 * n], refs[2 * n + 1]
        x, y, c = _place()
        for a in range(n):
            for k, peer in enumerate(_peers(x, y, c)):
                sender = _index(*peer)
                copy = pltpu.make_async_remote_copy(
                    src_ref=src_win[a](src[a], sender), dst_ref=dst_win[a](land[a], sender),
                    send_sem=send_sems.at[a * 7 + k], recv_sem=recv_sems.at[a * 7 + k],
                    device_id=peer, device_id_type=MESH)
                copy.wait_send()
                copy.wait_recv()

    out = pl.pallas_call(
        body, name=name, out_shape=tuple(pltpu.HBM(a.shape, a.dtype) for a in lands),
        in_specs=[_HBM] * (2 * n) + [_SEM, _SEM] + [pl.BlockSpec(memory_space=pl.ANY)] * len(after),
        out_specs=tuple([_HBM] * n),
        input_output_aliases={n + i: i for i in range(n)},
        compiler_params=pltpu.CompilerParams(has_side_effects=_EFFECT),
    )(*srcs, *lands, send_sems, recv_sems, *after)
    return list(out)


COPY_TILE_BYTES = 2 << 20


def _copy_own_block(src, land, idx, *, from_stack, name):
    k, n = land.shape[1:]
    tk = k
    while tk * n * land.dtype.itemsize > COPY_TILE_BYTES and tk % 32 == 0:
        tk //= 2

    def body(idx_ref, src_ref, land_ref, out_ref):
        out_ref[...] = src_ref[...]

    own = pl.BlockSpec((None, tk, n), lambda i, idx_ref: (idx_ref[0], i, 0))
    grid_spec = pltpu.PrefetchScalarGridSpec(
        num_scalar_prefetch=1, grid=(k // tk,),
        in_specs=[own if from_stack else pl.BlockSpec((tk, n), lambda i, idx_ref: (i, 0)),
                  pl.BlockSpec(memory_space=pl.ANY)],
        out_specs=own)
    return pl.pallas_call(
        body, grid_spec=grid_spec, out_shape=jax.ShapeDtypeStruct(land.shape, land.dtype),
        input_output_aliases={2: 0}, name=name,
    )(idx, src, land)


def _place_own(srcs, lands, src_win, dst_win, *, name):
    lands = list(lands)
    idx = _index(*_place()).astype(jnp.int32).reshape(1)
    rest = []
    for a in range(len(srcs)):
        if dst_win[a] is _whole and src_win[a] in (_whole, _shard_itself) and lands[a].ndim == 3:
            lands[a] = _copy_own_block(srcs[a], lands[a], idx, from_stack=src_win[a] is _whole,
                                       name=f"{name}_{a}")
        else:
            rest.append(a)
    if rest:
        placed = _dma_own([srcs[a] for a in rest], [lands[a] for a in rest], [src_win[a] for a in rest],
                          [dst_win[a] for a in rest], name=f"{name}_dma")
        for a, land in zip(rest, placed):
            lands[a] = land
    return lands


def _dma_own(srcs, lands, src_win, dst_win, *, name):
    n = len(srcs)

    def body(*refs):
        src, land = refs[:n], refs[n:2 * n]
        sems = refs[-1]
        my_idx = _index(*_place())
        copies = [pltpu.make_async_copy(src_win[a](src[a], my_idx), dst_win[a](land[a], my_idx), sems.at[a])
                  for a in range(n)]
        for cp in copies:
            cp.start()
        for cp in copies:
            cp.wait()

    hbm = pl.BlockSpec(memory_space=pl.ANY)
    return list(pl.pallas_call(
        body, name=name, in_specs=[hbm] * (2 * n), out_specs=[hbm] * n,
        out_shape=[jax.ShapeDtypeStruct(a.shape, a.dtype) for a in lands],
        input_output_aliases={n + i: i for i in range(n)},
        scratch_shapes=[pltpu.SemaphoreType.DMA((n,))],
    )(*srcs, *lands))


def _ada_fwd(c8, w_ada, b_ada8, *, name):
    depth, d, n = w_ada.shape

    def body(c_ref, w_ref, b_ref, mod_ref, sc_ref, c_all, prod, got, send_sems, recv_sems):
        x, y, c = _place()
        my_idx = _index(x, y, c)
        peers = [(x ^ (k >> 2), y ^ ((k >> 1) & 1), c ^ (k & 1)) for k in range(1, N_DEV)]

        def slab(ref, idx):
            return ref.at[pl.ds(pl.multiple_of(idx * 8, 8), 8)]

        def c_copy(k, peer, sender):
            return pltpu.make_async_remote_copy(
                src_ref=c_ref, dst_ref=slab(c_all, sender), send_sem=send_sems.at[k],
                recv_sem=recv_sems.at[k], device_id=peer, device_id_type=MESH)

        def m_copy(k, peer, sender):
            return pltpu.make_async_remote_copy(
                src_ref=prod.at[_index(*peer)], dst_ref=got.at[sender], send_sem=send_sems.at[7 + k],
                recv_sem=recv_sems.at[7 + k], device_id=peer, device_id_type=MESH)

        sends = [c_copy(k, peer, my_idx) for k, peer in enumerate(peers)]
        for cp in sends:
            cp.start()
        slab(c_all, my_idx)[...] = c_ref[...]
        for k, peer in enumerate(peers):
            c_copy(k, peer, _index(*peer)).wait_recv()
        cv = c_all[...]
        sc = cv * _sigmoid(cv)
        sc_ref[...] = sc
        for layer in range(depth):
            p = jnp.dot(sc, w_ref[layer], preferred_element_type=F32, precision=lax.Precision.HIGHEST)
            for s in range(N_DEV):
                prod[s, layer] = p[8 * s:8 * s + 8]
        sends2 = [m_copy(k, peer, my_idx) for k, peer in enumerate(peers)]
        for cp in sends2:
            cp.start()
        got[my_idx] = prod[my_idx]
        for k, peer in enumerate(peers):
            m_copy(k, peer, _index(*peer)).wait_recv()
        for layer in range(depth):
            for s in range(N_DEV):
                mod_ref[layer, :, s * n:(s + 1) * n] = got[s, layer] + b_ref[layer, :, s * n:(s + 1) * n]
        for cp in sends + sends2:
            cp.wait_send()

    vmem = pl.BlockSpec(memory_space=pltpu.VMEM)
    return pl.pallas_call(
        body, in_specs=[vmem] * 3, out_specs=[vmem] * 2,
        out_shape=[jax.ShapeDtypeStruct((depth, 8, N_DEV * n), F32),
                   jax.ShapeDtypeStruct((8 * N_DEV, d), F32)],
        scratch_shapes=[pltpu.VMEM((8 * N_DEV, d), F32), pltpu.VMEM((N_DEV, depth, 8, n), F32),
                        pltpu.VMEM((N_DEV, depth, 8, n), F32),
                        pltpu.SemaphoreType.DMA((14,)), pltpu.SemaphoreType.DMA((14,))],
        name=name, compiler_params=pltpu.CompilerParams(vmem_limit_bytes=56 << 20),
    )(c8, w_ada, b_ada8)


def _ada_bwd(sc8, dmod, *, name):
    depth, _, n = dmod.shape
    d = sc8.shape[1]

    def body(sc_ref, dm_ref, o_ref):
        for layer in range(depth):
            o_ref[layer] = lax.dot_general(sc_ref[...], dm_ref[layer], (((0,), (0,)), ((), ())),
                                           preferred_element_type=F32, precision=lax.Precision.HIGHEST)

    vmem = pl.BlockSpec(memory_space=pltpu.VMEM)
    return pl.pallas_call(
        body, in_specs=[vmem] * 2, out_specs=vmem, out_shape=jax.ShapeDtypeStruct((depth, d, n), F32),
        name=name, compiler_params=pltpu.CompilerParams(vmem_limit_bytes=40 << 20),
    )(sc8, dmod)


def _gather_small(vec, *, name):
    r = vec.shape[0]

    def body(v_ref, o_ref, send_sems, recv_sems):
        x, y, c = _place()
        my_idx = _index(x, y, c)
        peers = [(x ^ (k >> 2), y ^ ((k >> 1) & 1), c ^ (k & 1)) for k in range(1, N_DEV)]

        def copy(k, peer, sender):
            return pltpu.make_async_remote_copy(
                src_ref=v_ref, dst_ref=o_ref.at[sender], send_sem=send_sems.at[k],
                recv_sem=recv_sems.at[k], device_id=peer, device_id_type=MESH)

        sends = [copy(k, peer, my_idx) for k, peer in enumerate(peers)]
        for cp in sends:
            cp.start()
        o_ref[my_idx] = v_ref[...]
        for k, peer in enumerate(peers):
            copy(k, peer, _index(*peer)).wait_recv()
        for cp in sends:
            cp.wait_send()

    vmem = pl.BlockSpec(memory_space=pltpu.VMEM)
    return pl.pallas_call(
        body, in_specs=[vmem], out_specs=vmem, out_shape=jax.ShapeDtypeStruct((N_DEV, r, LANE), F32),
        scratch_shapes=[pltpu.SemaphoreType.DMA((7,)), pltpu.SemaphoreType.DMA((7,))], name=name,
    )(vec)


TM = 512
TM_WIDE = 2048
TMM_DEEP = 1024
TQ = 512


def _layer_fwd(x, mod, small, gw, *, d, tag, more_weights=None):
    shift, scale, res_gate = mod
    h = _prenorm(x, small["norm_g"], scale, shift, tm=TM, name=f"prenorm_{tag}")
    proj = _mm_nn(h, gw["w_in"], out_dtype=BF16, tm=TM, name=f"in_proj_{tag}")
    ya = _pool_fwd(proj, gw["pool_w"], small["pool_scale"], d=d, tm=TM, name=f"pool_fwd_{tag}")
    o, yb, tot = _attn_fwd(proj, d=d, tq=TQ, name=f"attn_fwd_{tag}")
    yc = _conv_fwd(proj, gw["conv_w"], d=d, tm=TM, cw=LANE * 2, name=f"conv_fwd_{tag}")
    if more_weights is not None:
        gw = dict(gw, **more_weights(o))
    gl = _mm_nn(h, gw["w_gate"], out_dtype=BF16, tm=TM, bias=small["b_gate"], name=f"gate_proj_{tag}")
    nblk = gw["w_br_a"].shape[0]
    bst = _mm_nn(ya, gw["w_br_a"], out_dtype=BF16, tm=TM_WIDE, out_cols=3 * d, name=f"branch_a_{tag}")
    bst = _mm_nn(yb, gw["w_br_b"], out_dtype=BF16, tm=TM, out=bst, out_col_block=1, name=f"branch_b_{tag}")
    bst = _mm_nn(yc, gw["w_br_c"], out_dtype=BF16, tm=TM_WIDE, out=bst, out_col_block=2 * nblk,
                 name=f"branch_c_{tag}")
    merged = _merge_fwd(gl, bst, d=d, tm=TM, tn=TM, name=f"merge_fwd_{tag}")
    x_new, mo = _out_fwd(merged, gw["w_out"][0], x, res_gate, tm=TM, name=f"out_fwd_{tag}")
    return x_new, dict(x=x, h=h, proj=proj, gl=gl, ya=ya, yb=yb, yc=yc, o=o, tot=tot, bst=bst, merged=merged,
                       mo=mo), gw


EARLY = ("w_gate", "w_br_a", "w_br_b", "w_br_c", "w_out")
LATE = ("w_in", "pool_w")


def _layer_bwd(dout, sv, mod, small, gw, *, d, tag, send_early=None, send_late=None):
    shift, scale, res_gate = mod
    nblk = gw["w_br_a"].shape[0]
    dmo, d_res_gate = _out_bwd(dout, sv["mo"], res_gate, tm=TM, name=f"out_bwd_{tag}")
    g_w_out = _mm_tn(sv["merged"], dmo, nb=1, n=d, tk=TM, tmm=TMM_DEEP, name=f"dw_out_{tag}")
    dmerged = _mm_nt(dmo, gw["w_out"], out_dtype=BF16, tm=TM, name=f"dmerged_{tag}")
    dbst, dgl, d_b_gate = _merge_bwd(dmerged, sv["gl"], sv["bst"], d=d, tm=TM, tn=TM, name=f"merge_bwd_{tag}")
    g_w_gate = _mm_tn(sv["h"], dgl, nb=nblk, n=3 * d // nblk, tk=d // 2, tmm=TMM_DEEP,
                      name=f"dw_gate_{tag}")
    dh = _mm_nt(dgl, gw["w_gate"], out_dtype=F32, tm=TM, name=f"dh_gate_{tag}")
    g_w_br_a = _mm_tn(sv["ya"], dbst, nb=nblk, n=d // nblk, tk=d // 2, tmm=TM_WIDE,
                      name=f"dw_br_a_{tag}")
    g_w_br_b = _mm_tn(sv["yb"], dbst, nb=1, n=d, tk=TM, tmm=TMM_DEEP, col_block=1, name=f"dw_br_b_{tag}")
    g_w_br_c = _mm_tn(sv["yc"], dbst, nb=nblk, n=d // nblk, tk=d // 2, tmm=TM_WIDE, col_block=2 * nblk,
                      name=f"dw_br_c_{tag}")
    dya = _mm_nt(dbst, gw["w_br_a"], out_dtype=BF16, tm=TM_WIDE, name=f"dya_{tag}")
    dyb = _mm_nt(dbst, gw["w_br_b"], out_dtype=BF16, tm=TM, col_block=1, name=f"dyb_{tag}")
    dyc = _mm_nt(dbst, gw["w_br_c"], out_dtype=BF16, tm=TM_WIDE, col_block=2 * nblk, name=f"dyc_{tag}")
    big = dict(w_gate=g_w_gate, w_br_a=g_w_br_a, w_br_b=g_w_br_b, w_br_c=g_w_br_c, w_out=g_w_out)
    if send_early is not None:
        token = send_early(big)
        small = dict(small, pool_scale=small["pool_scale"] + token, norm_g=small["norm_g"] + token)
        dyb = dyb + token.astype(BF16)
    dxa, dza, g_pool_w, d_pool_scale = _pool_bwd(dya, sv["proj"], gw["pool_w"], small["pool_scale"], d=d, tm=TM,
                                                 name=f"pool_bwd_{tag}")
    dq, dk, dv, dzb = _attn_bwd(dyb, sv["o"], sv["tot"], sv["proj"], d=d, tq=TQ, name=f"attn_bwd_{tag}")
    du, dbg, dcg, dzc, d_conv_w = _conv_bwd(dyc, sv["proj"], gw["conv_w"], d=d, tm=TM, cw=LANE * 2,
                                            name=f"conv_bwd_{tag}")
    dproj = jnp.concatenate([dxa, dza, dq, dk, dv, dzb, du, dbg, dcg, dzc], axis=1)
    g_w_in = _mm_tn(sv["h"], dproj, nb=nblk, n=7 * d // nblk, tk=d // 2, tmm=TMM_DEEP,
                    name=f"dw_in_{tag}")
    big.update(w_in=g_w_in, pool_w=g_pool_w.astype(BF16))
    if send_late is not None:
        dh = dh + send_late(big)
    dh = _mm_nt(dproj, gw["w_in"], out_dtype=F32, tm=TM, addend=dh, name=f"dh_in_{tag}")
    dx, d_norm_g, d_scale, d_shift = _prenorm_bwd(dh, sv["x"], dout, small["norm_g"], scale, tm=TM,
                                                  name=f"prenorm_bwd_{tag}")
    part = dict(norm_g=d_norm_g, mod=jnp.concatenate([d_shift, d_scale, d_res_gate], axis=1),
                pool_scale=d_pool_scale, b_gate=d_b_gate, conv_w=d_conv_w[:3])
    return dx, big, part


BIG = ("w_in", "w_gate", "w_br_a", "w_br_b", "w_br_c", "w_out", "pool_w")
SMALL_PER_LAYER = ("norm_g", "mod", "pool_scale", "b_gate", "conv_w")


def _gather_layer(w, layer, *, d):
    nd = N_DEV
    gd = d // 2 // N_POOL_GROUPS
    cc = d // 2
    shards = [w[k][layer].astype(BF16) for k in BIG]
    shards.append(jnp.pad(w["conv_w"][layer], ((0, 5), (0, 0))))
    blocks = [jax.ShapeDtypeStruct((nd,) + s.shape, BF16) for s in shards[:6]]
    blocks.append(jax.ShapeDtypeStruct((N_POOL_GROUPS, gd, gd), BF16))
    blocks.append(jax.ShapeDtypeStruct((8, cc), F32))
    placers = [_whole] * 6 + [_row_block(gd // nd), _lane_block(cc // nd)]
    return shards, blocks, placers


GATHERED = BIG + ("conv_w",)
FIRST = (0, 6, 7)
REST = (1, 2, 3, 4, 5)


def _as_weights(got, d, which=tuple(range(len(GATHERED)))):
    gw = dict(zip([GATHERED[i] for i in which], got))
    for k in ("w_br_b", "w_out"):
        if k in gw:
            gw[k] = gw[k].reshape(1, d, d)
    return gw


def _shard_itself(ref, idx):
    return ref


def _scatter_spec(big, keys, *, d):
    nd = N_DEV
    gd = d // 2 // N_POOL_GROUPS
    grads, pickers, shapes = [], [], []
    for k in keys:
        g = big[k]
        if k == "pool_w":
            pickers.append(_row_block(gd // nd))
            shapes.append(jax.ShapeDtypeStruct((nd, N_POOL_GROUPS, gd // nd, gd), BF16))
        else:
            if g.shape[0] == 1:
                g = g.reshape(nd, g.shape[1] // nd, g.shape[2])
            pickers.append(_whole)
            shapes.append(jax.ShapeDtypeStruct(g.shape, BF16))
        grads.append(g)
    return grads, pickers, shapes


class _Behind:
    def __init__(self, srcs, land_shapes, src_win, dst_win, after, name):
        srcs = [pltpu.with_memory_space_constraint(s, pltpu.HBM) for s in srcs]
        self.srcs, self.src_win, self.dst_win, self.name = srcs, src_win, dst_win, name
        self.send, self.recv, self.lands, self.token = _exchange_start(
            srcs, land_shapes, src_win, dst_win, after, name=f"{name}_start")

    def finish(self, after):
        lands = _exchange_wait(self.send, self.recv, self.srcs, self.lands, self.src_win, self.dst_win,
                               after, name=f"{self.name}_wait")
        return _place_own(self.srcs, lands, self.src_win, self.dst_win, name=f"{self.name}_own")


def _pack(pieces):
    return jnp.concatenate([p.reshape(-1, LANE) for p in pieces], axis=0)


def kernel(x, c, norm_g, w_ada, b_ada, w_in, pool_w, pool_scale, conv_w, w_br_a, w_br_b, w_br_c, w_gate, b_gate, w_out, final_g, loss_target, m_norm_g, m_w_ada, m_b_ada, m_w_in, m_pool_w, m_pool_scale, m_conv_w, m_w_br_a, m_w_br_b, m_w_br_c, m_w_gate, m_b_gate, m_w_out, m_final_g, v_norm_g, v_w_ada, v_b_ada, v_w_in, v_pool_w, v_pool_scale, v_conv_w, v_w_br_a, v_w_br_b, v_w_br_c, v_w_gate, v_b_gate, v_w_out, v_final_g):
    names = ("norm_g", "w_ada", "b_ada", "w_in", "pool_w", "pool_scale", "conv_w", "w_br_a", "w_br_b",
             "w_br_c", "w_gate", "b_gate", "w_out", "final_g")
    w = dict(zip(names, (norm_g, w_ada, b_ada, w_in, pool_w, pool_scale, conv_w, w_br_a, w_br_b, w_br_c,
                         w_gate, b_gate, w_out, final_g)))
    m = dict(zip(names, (m_norm_g, m_w_ada, m_b_ada, m_w_in, m_pool_w, m_pool_scale, m_conv_w, m_w_br_a,
                         m_w_br_b, m_w_br_c, m_w_gate, m_b_gate, m_w_out, m_final_g)))
    v = dict(zip(names, (v_norm_g, v_w_ada, v_b_ada, v_w_in, v_pool_w, v_pool_scale, v_conv_w, v_w_br_a,
                         v_w_br_b, v_w_br_c, v_w_gate, v_b_gate, v_w_out, v_final_g)))
    _, t, d = x.shape
    depth = norm_g.shape[0]
    cc = d // 2
    my_idx = _index(*_place())

    mod8, sc_all = _ada_fwd(jnp.broadcast_to(c, (8, d)), w_ada,
                            jnp.broadcast_to(b_ada[:, None, :], (depth, 8, 3 * d)), name="ada_fwd")
    mods = [tuple(mod8[l, 0:1, k * d:(k + 1) * d] for k in range(3)) for l in range(depth)]
    small = [dict(norm_g=norm_g[l][None], b_gate=b_gate[l][None], pool_scale=pool_scale[l][None])
             for l in range(depth)]

    shards, blocks, placers = _gather_layer(w, 0, d=d)
    pick = lambda seq, which: [seq[i] for i in which]
    first = _all_gather(pick(shards, FIRST), pick(blocks, FIRST), pick(placers, FIRST), name="gather_first_l0")
    rest = _Behind(pick(shards, REST), pick(blocks, REST), [_shard_itself] * len(REST), pick(placers, REST),
                   (mod8, first[-1]), "gather_rest_l0")
    gws = [_as_weights(first, d, FIRST)]
    order = rest.token
    xs = x[0]
    saved = []
    for l in range(depth):
        coming = None
        if l + 1 < depth:
            shards, blocks, placers = _gather_layer(w, l + 1, d=d)
            coming = []
            for which, part in ((FIRST, "first"), (REST, "rest")):
                coming.append(_Behind(pick(shards, which), pick(blocks, which), [_shard_itself] * len(which),
                                      pick(placers, which), (mod8, gws[l]["conv_w"], order.reshape(1, 1)),
                                      f"gather_{part}_l{l + 1}"))
                order = order + coming[-1].token
        shift, scale, res_gate = mods[l]
        mods[l] = (shift + order, scale, res_gate)
        xs, sv, gws[l] = _layer_fwd(xs, mods[l], small[l], gws[l], d=d, tag=f"l{l}",
                                    more_weights=lambda o, rest=rest: _as_weights(rest.finish((o,)), d, REST))
        saved.append(sv)
        if coming is not None:
            gws.append(_as_weights(coming[0].finish((xs,)), d, FIRST))
            rest = coming[1]
    dx, loss8, d_final_g = _loss_head(xs, loss_target[0], final_g[None], tm=TM, name="loss_head")
    loss = lax.psum(loss8[0, 0], ("x", "y", "c"))

    parts, small_parts = [dict() for _ in range(depth)], [None] * depth
    going, last = [], []

    def send(queue, keys, l, name):
        def start(grads_so_far):
            grads, pickers, shapes = _scatter_spec(grads_so_far, keys, d=d)
            queue.append((l, keys, _Behind(grads, shapes, pickers, [_whole] * len(grads), (grads[0],), name)))
            return queue[-1][2].token
        return start

    for l in reversed(range(depth)):
        if going:
            shift, scale, res_gate = mods[l]
            mods[l] = (shift, scale, res_gate + going[-1][2].token)
        lowest = l == 0
        dx, big, small_parts[l] = _layer_bwd(
            dx, saved[l], mods[l], small[l], gws[l], d=d, tag=f"l{l}",
            send_early=send(going, EARLY, l, f"scatter_early_l{l}") if lowest else None,
            send_late=send(last, LATE, l, f"scatter_late_l{l}") if lowest else None)
        if not lowest:
            send(going, BIG, l, f"scatter_grads_l{l}")(big)
    for l, keys, behind in going:
        parts[l].update(zip(keys, behind.finish((dx,))))

    pieces = [small_parts[l][k] for l in range(depth) for k in SMALL_PER_LAYER] + [d_final_g]
    sizes = [p.size for p in pieces]
    gathered = _gather_small(_pack(pieces), name="gather_small_grads")
    zero_conv = jnp.zeros((3, cc), F32)

    def packed(src):
        per_layer = lambda l: [src["norm_g"][l], src["b_ada"][l], src["pool_scale"][l], src["b_gate"][l],
                               zero_conv]
        return _pack([p for l in range(depth) for p in per_layer(l)] + [src["final_g"]])[None]

    small_out = _adamw(packed(w), packed(m), packed(v), gathered, 0, None, tr=gathered.shape[1],
                       name="adamw_small")

    def unpack(flat):
        flat = flat.reshape(-1)
        out, off = [], 0
        for n in sizes:
            out.append(flat[off:off + n])
            off += n
        return out

    small_res = [unpack(a) for a in small_out]
    k_per = len(SMALL_PER_LAYER)

    def small_stack(which, pos, shape):
        return jnp.stack([small_res[which][l * k_per + pos] for l in range(depth)]).reshape(shape)

    res = {}
    for pos, name in ((0, "norm_g"), (1, "b_ada"), (2, "pool_scale"), (3, "b_gate")):
        res[name] = tuple(small_stack(k, pos, w[name].shape) for k in range(4))
    res["final_g"] = tuple(small_res[k][-1].reshape(final_g.shape) for k in range(4))

    ncol = cc // N_DEV
    conv_total = small_stack(0, 4, (depth * 3, cc))
    conv_mine = lax.dynamic_slice_in_dim(conv_total, my_idx * ncol, ncol, axis=1)
    pad8 = lambda a: jnp.pad(a.reshape(depth * 3, ncol), ((0, 8 - depth * 3), (0, 0)))[None]
    conv_out = _adamw(pad8(conv_w), pad8(m["conv_w"]), pad8(v["conv_w"]), pad8(conv_mine), 0, None, tr=8,
                      name="adamw_conv_w")
    res["conv_w"] = tuple(a[0, :depth * 3].reshape(conv_w.shape) for a in conv_out)

    n_ada = 3 * d // N_DEV
    mod_off = [sum(sizes[:l * k_per + 1]) // LANE for l in range(depth)]
    dmod = jnp.stack([
        lax.dynamic_slice_in_dim(gathered[:, mod_off[l]:mod_off[l] + 3 * d // LANE].reshape(N_DEV, 3 * d),
                                 my_idx * n_ada, n_ada, axis=1) for l in range(depth)])
    g_w_ada = _ada_bwd(sc_all[::8], dmod, name="ada_bwd")
    res["w_ada"] = _adamw_layers(w_ada, m["w_ada"], v["w_ada"], [g_w_ada[l][None] for l in range(depth)],
                                 tr=128, name="adamw_w_ada")

    def big_adamw(name):
        out = _adamw_layers(w[name], m[name], v[name], [parts[l][name] for l in range(depth)], tr=128,
                            name=f"adamw_{name}")
        res[name] = tuple(a.reshape(w[name].shape) for a in out)

    for name in EARLY:
        big_adamw(name)
    for l, keys, behind in last:
        parts[l].update(zip(keys, behind.finish(tuple(res[name][1] for name in ("w_ada",) + EARLY))))
    for name in LATE:
        big_adamw(name)

    outs = [loss, dx[None]]
    for k in range(4):
        outs += [res[name][k] for name in names]
    return tuple(outs)
```

```python
import functools

import jax
import jax.numpy as jnp
from jax import lax
from jax.experimental import pallas as pl
from jax.experimental.pallas import tpu as pltpu

F32 = jnp.float32
BF16 = jnp.bfloat16
MESH = pl.DeviceIdType.MESH

N_DEV = 8
DEPTH = 2
HEAD_DIM = 128
N_POOL_GROUPS = 4
POOL_WINDOWS = (2, 4, 8, 16)
RMS_EPS = 1e-6
LANE = 128
HALO = 16
KEY_CHUNK = 256
KEY_BLOCK = 512
HEADS_PER_STEP = 2
ADAM_LR = 0.001
ADAM_B1 = 0.9
ADAM_B2 = 0.999
ADAM_EPS = 1e-08
ADAM_WD = 0.01
ADAM_STEP = 10


def _cparams(semantics, vmem_mb):
    return pltpu.CompilerParams(dimension_semantics=semantics, vmem_limit_bytes=vmem_mb << 20)


def _sigmoid(z):
    return 1.0 / (1.0 + jnp.exp(-z))


def _mm_nn(a, w, *, out_dtype, tm, name, bias=None, out=None, out_cols=None, out_col_block=0):
    m, k = a.shape
    nb, _, n = w.shape
    tm = min(tm, m)
    total_cols =out.shape[1] if out is not None else (nb * n if out_cols is None else out_cols)

    def body(*refs):
        if out is not None:
            refs = refs[:-2] + refs[-1:]
        if bias is not None:
            a_ref, w_ref, b_ref, o_ref = refs
        else:
            a_ref, w_ref, o_ref = refs
        acc = jnp.dot(a_ref[...], w_ref[...], preferred_element_type=F32)
        if bias is not None:
            acc = acc + b_ref[...]
        o_ref[...] = acc.astype(out_dtype)

    in_specs = [pl.BlockSpec((tm, k), lambda j, i: (i, 0)),
                pl.BlockSpec((None, k, n), lambda j, i: (j, 0, 0))]
    args = [a, w]
    if bias is not None:
        in_specs.append(pl.BlockSpec((1, n), lambda j, i: (0, j)))
        args.append(bias)
    aliases = {}
    if out is not None:
        in_specs.append(pl.BlockSpec(memory_space=pl.ANY))
        aliases = {len(args): 0}
        args.append(out)
    return pl.pallas_call(
        body, grid=(nb, m // tm), in_specs=in_specs,
        out_specs=pl.BlockSpec((tm, n), lambda j, i: (i, out_col_block + j)),
        out_shape=jax.ShapeDtypeStruct((m, total_cols), out_dtype),
        input_output_aliases=aliases, name=name,
        compiler_params=_cparams(("arbitrary", "arbitrary"), 56),
    )(*args)


def _mm_nt(dy, w, *, out_dtype, tm, name, addend=None, col_block=0):
    m = dy.shape[0]
    nb, k, n = w.shape
    tm = min(tm, m)

    def body(*refs):
        if addend is not None:
            dy_ref, w_ref, add_ref, o_ref, acc_ref = refs
        else:
            dy_ref, w_ref, o_ref, acc_ref = refs
        j = pl.program_id(1)
        part = lax.dot_general(dy_ref[...], w_ref[...], (((1,), (1,)), ((), ())),
                               preferred_element_type=F32)

        @pl.when(j == 0)
        def _():
            acc_ref[...] = part if addend is None else part + add_ref[...]

        @pl.when(j > 0)
        def _():
            acc_ref[...] += part

        @pl.when(j == nb - 1)
        def _():
            o_ref[...] = acc_ref[...].astype(out_dtype)

    in_specs = [pl.BlockSpec((tm, n), lambda i, j: (i, col_block + j)),
                pl.BlockSpec((None, k, n), lambda i, j: (j, 0, 0))]
    args = [dy, w]
    if addend is not None:
        in_specs.append(pl.BlockSpec((tm, k), lambda i, j: (i, 0)))
        args.append(addend)
    return pl.pallas_call(
        body, grid=(m // tm, nb), in_specs=in_specs,
        out_specs=pl.BlockSpec((tm, k), lambda i, j: (i, 0)),
        out_shape=jax.ShapeDtypeStruct((m, k), out_dtype),
        scratch_shapes=[pltpu.VMEM((tm, k), F32)], name=name,
        compiler_params=_cparams(("arbitrary", "arbitrary"), 56),
    )(*args)


def _mm_tn(a, dy, *, nb, n, tk, tmm, name, col_block=0, out_dtype=BF16):
    m, k = a.shape
    tmm = min(tmm, m)

    def body(a_ref, dy_ref, o_ref, acc_ref):
        t = pl.program_id(2)
        part = lax.dot_general(a_ref[...], dy_ref[...], (((0,), (0,)), ((), ())),
                               preferred_element_type=F32)

        @pl.when(t == 0)
        def _():
            acc_ref[...] = part

        @pl.when(t > 0)
        def _():
            acc_ref[...] += part

        @pl.when(t == pl.num_programs(2) - 1)
        def _():
            o_ref[...] = acc_ref[...].astype(out_dtype)

    return pl.pallas_call(
        body, grid=(nb, k // tk, m // tmm),
        in_specs=[pl.BlockSpec((tmm, tk), lambda j, kb, t: (t, kb)),
                  pl.BlockSpec((tmm, n), lambda j, kb, t: (t, col_block + j))],
        out_specs=pl.BlockSpec((None, tk, n), lambda j, kb, t: (j, kb, 0)),
        out_shape=jax.ShapeDtypeStruct((nb, k, n), out_dtype),
        scratch_shapes=[pltpu.VMEM((tk, n), F32)], name=name,
        compiler_params=_cparams(("arbitrary", "arbitrary", "arbitrary"), 56),
    )(a, dy)


def _prenorm(x, g, scale, shift, *, tm, name):
    t, d = x.shape

    def body(x_ref, g_ref, sc_ref, sh_ref, h_ref):
        xv = x_ref[...]
        r = lax.rsqrt(jnp.mean(xv * xv, axis=-1, keepdims=True) + RMS_EPS)
        h_ref[...] = ((xv * r) * g_ref[...] * (1.0 + sc_ref[...]) + sh_ref[...]).astype(BF16)

    row = pl.BlockSpec((1, d), lambda i: (0, 0))
    tile = pl.BlockSpec((tm, d), lambda i: (i, 0))
    return pl.pallas_call(
        body, grid=(t // tm,), in_specs=[tile, row, row, row], out_specs=tile,
        out_shape=jax.ShapeDtypeStruct((t, d), BF16), name=name,
        compiler_params=_cparams(("arbitrary",), 40),
    )(x, g, scale, shift)


def _prenorm_bwd(dh, x, dout, g, scale, *, tm, name):
    t, d = x.shape

    def body(dh_ref, x_ref, do_ref, g_ref, sc_ref, dx_ref, dg_ref, dsc_ref, dsh_ref):
        i = pl.program_id(0)
        xv = x_ref[...]
        dhv = dh_ref[...]
        r = lax.rsqrt(jnp.mean(xv * xv, axis=-1, keepdims=True) + RMS_EPS)
        xn = xv * r
        gain = g_ref[...] * (1.0 + sc_ref[...])
        dxn = dhv * gain
        dx_ref[...] = do_ref[...] + r * (dxn - xn * jnp.mean(dxn * xn, axis=-1, keepdims=True))
        dhxn = dhv * xn
        dg = jnp.sum(dhxn * (1.0 + sc_ref[...]), axis=0, keepdims=True)
        dsc = jnp.sum(dhxn * g_ref[...], axis=0, keepdims=True)
        dsh = jnp.sum(dhv, axis=0, keepdims=True)

        @pl.when(i == 0)
        def _():
            dg_ref[...] = dg
            dsc_ref[...] = dsc
            dsh_ref[...] = dsh

        @pl.when(i > 0)
        def _():
            dg_ref[...] += dg
            dsc_ref[...] += dsc
            dsh_ref[...] += dsh

    row = pl.BlockSpec((1, d), lambda i: (0, 0))
    tile = pl.BlockSpec((tm, d), lambda i: (i, 0))
    vec = jax.ShapeDtypeStruct((1, d), F32)
    return pl.pallas_call(
        body, grid=(t // tm,), in_specs=[tile, tile, tile, row, row],
        out_specs=[tile, row, row, row],
        out_shape=[jax.ShapeDtypeStruct((t, d), F32), vec, vec, vec], name=name,
        compiler_params=_cparams(("arbitrary",), 48),
    )(dh, x, dout, g, scale)


def _out_fwd(merged, w_out, x, res_gate, *, tm, name):
    t, d = x.shape
    k = merged.shape[1]

    def body(m_ref, w_ref, x_ref, rg_ref, xo_ref, mo_ref):
        mo = jnp.dot(m_ref[...], w_ref[...], preferred_element_type=F32)
        mo_ref[...] = mo.astype(BF16)
        xo_ref[...] = x_ref[...] + rg_ref[...] * mo

    tile = pl.BlockSpec((tm, d), lambda i: (i, 0))
    return pl.pallas_call(
        body, grid=(t // tm,),
        in_specs=[pl.BlockSpec((tm, k), lambda i: (i, 0)), pl.BlockSpec((k, d), lambda i: (0, 0)),
                  tile, pl.BlockSpec((1, d), lambda i: (0, 0))],
        out_specs=[tile, tile],
        out_shape=[jax.ShapeDtypeStruct((t, d), F32), jax.ShapeDtypeStruct((t, d), BF16)], name=name,
        compiler_params=_cparams(("arbitrary",), 56),
    )(merged, w_out, x, res_gate)


def _out_bwd(dout, mo, res_gate, *, tm, name):
    t, d = dout.shape

    def body(do_ref, mo_ref, rg_ref, dmo_ref, drg_ref):
        i = pl.program_id(0)
        dov = do_ref[...]
        dmo_ref[...] = (dov * rg_ref[...]).astype(BF16)
        drg = jnp.sum(dov * mo_ref[...].astype(F32), axis=0, keepdims=True)

        @pl.when(i == 0)
        def _():
            drg_ref[...] = drg

        @pl.when(i > 0)
        def _():
            drg_ref[...] += drg

    row = pl.BlockSpec((1, d), lambda i: (0, 0))
    tile = pl.BlockSpec((tm, d), lambda i: (i, 0))
    return pl.pallas_call(
        body, grid=(t // tm,), in_specs=[tile, tile, row], out_specs=[tile, row],
        out_shape=[jax.ShapeDtypeStruct((t, d), BF16), jax.ShapeDtypeStruct((1, d), F32)], name=name,
        compiler_params=_cparams(("arbitrary",), 40),
    )(dout, mo, res_gate)


def _loss_head(x, target, final_g, *, tm, name):
    t, d = x.shape

    def body(x_ref, t_ref, g_ref, dx_ref, loss_ref, dg_ref):
        i = pl.program_id(0)
        xv = x_ref[...]
        r = lax.rsqrt(jnp.mean(xv * xv, axis=-1, keepdims=True) + RMS_EPS)
        xn = xv * r
        err = xn * g_ref[...] - t_ref[...]
        part = (0.5 / d) * jnp.sum(jnp.sum(err * err, axis=1, keepdims=True), axis=0, keepdims=True)
        dy = err * (1.0 / d)
        dxn = dy * g_ref[...]
        dx_ref[...] = r * (dxn - xn * jnp.mean(dxn * xn, axis=-1, keepdims=True))
        dg = jnp.sum(dy * xn, axis=0, keepdims=True)

        @pl.when(i == 0)
        def _():
            loss_ref[...] = jnp.zeros_like(loss_ref) + part
            dg_ref[...] = dg

        @pl.when(i > 0)
        def _():
            loss_ref[...] += part
            dg_ref[...] += dg

    row = pl.BlockSpec((1, d), lambda i: (0, 0))
    tile = pl.BlockSpec((tm, d), lambda i: (i, 0))
    return pl.pallas_call(
        body, grid=(t // tm,), in_specs=[tile, tile, row],
        out_specs=[tile, pl.BlockSpec((8, LANE), lambda i: (0, 0)), row],
        out_shape=[jax.ShapeDtypeStruct((t, d), F32), jax.ShapeDtypeStruct((8, LANE), F32),
                   jax.ShapeDtypeStruct((1, d), F32)], name=name,
        compiler_params=_cparams(("arbitrary",), 40),
    )(x, target, final_g)


def _cur(tm, cw, col):
    return pl.BlockSpec((tm, cw), lambda cb, i: (i, col + cb))


def _prev(tm, cw, col):
    return pl.BlockSpec((HALO, cw), lambda cb, i: (jnp.maximum(i * (tm // HALO) - 1, 0), col + cb))


def _next(tm, cw, col, t):
    last = t // HALO - 1
    return pl.BlockSpec((HALO, cw), lambda cb, i: (jnp.minimum((i + 1) * (tm // HALO), last), col + cb))


def _with_prev(prev_ref, cur, first):
    prev = jnp.where(first, 0.0, prev_ref[...].astype(F32))
    return jnp.concatenate([prev, cur], axis=0)


def _with_next(cur, next_ref, last):
    nxt = jnp.where(last, 0.0, next_ref[...].astype(F32))
    return jnp.concatenate([cur, nxt], axis=0)


def _pool_mixed(xe, cur, g, row0, tm):
    s2 = xe + pltpu.roll(xe, 1, 0)
    s4 = s2 + pltpu.roll(s2, 2, 0)
    s8 = s4 + pltpu.roll(s4, 4, 0)
    s16 = s8 + pltpu.roll(s8, 8, 0)
    win = jnp.where(g == 0, s2, jnp.where(g == 1, s4, jnp.where(g == 2, s8, s16)))[HALO:]
    return win * _pool_inv_count(g, row0, tm, cur.shape[1]) - cur


def _pool_inv_count(g, row0, rows, cols):
    tpos = row0 + lax.broadcasted_iota(jnp.int32, (rows, cols), 0)
    width = jnp.left_shift(2, g)
    return 1.0 / jnp.minimum(tpos + 1, width).astype(F32)


def _pool_fwd(proj, pool_w, pool_scale, *, d, tm, name):
    t = proj.shape[0]
    pw = d // 2
    gd = pw // N_POOL_GROUPS
    za_col = pw // gd

    def body(xa_ref, xp_ref, za_ref, w_ref, ps_ref, ya_ref):
        g = pl.program_id(0)
        i = pl.program_id(1)
        cur = xa_ref[...].astype(F32)
        mixed = _pool_mixed(_with_prev(xp_ref, cur, i == 0), cur, g, i * tm, tm)
        ypre = jnp.dot(mixed.astype(BF16), w_ref[...], preferred_element_type=F32)
        za = za_ref[...].astype(F32)
        ya_ref[...] = (ypre * ps_ref[...] * (za * _sigmoid(za))).astype(BF16)

    return pl.pallas_call(
        body, grid=(N_POOL_GROUPS, t // tm),
        in_specs=[_cur(tm, gd, 0), _prev(tm, gd, 0), _cur(tm, gd, za_col),
                  pl.BlockSpec((None, gd, gd), lambda g, i: (g, 0, 0)),
                  pl.BlockSpec((1, gd), lambda g, i: (0, g))],
        out_specs=pl.BlockSpec((tm, gd), lambda g, i: (i, g)),
        out_shape=jax.ShapeDtypeStruct((t, pw), BF16), name=name,
        compiler_params=_cparams(("arbitrary", "arbitrary"), 40),
    )(proj, proj, proj, pool_w, pool_scale)


def _pool_bwd(dya, proj, pool_w, pool_scale, *, d, tm, name):
    t = proj.shape[0]
    pw = d // 2
    gd = pw // N_POOL_GROUPS
    za_col = pw // gd
    n_i = t // tm

    def body(xa_ref, xp_ref, za_ref, zn_ref, dy_ref, dyn_ref, w_ref, ps_ref,
             dxa_ref, dza_ref, dw_ref, dps_ref):
        g = pl.program_id(0)
        i = pl.program_id(1)
        last = i == n_i - 1
        cur = xa_ref[...].astype(F32)
        mixed = _pool_mixed(_with_prev(xp_ref, cur, i == 0), cur, g, i * tm, tm)
        mixed16 = mixed.astype(BF16)
        ypre = jnp.dot(mixed16, w_ref[...], preferred_element_type=F32)
        za = za_ref[...].astype(F32)
        sg = _sigmoid(za)
        dy = dy_ref[...].astype(F32)
        dza_ref[...] = (dy * (ypre * ps_ref[...]) * (sg * (1.0 + za * (1.0 - sg)))).astype(BF16)
        dysc = dy * (za * sg)
        za_e = _with_next(za, zn_ref, last)
        dy_e = _with_next(dy, dyn_ref, last)
        dypre_e = (dy_e * (za_e * _sigmoid(za_e)) * ps_ref[...]).astype(BF16)
        dm_e = lax.dot_general(dypre_e, w_ref[...], (((1,), (1,)), ((), ())), preferred_element_type=F32)
        rows = tm + HALO
        q = dm_e * _pool_inv_count(g, i * tm, rows, gd)
        f2 = q + pltpu.roll(q, rows - 1, 0)
        f4 = f2 + pltpu.roll(f2, rows - 2, 0)
        f8 = f4 + pltpu.roll(f4, rows - 4, 0)
        f16 = f8 + pltpu.roll(f8, rows - 8, 0)
        ahead = jnp.where(g == 0, f2, jnp.where(g == 1, f4, jnp.where(g == 2, f8, f16)))
        dxa_ref[...] = (ahead[:tm] - dm_e[:tm]).astype(BF16)
        dw = lax.dot_general(mixed16, dypre_e[:tm], (((0,), (0,)), ((), ())), preferred_element_type=F32)
        dps = jnp.sum(dysc * ypre, axis=0, keepdims=True)

        @pl.when(i == 0)
        def _():
            dw_ref[...] = dw
            dps_ref[...] = dps

        @pl.when(i > 0)
        def _():
            dw_ref[...] += dw
            dps_ref[...] += dps

    out_tile = pl.BlockSpec((tm, gd), lambda g, i: (i, g))
    return pl.pallas_call(
        body, grid=(N_POOL_GROUPS, n_i),
        in_specs=[_cur(tm, gd, 0), _prev(tm, gd, 0), _cur(tm, gd, za_col), _next(tm, gd, za_col, t),
                  _cur(tm, gd, 0), _next(tm, gd, 0, t),
                  pl.BlockSpec((None, gd, gd), lambda g, i: (g, 0, 0)),
                  pl.BlockSpec((1, gd), lambda g, i: (0, g))],
        out_specs=[out_tile, out_tile, pl.BlockSpec((None, gd, gd), lambda g, i: (g, 0, 0)),
                   pl.BlockSpec((1, gd), lambda g, i: (0, g))],
        out_shape=[jax.ShapeDtypeStruct((t, pw), BF16), jax.ShapeDtypeStruct((t, pw), BF16),
                   jax.ShapeDtypeStruct((N_POOL_GROUPS, gd, gd), F32), jax.ShapeDtypeStruct((1, pw), F32)],
        name=name, compiler_params=_cparams(("arbitrary", "arbitrary"), 48),
    )(proj, proj, proj, proj, dya, dya, pool_w, pool_scale)


def _conv_cols(d, cw):
    cc = d // 2
    base = (d + 4 * d) // cw
    return base, base + cc // cw, base + 2 * (cc // cw), base + 3 * (cc // cw)


def _conv_fwd(proj, conv_w, *, d, tm, cw, name):
    t = proj.shape[0]
    cc = d // 2
    u_col, b_col, c_col, z_col = _conv_cols(d, cw)

    def body(u_ref, up_ref, c_ref, cp_ref, b_ref, z_ref, w_ref, y_ref):
        i = pl.program_id(1)
        v = u_ref[...].astype(F32) * c_ref[...].astype(F32)
        vprev = up_ref[...].astype(F32) * cp_ref[...].astype(F32)
        ve = jnp.concatenate([jnp.where(i == 0, 0.0, vprev), v], axis=0)
        w = w_ref[...]
        y = w[0:1] * pltpu.roll(ve, 2, 0)[HALO:] + w[1:2] * pltpu.roll(ve, 1, 0)[HALO:] + w[2:3] * v
        z = z_ref[...].astype(F32)
        y_ref[...] = (b_ref[...].astype(F32) * y * (z * _sigmoid(z))).astype(BF16)

    return pl.pallas_call(
        body, grid=(cc // cw, t // tm),
        in_specs=[_cur(tm, cw, u_col), _prev(tm, cw, u_col), _cur(tm, cw, c_col), _prev(tm, cw, c_col),
                  _cur(tm, cw, b_col), _cur(tm, cw, z_col),
                  pl.BlockSpec((8, cw), lambda cb, i: (0, cb))],
        out_specs=pl.BlockSpec((tm, cw), lambda cb, i: (i, cb)),
        out_shape=jax.ShapeDtypeStruct((t, cc), BF16), name=name,
        compiler_params=_cparams(("arbitrary", "arbitrary"), 40),
    )(proj, proj, proj, proj, proj, proj, conv_w)


def _conv_bwd(dyc, proj, conv_w, *, d, tm, cw, name):
    t = proj.shape[0]
    cc = d // 2
    n_i = t // tm
    u_col, b_col, c_col, z_col = _conv_cols(d, cw)

    def body(u_ref, up_ref, c_ref, cp_ref, b_ref, bn_ref, z_ref, zn_ref, dy_ref, dyn_ref, w_ref,
             du_ref, db_ref, dc_ref, dz_ref, dw_ref):
        i = pl.program_id(1)
        last = i == n_i - 1
        u = u_ref[...].astype(F32)
        cg = c_ref[...].astype(F32)
        v = u * cg
        vprev = up_ref[...].astype(F32) * cp_ref[...].astype(F32)
        ve = jnp.concatenate([jnp.where(i == 0, 0.0, vprev), v], axis=0)
        v2 = pltpu.roll(ve, 2, 0)[HALO:]
        v1 = pltpu.roll(ve, 1, 0)[HALO:]
        w = w_ref[...]
        y = w[0:1] * v2 + w[1:2] * v1 + w[2:3] * v
        z = z_ref[...].astype(F32)
        sg = _sigmoid(z)
        bg = b_ref[...].astype(F32)
        dyc_v = dy_ref[...].astype(F32)
        dz_ref[...] = (dyc_v * bg * y * (sg * (1.0 + z * (1.0 - sg)))).astype(BF16)
        db_ref[...] = (dyc_v * y * (z * sg)).astype(BF16)
        ze = _with_next(z, zn_ref, last)
        dy_e = (_with_next(dyc_v, dyn_ref, last) * _with_next(bg, bn_ref, last) * (ze * _sigmoid(ze)))
        rows = tm + HALO
        dy0 = dy_e[:tm]
        dv = (w[2:3] * dy0 + w[1:2] * pltpu.roll(dy_e, rows - 1, 0)[:tm]
              + w[0:1] * pltpu.roll(dy_e, rows - 2, 0)[:tm])
        du_ref[...] = (dv * cg).astype(BF16)
        dc_ref[...] = (dv * u).astype(BF16)
        s0 = jnp.sum(dy0 * v2, axis=0, keepdims=True)
        s1 = jnp.sum(dy0 * v1, axis=0, keepdims=True)
        s2 = jnp.sum(dy0 * v, axis=0, keepdims=True)
        r = lax.broadcasted_iota(jnp.int32, (8, cw), 0)
        dw = jnp.where(r == 0, s0, jnp.where(r == 1, s1, jnp.where(r == 2, s2, 0.0)))

        @pl.when(i == 0)
        def _():
            dw_ref[...] = dw

        @pl.when(i > 0)
        def _():
            dw_ref[...] += dw

    out_tile = pl.BlockSpec((tm, cw), lambda cb, i: (i, cb))
    act = jax.ShapeDtypeStruct((t, cc), BF16)
    return pl.pallas_call(
        body, grid=(cc // cw, n_i),
        in_specs=[_cur(tm, cw, u_col), _prev(tm, cw, u_col), _cur(tm, cw, c_col), _prev(tm, cw, c_col),
                  _cur(tm, cw, b_col), _next(tm, cw, b_col, t), _cur(tm, cw, z_col), _next(tm, cw, z_col, t),
                  _cur(tm, cw, 0), _next(tm, cw, 0, t),
                  pl.BlockSpec((8, cw), lambda cb, i: (0, cb))],
        out_specs=[out_tile, out_tile, out_tile, out_tile, pl.BlockSpec((8, cw), lambda cb, i: (0, cb))],
        out_shape=[act, act, act, act, jax.ShapeDtypeStruct((8, cc), F32)], name=name,
        compiler_params=_cparams(("arbitrary", "arbitrary"), 48),
    )(proj, proj, proj, proj, proj, proj, proj, proj, dyc, dyc, conv_w)


def _sum_matrix(prefix):
    r = lax.broadcasted_iota(jnp.int32, (KEY_CHUNK, KEY_CHUNK), 0)
    c = lax.broadcasted_iota(jnp.int32, (KEY_CHUNK, KEY_CHUNK), 1)
    return jnp.where((r <= c) if prefix else (r > c), 1.0, 0.0).astype(BF16)


def _chunk_sums(val, mat, prefix):
    hi = val.astype(BF16)
    part = jnp.dot(hi, mat, preferred_element_type=F32)
    if prefix:
        return part, part[:, KEY_CHUNK - 1:]
    return part, part[:, :1] + hi[:, :1].astype(F32)


def _stick_logits(qs, kb, strict):
    z = lax.dot_general(qs, kb, (((1,), (1,)), ((), ())), preferred_element_type=F32)
    minus_abs = pltpu.bitcast(pltpu.bitcast(z, jnp.int32) | jnp.int32(-2 ** 31), F32)
    sp = jnp.maximum(z, 0.0) + jnp.log(1.0 + jnp.exp(minus_abs))
    log_beta = z - sp
    if strict is not None:
        sp = jnp.where(strict, sp, 0.0)
    return sp, log_beta


def _running(val, mat, carry, upwards):
    n = val.shape[1] // KEY_CHUNK
    out = [None] * n
    for c in (range(n) if upwards else reversed(range(n))):
        part, total = _chunk_sums(val[:, c * KEY_CHUNK:(c + 1) * KEY_CHUNK], mat, upwards)
        out[c] = part + carry
        carry = carry + total
    return jnp.concatenate(out, axis=1), carry


def _attn_cols(d):
    h = d // HEAD_DIM
    return h, 2 * h, 3 * h, 4 * h


def _sweep(i, per, block, carry, upwards):
    n = i * per
    odd = n % 2 == 1
    same = lambda cr: cr
    if upwards:
        carry = lax.fori_loop(0, n // 2, lambda it, cr: block(2 * it + 1, block(2 * it, cr, False), False), carry)
        carry = lax.cond(odd, lambda cr: block(n - 1, cr, False), same, carry)
        for dgl in range(per):
            carry = block(n + dgl, carry, True)
        return carry
    for dgl in range(per - 1, -1, -1):
        carry = block(n + dgl, carry, True)
    carry = lax.fori_loop(0, n // 2,
                          lambda it, cr: block(n - 2 - 2 * it, block(n - 1 - 2 * it, cr, False), False), carry)
    return lax.cond(odd, lambda cr: block(0, cr, False), same, carry)


def _strict_mask(i, tq, kb, width):
    t_idx = i * tq + lax.broadcasted_iota(jnp.int32, (tq, width), 0)
    s_idx = kb * width + lax.broadcasted_iota(jnp.int32, (tq, width), 1)
    return s_idx < t_idx


def _attn_fwd(proj, *, d, tq, name):
    t = proj.shape[0]
    groups = d // HEAD_DIM // HEADS_PER_STEP
    width = HEADS_PER_STEP * HEAD_DIM
    lanes = [slice(s * HEAD_DIM, (s + 1) * HEAD_DIM) for s in range(HEADS_PER_STEP)]
    q_col, k_col, v_col, z_col = (c // HEADS_PER_STEP for c in _attn_cols(d))

    def body(q_ref, k_ref, v_ref, z_ref, o_ref, y_ref, tot_ref):
        i = pl.program_id(1)
        qs = [(q_ref[:, ln].astype(F32) * (HEAD_DIM ** -0.5)).astype(BF16) for ln in lanes]
        mat = _sum_matrix(prefix=False)

        def block(kb, carry, masked):
            off = pl.multiple_of(kb * KEY_BLOCK, KEY_BLOCK)
            strict = _strict_mask(i, tq, kb, KEY_BLOCK) if masked else None
            out = []
            for s, ln in enumerate(lanes):
                acc, after = carry[s]
                sp, log_beta = _stick_logits(qs[s], k_ref[pl.ds(off, KEY_BLOCK), ln], strict)
                between, after = _running(sp, mat, after, upwards=False)
                a = jnp.exp(log_beta - between)
                if masked:
                    a = jnp.where(strict, a, 0.0)
                acc = acc + jnp.dot(a.astype(BF16), v_ref[pl.ds(off, KEY_BLOCK), ln], preferred_element_type=F32)
                out.append((acc, after))
            return tuple(out)

        start = (jnp.zeros((tq, HEAD_DIM), F32), jnp.zeros((tq, 1), F32))
        done = _sweep(i, tq // KEY_BLOCK, block, (start,) * len(lanes), upwards=False)
        for (o, total), ln in zip(done, lanes):
            o_ref[:, ln] = o.astype(BF16)
            tot_ref[:, ln] = jnp.broadcast_to(total, (tq, HEAD_DIM))
            z = z_ref[:, ln].astype(F32)
            y_ref[:, ln] = (o * (z * _sigmoid(z))).astype(BF16)

    tile = lambda col: pl.BlockSpec((tq, width), lambda h, i: (i, col + h))
    full = lambda col: pl.BlockSpec((t, width), lambda h, i: (0, col + h))
    act = jax.ShapeDtypeStruct((t, d), BF16)
    return pl.pallas_call(
        body, grid=(groups, t // tq),
        in_specs=[tile(q_col), full(k_col), full(v_col), tile(z_col)],
        out_specs=[tile(0), tile(0), tile(0)], out_shape=[act, act, jax.ShapeDtypeStruct((t, d), F32)],
        name=name, compiler_params=_cparams(("arbitrary", "arbitrary"), 48),
    )(proj, proj, proj, proj)


def _attn_bwd(dyb, o, tot, proj, *, d, tq, name):
    t = proj.shape[0]
    groups = d // HEAD_DIM // HEADS_PER_STEP
    width = HEADS_PER_STEP * HEAD_DIM
    lanes = [slice(s * HEAD_DIM, (s + 1) * HEAD_DIM) for s in range(HEADS_PER_STEP)]
    n_i = t // tq
    q_col, k_col, v_col, z_col = (c // HEADS_PER_STEP for c in _attn_cols(d))
    scale = HEAD_DIM ** -0.5

    def body(q_ref, k_ref, v_ref, z_ref, o_ref, tot_ref, dy_ref, dq_ref, dk_ref, dv_ref, dz_ref, dk_acc, dv_acc):
        i = pl.program_id(1)

        @pl.when(i == 0)
        def _():
            dk_acc[...] = jnp.zeros_like(dk_acc)
            dv_acc[...] = jnp.zeros_like(dv_acc)

        qs, do16, all_keep = [], [], []
        for ln in lanes:
            qs.append((q_ref[:, ln].astype(F32) * scale).astype(BF16))
            z = z_ref[:, ln].astype(F32)
            sg = _sigmoid(z)
            dy = dy_ref[:, ln].astype(F32)
            dz_ref[:, ln] = (dy * o_ref[:, ln].astype(F32) * (sg * (1.0 + z * (1.0 - sg)))).astype(BF16)
            do16.append((dy * (z * sg)).astype(BF16))
            all_keep.append(jnp.concatenate([tot_ref[:, ln]] * (KEY_BLOCK // HEAD_DIM), axis=1))
        mat = _sum_matrix(prefix=True)

        def block(kb, carry, masked):
            off = pl.multiple_of(kb * KEY_BLOCK, KEY_BLOCK)
            strict = _strict_mask(i, tq, kb, KEY_BLOCK) if masked else None
            out = []
            for s, ln in enumerate(lanes):
                dq, sp_seen, g_seen = carry[s]
                kblk = k_ref[pl.ds(off, KEY_BLOCK), ln]
                vblk = v_ref[pl.ds(off, KEY_BLOCK), ln]
                sp, log_beta = _stick_logits(qs[s], kblk, strict)
                sp_upto, sp_seen = _running(sp, mat, sp_seen, upwards=True)
                a = jnp.exp(log_beta - (all_keep[s] - sp_upto))
                if masked:
                    a = jnp.where(strict, a, 0.0)
                g = a * lax.dot_general(do16[s], vblk, (((1,), (1,)), ((), ())), preferred_element_type=F32)
                g_upto, g_seen = _running(g, mat, g_seen, upwards=True)
                dz_ts = g - jnp.exp(log_beta) * g_upto
                if masked:
                    dz_ts = jnp.where(strict, dz_ts, 0.0)
                dz16 = dz_ts.astype(BF16)
                dq = dq + jnp.dot(dz16, kblk, preferred_element_type=F32)
                dk_acc[pl.ds(off, KEY_BLOCK), ln] += lax.dot_general(
                    dz16, qs[s], (((0,), (0,)), ((), ())), preferred_element_type=F32)
                dv_acc[pl.ds(off, KEY_BLOCK), ln] += lax.dot_general(
                    a.astype(BF16), do16[s], (((0,), (0,)), ((), ())), preferred_element_type=F32)
                out.append((dq, sp_seen, g_seen))
            return tuple(out)

        column = jnp.zeros((tq, 1), F32)
        start = (jnp.zeros((tq, HEAD_DIM), F32), column, column)
        done = _sweep(i, tq // KEY_BLOCK, block, (start,) * len(lanes), upwards=True)
        for (dq, _, _), ln in zip(done, lanes):
            dq_ref[:, ln] = (dq * scale).astype(BF16)

        @pl.when(i == n_i - 1)
        def _():
            dk_ref[...] = dk_acc[...].astype(BF16)
            dv_ref[...] = dv_acc[...].astype(BF16)

    tile = lambda col: pl.BlockSpec((tq, width), lambda h, i: (i, col + h))
    full = lambda col: pl.BlockSpec((t, width), lambda h, i: (0, col + h))
    act = jax.ShapeDtypeStruct((t, d), BF16)
    return pl.pallas_call(
        body, grid=(groups, n_i),
        in_specs=[tile(q_col), full(k_col), full(v_col), tile(z_col), tile(0), tile(0), tile(0)],
        out_specs=[tile(0), full(0), full(0), tile(0)], out_shape=[act, act, act, act],
        scratch_shapes=[pltpu.VMEM((t, width), F32), pltpu.VMEM((t, width), F32)], name=name,
        compiler_params=_cparams(("arbitrary", "arbitrary"), 56),
    )(proj, proj, proj, proj, o, tot, dyb)


def _merge_fwd(gl, bst, *, d, tm, tn, name):
    t = gl.shape[0]
    nk = d // tn

    def body(g0, g1, g2, b0, b1, b2, m_ref):
        acc = _sigmoid(g0[...].astype(F32)) * b0[...].astype(F32)
        acc += _sigmoid(g1[...].astype(F32)) * b1[...].astype(F32)
        acc += _sigmoid(g2[...].astype(F32)) * b2[...].astype(F32)
        m_ref[...] = acc.astype(BF16)

    spec = lambda b: pl.BlockSpec((tm, tn), lambda i, kb: (i, b * nk + kb))
    return pl.pallas_call(
        body, grid=(t // tm, nk), in_specs=[spec(0), spec(1), spec(2)] * 2,
        out_specs=pl.BlockSpec((tm, tn), lambda i, kb: (i, kb)),
        out_shape=jax.ShapeDtypeStruct((t, d), BF16), name=name,
        compiler_params=_cparams(("arbitrary", "arbitrary"), 40),
    )(gl, gl, gl, bst, bst, bst)


def _merge_bwd(dmerged, gl, bst, *, d, tm, tn, name):
    t = gl.shape[0]
    nk = d // tn

    def body(dm_ref, g_ref, b_ref, db_ref, dg_ref, dbias_ref):
        i = pl.program_id(2)
        dm = dm_ref[...].astype(F32)
        sg = _sigmoid(g_ref[...].astype(F32))
        db_ref[...] = (dm * sg).astype(BF16)
        dgl = dm * b_ref[...].astype(F32) * (sg * (1.0 - sg))
        dg_ref[...] = dgl.astype(BF16)
        dbias = jnp.sum(dgl, axis=0, keepdims=True)

        @pl.when(i == 0)
        def _():
            dbias_ref[...] = dbias

        @pl.when(i > 0)
        def _():
            dbias_ref[...] += dbias

    wide = pl.BlockSpec((tm, tn), lambda b, kb, i: (i, b * nk + kb))
    act = jax.ShapeDtypeStruct((t, 3 * d), BF16)
    return pl.pallas_call(
        body, grid=(3, nk, t // tm),
        in_specs=[pl.BlockSpec((tm, tn), lambda b, kb, i: (i, kb)), wide, wide],
        out_specs=[wide, wide, pl.BlockSpec((1, tn), lambda b, kb, i: (0, b * nk + kb))],
        out_shape=[act, act, jax.ShapeDtypeStruct((1, 3 * d), F32)], name=name,
        compiler_params=_cparams(("arbitrary", "arbitrary", "arbitrary"), 40),
    )(dmerged, gl, bst)


def _adamw(w, m, v, parts, layer, prev, *, tr, name):
    depth, r, c = w.shape
    n = parts.shape[0]

    def body(*refs):
        w_ref, m_ref, v_ref, p_ref = refs[:4]
        g_ref, d_ref, nm_ref, nv_ref = refs[-4:]
        g = p_ref[0].astype(F32)
        for s in range(1, n):
            g = g + p_ref[s].astype(F32)
        wv = w_ref[...]
        nm = ADAM_B1 * m_ref[...] + (1.0 - ADAM_B1) * g
        nv = ADAM_B2 * v_ref[...] + (1.0 - ADAM_B2) * (g * g)
        m_hat = nm / (1.0 - ADAM_B1 ** ADAM_STEP)
        v_hat = nv / (1.0 - ADAM_B2 ** ADAM_STEP)
        g_ref[...] = g
        nm_ref[...] = nm
        nv_ref[...] = nv
        d_ref[...] = -ADAM_LR * (m_hat / (jnp.sqrt(v_hat) + ADAM_EPS) + ADAM_WD * wv)

    tile = pl.BlockSpec((None, tr, c), lambda i: (layer, i, 0))
    in_specs = [tile, tile, tile, pl.BlockSpec((n, tr, c), lambda i: (0, i, 0))]
    args = [w, m, v, parts]
    aliases = {}
    if prev is not None:
        in_specs += [pl.BlockSpec(memory_space=pl.ANY)] * 4
        aliases = {4 + k: k for k in range(4)}
        args += list(prev)
    full = jax.ShapeDtypeStruct((depth, r, c), F32)
    return pl.pallas_call(
        body, grid=(r // tr,), in_specs=in_specs, out_specs=[tile] * 4, out_shape=[full] * 4,
        input_output_aliases=aliases, name=name, compiler_params=_cparams(("arbitrary",), 48),
    )(*args)


def _adamw_layers(w, m, v, parts_by_layer, *, tr, name):
    lead = w.shape[0]
    flat = lambda a: a.reshape(lead, -1, a.shape[-1])
    w, m, v = flat(w), flat(m), flat(v)
    out = None
    for layer in reversed(range(lead)):
        parts = parts_by_layer[layer]
        parts = parts.reshape(parts.shape[0], -1, parts.shape[-1])
        out = _adamw(w, m, v, parts, layer, out, tr=min(tr, w.shape[1]), name=f"{name}_l{layer}")
    return out


def _place():
    return lax.axis_index("x"), lax.axis_index("y"), lax.axis_index("c")


def _index(px, py, pc):
    return 4 * px + 2 * py + pc


def _whole(ref, idx):
    return ref.at[idx]


def _row_block(rows):
    return lambda ref, idx: ref.at[:, pl.ds(idx * rows, rows), :]


def _lane_block(cols):
    return lambda ref, idx: ref.at[:, pl.ds(idx * cols, cols)]


def _all_gather(shards, out_shapes, placers, *, name):
    n = len(shards)

    def body(*refs):
        src, out = refs[:n], refs[n:2 * n]
        send_sems, recv_sems, local_sems = refs[2 * n:]
        x, y, c = _place()
        me, sibling = (x, y, c), (x, y, 1 - c)
        chips = [(1 - x, y), (x, 1 - y), (1 - x, 1 - y)]

        def copy(a, k, block, to, from_shard=False):
            window = placers[a](out[a], _index(*block))
            return pltpu.make_async_remote_copy(
                src_ref=src[a] if from_shard else window, dst_ref=window,
                send_sem=send_sems.at[a * 7 + k], recv_sem=recv_sems.at[a * 7 + k],
                device_id=to, device_id_type=MESH)

        mine = [pltpu.make_async_copy(src[a], placers[a](out[a], _index(*me)), local_sems.at[a])
                for a in range(n)]
        started = []
        for a in range(n):
            mine[a].start()
            started.append(copy(a, 0, me, sibling, True))
            started += [copy(a, 1 + j, me, (*chip, c), True) for j, chip in enumerate(chips)]
        for cp in started:
            cp.start()
        for j, chip in enumerate(chips):
            for a in range(n):
                copy(a, 1 + j, (*chip, c), me).wait_recv()
                passed = copy(a, 4 + j, (*chip, c), sibling)
                passed.start()
                started.append(passed)
        for a in range(n):
            copy(a, 0, sibling, me).wait_recv()
            for j, chip in enumerate(chips):
                copy(a, 4 + j, (*chip, 1 - c), me).wait_recv()
        for cp in started:
            cp.wait_send()
        for cp in mine:
            cp.wait()

    hbm = pl.BlockSpec(memory_space=pl.ANY)
    return pl.pallas_call(
        body, in_specs=[hbm] * n, out_specs=[hbm] * n, out_shape=out_shapes,
        scratch_shapes=[pltpu.SemaphoreType.DMA((7 * n,)), pltpu.SemaphoreType.DMA((7 * n,)),
                        pltpu.SemaphoreType.DMA((n,))],
        name=name,
    )(*shards)


def _scatter_parts(grads, pickers, part_shapes, *, name):
    n = len(grads)

    def body(*refs):
        src, out = refs[:n], refs[n:2 * n]
        send_sems, recv_sems, local_sems = refs[2 * n:]
        x, y, c = _place()
        my_idx = _index(x, y, c)
        peers = [(x ^ (k >> 2), y ^ ((k >> 1) & 1), c ^ (k & 1)) for k in range(1, N_DEV)]

        def copy(a, k, peer):
            return pltpu.make_async_remote_copy(
                src_ref=pickers[a](src[a], _index(*peer)), dst_ref=out[a].at[my_idx],
                send_sem=send_sems.at[a * 7 + k], recv_sem=recv_sems.at[a * 7 + k],
                device_id=peer, device_id_type=MESH)

        def arrival(a, k, peer):
            return pltpu.make_async_remote_copy(
                src_ref=pickers[a](src[a], my_idx), dst_ref=out[a].at[_index(*peer)],
                send_sem=send_sems.at[a * 7 + k], recv_sem=recv_sems.at[a * 7 + k],
                device_id=peer, device_id_type=MESH)

        mine = [pltpu.make_async_copy(pickers[a](src[a], my_idx), out[a].at[my_idx], local_sems.at[a])
                for a in range(n)]
        sent = [copy(a, k, peer) for a in range(n) for k, peer in enumerate(peers)]
        for cp in mine + sent:
            cp.start()
        for a in range(n):
            for k, peer in enumerate(peers):
                arrival(a, k, peer).wait_recv()
        for cp in sent:
            cp.wait_send()
        for cp in mine:
            cp.wait()

    hbm = pl.BlockSpec(memory_space=pl.ANY)
    return pl.pallas_call(
        body, in_specs=[hbm] * n, out_specs=[hbm] * n, out_shape=part_shapes,
        scratch_shapes=[pltpu.SemaphoreType.DMA((7 * n,)), pltpu.SemaphoreType.DMA((7 * n,)),
                        pltpu.SemaphoreType.DMA((n,))],
        name=name,
    )(*grads)


def _peers(x, y, c):
    return [(x ^ (k >> 2), y ^ ((k >> 1) & 1), c ^ (k & 1)) for k in range(1, N_DEV)]


_HBM = pl.BlockSpec(memory_space=pltpu.HBM)
_SEM = pl.BlockSpec(memory_space=pltpu.SEMAPHORE)
_EFFECT = pltpu.SideEffectType.DATAFLOW_SIDE_EFFECTING


def _exchange_start(srcs, land_shapes, src_win, dst_win, after, *, name):
    n = len(srcs)

    def body(*refs):
        src, land = refs[:n], refs[n:2 * n]
        send_sems, recv_sems, token = refs[2 * n + len(after)], refs[2 * n + len(after) + 1], refs[-1]
        x, y, c = _place()
        my_idx = _index(x, y, c)
        for a in range(n):
            for k, peer in enumerate(_peers(x, y, c)):
                pltpu.make_async_remote_copy(
                    src_ref=src_win[a](src[a], _index(*peer)), dst_ref=dst_win[a](land[a], my_idx),
                    send_sem=send_sems.at[a * 7 + k], recv_sem=recv_sems.at[a * 7 + k],
                    device_id=peer, device_id_type=MESH).start()
        token[...] = jnp.zeros_like(token)

    hbm = lambda a: pltpu.with_memory_space_constraint(a, pltpu.HBM)
    lands = [hbm(lax.empty(s.shape, s.dtype)) for s in land_shapes]
    out = pl.pallas_call(
        body, name=name,
        out_shape=(pltpu.SemaphoreType.DMA((7 * n,)), pltpu.SemaphoreType.DMA((7 * n,)),
                   *[pltpu.HBM(a.shape, a.dtype) for a in lands], jax.ShapeDtypeStruct((8, LANE), F32)),
        in_specs=[_HBM] * (2 * n) + [pl.BlockSpec(memory_space=pl.ANY)] * len(after),
        out_specs=(_SEM, _SEM, *([_HBM] * n), pl.BlockSpec(memory_space=pltpu.VMEM)),
        input_output_aliases={n + i: 2 + i for i in range(n)},
        compiler_params=pltpu.CompilerParams(has_side_effects=_EFFECT),
    )(*srcs, *lands, *after)
    return out[0], out[1], list(out[2:2 + n]), out[-1][0, 0]


def _exchange_wait(send_sems, recv_sems, srcs, lands, src_win, dst_win, after, *, name):
    n = len(srcs)

    def body(*refs):
        src, land = refs[:n], refs[n:2 * n]
        send_sems, recv_sems = refs[2 * n], refs[2 * n + 1]
        x, y, c = _place()
        for a in range(n):
            for k, peer in enumerate(_peers(x, y, c)):
                sender = _index(*peer)
                copy = pltpu.make_async_remote_copy(
                    src_ref=src_win[a](src[a], sender), dst_ref=dst_win[a](land[a], sender),
                    send_sem=send_sems.at[a * 7 + k], recv_sem=recv_sems.at[a * 7 + k],
                    device_id=peer, device_id_type=MESH)
                copy.wait_send()
                copy.wait_recv()

    out = pl.pallas_call(
        body, name=name, out_shape=tuple(pltpu.HBM(a.shape, a.dtype) for a in lands),
        in_specs=[_HBM] * (2 * n) + [_SEM, _SEM] + [pl.BlockSpec(memory_space=pl.ANY)] * len(after),
        out_specs=tuple([_HBM] * n),
        input_output_aliases={n + i: i for i in range(n)},
        compiler_params=pltpu.CompilerParams(has_side_effects=_EFFECT),
    )(*srcs, *lands, send_sems, recv_sems, *after)
    return list(out)


COPY_TILE_BYTES = 2 << 20


def _copy_own_block(src, land, idx, *, from_stack, name):
    k, n = land.shape[1:]
    tk = k
    while tk * n * land.dtype.itemsize > COPY_TILE_BYTES and tk % 32 == 0:
        tk //= 2

    def body(idx_ref, src_ref, land_ref, out_ref):
        out_ref[...] = src_ref[...]

    own = pl.BlockSpec((None, tk, n), lambda i, idx_ref: (idx_ref[0], i, 0))
    grid_spec = pltpu.PrefetchScalarGridSpec(
        num_scalar_prefetch=1, grid=(k // tk,),
        in_specs=[own if from_stack else pl.BlockSpec((tk, n), lambda i, idx_ref: (i, 0)),
                  pl.BlockSpec(memory_space=pl.ANY)],
        out_specs=own)
    return pl.pallas_call(
        body, grid_spec=grid_spec, out_shape=jax.ShapeDtypeStruct(land.shape, land.dtype),
        input_output_aliases={2: 0}, name=name,
    )(idx, src, land)


def _place_own(srcs, lands, src_win, dst_win, *, name):
    lands = list(lands)
    idx = _index(*_place()).astype(jnp.int32).reshape(1)
    rest = []
    for a in range(len(srcs)):
        if dst_win[a] is _whole and src_win[a] in (_whole, _shard_itself) and lands[a].ndim == 3:
            lands[a] = _copy_own_block(srcs[a], lands[a], idx, from_stack=src_win[a] is _whole,
                                       name=f"{name}_{a}")
        else:
            rest.append(a)
    if rest:
        placed = _dma_own([srcs[a] for a in rest], [lands[a] for a in rest], [src_win[a] for a in rest],
                          [dst_win[a] for a in rest], name=f"{name}_dma")
        for a, land in zip(rest, placed):
            lands[a] = land
    return lands


def _dma_own(srcs, lands, src_win, dst_win, *, name):
    n = len(srcs)

    def body(*refs):
        src, land = refs[:n], refs[n:2 * n]
        sems = refs[-1]
        my_idx = _index(*_place())
        copies = [pltpu.make_async_copy(src_win[a](src[a], my_idx), dst_win[a](land[a], my_idx), sems.at[a])
                  for a in range(n)]
        for cp in copies:
            cp.start()
        for cp in copies:
            cp.wait()

    hbm = pl.BlockSpec(memory_space=pl.ANY)
    return list(pl.pallas_call(
        body, name=name, in_specs=[hbm] * (2 * n), out_specs=[hbm] * n,
        out_shape=[jax.ShapeDtypeStruct(a.shape, a.dtype) for a in lands],
        input_output_aliases={n + i: i for i in range(n)},
        scratch_shapes=[pltpu.SemaphoreType.DMA((n,))],
    )(*srcs, *lands))


def _ada_fwd(c8, w_ada, b_ada8, *, name):
    depth, d, n = w_ada.shape

    def body(c_ref, w_ref, b_ref, mod_ref, sc_ref, c_all, prod, got, send_sems, recv_sems):
        x, y, c = _place()
        my_idx = _index(x, y, c)
        peers = [(x ^ (k >> 2), y ^ ((k >> 1) & 1), c ^ (k & 1)) for k in range(1, N_DEV)]

        def slab(ref, idx):
            return ref.at[pl.ds(pl.multiple_of(idx * 8, 8), 8)]

        def c_copy(k, peer, sender):
            return pltpu.make_async_remote_copy(
                src_ref=c_ref, dst_ref=slab(c_all, sender), send_sem=send_sems.at[k],
                recv_sem=recv_sems.at[k], device_id=peer, device_id_type=MESH)

        def m_copy(k, peer, sender):
            return pltpu.make_async_remote_copy(
                src_ref=prod.at[_index(*peer)], dst_ref=got.at[sender], send_sem=send_sems.at[7 + k],
                recv_sem=recv_sems.at[7 + k], device_id=peer, device_id_type=MESH)

        sends = [c_copy(k, peer, my_idx) for k, peer in enumerate(peers)]
        for cp in sends:
            cp.start()
        slab(c_all, my_idx)[...] = c_ref[...]
        for k, peer in enumerate(peers):
            c_copy(k, peer, _index(*peer)).wait_recv()
        cv = c_all[...]
        sc = cv * _sigmoid(cv)
        sc_ref[...] = sc
        for layer in range(depth):
            p = jnp.dot(sc, w_ref[layer], preferred_element_type=F32, precision=lax.Precision.HIGHEST)
            for s in range(N_DEV):
                prod[s, layer] = p[8 * s:8 * s + 8]
        sends2 = [m_copy(k, peer, my_idx) for k, peer in enumerate(peers)]
        for cp in sends2:
            cp.start()
        got[my_idx] = prod[my_idx]
        for k, peer in enumerate(peers):
            m_copy(k, peer, _index(*peer)).wait_recv()
        for layer in range(depth):
            for s in range(N_DEV):
                mod_ref[layer, :, s * n:(s + 1) * n] = got[s, layer] + b_ref[layer, :, s * n:(s + 1) * n]
        for cp in sends + sends2:
            cp.wait_send()

    vmem = pl.BlockSpec(memory_space=pltpu.VMEM)
    return pl.pallas_call(
        body, in_specs=[vmem] * 3, out_specs=[vmem] * 2,
        out_shape=[jax.ShapeDtypeStruct((depth, 8, N_DEV * n), F32),
                   jax.ShapeDtypeStruct((8 * N_DEV, d), F32)],
        scratch_shapes=[pltpu.VMEM((8 * N_DEV, d), F32), pltpu.VMEM((N_DEV, depth, 8, n), F32),
                        pltpu.VMEM((N_DEV, depth, 8, n), F32),
                        pltpu.SemaphoreType.DMA((14,)), pltpu.SemaphoreType.DMA((14,))],
        name=name, compiler_params=pltpu.CompilerParams(vmem_limit_bytes=56 << 20),
    )(c8, w_ada, b_ada8)


def _ada_bwd(sc8, dmod, *, name):
    depth, _, n = dmod.shape
    d = sc8.shape[1]

    def body(sc_ref, dm_ref, o_ref):
        for layer in range(depth):
            o_ref[layer] = lax.dot_general(sc_ref[...], dm_ref[layer], (((0,), (0,)), ((), ())),
                                           preferred_element_type=F32, precision=lax.Precision.HIGHEST)

    vmem = pl.BlockSpec(memory_space=pltpu.VMEM)
    return pl.pallas_call(
        body, in_specs=[vmem] * 2, out_specs=vmem, out_shape=jax.ShapeDtypeStruct((depth, d, n), F32),
        name=name, compiler_params=pltpu.CompilerParams(vmem_limit_bytes=40 << 20),
    )(sc8, dmod)


def _gather_small(vec, *, name):
    r = vec.shape[0]

    def body(v_ref, o_ref, send_sems, recv_sems):
        x, y, c = _place()
        my_idx = _index(x, y, c)
        peers = [(x ^ (k >> 2), y ^ ((k >> 1) & 1), c ^ (k & 1)) for k in range(1, N_DEV)]

        def copy(k, peer, sender):
            return pltpu.make_async_remote_copy(
                src_ref=v_ref, dst_ref=o_ref.at[sender], send_sem=send_sems.at[k],
                recv_sem=recv_sems.at[k], device_id=peer, device_id_type=MESH)

        sends = [copy(k, peer, my_idx) for k, peer in enumerate(peers)]
        for cp in sends:
            cp.start()
        o_ref[my_idx] = v_ref[...]
        for k, peer in enumerate(peers):
            copy(k, peer, _index(*peer)).wait_recv()
        for cp in sends:
            cp.wait_send()

    vmem = pl.BlockSpec(memory_space=pltpu.VMEM)
    return pl.pallas_call(
        body, in_specs=[vmem], out_specs=vmem, out_shape=jax.ShapeDtypeStruct((N_DEV, r, LANE), F32),
        scratch_shapes=[pltpu.SemaphoreType.DMA((7,)), pltpu.SemaphoreType.DMA((7,))], name=name,
    )(vec)


TM = 512
TM_WIDE = 2048
TMM_DEEP = 1024
TQ = 512


def _layer_fwd(x, mod, small, gw, *, d, tag, more_weights=None):
    shift, scale, res_gate = mod
    h = _prenorm(x, small["norm_g"], scale, shift, tm=TM, name=f"prenorm_{tag}")
    proj = _mm_nn(h, gw["w_in"], out_dtype=BF16, tm=TM, name=f"in_proj_{tag}")
    ya = _pool_fwd(proj, gw["pool_w"], small["pool_scale"], d=d, tm=TM, name=f"pool_fwd_{tag}")
    o, yb, tot = _attn_fwd(proj, d=d, tq=TQ, name=f"attn_fwd_{tag}")
    yc = _conv_fwd(proj, gw["conv_w"], d=d, tm=TM, cw=LANE * 2, name=f"conv_fwd_{tag}")
    if more_weights is not None:
        gw = dict(gw, **more_weights(o))
    gl = _mm_nn(h, gw["w_gate"], out_dtype=BF16, tm=TM, bias=small["b_gate"], name=f"gate_proj_{tag}")
    nblk = gw["w_br_a"].shape[0]
    bst = _mm_nn(ya, gw["w_br_a"], out_dtype=BF16, tm=TM_WIDE, out_cols=3 * d, name=f"branch_a_{tag}")
    bst = _mm_nn(yb, gw["w_br_b"], out_dtype=BF16, tm=TM, out=bst, out_col_block=1, name=f"branch_b_{tag}")
    bst = _mm_nn(yc, gw["w_br_c"], out_dtype=BF16, tm=TM_WIDE, out=bst, out_col_block=2 * nblk,
                 name=f"branch_c_{tag}")
    merged = _merge_fwd(gl, bst, d=d, tm=TM, tn=TM, name=f"merge_fwd_{tag}")
    x_new, mo = _out_fwd(merged, gw["w_out"][0], x, res_gate, tm=TM, name=f"out_fwd_{tag}")
    return x_new, dict(x=x, h=h, proj=proj, gl=gl, ya=ya, yb=yb, yc=yc, o=o, tot=tot, bst=bst, merged=merged,
                       mo=mo), gw


EARLY = ("w_gate", "w_br_a", "w_br_b", "w_br_c", "w_out")
LATE = ("w_in", "pool_w")


def _layer_bwd(dout, sv, mod, small, gw, *, d, tag, send_early=None, send_late=None):
    shift, scale, res_gate = mod
    nblk = gw["w_br_a"].shape[0]
    dmo, d_res_gate = _out_bwd(dout, sv["mo"], res_gate, tm=TM, name=f"out_bwd_{tag}")
    g_w_out = _mm_tn(sv["merged"], dmo, nb=1, n=d, tk=TM, tmm=TMM_DEEP, name=f"dw_out_{tag}")
    dmerged = _mm_nt(dmo, gw["w_out"], out_dtype=BF16, tm=TM, name=f"dmerged_{tag}")
    dbst, dgl, d_b_gate = _merge_bwd(dmerged, sv["gl"], sv["bst"], d=d, tm=TM, tn=TM, name=f"merge_bwd_{tag}")
    g_w_gate = _mm_tn(sv["h"], dgl, nb=nblk, n=3 * d // nblk, tk=d // 2, tmm=TMM_DEEP,
                      name=f"dw_gate_{tag}")
    dh = _mm_nt(dgl, gw["w_gate"], out_dtype=F32, tm=TM, name=f"dh_gate_{tag}")
    g_w_br_a = _mm_tn(sv["ya"], dbst, nb=nblk, n=d // nblk, tk=d // 2, tmm=TM_WIDE,
                      name=f"dw_br_a_{tag}")
    g_w_br_b = _mm_tn(sv["yb"], dbst, nb=1, n=d, tk=TM, tmm=TMM_DEEP, col_block=1, name=f"dw_br_b_{tag}")
    g_w_br_c = _mm_tn(sv["yc"], dbst, nb=nblk, n=d // nblk, tk=d // 2, tmm=TM_WIDE, col_block=2 * nblk,
                      name=f"dw_br_c_{tag}")
    dya = _mm_nt(dbst, gw["w_br_a"], out_dtype=BF16, tm=TM_WIDE, name=f"dya_{tag}")
    dyb = _mm_nt(dbst, gw["w_br_b"], out_dtype=BF16, tm=TM, col_block=1, name=f"dyb_{tag}")
    dyc = _mm_nt(dbst, gw["w_br_c"], out_dtype=BF16, tm=TM_WIDE, col_block=2 * nblk, name=f"dyc_{tag}")
    big = dict(w_gate=g_w_gate, w_br_a=g_w_br_a, w_br_b=g_w_br_b, w_br_c=g_w_br_c, w_out=g_w_out)
    if send_early is not None:
        token = send_early(big)
        small = dict(small, pool_scale=small["pool_scale"] + token, norm_g=small["norm_g"] + token)
        dyb = dyb + token.astype(BF16)
    dxa, dza, g_pool_w, d_pool_scale = _pool_bwd(dya, sv["proj"], gw["pool_w"], small["pool_scale"], d=d, tm=TM,
                                                 name=f"pool_bwd_{tag}")
    dq, dk, dv, dzb = _attn_bwd(dyb, sv["o"], sv["tot"], sv["proj"], d=d, tq=TQ, name=f"attn_bwd_{tag}")
    du, dbg, dcg, dzc, d_conv_w = _conv_bwd(dyc, sv["proj"], gw["conv_w"], d=d, tm=TM, cw=LANE * 2,
                                            name=f"conv_bwd_{tag}")
    dproj = jnp.concatenate([dxa, dza, dq, dk, dv, dzb, du, dbg, dcg, dzc], axis=1)
    g_w_in = _mm_tn(sv["h"], dproj, nb=nblk, n=7 * d // nblk, tk=d // 2, tmm=TMM_DEEP,
                    name=f"dw_in_{tag}")
    big.update(w_in=g_w_in, pool_w=g_pool_w.astype(BF16))
    if send_late is not None:
        dh = dh + send_late(big)
    dh = _mm_nt(dproj, gw["w_in"], out_dtype=F32, tm=TM, addend=dh, name=f"dh_in_{tag}")
    dx, d_norm_g, d_scale, d_shift = _prenorm_bwd(dh, sv["x"], dout, small["norm_g"], scale, tm=TM,
                                                  name=f"prenorm_bwd_{tag}")
    part = dict(norm_g=d_norm_g, mod=jnp.concatenate([d_shift, d_scale, d_res_gate], axis=1),
                pool_scale=d_pool_scale, b_gate=d_b_gate, conv_w=d_conv_w[:3])
    return dx, big, part


BIG = ("w_in", "w_gate", "w_br_a", "w_br_b", "w_br_c", "w_out", "pool_w")
SMALL_PER_LAYER = ("norm_g", "mod", "pool_scale", "b_gate", "conv_w")


def _gather_layer(w, layer, *, d):
    nd = N_DEV
    gd = d // 2 // N_POOL_GROUPS
    cc = d // 2
    shards = [w[k][layer].astype(BF16) for k in BIG]
    shards.append(jnp.pad(w["conv_w"][layer], ((0, 5), (0, 0))))
    blocks = [jax.ShapeDtypeStruct((nd,) + s.shape, BF16) for s in shards[:6]]
    blocks.append(jax.ShapeDtypeStruct((N_POOL_GROUPS, gd, gd), BF16))
    blocks.append(jax.ShapeDtypeStruct((8, cc), F32))
    placers = [_whole] * 6 + [_row_block(gd // nd), _lane_block(cc // nd)]
    return shards, blocks, placers


GATHERED = BIG + ("conv_w",)
FIRST = (0, 6, 7)
REST = (1, 2, 3, 4, 5)


def _as_weights(got, d, which=tuple(range(len(GATHERED)))):
    gw = dict(zip([GATHERED[i] for i in which], got))
    for k in ("w_br_b", "w_out"):
        if k in gw:
            gw[k] = gw[k].reshape(1, d, d)
    return gw


def _shard_itself(ref, idx):
    return ref


def _scatter_spec(big, keys, *, d):
    nd = N_DEV
    gd = d // 2 // N_POOL_GROUPS
    grads, pickers, shapes = [], [], []
    for k in keys:
        g = big[k]
        if k == "pool_w":
            pickers.append(_row_block(gd // nd))
            shapes.append(jax.ShapeDtypeStruct((nd, N_POOL_GROUPS, gd // nd, gd), BF16))
        else:
            if g.shape[0] == 1:
                g = g.reshape(nd, g.shape[1] // nd, g.shape[2])
            pickers.append(_whole)
            shapes.append(jax.ShapeDtypeStruct(g.shape, BF16))
        grads.append(g)
    return grads, pickers, shapes


class _Behind:
    def __init__(self, srcs, land_shapes, src_win, dst_win, after, name):
        srcs = [pltpu.with_memory_space_constraint(s, pltpu.HBM) for s in srcs]
        self.srcs, self.src_win, self.dst_win, self.name = srcs, src_win, dst_win, name
        self.send, self.recv, self.lands, self.token = _exchange_start(
            srcs, land_shapes, src_win, dst_win, after, name=f"{name}_start")

    def finish(self, after):
        lands = _exchange_wait(self.send, self.recv, self.srcs, self.lands, self.src_win, self.dst_win,
                               after, name=f"{self.name}_wait")
        return _place_own(self.srcs, lands, self.src_win, self.dst_win, name=f"{self.name}_own")


def _pack(pieces):
    return jnp.concatenate([p.reshape(-1, LANE) for p in pieces], axis=0)


def kernel(x, c, norm_g, w_ada, b_ada, w_in, pool_w, pool_scale, conv_w, w_br_a, w_br_b, w_br_c, w_gate, b_gate, w_out, final_g, loss_target, m_norm_g, m_w_ada, m_b_ada, m_w_in, m_pool_w, m_pool_scale, m_conv_w, m_w_br_a, m_w_br_b, m_w_br_c, m_w_gate, m_b_gate, m_w_out, m_final_g, v_norm_g, v_w_ada, v_b_ada, v_w_in, v_pool_w, v_pool_scale, v_conv_w, v_w_br_a, v_w_br_b, v_w_br_c, v_w_gate, v_b_gate, v_w_out, v_final_g):
    names = ("norm_g", "w_ada", "b_ada", "w_in", "pool_w", "pool_scale", "conv_w", "w_br_a", "w_br_b",
             "w_br_c", "w_gate", "b_gate", "w_out", "final_g")
    w = dict(zip(names, (norm_g, w_ada, b_ada, w_in, pool_w, pool_scale, conv_w, w_br_a, w_br_b, w_br_c,
                         w_gate, b_gate, w_out, final_g)))
    m = dict(zip(names, (m_norm_g, m_w_ada, m_b_ada, m_w_in, m_pool_w, m_pool_scale, m_conv_w, m_w_br_a,
                         m_w_br_b, m_w_br_c, m_w_gate, m_b_gate, m_w_out, m_final_g)))
    v = dict(zip(names, (v_norm_g, v_w_ada, v_b_ada, v_w_in, v_pool_w, v_pool_scale, v_conv_w, v_w_br_a,
                         v_w_br_b, v_w_br_c, v_w_gate, v_b_gate, v_w_out, v_final_g)))
    _, t, d = x.shape
    depth = norm_g.shape[0]
    cc = d // 2
    my_idx = _index(*_place())

    mod8, sc_all = _ada_fwd(jnp.broadcast_to(c, (8, d)), w_ada,
                            jnp.broadcast_to(b_ada[:, None, :], (depth, 8, 3 * d)), name="ada_fwd")
    mods = [tuple(mod8[l, 0:1, k * d:(k + 1) * d] for k in range(3)) for l in range(depth)]
    small = [dict(norm_g=norm_g[l][None], b_gate=b_gate[l][None], pool_scale=pool_scale[l][None])
             for l in range(depth)]

    shards, blocks, placers = _gather_layer(w, 0, d=d)
    pick = lambda seq, which: [seq[i] for i in which]
    first = _all_gather(pick(shards, FIRST), pick(blocks, FIRST), pick(placers, FIRST), name="gather_first_l0")
    rest = _Behind(pick(shards, REST), pick(blocks, REST), [_shard_itself] * len(REST), pick(placers, REST),
                   (mod8, first[-1]), "gather_rest_l0")
    gws = [_as_weights(first, d, FIRST)]
    order = rest.token
    xs = x[0]
    saved = []
    for l in range(depth):
        coming = None
        if l + 1 < depth:
            shards, blocks, placers = _gather_layer(w, l + 1, d=d)
            coming = []
            for which, part in ((FIRST, "first"), (REST, "rest")):
                coming.append(_Behind(pick(shards, which), pick(blocks, which), [_shard_itself] * len(which),
                                      pick(placers, which), (mod8, gws[l]["conv_w"], order.reshape(1, 1)),
                                      f"gather_{part}_l{l + 1}"))
                order = order + coming[-1].token
        shift, scale, res_gate = mods[l]
        mods[l] = (shift + order, scale, res_gate)
        xs, sv, gws[l] = _layer_fwd(xs, mods[l], small[l], gws[l], d=d, tag=f"l{l}",
                                    more_weights=lambda o, rest=rest: _as_weights(rest.finish((o,)), d, REST))
        saved.append(sv)
        if coming is not None:
            gws.append(_as_weights(coming[0].finish((xs,)), d, FIRST))
            rest = coming[1]
    dx, loss8, d_final_g = _loss_head(xs, loss_target[0], final_g[None], tm=TM, name="loss_head")
    loss = lax.psum(loss8[0, 0], ("x", "y", "c"))

    parts, small_parts = [dict() for _ in range(depth)], [None] * depth
    going, last = [], []

    def send(queue, keys, l, name):
        def start(grads_so_far):
            grads, pickers, shapes = _scatter_spec(grads_so_far, keys, d=d)
            queue.append((l, keys, _Behind(grads, shapes, pickers, [_whole] * len(grads), (grads[0],), name)))
            return queue[-1][2].token
        return start

    for l in reversed(range(depth)):
        if going:
            shift, scale, res_gate = mods[l]
            mods[l] = (shift, scale, res_gate + going[-1][2].token)
        lowest = l == 0
        dx, big, small_parts[l] = _layer_bwd(
            dx, saved[l], mods[l], small[l], gws[l], d=d, tag=f"l{l}",
            send_early=send(going, EARLY, l, f"scatter_early_l{l}") if lowest else None,
            send_late=send(last, LATE, l, f"scatter_late_l{l}") if lowest else None)
        if not lowest:
            send(going, BIG, l, f"scatter_grads_l{l}")(big)
    for l, keys, behind in going:
        parts[l].update(zip(keys, behind.finish((dx,))))

    pieces = [small_parts[l][k] for l in range(depth) for k in SMALL_PER_LAYER] + [d_final_g]
    sizes = [p.size for p in pieces]
    gathered = _gather_small(_pack(pieces), name="gather_small_grads")
    zero_conv = jnp.zeros((3, cc), F32)

    def packed(src):
        per_layer = lambda l: [src["norm_g"][l], src["b_ada"][l], src["pool_scale"][l], src["b_gate"][l],
                               zero_conv]
        return _pack([p for l in range(depth) for p in per_layer(l)] + [src["final_g"]])[None]

    small_out = _adamw(packed(w), packed(m), packed(v), gathered, 0, None, tr=gathered.shape[1],
                       name="adamw_small")

    def unpack(flat):
        flat = flat.reshape(-1)
        out, off = [], 0
        for n in sizes:
            out.append(flat[off:off + n])
            off += n
        return out

    small_res = [unpack(a) for a in small_out]
    k_per = len(SMALL_PER_LAYER)

    def small_stack(which, pos, shape):
        return jnp.stack([small_res[which][l * k_per + pos] for l in range(depth)]).reshape(shape)

    res = {}
    for pos, name in ((0, "norm_g"), (1, "b_ada"), (2, "pool_scale"), (3, "b_gate")):
        res[name] = tuple(small_stack(k, pos, w[name].shape) for k in range(4))
    res["final_g"] = tuple(small_res[k][-1].reshape(final_g.shape) for k in range(4))

    ncol = cc // N_DEV
    conv_total = small_stack(0, 4, (depth * 3, cc))
    conv_mine = lax.dynamic_slice_in_dim(conv_total, my_idx * ncol, ncol, axis=1)
    pad8 = lambda a: jnp.pad(a.reshape(depth * 3, ncol), ((0, 8 - depth * 3), (0, 0)))[None]
    conv_out = _adamw(pad8(conv_w), pad8(m["conv_w"]), pad8(v["conv_w"]), pad8(conv_mine), 0, None, tr=8,
                      name="adamw_conv_w")
    res["conv_w"] = tuple(a[0, :depth * 3].reshape(conv_w.shape) for a in conv_out)

    n_ada = 3 * d // N_DEV
    mod_off = [sum(sizes[:l * k_per + 1]) // LANE for l in range(depth)]
    dmod = jnp.stack([
        lax.dynamic_slice_in_dim(gathered[:, mod_off[l]:mod_off[l] + 3 * d // LANE].reshape(N_DEV, 3 * d),
                                 my_idx * n_ada, n_ada, axis=1) for l in range(depth)])
    g_w_ada = _ada_bwd(sc_all[::8], dmod, name="ada_bwd")
    res["w_ada"] = _adamw_layers(w_ada, m["w_ada"], v["w_ada"], [g_w_ada[l][None] for l in range(depth)],
                                 tr=128, name="adamw_w_ada")

    def big_adamw(name):
        out = _adamw_layers(w[name], m[name], v[name], [parts[l][name] for l in range(depth)], tr=128,
                            name=f"adamw_{name}")
        res[name] = tuple(a.reshape(w[name].shape) for a in out)

    for name in EARLY:
        big_adamw(name)
    for l, keys, behind in last:
        parts[l].update(zip(keys, behind.finish(tuple(res[name][1] for name in ("w_ada",) + EARLY))))
    for name in LATE:
        big_adamw(name)

    outs = [loss, dx[None]]
    for k in range(4):
        outs += [res[name][k] for name in names]
    return tuple(outs)
```

```python
import functools

import jax
import jax.numpy as jnp
from jax import lax
from jax.experimental import pallas as pl
from jax.experimental.pallas import tpu as pltpu

F32 = jnp.float32
BF16 = jnp.bfloat16
MESH = pl.DeviceIdType.MESH

N_DEV = 8
DEPTH = 2
HEAD_DIM = 128
N_POOL_GROUPS = 4
POOL_WINDOWS = (2, 4, 8, 16)
RMS_EPS = 1e-6
LANE = 128
HALO = 16
KEY_CHUNK = 256
KEY_BLOCK = 512
HEADS_PER_STEP = 2
ADAM_LR = 0.001
ADAM_B1 = 0.9
ADAM_B2 = 0.999
ADAM_EPS = 1e-08
ADAM_WD = 0.01
ADAM_STEP = 10


def _cparams(semantics, vmem_mb):
    return pltpu.CompilerParams(dimension_semantics=semantics, vmem_limit_bytes=vmem_mb << 20)


def _sigmoid(z):
    return 1.0 / (1.0 + jnp.exp(-z))


def _mm_nn(a, w, *, out_dtype, tm, name, bias=None, out=None, out_cols=None, out_col_block=0):
    m, k = a.shape
    nb, _, n = w.shape
    tm = min(tm, m)
    total_cols =out.shape[1] if out is not None else (nb * n if out_cols is None else out_cols)

    def body(*refs):
        if out is not None:
            refs = refs[:-2] + refs[-1:]
        if bias is not None:
            a_ref, w_ref, b_ref, o_ref = refs
        else:
            a_ref, w_ref, o_ref = refs
        acc = jnp.dot(a_ref[...], w_ref[...], preferred_element_type=F32)
        if bias is not None:
            acc = acc + b_ref[...]
        o_ref[...] = acc.astype(out_dtype)

    in_specs = [pl.BlockSpec((tm, k), lambda j, i: (i, 0)),
                pl.BlockSpec((None, k, n), lambda j, i: (j, 0, 0))]
    args = [a, w]
    if bias is not None:
        in_specs.append(pl.BlockSpec((1, n), lambda j, i: (0, j)))
        args.append(bias)
    aliases = {}
    if out is not None:
        in_specs.append(pl.BlockSpec(memory_space=pl.ANY))
        aliases = {len(args): 0}
        args.append(out)
    return pl.pallas_call(
        body, grid=(nb, m // tm), in_specs=in_specs,
        out_specs=pl.BlockSpec((tm, n), lambda j, i: (i, out_col_block + j)),
        out_shape=jax.ShapeDtypeStruct((m, total_cols), out_dtype),
        input_output_aliases=aliases, name=name,
        compiler_params=_cparams(("arbitrary", "arbitrary"), 56),
    )(*args)


def _mm_nt(dy, w, *, out_dtype, tm, name, addend=None, col_block=0, after=None):
    m = dy.shape[0]
    nb, k, n = w.shape
    tm = min(tm, m)

    def body(*refs):
        dy_ref, w_ref = refs[:2]
        add_ref = refs[2] if addend is not None else None
        o_ref, acc_ref = refs[-2:]
        j = pl.program_id(1)
        part = lax.dot_general(dy_ref[...], w_ref[...], (((1,), (1,)), ((), ())),
                               preferred_element_type=F32)

        @pl.when(j == 0)
        def _():
            acc_ref[...] = part if addend is None else part + add_ref[...]

        @pl.when(j > 0)
        def _():
            acc_ref[...] += part

        @pl.when(j == nb - 1)
        def _():
            o_ref[...] = acc_ref[...].astype(out_dtype)

    in_specs = [pl.BlockSpec((tm, n), lambda i, j: (i, col_block + j)),
                pl.BlockSpec((None, k, n), lambda i, j: (j, 0, 0))]
    args = [dy, w]
    if addend is not None:
        in_specs.append(pl.BlockSpec((tm, k), lambda i, j: (i, 0)))
        args.append(addend)
    if after is not None:
        in_specs.append(pl.BlockSpec(memory_space=pl.ANY))
        args.append(after)
    return pl.pallas_call(
        body, grid=(m // tm, nb), in_specs=in_specs,
        out_specs=pl.BlockSpec((tm, k), lambda i, j: (i, 0)),
        out_shape=jax.ShapeDtypeStruct((m, k), out_dtype),
        scratch_shapes=[pltpu.VMEM((tm, k), F32)], name=name,
        compiler_params=_cparams(("arbitrary", "arbitrary"), 56),
    )(*args)


def _mm_tn(a, dy, *, nb, n, tk, tmm, name, col_block=0, out_dtype=BF16):
    m, k = a.shape
    tmm = min(tmm, m)

    def body(a_ref, dy_ref, o_ref, acc_ref):
        t = pl.program_id(2)
        part = lax.dot_general(a_ref[...], dy_ref[...], (((0,), (0,)), ((), ())),
                               preferred_element_type=F32)

        @pl.when(t == 0)
        def _():
            acc_ref[...] = part

        @pl.when(t > 0)
        def _():
            acc_ref[...] += part

        @pl.when(t == pl.num_programs(2) - 1)
        def _():
            o_ref[...] = acc_ref[...].astype(out_dtype)

    return pl.pallas_call(
        body, grid=(nb, k // tk, m // tmm),
        in_specs=[pl.BlockSpec((tmm, tk), lambda j, kb, t: (t, kb)),
                  pl.BlockSpec((tmm, n), lambda j, kb, t: (t, col_block + j))],
        out_specs=pl.BlockSpec((None, tk, n), lambda j, kb, t: (j, kb, 0)),
        out_shape=jax.ShapeDtypeStruct((nb, k, n), out_dtype),
        scratch_shapes=[pltpu.VMEM((tk, n), F32)], name=name,
        compiler_params=_cparams(("arbitrary", "arbitrary", "arbitrary"), 56),
    )(a, dy)


def _prenorm(x, g, scale, shift, *, tm, name):
    t, d = x.shape

    def body(x_ref, g_ref, sc_ref, sh_ref, h_ref):
        xv = x_ref[...]
        r = lax.rsqrt(jnp.mean(xv * xv, axis=-1, keepdims=True) + RMS_EPS)
        h_ref[...] = ((xv * r) * g_ref[...] * (1.0 + sc_ref[...]) + sh_ref[...]).astype(BF16)

    row = pl.BlockSpec((1, d), lambda i: (0, 0))
    tile = pl.BlockSpec((tm, d), lambda i: (i, 0))
    return pl.pallas_call(
        body, grid=(t // tm,), in_specs=[tile, row, row, row], out_specs=tile,
        out_shape=jax.ShapeDtypeStruct((t, d), BF16), name=name,
        compiler_params=_cparams(("arbitrary",), 40),
    )(x, g, scale, shift)


def _prenorm_bwd(dh, x, dout, g, scale, *, tm, name):
    t, d = x.shape

    def body(dh_ref, x_ref, do_ref, g_ref, sc_ref, dx_ref, dg_ref, dsc_ref, dsh_ref):
        i = pl.program_id(0)
        xv = x_ref[...]
        dhv = dh_ref[...]
        r = lax.rsqrt(jnp.mean(xv * xv, axis=-1, keepdims=True) + RMS_EPS)
        xn = xv * r
        gain = g_ref[...] * (1.0 + sc_ref[...])
        dxn = dhv * gain
        dx_ref[...] = do_ref[...] + r * (dxn - xn * jnp.mean(dxn * xn, axis=-1, keepdims=True))
        dhxn = dhv * xn
        dg = jnp.sum(dhxn * (1.0 + sc_ref[...]), axis=0, keepdims=True)
        dsc = jnp.sum(dhxn * g_ref[...], axis=0, keepdims=True)
        dsh = jnp.sum(dhv, axis=0, keepdims=True)

        @pl.when(i == 0)
        def _():
            dg_ref[...] = dg
            dsc_ref[...] = dsc
            dsh_ref[...] = dsh

        @pl.when(i > 0)
        def _():
            dg_ref[...] += dg
            dsc_ref[...] += dsc
            dsh_ref[...] += dsh

    row = pl.BlockSpec((1, d), lambda i: (0, 0))
    tile = pl.BlockSpec((tm, d), lambda i: (i, 0))
    vec = jax.ShapeDtypeStruct((1, d), F32)
    return pl.pallas_call(
        body, grid=(t // tm,), in_specs=[tile, tile, tile, row, row],
        out_specs=[tile, row, row, row],
        out_shape=[jax.ShapeDtypeStruct((t, d), F32), vec, vec, vec], name=name,
        compiler_params=_cparams(("arbitrary",), 48),
    )(dh, x, dout, g, scale)


def _out_fwd(merged, w_out, x, res_gate, *, tm, name):
    t, d = x.shape
    k = merged.shape[1]

    def body(m_ref, w_ref, x_ref, rg_ref, xo_ref, mo_ref):
        mo = jnp.dot(m_ref[...], w_ref[...], preferred_element_type=F32)
        mo_ref[...] = mo.astype(BF16)
        xo_ref[...] = x_ref[...] + rg_ref[...] * mo

    tile = pl.BlockSpec((tm, d), lambda i: (i, 0))
    return pl.pallas_call(
        body, grid=(t // tm,),
        in_specs=[pl.BlockSpec((tm, k), lambda i: (i, 0)), pl.BlockSpec((k, d), lambda i: (0, 0)),
                  tile, pl.BlockSpec((1, d), lambda i: (0, 0))],
        out_specs=[tile, tile],
        out_shape=[jax.ShapeDtypeStruct((t, d), F32), jax.ShapeDtypeStruct((t, d), BF16)], name=name,
        compiler_params=_cparams(("arbitrary",), 56),
    )(merged, w_out, x, res_gate)


def _out_bwd(dout, mo, res_gate, *, tm, name):
    t, d = dout.shape

    def body(do_ref, mo_ref, rg_ref, dmo_ref, drg_ref):
        i = pl.program_id(0)
        dov = do_ref[...]
        dmo_ref[...] = (dov * rg_ref[...]).astype(BF16)
        drg = jnp.sum(dov * mo_ref[...].astype(F32), axis=0, keepdims=True)

        @pl.when(i == 0)
        def _():
            drg_ref[...] = drg

        @pl.when(i > 0)
        def _():
            drg_ref[...] += drg

    row = pl.BlockSpec((1, d), lambda i: (0, 0))
    tile = pl.BlockSpec((tm, d), lambda i: (i, 0))
    return pl.pallas_call(
        body, grid=(t // tm,), in_specs=[tile, tile, row], out_specs=[tile, row],
        out_shape=[jax.ShapeDtypeStruct((t, d), BF16), jax.ShapeDtypeStruct((1, d), F32)], name=name,
        compiler_params=_cparams(("arbitrary",), 40),
    )(dout, mo, res_gate)


def _loss_head(x, target, final_g, *, tm, name):
    t, d = x.shape

    def body(x_ref, t_ref, g_ref, dx_ref, loss_ref, dg_ref):
        i = pl.program_id(0)
        xv = x_ref[...]
        r = lax.rsqrt(jnp.mean(xv * xv, axis=-1, keepdims=True) + RMS_EPS)
        xn = xv * r
        err = xn * g_ref[...] - t_ref[...]
        part = (0.5 / d) * jnp.sum(jnp.sum(err * err, axis=1, keepdims=True), axis=0, keepdims=True)
        dy = err * (1.0 / d)
        dxn = dy * g_ref[...]
        dx_ref[...] = r * (dxn - xn * jnp.mean(dxn * xn, axis=-1, keepdims=True))
        dg = jnp.sum(dy * xn, axis=0, keepdims=True)

        @pl.when(i == 0)
        def _():
            loss_ref[...] = jnp.zeros_like(loss_ref) + part
            dg_ref[...] = dg

        @pl.when(i > 0)
        def _():
            loss_ref[...] += part
            dg_ref[...] += dg

    row = pl.BlockSpec((1, d), lambda i: (0, 0))
    tile = pl.BlockSpec((tm, d), lambda i: (i, 0))
    return pl.pallas_call(
        body, grid=(t // tm,), in_specs=[tile, tile, row],
        out_specs=[tile, pl.BlockSpec((8, LANE), lambda i: (0, 0)), row],
        out_shape=[jax.ShapeDtypeStruct((t, d), F32), jax.ShapeDtypeStruct((8, LANE), F32),
                   jax.ShapeDtypeStruct((1, d), F32)], name=name,
        compiler_params=_cparams(("arbitrary",), 40),
    )(x, target, final_g)


def _cur(tm, cw, col):
    return pl.BlockSpec((tm, cw), lambda cb, i: (i, col + cb))


def _prev(tm, cw, col):
    return pl.BlockSpec((HALO, cw), lambda cb, i: (jnp.maximum(i * (tm // HALO) - 1, 0), col + cb))


def _next(tm, cw, col, t):
    last = t // HALO - 1
    return pl.BlockSpec((HALO, cw), lambda cb, i: (jnp.minimum((i + 1) * (tm // HALO), last), col + cb))


def _with_prev(prev_ref, cur, first):
    prev = jnp.where(first, 0.0, prev_ref[...].astype(F32))
    return jnp.concatenate([prev, cur], axis=0)


def _with_next(cur, next_ref, last):
    nxt = jnp.where(last, 0.0, next_ref[...].astype(F32))
    return jnp.concatenate([cur, nxt], axis=0)


def _pool_mixed(xe, cur, g, row0, tm):
    s2 = xe + pltpu.roll(xe, 1, 0)
    s4 = s2 + pltpu.roll(s2, 2, 0)
    s8 = s4 + pltpu.roll(s4, 4, 0)
    s16 = s8 + pltpu.roll(s8, 8, 0)
    win = jnp.where(g == 0, s2, jnp.where(g == 1, s4, jnp.where(g == 2, s8, s16)))[HALO:]
    return win * _pool_inv_count(g, row0, tm, cur.shape[1]) - cur


def _pool_inv_count(g, row0, rows, cols):
    tpos = row0 + lax.broadcasted_iota(jnp.int32, (rows, cols), 0)
    width = jnp.left_shift(2, g)
    return 1.0 / jnp.minimum(tpos + 1, width).astype(F32)


def _pool_fwd(proj, pool_w, pool_scale, *, d, tm, name):
    t = proj.shape[0]
    pw = d // 2
    gd = pw // N_POOL_GROUPS
    za_col = pw // gd

    def body(xa_ref, xp_ref, za_ref, w_ref, ps_ref, ya_ref):
        g = pl.program_id(0)
        i = pl.program_id(1)
        cur = xa_ref[...].astype(F32)
        mixed = _pool_mixed(_with_prev(xp_ref, cur, i == 0), cur, g, i * tm, tm)
        ypre = jnp.dot(mixed.astype(BF16), w_ref[...], preferred_element_type=F32)
        za = za_ref[...].astype(F32)
        ya_ref[...] = (ypre * ps_ref[...] * (za * _sigmoid(za))).astype(BF16)

    return pl.pallas_call(
        body, grid=(N_POOL_GROUPS, t // tm),
        in_specs=[_cur(tm, gd, 0), _prev(tm, gd, 0), _cur(tm, gd, za_col),
                  pl.BlockSpec((None, gd, gd), lambda g, i: (g, 0, 0)),
                  pl.BlockSpec((1, gd), lambda g, i: (0, g))],
        out_specs=pl.BlockSpec((tm, gd), lambda g, i: (i, g)),
        out_shape=jax.ShapeDtypeStruct((t, pw), BF16), name=name,
        compiler_params=_cparams(("arbitrary", "arbitrary"), 40),
    )(proj, proj, proj, pool_w, pool_scale)


def _pool_bwd(dya, proj, pool_w, pool_scale, *, d, tm, name):
    t = proj.shape[0]
    pw = d // 2
    gd = pw // N_POOL_GROUPS
    za_col = pw // gd
    n_i = t // tm

    def body(xa_ref, xp_ref, za_ref, zn_ref, dy_ref, dyn_ref, w_ref, ps_ref,
             dxa_ref, dza_ref, dw_ref, dps_ref):
        g = pl.program_id(0)
        i = pl.program_id(1)
        last = i == n_i - 1
        cur = xa_ref[...].astype(F32)
        mixed = _pool_mixed(_with_prev(xp_ref, cur, i == 0), cur, g, i * tm, tm)
        mixed16 = mixed.astype(BF16)
        ypre = jnp.dot(mixed16, w_ref[...], preferred_element_type=F32)
        za = za_ref[...].astype(F32)
        sg = _sigmoid(za)
        dy = dy_ref[...].astype(F32)
        dza_ref[...] = (dy * (ypre * ps_ref[...]) * (sg * (1.0 + za * (1.0 - sg)))).astype(BF16)
        dysc = dy * (za * sg)
        za_e = _with_next(za, zn_ref, last)
        dy_e = _with_next(dy, dyn_ref, last)
        dypre_e = (dy_e * (za_e * _sigmoid(za_e)) * ps_ref[...]).astype(BF16)
        dm_e = lax.dot_general(dypre_e, w_ref[...], (((1,), (1,)), ((), ())), preferred_element_type=F32)
        rows = tm + HALO
        q = dm_e * _pool_inv_count(g, i * tm, rows, gd)
        f2 = q + pltpu.roll(q, rows - 1, 0)
        f4 = f2 + pltpu.roll(f2, rows - 2, 0)
        f8 = f4 + pltpu.roll(f4, rows - 4, 0)
        f16 = f8 + pltpu.roll(f8, rows - 8, 0)
        ahead = jnp.where(g == 0, f2, jnp.where(g == 1, f4, jnp.where(g == 2, f8, f16)))
        dxa_ref[...] = (ahead[:tm] - dm_e[:tm]).astype(BF16)
        dw = lax.dot_general(mixed16, dypre_e[:tm], (((0,), (0,)), ((), ())), preferred_element_type=F32)
        dps = jnp.sum(dysc * ypre, axis=0, keepdims=True)

        @pl.when(i == 0)
        def _():
            dw_ref[...] = dw
            dps_ref[...] = dps

        @pl.when(i > 0)
        def _():
            dw_ref[...] += dw
            dps_ref[...] += dps

    out_tile = pl.BlockSpec((tm, gd), lambda g, i: (i, g))
    return pl.pallas_call(
        body, grid=(N_POOL_GROUPS, n_i),
        in_specs=[_cur(tm, gd, 0), _prev(tm, gd, 0), _cur(tm, gd, za_col), _next(tm, gd, za_col, t),
                  _cur(tm, gd, 0), _next(tm, gd, 0, t),
                  pl.BlockSpec((None, gd, gd), lambda g, i: (g, 0, 0)),
                  pl.BlockSpec((1, gd), lambda g, i: (0, g))],
        out_specs=[out_tile, out_tile, pl.BlockSpec((None, gd, gd), lambda g, i: (g, 0, 0)),
                   pl.BlockSpec((1, gd), lambda g, i: (0, g))],
        out_shape=[jax.ShapeDtypeStruct((t, pw), BF16), jax.ShapeDtypeStruct((t, pw), BF16),
                   jax.ShapeDtypeStruct((N_POOL_GROUPS, gd, gd), F32), jax.ShapeDtypeStruct((1, pw), F32)],
        name=name, compiler_params=_cparams(("arbitrary", "arbitrary"), 48),
    )(proj, proj, proj, proj, dya, dya, pool_w, pool_scale)


def _conv_cols(d, cw):
    cc = d // 2
    base = (d + 4 * d) // cw
    return base, base + cc // cw, base + 2 * (cc // cw), base + 3 * (cc // cw)


def _conv_fwd(proj, conv_w, *, d, tm, cw, name):
    t = proj.shape[0]
    cc = d // 2
    u_col, b_col, c_col, z_col = _conv_cols(d, cw)

    def body(u_ref, up_ref, c_ref, cp_ref, b_ref, z_ref, w_ref, y_ref):
        i = pl.program_id(1)
        v = u_ref[...].astype(F32) * c_ref[...].astype(F32)
        vprev = up_ref[...].astype(F32) * cp_ref[...].astype(F32)
        ve = jnp.concatenate([jnp.where(i == 0, 0.0, vprev), v], axis=0)
        w = w_ref[...]
        y = w[0:1] * pltpu.roll(ve, 2, 0)[HALO:] + w[1:2] * pltpu.roll(ve, 1, 0)[HALO:] + w[2:3] * v
        z = z_ref[...].astype(F32)
        y_ref[...] = (b_ref[...].astype(F32) * y * (z * _sigmoid(z))).astype(BF16)

    return pl.pallas_call(
        body, grid=(cc // cw, t // tm),
        in_specs=[_cur(tm, cw, u_col), _prev(tm, cw, u_col), _cur(tm, cw, c_col), _prev(tm, cw, c_col),
                  _cur(tm, cw, b_col), _cur(tm, cw, z_col),
                  pl.BlockSpec((8, cw), lambda cb, i: (0, cb))],
        out_specs=pl.BlockSpec((tm, cw), lambda cb, i: (i, cb)),
        out_shape=jax.ShapeDtypeStruct((t, cc), BF16), name=name,
        compiler_params=_cparams(("arbitrary", "arbitrary"), 40),
    )(proj, proj, proj, proj, proj, proj, conv_w)


def _conv_bwd(dyc, proj, conv_w, *, d, tm, cw, name):
    t = proj.shape[0]
    cc = d // 2
    n_i = t // tm
    u_col, b_col, c_col, z_col = _conv_cols(d, cw)

    def body(u_ref, up_ref, c_ref, cp_ref, b_ref, bn_ref, z_ref, zn_ref, dy_ref, dyn_ref, w_ref,
             du_ref, db_ref, dc_ref, dz_ref, dw_ref):
        i = pl.program_id(1)
        last = i == n_i - 1
        u = u_ref[...].astype(F32)
        cg = c_ref[...].astype(F32)
        v = u * cg
        vprev = up_ref[...].astype(F32) * cp_ref[...].astype(F32)
        ve = jnp.concatenate([jnp.where(i == 0, 0.0, vprev), v], axis=0)
        v2 = pltpu.roll(ve, 2, 0)[HALO:]
        v1 = pltpu.roll(ve, 1, 0)[HALO:]
        w = w_ref[...]
        y = w[0:1] * v2 + w[1:2] * v1 + w[2:3] * v
        z = z_ref[...].astype(F32)
        sg = _sigmoid(z)
        bg = b_ref[...].astype(F32)
        dyc_v = dy_ref[...].astype(F32)
        dz_ref[...] = (dyc_v * bg * y * (sg * (1.0 + z * (1.0 - sg)))).astype(BF16)
        db_ref[...] = (dyc_v * y * (z * sg)).astype(BF16)
        ze = _with_next(z, zn_ref, last)
        dy_e = (_with_next(dyc_v, dyn_ref, last) * _with_next(bg, bn_ref, last) * (ze * _sigmoid(ze)))
        rows = tm + HALO
        dy0 = dy_e[:tm]
        dv = (w[2:3] * dy0 + w[1:2] * pltpu.roll(dy_e, rows - 1, 0)[:tm]
              + w[0:1] * pltpu.roll(dy_e, rows - 2, 0)[:tm])
        du_ref[...] = (dv * cg).astype(BF16)
        dc_ref[...] = (dv * u).astype(BF16)
        s0 = jnp.sum(dy0 * v2, axis=0, keepdims=True)
        s1 = jnp.sum(dy0 * v1, axis=0, keepdims=True)
        s2 = jnp.sum(dy0 * v, axis=0, keepdims=True)
        r = lax.broadcasted_iota(jnp.int32, (8, cw), 0)
        dw = jnp.where(r == 0, s0, jnp.where(r == 1, s1, jnp.where(r == 2, s2, 0.0)))

        @pl.when(i == 0)
        def _():
            dw_ref[...] = dw

        @pl.when(i > 0)
        def _():
            dw_ref[...] += dw

    out_tile = pl.BlockSpec((tm, cw), lambda cb, i: (i, cb))
    act = jax.ShapeDtypeStruct((t, cc), BF16)
    return pl.pallas_call(
        body, grid=(cc // cw, n_i),
        in_specs=[_cur(tm, cw, u_col), _prev(tm, cw, u_col), _cur(tm, cw, c_col), _prev(tm, cw, c_col),
                  _cur(tm, cw, b_col), _next(tm, cw, b_col, t), _cur(tm, cw, z_col), _next(tm, cw, z_col, t),
                  _cur(tm, cw, 0), _next(tm, cw, 0, t),
                  pl.BlockSpec((8, cw), lambda cb, i: (0, cb))],
        out_specs=[out_tile, out_tile, out_tile, out_tile, pl.BlockSpec((8, cw), lambda cb, i: (0, cb))],
        out_shape=[act, act, act, act, jax.ShapeDtypeStruct((8, cc), F32)], name=name,
        compiler_params=_cparams(("arbitrary", "arbitrary"), 48),
    )(proj, proj, proj, proj, proj, proj, proj, proj, dyc, dyc, conv_w)


def _sum_matrix(prefix):
    r = lax.broadcasted_iota(jnp.int32, (KEY_CHUNK, KEY_CHUNK), 0)
    c = lax.broadcasted_iota(jnp.int32, (KEY_CHUNK, KEY_CHUNK), 1)
    return jnp.where((r <= c) if prefix else (r > c), 1.0, 0.0).astype(BF16)


def _chunk_sums(val, mat, prefix):
    hi = val.astype(BF16)
    part = jnp.dot(hi, mat, preferred_element_type=F32)
    if prefix:
        return part, part[:, KEY_CHUNK - 1:]
    return part, part[:, :1] + hi[:, :1].astype(F32)


def _stick_logits(qs, kb, strict):
    z = lax.dot_general(qs, kb, (((1,), (1,)), ((), ())), preferred_element_type=F32)
    minus_abs = pltpu.bitcast(pltpu.bitcast(z, jnp.int32) | jnp.int32(-2 ** 31), F32)
    sp = jnp.maximum(z, 0.0) + jnp.log(1.0 + jnp.exp(minus_abs))
    log_beta = z - sp
    if strict is not None:
        sp = jnp.where(strict, sp, 0.0)
    return sp, log_beta


def _running(val, mat, carry, upwards):
    n = val.shape[1] // KEY_CHUNK
    out = [None] * n
    for c in (range(n) if upwards else reversed(range(n))):
        part, total = _chunk_sums(val[:, c * KEY_CHUNK:(c + 1) * KEY_CHUNK], mat, upwards)
        out[c] = part + carry
        carry = carry + total
    return jnp.concatenate(out, axis=1), carry


def _attn_cols(d):
    h = d // HEAD_DIM
    return h, 2 * h, 3 * h, 4 * h


def _sweep(i, per, block, carry, upwards):
    n = i * per
    odd = n % 2 == 1
    same = lambda cr: cr
    if upwards:
        carry = lax.fori_loop(0, n // 2, lambda it, cr: block(2 * it + 1, block(2 * it, cr, False), False), carry)
        carry = lax.cond(odd, lambda cr: block(n - 1, cr, False), same, carry)
        for dgl in range(per):
            carry = block(n + dgl, carry, True)
        return carry
    for dgl in range(per - 1, -1, -1):
        carry = block(n + dgl, carry, True)
    carry = lax.fori_loop(0, n // 2,
                          lambda it, cr: block(n - 2 - 2 * it, block(n - 1 - 2 * it, cr, False), False), carry)
    return lax.cond(odd, lambda cr: block(0, cr, False), same, carry)


def _diagonal_halves(tq):
    assert tq == KEY_BLOCK == 2 * KEY_CHUNK
    return (slice(0, tq // 2), tq // 2), (slice(tq // 2, tq), tq)


def _diagonal_mask(rows, width):
    n = rows.stop - rows.start
    t_idx = rows.start + lax.broadcasted_iota(jnp.int32, (n, width), 0)
    return lax.broadcasted_iota(jnp.int32, (n, width), 1) < t_idx


def _attn_fwd(proj, *, d, tq, name):
    t = proj.shape[0]
    groups = d // HEAD_DIM // HEADS_PER_STEP
    width = HEADS_PER_STEP * HEAD_DIM
    lanes = [slice(s * HEAD_DIM, (s + 1) * HEAD_DIM) for s in range(HEADS_PER_STEP)]
    q_col, k_col, v_col, z_col = (c // HEADS_PER_STEP for c in _attn_cols(d))

    def body(q_ref, k_ref, v_ref, z_ref, o_ref, y_ref, tot_ref):
        i = pl.program_id(1)
        qs = [(q_ref[:, ln].astype(F32) * (HEAD_DIM ** -0.5)).astype(BF16) for ln in lanes]
        mat = _sum_matrix(prefix=False)

        def block(kb, carry, masked):
            off = pl.multiple_of(kb * KEY_BLOCK, KEY_BLOCK)
            out = []
            for s, ln in enumerate(lanes):
                pieces = []
                for rows, width in (_diagonal_halves(tq) if masked else ((slice(0, tq), KEY_BLOCK),)):
                    acc, after = (c[rows] for c in carry[s])
                    strict = _diagonal_mask(rows, width) if masked else None
                    sp, log_beta = _stick_logits(qs[s][rows], k_ref[pl.ds(off, width), ln], strict)
                    between, after = _running(sp, mat, after, upwards=False)
                    a = jnp.exp(log_beta - between)
                    if masked:
                        a = jnp.where(strict, a, 0.0)
                    acc = acc + jnp.dot(a.astype(BF16), v_ref[pl.ds(off, width), ln], preferred_element_type=F32)
                    pieces.append((acc, after))
                out.append(tuple(jnp.concatenate(c, axis=0) for c in zip(*pieces)))
            return tuple(out)

        start = (jnp.zeros((tq, HEAD_DIM), F32), jnp.zeros((tq, 1), F32))
        done = _sweep(i, tq // KEY_BLOCK, block, (start,) * len(lanes), upwards=False)
        for (o, total), ln in zip(done, lanes):
            o_ref[:, ln] = o.astype(BF16)
            tot_ref[:, ln] = jnp.broadcast_to(total, (tq, HEAD_DIM))
            z = z_ref[:, ln].astype(F32)
            y_ref[:, ln] = (o * (z * _sigmoid(z))).astype(BF16)

    tile = lambda col: pl.BlockSpec((tq, width), lambda h, i: (i, col + h))
    full = lambda col: pl.BlockSpec((t, width), lambda h, i: (0, col + h))
    act = jax.ShapeDtypeStruct((t, d), BF16)
    return pl.pallas_call(
        body, grid=(groups, t // tq),
        in_specs=[tile(q_col), full(k_col), full(v_col), tile(z_col)],
        out_specs=[tile(0), tile(0), tile(0)], out_shape=[act, act, jax.ShapeDtypeStruct((t, d), F32)],
        name=name, compiler_params=_cparams(("arbitrary", "arbitrary"), 48),
    )(proj, proj, proj, proj)


def _attn_bwd(dyb, o, tot, proj, *, d, tq, name, after=None):
    t = proj.shape[0]
    groups = d // HEAD_DIM // HEADS_PER_STEP
    width = HEADS_PER_STEP * HEAD_DIM
    lanes = [slice(s * HEAD_DIM, (s + 1) * HEAD_DIM) for s in range(HEADS_PER_STEP)]
    n_i = t // tq
    q_col, k_col, v_col, z_col = (c // HEADS_PER_STEP for c in _attn_cols(d))
    scale = HEAD_DIM ** -0.5

    def body(*refs):
        q_ref, k_ref, v_ref, z_ref, o_ref, tot_ref, dy_ref = refs[:7]
        dq_ref, dk_ref, dv_ref, dz_ref, dk_acc, dv_acc = refs[-6:]
        i = pl.program_id(1)

        @pl.when(i == 0)
        def _():
            dk_acc[...] = jnp.zeros_like(dk_acc)
            dv_acc[...] = jnp.zeros_like(dv_acc)

        qs, do16, all_keep = [], [], []
        for ln in lanes:
            qs.append((q_ref[:, ln].astype(F32) * scale).astype(BF16))
            z = z_ref[:, ln].astype(F32)
            sg = _sigmoid(z)
            dy = dy_ref[:, ln].astype(F32)
            dz_ref[:, ln] = (dy * o_ref[:, ln].astype(F32) * (sg * (1.0 + z * (1.0 - sg)))).astype(BF16)
            do16.append((dy * (z * sg)).astype(BF16))
            all_keep.append(jnp.concatenate([tot_ref[:, ln]] * (KEY_BLOCK // HEAD_DIM), axis=1))
        mat = _sum_matrix(prefix=True)

        def block(kb, carry, masked):
            off = pl.multiple_of(kb * KEY_BLOCK, KEY_BLOCK)
            out = []
            for s, ln in enumerate(lanes):
                pieces = []
                for rows, width in (_diagonal_halves(tq) if masked else ((slice(0, tq), KEY_BLOCK),)):
                    dq, sp_seen, g_seen = (c[rows] for c in carry[s])
                    kblk = k_ref[pl.ds(off, width), ln]
                    vblk = v_ref[pl.ds(off, width), ln]
                    strict = _diagonal_mask(rows, width) if masked else None
                    sp, log_beta = _stick_logits(qs[s][rows], kblk, strict)
                    sp_upto, sp_seen = _running(sp, mat, sp_seen, upwards=True)
                    a = jnp.exp(log_beta - (all_keep[s][rows, :width] - sp_upto))
                    if masked:
                        a = jnp.where(strict, a, 0.0)
                    g = a * lax.dot_general(do16[s][rows], vblk, (((1,), (1,)), ((), ())),
                                            preferred_element_type=F32)
                    g_upto, g_seen = _running(g, mat, g_seen, upwards=True)
                    dz_ts = g - jnp.exp(log_beta) * g_upto
                    if masked:
                        dz_ts = jnp.where(strict, dz_ts, 0.0)
                    dz16 = dz_ts.astype(BF16)
                    dq = dq + jnp.dot(dz16, kblk, preferred_element_type=F32)
                    dk_acc[pl.ds(off, width), ln] += lax.dot_general(
                        dz16, qs[s][rows], (((0,), (0,)), ((), ())), preferred_element_type=F32)
                    dv_acc[pl.ds(off, width), ln] += lax.dot_general(
                        a.astype(BF16), do16[s][rows], (((0,), (0,)), ((), ())), preferred_element_type=F32)
                    pieces.append((dq, sp_seen, g_seen))
                out.append(tuple(jnp.concatenate(c, axis=0) for c in zip(*pieces)))
            return tuple(out)

        column = jnp.zeros((tq, 1), F32)
        start = (jnp.zeros((tq, HEAD_DIM), F32), column, column)
        done = _sweep(i, tq // KEY_BLOCK, block, (start,) * len(lanes), upwards=True)
        for (dq, _, _), ln in zip(done, lanes):
            dq_ref[:, ln] = (dq * scale).astype(BF16)

        @pl.when(i == n_i - 1)
        def _():
            dk_ref[...] = dk_acc[...].astype(BF16)
            dv_ref[...] = dv_acc[...].astype(BF16)

    tile = lambda col: pl.BlockSpec((tq, width), lambda h, i: (i, col + h))
    full = lambda col: pl.BlockSpec((t, width), lambda h, i: (0, col + h))
    act = jax.ShapeDtypeStruct((t, d), BF16)
    in_specs = [tile(q_col), full(k_col), full(v_col), tile(z_col), tile(0), tile(0), tile(0)]
    args = [proj, proj, proj, proj, o, tot, dyb]
    if after is not None:
        in_specs.append(pl.BlockSpec(memory_space=pl.ANY))
        args.append(after)
    return pl.pallas_call(
        body, grid=(groups, n_i), in_specs=in_specs,
        out_specs=[tile(0), full(0), full(0), tile(0)], out_shape=[act, act, act, act],
        scratch_shapes=[pltpu.VMEM((t, width), F32), pltpu.VMEM((t, width), F32)], name=name,
        compiler_params=_cparams(("arbitrary", "arbitrary"), 56),
    )(*args)


def _merge_fwd(gl, bst, *, d, tm, tn, name):
    t = gl.shape[0]
    nk = d // tn

    def body(g0, g1, g2, b0, b1, b2, m_ref):
        acc = _sigmoid(g0[...].astype(F32)) * b0[...].astype(F32)
        acc += _sigmoid(g1[...].astype(F32)) * b1[...].astype(F32)
        acc += _sigmoid(g2[...].astype(F32)) * b2[...].astype(F32)
        m_ref[...] = acc.astype(BF16)

    spec = lambda b: pl.BlockSpec((tm, tn), lambda i, kb: (i, b * nk + kb))
    return pl.pallas_call(
        body, grid=(t // tm, nk), in_specs=[spec(0), spec(1), spec(2)] * 2,
        out_specs=pl.BlockSpec((tm, tn), lambda i, kb: (i, kb)),
        out_shape=jax.ShapeDtypeStruct((t, d), BF16), name=name,
        compiler_params=_cparams(("arbitrary", "arbitrary"), 40),
    )(gl, gl, gl, bst, bst, bst)


def _merge_bwd(dmerged, gl, bst, *, d, tm, tn, name):
    t = gl.shape[0]
    nk = d // tn

    def body(dm_ref, g_ref, b_ref, db_ref, dg_ref, dbias_ref):
        i = pl.program_id(2)
        dm = dm_ref[...].astype(F32)
        sg = _sigmoid(g_ref[...].astype(F32))
        db_ref[...] = (dm * sg).astype(BF16)
        dgl = dm * b_ref[...].astype(F32) * (sg * (1.0 - sg))
        dg_ref[...] = dgl.astype(BF16)
        dbias = jnp.sum(dgl, axis=0, keepdims=True)

        @pl.when(i == 0)
        def _():
            dbias_ref[...] = dbias

        @pl.when(i > 0)
        def _():
            dbias_ref[...] += dbias

    wide = pl.BlockSpec((tm, tn), lambda b, kb, i: (i, b * nk + kb))
    act = jax.ShapeDtypeStruct((t, 3 * d), BF16)
    return pl.pallas_call(
        body, grid=(3, nk, t // tm),
        in_specs=[pl.BlockSpec((tm, tn), lambda b, kb, i: (i, kb)), wide, wide],
        out_specs=[wide, wide, pl.BlockSpec((1, tn), lambda b, kb, i: (0, b * nk + kb))],
        out_shape=[act, act, jax.ShapeDtypeStruct((1, 3 * d), F32)], name=name,
        compiler_params=_cparams(("arbitrary", "arbitrary", "arbitrary"), 40),
    )(dmerged, gl, bst)


def _adamw(w, m, v, parts, layer, prev, *, tr, name):
    depth, r, c = w.shape
    n = parts.shape[0]

    def body(*refs):
        w_ref, m_ref, v_ref, p_ref = refs[:4]
        g_ref, d_ref, nm_ref, nv_ref = refs[-4:]
        g = p_ref[0].astype(F32)
        for s in range(1, n):
            g = g + p_ref[s].astype(F32)
        wv = w_ref[...]
        nm = ADAM_B1 * m_ref[...] + (1.0 - ADAM_B1) * g
        nv = ADAM_B2 * v_ref[...] + (1.0 - ADAM_B2) * (g * g)
        m_hat = nm / (1.0 - ADAM_B1 ** ADAM_STEP)
        v_hat = nv / (1.0 - ADAM_B2 ** ADAM_STEP)
        g_ref[...] = g
        nm_ref[...] = nm
        nv_ref[...] = nv
        d_ref[...] = -ADAM_LR * (m_hat / (jnp.sqrt(v_hat) + ADAM_EPS) + ADAM_WD * wv)

    tile = pl.BlockSpec((None, tr, c), lambda i: (layer, i, 0))
    in_specs = [tile, tile, tile, pl.BlockSpec((n, tr, c), lambda i: (0, i, 0))]
    args = [w, m, v, parts]
    aliases = {}
    if prev is not None:
        in_specs += [pl.BlockSpec(memory_space=pl.ANY)] * 4
        aliases = {4 + k: k for k in range(4)}
        args += list(prev)
    full = jax.ShapeDtypeStruct((depth, r, c), F32)
    return pl.pallas_call(
        body, grid=(r // tr,), in_specs=in_specs, out_specs=[tile] * 4, out_shape=[full] * 4,
        input_output_aliases=aliases, name=name, compiler_params=_cparams(("arbitrary",), 48),
    )(*args)


def _adamw_layers(w, m, v, parts_by_layer, *, tr, name):
    lead = w.shape[0]
    flat = lambda a: a.reshape(lead, -1, a.shape[-1])
    w, m, v = flat(w), flat(m), flat(v)
    out = None
    for layer in reversed(range(lead)):
        parts = parts_by_layer[layer]
        parts = parts.reshape(parts.shape[0], -1, parts.shape[-1])
        out = _adamw(w, m, v, parts, layer, out, tr=min(tr, w.shape[1]), name=f"{name}_l{layer}")
    return out


def _place():
    return lax.axis_index("x"), lax.axis_index("y"), lax.axis_index("c")


def _index(px, py, pc):
    return 4 * px + 2 * py + pc


def _whole(ref, idx):
    return ref.at[idx]


def _row_block(rows):
    return lambda ref, idx: ref.at[:, pl.ds(idx * rows, rows), :]


def _lane_block(cols):
    return lambda ref, idx: ref.at[:, pl.ds(idx * cols, cols)]


def _all_gather(shards, out_shapes, placers, *, name):
    n = len(shards)

    def body(*refs):
        src, out = refs[:n], refs[n:2 * n]
        send_sems, recv_sems, local_sems = refs[2 * n:]
        x, y, c = _place()
        me, sibling = (x, y, c), (x, y, 1 - c)
        chips = [(1 - x, y), (x, 1 - y), (1 - x, 1 - y)]

        def copy(a, k, block, to, from_shard=False):
            window = placers[a](out[a], _index(*block))
            return pltpu.make_async_remote_copy(
                src_ref=src[a] if from_shard else window, dst_ref=window,
                send_sem=send_sems.at[a * 7 + k], recv_sem=recv_sems.at[a * 7 + k],
                device_id=to, device_id_type=MESH)

        mine = [pltpu.make_async_copy(src[a], placers[a](out[a], _index(*me)), local_sems.at[a])
                for a in range(n)]
        started = []
        for a in range(n):
            mine[a].start()
            started.append(copy(a, 0, me, sibling, True))
            started += [copy(a, 1 + j, me, (*chip, c), True) for j, chip in enumerate(chips)]
        for cp in started:
            cp.start()
        for j, chip in enumerate(chips):
            for a in range(n):
                copy(a, 1 + j, (*chip, c), me).wait_recv()
                passed = copy(a, 4 + j, (*chip, c), sibling)
                passed.start()
                started.append(passed)
        for a in range(n):
            copy(a, 0, sibling, me).wait_recv()
            for j, chip in enumerate(chips):
                copy(a, 4 + j, (*chip, 1 - c), me).wait_recv()
        for cp in started:
            cp.wait_send()
        for cp in mine:
            cp.wait()

    hbm = pl.BlockSpec(memory_space=pl.ANY)
    return pl.pallas_call(
        body, in_specs=[hbm] * n, out_specs=[hbm] * n, out_shape=out_shapes,
        scratch_shapes=[pltpu.SemaphoreType.DMA((7 * n,)), pltpu.SemaphoreType.DMA((7 * n,)),
                        pltpu.SemaphoreType.DMA((n,))],
        name=name,
    )(*shards)


def _scatter_parts(grads, pickers, part_shapes, *, name):
    n = len(grads)

    def body(*refs):
        src, out = refs[:n], refs[n:2 * n]
        send_sems, recv_sems, local_sems = refs[2 * n:]
        x, y, c = _place()
        my_idx = _index(x, y, c)
        peers = [(x ^ (k >> 2), y ^ ((k >> 1) & 1), c ^ (k & 1)) for k in range(1, N_DEV)]

        def copy(a, k, peer):
            return pltpu.make_async_remote_copy(
                src_ref=pickers[a](src[a], _index(*peer)), dst_ref=out[a].at[my_idx],
                send_sem=send_sems.at[a * 7 + k], recv_sem=recv_sems.at[a * 7 + k],
                device_id=peer, device_id_type=MESH)

        def arrival(a, k, peer):
            return pltpu.make_async_remote_copy(
                src_ref=pickers[a](src[a], my_idx), dst_ref=out[a].at[_index(*peer)],
                send_sem=send_sems.at[a * 7 + k], recv_sem=recv_sems.at[a * 7 + k],
                device_id=peer, device_id_type=MESH)

        mine = [pltpu.make_async_copy(pickers[a](src[a], my_idx), out[a].at[my_idx], local_sems.at[a])
                for a in range(n)]
        sent = [copy(a, k, peer) for a in range(n) for k, peer in enumerate(peers)]
        for cp in mine + sent:
            cp.start()
        for a in range(n):
            for k, peer in enumerate(peers):
                arrival(a, k, peer).wait_recv()
        for cp in sent:
            cp.wait_send()
        for cp in mine:
            cp.wait()

    hbm = pl.BlockSpec(memory_space=pl.ANY)
    return pl.pallas_call(
        body, in_specs=[hbm] * n, out_specs=[hbm] * n, out_shape=part_shapes,
        scratch_shapes=[pltpu.SemaphoreType.DMA((7 * n,)), pltpu.SemaphoreType.DMA((7 * n,)),
                        pltpu.SemaphoreType.DMA((n,))],
        name=name,
    )(*grads)


def _peers(x, y, c):
    return [(x ^ (k >> 2), y ^ ((k >> 1) & 1), c ^ (k & 1)) for k in range(1, N_DEV)]


_HBM = pl.BlockSpec(memory_space=pltpu.HBM)
_SEM = pl.BlockSpec(memory_space=pltpu.SEMAPHORE)
_EFFECT = pltpu.SideEffectType.DATAFLOW_SIDE_EFFECTING


def _exchange_start(srcs, land_shapes, src_win, dst_win, after, *, name):
    n = len(srcs)

    def body(*refs):
        src, land = refs[:n], refs[n:2 * n]
        send_sems, recv_sems, token = refs[2 * n + len(after)], refs[2 * n + len(after) + 1], refs[-1]
        x, y, c = _place()
        my_idx = _index(x, y, c)
        for a in range(n):
            for k, peer in enumerate(_peers(x, y, c)):
                pltpu.make_async_remote_copy(
                    src_ref=src_win[a](src[a], _index(*peer)), dst_ref=dst_win[a](land[a], my_idx),
                    send_sem=send_sems.at[a * 7 + k], recv_sem=recv_sems.at[a * 7 + k],
                    device_id=peer, device_id_type=MESH).start()
        token[...] = jnp.zeros_like(token)

    hbm = lambda a: pltpu.with_memory_space_constraint(a, pltpu.HBM)
    lands = [hbm(lax.empty(s.shape, s.dtype)) for s in land_shapes]
    out = pl.pallas_call(
        body, name=name,
        out_shape=(pltpu.SemaphoreType.DMA((7 * n,)), pltpu.SemaphoreType.DMA((7 * n,)),
                   *[pltpu.HBM(a.shape, a.dtype) for a in lands], jax.ShapeDtypeStruct((8, LANE), F32)),
        in_specs=[_HBM] * (2 * n) + [pl.BlockSpec(memory_space=pl.ANY)] * len(after),
        out_specs=(_SEM, _SEM, *([_HBM] * n), pl.BlockSpec(memory_space=pltpu.VMEM)),
        input_output_aliases={n + i: 2 + i for i in range(n)},
        compiler_params=pltpu.CompilerParams(has_side_effects=_EFFECT),
    )(*srcs, *lands, *after)
    return out[0], out[1], list(out[2:2 + n]), out[-1][0, 0]


def _exchange_wait(send_sems, recv_sems, srcs, lands, src_win, dst_win, after, *, name):
    n = len(srcs)

    def body(*refs):
        src, land = refs[:n], refs[n:2 * n]
        send_sems, recv_sems = refs[2 * n], refs[2 * n + 1]
        x, y, c = _place()
        for a in range(n):
            for k, peer in enumerate(_peers(x, y, c)):
                sender = _index(*peer)
                copy = pltpu.make_async_remote_copy(
                    src_ref=src_win[a](src[a], sender), dst_ref=dst_win[a](land[a], sender),
                    send_sem=send_sems.at[a * 7 + k], recv_sem=recv_sems.at[a * 7 + k],
                    device_id=peer, device_id_type=MESH)
                copy.wait_send()
                copy.wait_recv()

    out = pl.pallas_call(
        body, name=name, out_shape=tuple(pltpu.HBM(a.shape, a.dtype) for a in lands),
        in_specs=[_HBM] * (2 * n) + [_SEM, _SEM] + [pl.BlockSpec(memory_space=pl.ANY)] * len(after),
        out_specs=tuple([_HBM] * n),
        input_output_aliases={n + i: i for i in range(n)},
        compiler_params=pltpu.CompilerParams(has_side_effects=_EFFECT),
    )(*srcs, *lands, send_sems, recv_sems, *after)
    return list(out)


COPY_TILE_BYTES = 2 << 20


def _copy_own_block(src, land, idx, *, from_stack, name):
    k, n = land.shape[1:]
    tk = k
    while tk * n * land.dtype.itemsize > COPY_TILE_BYTES and tk % 32 == 0:
        tk //= 2

    def body(idx_ref, src_ref, land_ref, out_ref):
        out_ref[...] = src_ref[...]

    own = pl.BlockSpec((None, tk, n), lambda i, idx_ref: (idx_ref[0], i, 0))
    grid_spec = pltpu.PrefetchScalarGridSpec(
        num_scalar_prefetch=1, grid=(k // tk,),
        in_specs=[own if from_stack else pl.BlockSpec((tk, n), lambda i, idx_ref: (i, 0)),
                  pl.BlockSpec(memory_space=pl.ANY)],
        out_specs=own)
    return pl.pallas_call(
        body, grid_spec=grid_spec, out_shape=jax.ShapeDtypeStruct(land.shape, land.dtype),
        input_output_aliases={2: 0}, name=name,
    )(idx, src, land)


def _place_own(srcs, lands, src_win, dst_win, *, name):
    lands = list(lands)
    idx = _index(*_place()).astype(jnp.int32).reshape(1)
    rest = []
    for a in range(len(srcs)):
        if dst_win[a] is _whole and src_win[a] in (_whole, _shard_itself) and lands[a].ndim == 3:
            lands[a] = _copy_own_block(srcs[a], lands[a], idx, from_stack=src_win[a] is _whole,
                                       name=f"{name}_{a}")
        else:
            rest.append(a)
    if rest:
        placed = _dma_own([srcs[a] for a in rest], [lands[a] for a in rest], [src_win[a] for a in rest],
                          [dst_win[a] for a in rest], name=f"{name}_dma")
        for a, land in zip(rest, placed):
            lands[a] = land
    return lands


def _dma_own(srcs, lands, src_win, dst_win, *, name):
    n = len(srcs)

    def body(*refs):
        src, land = refs[:n], refs[n:2 * n]
        sems = refs[-1]
        my_idx = _index(*_place())
        copies = [pltpu.make_async_copy(src_win[a](src[a], my_idx), dst_win[a](land[a], my_idx), sems.at[a])
                  for a in range(n)]
        for cp in copies:
            cp.start()
        for cp in copies:
            cp.wait()

    hbm = pl.BlockSpec(memory_space=pl.ANY)
    return list(pl.pallas_call(
        body, name=name, in_specs=[hbm] * (2 * n), out_specs=[hbm] * n,
        out_shape=[jax.ShapeDtypeStruct(a.shape, a.dtype) for a in lands],
        input_output_aliases={n + i: i for i in range(n)},
        scratch_shapes=[pltpu.SemaphoreType.DMA((n,))],
    )(*srcs, *lands))


def _ada_fwd(c8, w_ada, b_ada8, *, name):
    depth, d, n = w_ada.shape

    def body(c_ref, w_ref, b_ref, mod_ref, sc_ref, c_all, prod, got, send_sems, recv_sems):
        x, y, c = _place()
        my_idx = _index(x, y, c)
        peers = [(x ^ (k >> 2), y ^ ((k >> 1) & 1), c ^ (k & 1)) for k in range(1, N_DEV)]

        def slab(ref, idx):
            return ref.at[pl.ds(pl.multiple_of(idx * 8, 8), 8)]

        def c_copy(k, peer, sender):
            return pltpu.make_async_remote_copy(
                src_ref=c_ref, dst_ref=slab(c_all, sender), send_sem=send_sems.at[k],
                recv_sem=recv_sems.at[k], device_id=peer, device_id_type=MESH)

        def m_copy(k, peer, sender):
            return pltpu.make_async_remote_copy(
                src_ref=prod.at[_index(*peer)], dst_ref=got.at[sender], send_sem=send_sems.at[7 + k],
                recv_sem=recv_sems.at[7 + k], device_id=peer, device_id_type=MESH)

        sends = [c_copy(k, peer, my_idx) for k, peer in enumerate(peers)]
        for cp in sends:
            cp.start()
        slab(c_all, my_idx)[...] = c_ref[...]
        for k, peer in enumerate(peers):
            c_copy(k, peer, _index(*peer)).wait_recv()
        cv = c_all[...]
        sc = cv * _sigmoid(cv)
        sc_ref[...] = sc
        for layer in range(depth):
            p = jnp.dot(sc, w_ref[layer], preferred_element_type=F32, precision=lax.Precision.HIGHEST)
            for s in range(N_DEV):
                prod[s, layer] = p[8 * s:8 * s + 8]
        sends2 = [m_copy(k, peer, my_idx) for k, peer in enumerate(peers)]
        for cp in sends2:
            cp.start()
        got[my_idx] = prod[my_idx]
        for k, peer in enumerate(peers):
            m_copy(k, peer, _index(*peer)).wait_recv()
        for layer in range(depth):
            for s in range(N_DEV):
                mod_ref[layer, :, s * n:(s + 1) * n] = got[s, layer] + b_ref[layer, :, s * n:(s + 1) * n]
        for cp in sends + sends2:
            cp.wait_send()

    vmem = pl.BlockSpec(memory_space=pltpu.VMEM)
    return pl.pallas_call(
        body, in_specs=[vmem] * 3, out_specs=[vmem] * 2,
        out_shape=[jax.ShapeDtypeStruct((depth, 8, N_DEV * n), F32),
                   jax.ShapeDtypeStruct((8 * N_DEV, d), F32)],
        scratch_shapes=[pltpu.VMEM((8 * N_DEV, d), F32), pltpu.VMEM((N_DEV, depth, 8, n), F32),
                        pltpu.VMEM((N_DEV, depth, 8, n), F32),
                        pltpu.SemaphoreType.DMA((14,)), pltpu.SemaphoreType.DMA((14,))],
        name=name, compiler_params=pltpu.CompilerParams(vmem_limit_bytes=56 << 20),
    )(c8, w_ada, b_ada8)


def _ada_bwd(sc8, dmod, *, name):
    depth, _, n = dmod.shape
    d = sc8.shape[1]

    def body(sc_ref, dm_ref, o_ref):
        for layer in range(depth):
            o_ref[layer] = lax.dot_general(sc_ref[...], dm_ref[layer], (((0,), (0,)), ((), ())),
                                           preferred_element_type=F32, precision=lax.Precision.HIGHEST)

    vmem = pl.BlockSpec(memory_space=pltpu.VMEM)
    return pl.pallas_call(
        body, in_specs=[vmem] * 2, out_specs=vmem, out_shape=jax.ShapeDtypeStruct((depth, d, n), F32),
        name=name, compiler_params=pltpu.CompilerParams(vmem_limit_bytes=40 << 20),
    )(sc8, dmod)


def _gather_small(vec, *, name):
    r = vec.shape[0]

    def body(v_ref, o_ref, send_sems, recv_sems):
        x, y, c = _place()
        my_idx = _index(x, y, c)
        peers = [(x ^ (k >> 2), y ^ ((k >> 1) & 1), c ^ (k & 1)) for k in range(1, N_DEV)]

        def copy(k, peer, sender):
            return pltpu.make_async_remote_copy(
                src_ref=v_ref, dst_ref=o_ref.at[sender], send_sem=send_sems.at[k],
                recv_sem=recv_sems.at[k], device_id=peer, device_id_type=MESH)

        sends = [copy(k, peer, my_idx) for k, peer in enumerate(peers)]
        for cp in sends:
            cp.start()
        o_ref[my_idx] = v_ref[...]
        for k, peer in enumerate(peers):
            copy(k, peer, _index(*peer)).wait_recv()
        for cp in sends:
            cp.wait_send()

    vmem = pl.BlockSpec(memory_space=pltpu.VMEM)
    return pl.pallas_call(
        body, in_specs=[vmem], out_specs=vmem, out_shape=jax.ShapeDtypeStruct((N_DEV, r, LANE), F32),
        scratch_shapes=[pltpu.SemaphoreType.DMA((7,)), pltpu.SemaphoreType.DMA((7,))], name=name,
    )(vec)


TM = 512
TM_WIDE = 2048
TMM_DEEP = 1024
TQ = 512


def _layer_fwd(x, mod, small, gw, *, d, tag, more_weights=None):
    shift, scale, res_gate = mod
    h = _prenorm(x, small["norm_g"], scale, shift, tm=TM, name=f"prenorm_{tag}")
    proj = _mm_nn(h, gw["w_in"], out_dtype=BF16, tm=TM, name=f"in_proj_{tag}")
    ya = _pool_fwd(proj, gw["pool_w"], small["pool_scale"], d=d, tm=TM, name=f"pool_fwd_{tag}")
    o, yb, tot = _attn_fwd(proj, d=d, tq=TQ, name=f"attn_fwd_{tag}")
    yc = _conv_fwd(proj, gw["conv_w"], d=d, tm=TM, cw=LANE * 2, name=f"conv_fwd_{tag}")
    if more_weights is not None:
        gw = dict(gw, **more_weights(o))
    gl = _mm_nn(h, gw["w_gate"], out_dtype=BF16, tm=TM, bias=small["b_gate"], name=f"gate_proj_{tag}")
    nblk = gw["w_br_a"].shape[0]
    bst = _mm_nn(ya, gw["w_br_a"], out_dtype=BF16, tm=TM_WIDE, out_cols=3 * d, name=f"branch_a_{tag}")
    bst = _mm_nn(yb, gw["w_br_b"], out_dtype=BF16, tm=TM, out=bst, out_col_block=1, name=f"branch_b_{tag}")
    bst = _mm_nn(yc, gw["w_br_c"], out_dtype=BF16, tm=TM_WIDE, out=bst, out_col_block=2 * nblk,
                 name=f"branch_c_{tag}")
    merged = _merge_fwd(gl, bst, d=d, tm=TM, tn=TM, name=f"merge_fwd_{tag}")
    x_new, mo = _out_fwd(merged, gw["w_out"][0], x, res_gate, tm=TM, name=f"out_fwd_{tag}")
    return x_new, dict(x=x, h=h, proj=proj, gl=gl, ya=ya, yb=yb, yc=yc, o=o, tot=tot, bst=bst, merged=merged,
                       mo=mo), gw


EARLY = ("w_gate", "w_br_a", "w_br_b", "w_br_c", "w_out")
LATE = ("w_in", "pool_w")


def _layer_bwd(dout, sv, mod, small, gw, *, d, tag, send_early=None, send_late=None):
    shift, scale, res_gate = mod
    nblk = gw["w_br_a"].shape[0]
    dmo, d_res_gate = _out_bwd(dout, sv["mo"], res_gate, tm=TM, name=f"out_bwd_{tag}")
    g_w_out = _mm_tn(sv["merged"], dmo, nb=1, n=d, tk=TM, tmm=TMM_DEEP, name=f"dw_out_{tag}")
    dmerged = _mm_nt(dmo, gw["w_out"], out_dtype=BF16, tm=TM, name=f"dmerged_{tag}")
    dbst, dgl, d_b_gate = _merge_bwd(dmerged, sv["gl"], sv["bst"], d=d, tm=TM, tn=TM, name=f"merge_bwd_{tag}")
    g_w_gate = _mm_tn(sv["h"], dgl, nb=nblk, n=3 * d // nblk, tk=d // 2, tmm=TMM_DEEP,
                      name=f"dw_gate_{tag}")
    dh = _mm_nt(dgl, gw["w_gate"], out_dtype=F32, tm=TM, name=f"dh_gate_{tag}")
    g_w_br_a = _mm_tn(sv["ya"], dbst, nb=nblk, n=d // nblk, tk=d // 2, tmm=TM_WIDE,
                      name=f"dw_br_a_{tag}")
    g_w_br_b = _mm_tn(sv["yb"], dbst, nb=1, n=d, tk=TM, tmm=TMM_DEEP, col_block=1, name=f"dw_br_b_{tag}")
    g_w_br_c = _mm_tn(sv["yc"], dbst, nb=nblk, n=d // nblk, tk=d // 2, tmm=TM_WIDE, col_block=2 * nblk,
                      name=f"dw_br_c_{tag}")
    dya = _mm_nt(dbst, gw["w_br_a"], out_dtype=BF16, tm=TM_WIDE, name=f"dya_{tag}")
    dyb = _mm_nt(dbst, gw["w_br_b"], out_dtype=BF16, tm=TM, col_block=1, name=f"dyb_{tag}")
    dyc = _mm_nt(dbst, gw["w_br_c"], out_dtype=BF16, tm=TM_WIDE, col_block=2 * nblk, name=f"dyc_{tag}")
    big = dict(w_gate=g_w_gate, w_br_a=g_w_br_a, w_br_b=g_w_br_b, w_br_c=g_w_br_c, w_out=g_w_out)
    early_sent = None
    if send_early is not None:
        token = send_early(big)
        small = dict(small, pool_scale=small["pool_scale"] + token, norm_g=small["norm_g"] + token)
        early_sent = token.reshape(1, 1)
    dxa, dza, g_pool_w, d_pool_scale = _pool_bwd(dya, sv["proj"], gw["pool_w"], small["pool_scale"], d=d, tm=TM,
                                                 name=f"pool_bwd_{tag}")
    dq, dk, dv, dzb = _attn_bwd(dyb, sv["o"], sv["tot"], sv["proj"], d=d, tq=TQ, name=f"attn_bwd_{tag}",
                                after=early_sent)
    du, dbg, dcg, dzc, d_conv_w = _conv_bwd(dyc, sv["proj"], gw["conv_w"], d=d, tm=TM, cw=LANE * 2,
                                            name=f"conv_bwd_{tag}")
    dproj = jnp.concatenate([dxa, dza, dq, dk, dv, dzb, du, dbg, dcg, dzc], axis=1)
    g_w_in = _mm_tn(sv["h"], dproj, nb=nblk, n=7 * d // nblk, tk=d // 2, tmm=TMM_DEEP,
                    name=f"dw_in_{tag}")
    big.update(w_in=g_w_in, pool_w=g_pool_w.astype(BF16))
    late_sent = send_late(big).reshape(1, 1) if send_late is not None else None
    dh = _mm_nt(dproj, gw["w_in"], out_dtype=F32, tm=TM, addend=dh, after=late_sent, name=f"dh_in_{tag}")
    dx, d_norm_g, d_scale, d_shift = _prenorm_bwd(dh, sv["x"], dout, small["norm_g"], scale, tm=TM,
                                                  name=f"prenorm_bwd_{tag}")
    part = dict(norm_g=d_norm_g, mod=jnp.concatenate([d_shift, d_scale, d_res_gate], axis=1),
                pool_scale=d_pool_scale, b_gate=d_b_gate, conv_w=d_conv_w[:3])
    return dx, big, part


BIG = ("w_in", "w_gate", "w_br_a", "w_br_b", "w_br_c", "w_out", "pool_w")
SMALL_PER_LAYER = ("norm_g", "mod", "pool_scale", "b_gate", "conv_w")


def _gather_layer(w, layer, *, d):
    nd = N_DEV
    gd = d // 2 // N_POOL_GROUPS
    cc = d // 2
    shards = [w[k][layer].astype(BF16) for k in BIG]
    shards.append(jnp.pad(w["conv_w"][layer], ((0, 5), (0, 0))))
    blocks = [jax.ShapeDtypeStruct((nd,) + s.shape, BF16) for s in shards[:6]]
    blocks.append(jax.ShapeDtypeStruct((N_POOL_GROUPS, gd, gd), BF16))
    blocks.append(jax.ShapeDtypeStruct((8, cc), F32))
    placers = [_whole] * 6 + [_row_block(gd // nd), _lane_block(cc // nd)]
    return shards, blocks, placers


GATHERED = BIG + ("conv_w",)
FIRST = (0, 6, 7)
REST = (1, 2, 3, 4, 5)


def _as_weights(got, d, which=tuple(range(len(GATHERED)))):
    gw = dict(zip([GATHERED[i] for i in which], got))
    for k in ("w_br_b", "w_out"):
        if k in gw:
            gw[k] = gw[k].reshape(1, d, d)
    return gw


def _shard_itself(ref, idx):
    return ref


def _scatter_spec(big, keys, *, d):
    nd = N_DEV
    gd = d // 2 // N_POOL_GROUPS
    grads, pickers, shapes = [], [], []
    for k in keys:
        g = big[k]
        if k == "pool_w":
            pickers.append(_row_block(gd // nd))
            shapes.append(jax.ShapeDtypeStruct((nd, N_POOL_GROUPS, gd // nd, gd), BF16))
        else:
            if g.shape[0] == 1:
                g = g.reshape(nd, g.shape[1] // nd, g.shape[2])
            pickers.append(_whole)
            shapes.append(jax.ShapeDtypeStruct(g.shape, BF16))
        grads.append(g)
    return grads, pickers, shapes


class _Behind:
    def __init__(self, srcs, land_shapes, src_win, dst_win, after, name):
        srcs = [pltpu.with_memory_space_constraint(s, pltpu.HBM) for s in srcs]
        self.srcs, self.src_win, self.dst_win, self.name = srcs, src_win, dst_win, name
        self.send, self.recv, self.lands, self.token = _exchange_start(
            srcs, land_shapes, src_win, dst_win, after, name=f"{name}_start")

    def finish(self, after):
        lands = _exchange_wait(self.send, self.recv, self.srcs, self.lands, self.src_win, self.dst_win,
                               after, name=f"{self.name}_wait")
        return _place_own(self.srcs, lands, self.src_win, self.dst_win, name=f"{self.name}_own")


def _pack(pieces):
    return jnp.concatenate([p.reshape(-1, LANE) for p in pieces], axis=0)


def kernel(x, c, norm_g, w_ada, b_ada, w_in, pool_w, pool_scale, conv_w, w_br_a, w_br_b, w_br_c, w_gate, b_gate, w_out, final_g, loss_target, m_norm_g, m_w_ada, m_b_ada, m_w_in, m_pool_w, m_pool_scale, m_conv_w, m_w_br_a, m_w_br_b, m_w_br_c, m_w_gate, m_b_gate, m_w_out, m_final_g, v_norm_g, v_w_ada, v_b_ada, v_w_in, v_pool_w, v_pool_scale, v_conv_w, v_w_br_a, v_w_br_b, v_w_br_c, v_w_gate, v_b_gate, v_w_out, v_final_g):
    names = ("norm_g", "w_ada", "b_ada", "w_in", "pool_w", "pool_scale", "conv_w", "w_br_a", "w_br_b",
             "w_br_c", "w_gate", "b_gate", "w_out", "final_g")
    w = dict(zip(names, (norm_g, w_ada, b_ada, w_in, pool_w, pool_scale, conv_w, w_br_a, w_br_b, w_br_c,
                         w_gate, b_gate, w_out, final_g)))
    m = dict(zip(names, (m_norm_g, m_w_ada, m_b_ada, m_w_in, m_pool_w, m_pool_scale, m_conv_w, m_w_br_a,
                         m_w_br_b, m_w_br_c, m_w_gate, m_b_gate, m_w_out, m_final_g)))
    v = dict(zip(names, (v_norm_g, v_w_ada, v_b_ada, v_w_in, v_pool_w, v_pool_scale, v_conv_w, v_w_br_a,
                         v_w_br_b, v_w_br_c, v_w_gate, v_b_gate, v_w_out, v_final_g)))
    _, t, d = x.shape
    depth = norm_g.shape[0]
    cc = d // 2
    my_idx = _index(*_place())

    mod8, sc_all = _ada_fwd(jnp.broadcast_to(c, (8, d)), w_ada,
                            jnp.broadcast_to(b_ada[:, None, :], (depth, 8, 3 * d)), name="ada_fwd")
    mods = [tuple(mod8[l, 0:1, k * d:(k + 1) * d] for k in range(3)) for l in range(depth)]
    small = [dict(norm_g=norm_g[l][None], b_gate=b_gate[l][None], pool_scale=pool_scale[l][None])
             for l in range(depth)]

    shards, blocks, placers = _gather_layer(w, 0, d=d)
    pick = lambda seq, which: [seq[i] for i in which]
    first = _all_gather(pick(shards, FIRST), pick(blocks, FIRST), pick(placers, FIRST), name="gather_first_l0")
    rest = _Behind(pick(shards, REST), pick(blocks, REST), [_shard_itself] * len(REST), pick(placers, REST),
                   (mod8, first[-1]), "gather_rest_l0")
    gws = [_as_weights(first, d, FIRST)]
    order = rest.token
    xs = x[0]
    saved = []
    for l in range(depth):
        coming = None
        if l + 1 < depth:
            shards, blocks, placers = _gather_layer(w, l + 1, d=d)
            coming = []
            for which, part in ((FIRST, "first"), (REST, "rest")):
                coming.append(_Behind(pick(shards, which), pick(blocks, which), [_shard_itself] * len(which),
                                      pick(placers, which), (mod8, gws[l]["conv_w"], order.reshape(1, 1)),
                                      f"gather_{part}_l{l + 1}"))
                order = order + coming[-1].token
        shift, scale, res_gate = mods[l]
        mods[l] = (shift + order, scale, res_gate)
        xs, sv, gws[l] = _layer_fwd(xs, mods[l], small[l], gws[l], d=d, tag=f"l{l}",
                                    more_weights=lambda o, rest=rest: _as_weights(rest.finish((o,)), d, REST))
        saved.append(sv)
        if coming is not None:
            gws.append(_as_weights(coming[0].finish((xs,)), d, FIRST))
            rest = coming[1]
    dx, loss8, d_final_g = _loss_head(xs, loss_target[0], final_g[None], tm=TM, name="loss_head")
    loss = lax.psum(loss8[0, 0], ("x", "y", "c"))

    parts, small_parts = [dict() for _ in range(depth)], [None] * depth
    going, last = [], []

    def send(queue, keys, l, name):
        def start(grads_so_far):
            grads, pickers, shapes = _scatter_spec(grads_so_far, keys, d=d)
            queue.append((l, keys, _Behind(grads, shapes, pickers, [_whole] * len(grads), (grads[0],), name)))
            return queue[-1][2].token
        return start

    for l in reversed(range(depth)):
        if going:
            shift, scale, res_gate = mods[l]
            mods[l] = (shift, scale, res_gate + going[-1][2].token)
        lowest = l == 0
        dx, big, small_parts[l] = _layer_bwd(
            dx, saved[l], mods[l], small[l], gws[l], d=d, tag=f"l{l}",
            send_early=send(going, EARLY, l, f"scatter_early_l{l}") if lowest else None,
            send_late=send(last, LATE, l, f"scatter_late_l{l}") if lowest else None)
        if not lowest:
            send(going, BIG, l, f"scatter_grads_l{l}")(big)
    for l, keys, behind in going:
        parts[l].update(zip(keys, behind.finish((dx,))))

    pieces = [small_parts[l][k] for l in range(depth) for k in SMALL_PER_LAYER] + [d_final_g]
    sizes = [p.size for p in pieces]
    gathered = _gather_small(_pack(pieces), name="gather_small_grads")
    zero_conv = jnp.zeros((3, cc), F32)

    def packed(src):
        per_layer = lambda l: [src["norm_g"][l], src["b_ada"][l], src["pool_scale"][l], src["b_gate"][l],
                               zero_conv]
        return _pack([p for l in range(depth) for p in per_layer(l)] + [src["final_g"]])[None]

    small_out = _adamw(packed(w), packed(m), packed(v), gathered, 0, None, tr=gathered.shape[1],
                       name="adamw_small")

    def unpack(flat):
        flat = flat.reshape(-1)
        out, off = [], 0
        for n in sizes:
            out.append(flat[off:off + n])
            off += n
        return out

    small_res = [unpack(a) for a in small_out]
    k_per = len(SMALL_PER_LAYER)

    def small_stack(which, pos, shape):
        return jnp.stack([small_res[which][l * k_per + pos] for l in range(depth)]).reshape(shape)

    res = {}
    for pos, name in ((0, "norm_g"), (1, "b_ada"), (2, "pool_scale"), (3, "b_gate")):
        res[name] = tuple(small_stack(k, pos, w[name].shape) for k in range(4))
    res["final_g"] = tuple(small_res[k][-1].reshape(final_g.shape) for k in range(4))

    ncol = cc // N_DEV
    conv_total = small_stack(0, 4, (depth * 3, cc))
    conv_mine = lax.dynamic_slice_in_dim(conv_total, my_idx * ncol, ncol, axis=1)
    pad8 = lambda a: jnp.pad(a.reshape(depth * 3, ncol), ((0, 8 - depth * 3), (0, 0)))[None]
    conv_out = _adamw(pad8(conv_w), pad8(m["conv_w"]), pad8(v["conv_w"]), pad8(conv_mine), 0, None, tr=8,
                      name="adamw_conv_w")
    res["conv_w"] = tuple(a[0, :depth * 3].reshape(conv_w.shape) for a in conv_out)

    n_ada = 3 * d // N_DEV
    mod_off = [sum(sizes[:l * k_per + 1]) // LANE for l in range(depth)]
    dmod = jnp.stack([
        lax.dynamic_slice_in_dim(gathered[:, mod_off[l]:mod_off[l] + 3 * d // LANE].reshape(N_DEV, 3 * d),
                                 my_idx * n_ada, n_ada, axis=1) for l in range(depth)])
    g_w_ada = _ada_bwd(sc_all[::8], dmod, name="ada_bwd")
    res["w_ada"] = _adamw_layers(w_ada, m["w_ada"], v["w_ada"], [g_w_ada[l][None] for l in range(depth)],
                                 tr=128, name="adamw_w_ada")

    def big_adamw(name):
        out = _adamw_layers(w[name], m[name], v[name], [parts[l][name] for l in range(depth)], tr=128,
                            name=f"adamw_{name}")
        res[name] = tuple(a.reshape(w[name].shape) for a in out)

    for name in EARLY:
        big_adamw(name)
    for l, keys, behind in last:
        parts[l].update(zip(keys, behind.finish(tuple(res[name][1] for name in ("w_ada",) + EARLY))))
    for name in LATE:
        big_adamw(name)

    outs = [loss, dx[None]]
    for k in range(4):
        outs += [res[name][k] for name in names]
    return tuple(outs)
```

```python
import functools

import jax
import jax.numpy as jnp
from jax import lax
from jax.experimental import pallas as pl
from jax.experimental.pallas import tpu as pltpu

F32 = jnp.float32
BF16 = jnp.bfloat16
MESH = pl.DeviceIdType.MESH

N_DEV = 8
DEPTH = 2
HEAD_DIM = 128
N_POOL_GROUPS = 4
POOL_WINDOWS = (2, 4, 8, 16)
RMS_EPS = 1e-6
LANE = 128
HALO = 16
KEY_CHUNK = 256
KEY_BLOCK = 512
HEADS_PER_STEP = 2
ADAM_LR = 0.001
ADAM_B1 = 0.9
ADAM_B2 = 0.999
ADAM_EPS = 1e-08
ADAM_WD = 0.01
ADAM_STEP = 10


def _cparams(semantics, vmem_mb):
    return pltpu.CompilerParams(dimension_semantics=semantics, vmem_limit_bytes=vmem_mb << 20)


def _sigmoid(z):
    return 1.0 / (1.0 + jnp.exp(-z))


def _mm_nn(a, w, *, out_dtype, tm, name, bias=None, out=None, out_cols=None, out_col_block=0):
    m, k = a.shape
    nb, _, n = w.shape
    tm = min(tm, m)
    total_cols =out.shape[1] if out is not None else (nb * n if out_cols is None else out_cols)

    def body(*refs):
        if out is not None:
            refs = refs[:-2] + refs[-1:]
        if bias is not None:
            a_ref, w_ref, b_ref, o_ref = refs
        else:
            a_ref, w_ref, o_ref = refs
        acc = jnp.dot(a_ref[...], w_ref[...], preferred_element_type=F32)
        if bias is not None:
            acc = acc + b_ref[...]
        o_ref[...] = acc.astype(out_dtype)

    in_specs = [pl.BlockSpec((tm, k), lambda j, i: (i, 0)),
                pl.BlockSpec((None, k, n), lambda j, i: (j, 0, 0))]
    args = [a, w]
    if bias is not None:
        in_specs.append(pl.BlockSpec((1, n), lambda j, i: (0, j)))
        args.append(bias)
    aliases = {}
    if out is not None:
        in_specs.append(pl.BlockSpec(memory_space=pl.ANY))
        aliases = {len(args): 0}
        args.append(out)
    return pl.pallas_call(
        body, grid=(nb, m // tm), in_specs=in_specs,
        out_specs=pl.BlockSpec((tm, n), lambda j, i: (i, out_col_block + j)),
        out_shape=jax.ShapeDtypeStruct((m, total_cols), out_dtype),
        input_output_aliases=aliases, name=name,
        compiler_params=_cparams(("arbitrary", "arbitrary"), 56),
    )(*args)


def _mm_nt(dy, w, *, out_dtype, tm, name, addend=None, col_block=0, after=None):
    m = dy.shape[0]
    nb, k, n = w.shape
    tm = min(tm, m)

    def body(*refs):
        dy_ref, w_ref = refs[:2]
        add_ref = refs[2] if addend is not None else None
        o_ref, acc_ref = refs[-2:]
        j = pl.program_id(1)
        part = lax.dot_general(dy_ref[...], w_ref[...], (((1,), (1,)), ((), ())),
                               preferred_element_type=F32)

        @pl.when(j == 0)
        def _():
            acc_ref[...] = part if addend is None else part + add_ref[...]

        @pl.when(j > 0)
        def _():
            acc_ref[...] += part

        @pl.when(j == nb - 1)
        def _():
            o_ref[...] = acc_ref[...].astype(out_dtype)

    in_specs = [pl.BlockSpec((tm, n), lambda i, j: (i, col_block + j)),
                pl.BlockSpec((None, k, n), lambda i, j: (j, 0, 0))]
    args = [dy, w]
    if addend is not None:
        in_specs.append(pl.BlockSpec((tm, k), lambda i, j: (i, 0)))
        args.append(addend)
    if after is not None:
        in_specs.append(pl.BlockSpec(memory_space=pl.ANY))
        args.append(after)
    return pl.pallas_call(
        body, grid=(m // tm, nb), in_specs=in_specs,
        out_specs=pl.BlockSpec((tm, k), lambda i, j: (i, 0)),
        out_shape=jax.ShapeDtypeStruct((m, k), out_dtype),
        scratch_shapes=[pltpu.VMEM((tm, k), F32)], name=name,
        compiler_params=_cparams(("arbitrary", "arbitrary"), 56),
    )(*args)


def _mm_nt_pair(dy1, w1, dy2, w2, *, tm, name, after=None):
    m = dy1.shape[0]
    nb1, k, n1 = w1.shape
    nb2, _, n2 = w2.shape
    tm = min(tm, m)

    def body(*refs):
        dy1_ref, w1_ref, dy2_ref, w2_ref = refs[:4]
        o_ref = refs[-1]
        j = pl.program_id(1)
        contract = (((1,), (1,)), ((), ()))

        @pl.when(j == 0)
        def _():
            o_ref[...] = lax.dot_general(dy1_ref[...], w1_ref[...], contract, preferred_element_type=F32)

        @pl.when((j > 0) & (j < nb1))
        def _():
            o_ref[...] += lax.dot_general(dy1_ref[...], w1_ref[...], contract, preferred_element_type=F32)

        @pl.when(j >= nb1)
        def _():
            o_ref[...] += lax.dot_general(dy2_ref[...], w2_ref[...], contract, preferred_element_type=F32)

    first = lambda j: jnp.minimum(j, nb1 - 1)
    second = lambda j: jnp.maximum(j - nb1, 0)
    in_specs = [pl.BlockSpec((tm, n1), lambda i, j: (i, first(j))),
                pl.BlockSpec((None, k, n1), lambda i, j: (first(j), 0, 0)),
                pl.BlockSpec((tm, n2), lambda i, j: (i, second(j))),
                pl.BlockSpec((None, k, n2), lambda i, j: (second(j), 0, 0))]
    args = [dy1, w1, dy2, w2]
    if after is not None:
        in_specs.append(pl.BlockSpec(memory_space=pl.ANY))
        args.append(after)
    return pl.pallas_call(
        body, grid=(m // tm, nb1 + nb2), in_specs=in_specs,
        out_specs=pl.BlockSpec((tm, k), lambda i, j: (i, 0)),
        out_shape=jax.ShapeDtypeStruct((m, k), F32), name=name,
        compiler_params=_cparams(("arbitrary", "arbitrary"), 56),
    )(*args)


def _mm_tn(a, dy, *, nb, n, tk, tmm, name, col_block=0, out_dtype=BF16):
    m, k = a.shape
    tmm = min(tmm, m)

    def body(a_ref, dy_ref, o_ref, acc_ref):
        t = pl.program_id(2)
        part = lax.dot_general(a_ref[...], dy_ref[...], (((0,), (0,)), ((), ())),
                               preferred_element_type=F32)

        @pl.when(t == 0)
        def _():
            acc_ref[...] = part

        @pl.when(t > 0)
        def _():
            acc_ref[...] += part

        @pl.when(t == pl.num_programs(2) - 1)
        def _():
            o_ref[...] = acc_ref[...].astype(out_dtype)

    return pl.pallas_call(
        body, grid=(nb, k // tk, m // tmm),
        in_specs=[pl.BlockSpec((tmm, tk), lambda j, kb, t: (t, kb)),
                  pl.BlockSpec((tmm, n), lambda j, kb, t: (t, col_block + j))],
        out_specs=pl.BlockSpec((None, tk, n), lambda j, kb, t: (j, kb, 0)),
        out_shape=jax.ShapeDtypeStruct((nb, k, n), out_dtype),
        scratch_shapes=[pltpu.VMEM((tk, n), F32)], name=name,
        compiler_params=_cparams(("arbitrary", "arbitrary", "arbitrary"), 56),
    )(a, dy)


def _prenorm(x, g, scale, shift, *, tm, name):
    t, d = x.shape

    def body(x_ref, g_ref, sc_ref, sh_ref, h_ref):
        xv = x_ref[...]
        r = lax.rsqrt(jnp.mean(xv * xv, axis=-1, keepdims=True) + RMS_EPS)
        h_ref[...] = ((xv * r) * g_ref[...] * (1.0 + sc_ref[...]) + sh_ref[...]).astype(BF16)

    row = pl.BlockSpec((1, d), lambda i: (0, 0))
    tile = pl.BlockSpec((tm, d), lambda i: (i, 0))
    return pl.pallas_call(
        body, grid=(t // tm,), in_specs=[tile, row, row, row], out_specs=tile,
        out_shape=jax.ShapeDtypeStruct((t, d), BF16), name=name,
        compiler_params=_cparams(("arbitrary",), 40),
    )(x, g, scale, shift)


def _prenorm_bwd(dh, x, dout, g, scale, *, tm, name):
    t, d = x.shape

    def body(dh_ref, x_ref, do_ref, g_ref, sc_ref, dx_ref, dg_ref, dsc_ref, dsh_ref):
        i = pl.program_id(0)
        xv = x_ref[...]
        dhv = dh_ref[...]
        r = lax.rsqrt(jnp.mean(xv * xv, axis=-1, keepdims=True) + RMS_EPS)
        xn = xv * r
        gain = g_ref[...] * (1.0 + sc_ref[...])
        dxn = dhv * gain
        dx_ref[...] = do_ref[...] + r * (dxn - xn * jnp.mean(dxn * xn, axis=-1, keepdims=True))
        dhxn = dhv * xn
        dg = jnp.sum(dhxn * (1.0 + sc_ref[...]), axis=0, keepdims=True)
        dsc = jnp.sum(dhxn * g_ref[...], axis=0, keepdims=True)
        dsh = jnp.sum(dhv, axis=0, keepdims=True)

        @pl.when(i == 0)
        def _():
            dg_ref[...] = dg
            dsc_ref[...] = dsc
            dsh_ref[...] = dsh

        @pl.when(i > 0)
        def _():
            dg_ref[...] += dg
            dsc_ref[...] += dsc
            dsh_ref[...] += dsh

    row = pl.BlockSpec((1, d), lambda i: (0, 0))
    tile = pl.BlockSpec((tm, d), lambda i: (i, 0))
    vec = jax.ShapeDtypeStruct((1, d), F32)
    return pl.pallas_call(
        body, grid=(t // tm,), in_specs=[tile, tile, tile, row, row],
        out_specs=[tile, row, row, row],
        out_shape=[jax.ShapeDtypeStruct((t, d), F32), vec, vec, vec], name=name,
        compiler_params=_cparams(("arbitrary",), 48),
    )(dh, x, dout, g, scale)


def _out_fwd(merged, w_out, x, res_gate, *, tm, name):
    t, d = x.shape
    k = merged.shape[1]

    def body(m_ref, w_ref, x_ref, rg_ref, xo_ref, mo_ref):
        mo = jnp.dot(m_ref[...], w_ref[...], preferred_element_type=F32)
        mo_ref[...] = mo.astype(BF16)
        xo_ref[...] = x_ref[...] + rg_ref[...] * mo

    tile = pl.BlockSpec((tm, d), lambda i: (i, 0))
    return pl.pallas_call(
        body, grid=(t // tm,),
        in_specs=[pl.BlockSpec((tm, k), lambda i: (i, 0)), pl.BlockSpec((k, d), lambda i: (0, 0)),
                  tile, pl.BlockSpec((1, d), lambda i: (0, 0))],
        out_specs=[tile, tile],
        out_shape=[jax.ShapeDtypeStruct((t, d), F32), jax.ShapeDtypeStruct((t, d), BF16)], name=name,
        compiler_params=_cparams(("arbitrary",), 56),
    )(merged, w_out, x, res_gate)


def _out_bwd(dout, mo, res_gate, *, tm, name):
    t, d = dout.shape

    def body(do_ref, mo_ref, rg_ref, dmo_ref, drg_ref):
        i = pl.program_id(0)
        dov = do_ref[...]
        dmo_ref[...] = (dov * rg_ref[...]).astype(BF16)
        drg = jnp.sum(dov * mo_ref[...].astype(F32), axis=0, keepdims=True)

        @pl.when(i == 0)
        def _():
            drg_ref[...] = drg

        @pl.when(i > 0)
        def _():
            drg_ref[...] += drg

    row = pl.BlockSpec((1, d), lambda i: (0, 0))
    tile = pl.BlockSpec((tm, d), lambda i: (i, 0))
    return pl.pallas_call(
        body, grid=(t // tm,), in_specs=[tile, tile, row], out_specs=[tile, row],
        out_shape=[jax.ShapeDtypeStruct((t, d), BF16), jax.ShapeDtypeStruct((1, d), F32)], name=name,
        compiler_params=_cparams(("arbitrary",), 40),
    )(dout, mo, res_gate)


def _loss_head(x, target, final_g, *, tm, name):
    t, d = x.shape

    def body(x_ref, t_ref, g_ref, dx_ref, loss_ref, dg_ref):
        i = pl.program_id(0)
        xv = x_ref[...]
        r = lax.rsqrt(jnp.mean(xv * xv, axis=-1, keepdims=True) + RMS_EPS)
        xn = xv * r
        err = xn * g_ref[...] - t_ref[...]
        part = (0.5 / d) * jnp.sum(jnp.sum(err * err, axis=1, keepdims=True), axis=0, keepdims=True)
        dy = err * (1.0 / d)
        dxn = dy * g_ref[...]
        dx_ref[...] = r * (dxn - xn * jnp.mean(dxn * xn, axis=-1, keepdims=True))
        dg = jnp.sum(dy * xn, axis=0, keepdims=True)

        @pl.when(i == 0)
        def _():
            loss_ref[...] = jnp.zeros_like(loss_ref) + part
            dg_ref[...] = dg

        @pl.when(i > 0)
        def _():
            loss_ref[...] += part
            dg_ref[...] += dg

    row = pl.BlockSpec((1, d), lambda i: (0, 0))
    tile = pl.BlockSpec((tm, d), lambda i: (i, 0))
    return pl.pallas_call(
        body, grid=(t // tm,), in_specs=[tile, tile, row],
        out_specs=[tile, pl.BlockSpec((8, LANE), lambda i: (0, 0)), row],
        out_shape=[jax.ShapeDtypeStruct((t, d), F32), jax.ShapeDtypeStruct((8, LANE), F32),
                   jax.ShapeDtypeStruct((1, d), F32)], name=name,
        compiler_params=_cparams(("arbitrary",), 40),
    )(x, target, final_g)


def _cur(tm, cw, col):
    return pl.BlockSpec((tm, cw), lambda cb, i: (i, col + cb))


def _prev(tm, cw, col):
    return pl.BlockSpec((HALO, cw), lambda cb, i: (jnp.maximum(i * (tm // HALO) - 1, 0), col + cb))


def _next(tm, cw, col, t):
    last = t // HALO - 1
    return pl.BlockSpec((HALO, cw), lambda cb, i: (jnp.minimum((i + 1) * (tm // HALO), last), col + cb))


def _with_prev(prev_ref, cur, first):
    prev = jnp.where(first, 0.0, prev_ref[...].astype(F32))
    return jnp.concatenate([prev, cur], axis=0)


def _with_next(cur, next_ref, last):
    nxt = jnp.where(last, 0.0, next_ref[...].astype(F32))
    return jnp.concatenate([cur, nxt], axis=0)


def _pool_mixed(xe, cur, g, row0, tm):
    s2 = xe + pltpu.roll(xe, 1, 0)
    s4 = s2 + pltpu.roll(s2, 2, 0)
    s8 = s4 + pltpu.roll(s4, 4, 0)
    s16 = s8 + pltpu.roll(s8, 8, 0)
    win = jnp.where(g == 0, s2, jnp.where(g == 1, s4, jnp.where(g == 2, s8, s16)))[HALO:]
    return win * _pool_inv_count(g, row0, tm, cur.shape[1]) - cur


def _pool_inv_count(g, row0, rows, cols):
    tpos = row0 + lax.broadcasted_iota(jnp.int32, (rows, cols), 0)
    width = jnp.left_shift(2, g)
    return 1.0 / jnp.minimum(tpos + 1, width).astype(F32)


def _pool_fwd(proj, pool_w, pool_scale, *, d, tm, name):
    t = proj.shape[0]
    pw = d // 2
    gd = pw // N_POOL_GROUPS
    za_col = pw // gd

    def body(xa_ref, xp_ref, za_ref, w_ref, ps_ref, ya_ref):
        g = pl.program_id(0)
        i = pl.program_id(1)
        cur = xa_ref[...].astype(F32)
        mixed = _pool_mixed(_with_prev(xp_ref, cur, i == 0), cur, g, i * tm, tm)
        ypre = jnp.dot(mixed.astype(BF16), w_ref[...], preferred_element_type=F32)
        za = za_ref[...].astype(F32)
        ya_ref[...] = (ypre * ps_ref[...] * (za * _sigmoid(za))).astype(BF16)

    return pl.pallas_call(
        body, grid=(N_POOL_GROUPS, t // tm),
        in_specs=[_cur(tm, gd, 0), _prev(tm, gd, 0), _cur(tm, gd, za_col),
                  pl.BlockSpec((None, gd, gd), lambda g, i: (g, 0, 0)),
                  pl.BlockSpec((1, gd), lambda g, i: (0, g))],
        out_specs=pl.BlockSpec((tm, gd), lambda g, i: (i, g)),
        out_shape=jax.ShapeDtypeStruct((t, pw), BF16), name=name,
        compiler_params=_cparams(("arbitrary", "arbitrary"), 40),
    )(proj, proj, proj, pool_w, pool_scale)


def _pool_bwd(dya, proj, pool_w, pool_scale, *, d, tm, name):
    t = proj.shape[0]
    pw = d // 2
    gd = pw // N_POOL_GROUPS
    za_col = pw // gd
    n_i = t // tm

    def body(xa_ref, xp_ref, za_ref, zn_ref, dy_ref, dyn_ref, w_ref, ps_ref,
             dxa_ref, dza_ref, dw_ref, dps_ref):
        g = pl.program_id(0)
        i = pl.program_id(1)
        last = i == n_i - 1
        cur = xa_ref[...].astype(F32)
        mixed = _pool_mixed(_with_prev(xp_ref, cur, i == 0), cur, g, i * tm, tm)
        mixed16 = mixed.astype(BF16)
        ypre = jnp.dot(mixed16, w_ref[...], preferred_element_type=F32)
        za = za_ref[...].astype(F32)
        sg = _sigmoid(za)
        dy = dy_ref[...].astype(F32)
        dza_ref[...] = (dy * (ypre * ps_ref[...]) * (sg * (1.0 + za * (1.0 - sg)))).astype(BF16)
        dysc = dy * (za * sg)
        za_e = _with_next(za, zn_ref, last)
        dy_e = _with_next(dy, dyn_ref, last)
        dypre_e = (dy_e * (za_e * _sigmoid(za_e)) * ps_ref[...]).astype(BF16)
        dm_e = lax.dot_general(dypre_e, w_ref[...], (((1,), (1,)), ((), ())), preferred_element_type=F32)
        rows = tm + HALO
        q = dm_e * _pool_inv_count(g, i * tm, rows, gd)
        f2 = q + pltpu.roll(q, rows - 1, 0)
        f4 = f2 + pltpu.roll(f2, rows - 2, 0)
        f8 = f4 + pltpu.roll(f4, rows - 4, 0)
        f16 = f8 + pltpu.roll(f8, rows - 8, 0)
        ahead = jnp.where(g == 0, f2, jnp.where(g == 1, f4, jnp.where(g == 2, f8, f16)))
        dxa_ref[...] = (ahead[:tm] - dm_e[:tm]).astype(BF16)
        dw = lax.dot_general(mixed16, dypre_e[:tm], (((0,), (0,)), ((), ())), preferred_element_type=F32)
        dps = jnp.sum(dysc * ypre, axis=0, keepdims=True)

        @pl.when(i == 0)
        def _():
            dw_ref[...] = dw
            dps_ref[...] = dps

        @pl.when(i > 0)
        def _():
            dw_ref[...] += dw
            dps_ref[...] += dps

    out_tile = pl.BlockSpec((tm, gd), lambda g, i: (i, g))
    return pl.pallas_call(
        body, grid=(N_POOL_GROUPS, n_i),
        in_specs=[_cur(tm, gd, 0), _prev(tm, gd, 0), _cur(tm, gd, za_col), _next(tm, gd, za_col, t),
                  _cur(tm, gd, 0), _next(tm, gd, 0, t),
                  pl.BlockSpec((None, gd, gd), lambda g, i: (g, 0, 0)),
                  pl.BlockSpec((1, gd), lambda g, i: (0, g))],
        out_specs=[out_tile, out_tile, pl.BlockSpec((None, gd, gd), lambda g, i: (g, 0, 0)),
                   pl.BlockSpec((1, gd), lambda g, i: (0, g))],
        out_shape=[jax.ShapeDtypeStruct((t, pw), BF16), jax.ShapeDtypeStruct((t, pw), BF16),
                   jax.ShapeDtypeStruct((N_POOL_GROUPS, gd, gd), F32), jax.ShapeDtypeStruct((1, pw), F32)],
        name=name, compiler_params=_cparams(("arbitrary", "arbitrary"), 48),
    )(proj, proj, proj, proj, dya, dya, pool_w, pool_scale)


def _conv_cols(d, cw):
    cc = d // 2
    base = (d + 4 * d) // cw
    return base, base + cc // cw, base + 2 * (cc // cw), base + 3 * (cc // cw)


def _conv_fwd(proj, conv_w, *, d, tm, cw, name):
    t = proj.shape[0]
    cc = d // 2
    u_col, b_col, c_col, z_col = _conv_cols(d, cw)

    def body(u_ref, up_ref, c_ref, cp_ref, b_ref, z_ref, w_ref, y_ref):
        i = pl.program_id(1)
        v = u_ref[...].astype(F32) * c_ref[...].astype(F32)
        vprev = up_ref[...].astype(F32) * cp_ref[...].astype(F32)
        ve = jnp.concatenate([jnp.where(i == 0, 0.0, vprev), v], axis=0)
        w = w_ref[...]
        y = w[0:1] * pltpu.roll(ve, 2, 0)[HALO:] + w[1:2] * pltpu.roll(ve, 1, 0)[HALO:] + w[2:3] * v
        z = z_ref[...].astype(F32)
        y_ref[...] = (b_ref[...].astype(F32) * y * (z * _sigmoid(z))).astype(BF16)

    return pl.pallas_call(
        body, grid=(cc // cw, t // tm),
        in_specs=[_cur(tm, cw, u_col), _prev(tm, cw, u_col), _cur(tm, cw, c_col), _prev(tm, cw, c_col),
                  _cur(tm, cw, b_col), _cur(tm, cw, z_col),
                  pl.BlockSpec((8, cw), lambda cb, i: (0, cb))],
        out_specs=pl.BlockSpec((tm, cw), lambda cb, i: (i, cb)),
        out_shape=jax.ShapeDtypeStruct((t, cc), BF16), name=name,
        compiler_params=_cparams(("arbitrary", "arbitrary"), 40),
    )(proj, proj, proj, proj, proj, proj, conv_w)


def _conv_bwd(dyc, proj, conv_w, *, d, tm, cw, name):
    t = proj.shape[0]
    cc = d // 2
    n_i = t // tm
    u_col, b_col, c_col, z_col = _conv_cols(d, cw)

    def body(u_ref, up_ref, c_ref, cp_ref, b_ref, bn_ref, z_ref, zn_ref, dy_ref, dyn_ref, w_ref,
             du_ref, db_ref, dc_ref, dz_ref, dw_ref):
        i = pl.program_id(1)
        last = i == n_i - 1
        u = u_ref[...].astype(F32)
        cg = c_ref[...].astype(F32)
        v = u * cg
        vprev = up_ref[...].astype(F32) * cp_ref[...].astype(F32)
        ve = jnp.concatenate([jnp.where(i == 0, 0.0, vprev), v], axis=0)
        v2 = pltpu.roll(ve, 2, 0)[HALO:]
        v1 = pltpu.roll(ve, 1, 0)[HALO:]
        w = w_ref[...]
        y = w[0:1] * v2 + w[1:2] * v1 + w[2:3] * v
        z = z_ref[...].astype(F32)
        sg = _sigmoid(z)
        bg = b_ref[...].astype(F32)
        dyc_v = dy_ref[...].astype(F32)
        dz_ref[...] = (dyc_v * bg * y * (sg * (1.0 + z * (1.0 - sg)))).astype(BF16)
        db_ref[...] = (dyc_v * y * (z * sg)).astype(BF16)
        ze = _with_next(z, zn_ref, last)
        dy_e = (_with_next(dyc_v, dyn_ref, last) * _with_next(bg, bn_ref, last) * (ze * _sigmoid(ze)))
        rows = tm + HALO
        dy0 = dy_e[:tm]
        dv = (w[2:3] * dy0 + w[1:2] * pltpu.roll(dy_e, rows - 1, 0)[:tm]
              + w[0:1] * pltpu.roll(dy_e, rows - 2, 0)[:tm])
        du_ref[...] = (dv * cg).astype(BF16)
        dc_ref[...] = (dv * u).astype(BF16)
        s0 = jnp.sum(dy0 * v2, axis=0, keepdims=True)
        s1 = jnp.sum(dy0 * v1, axis=0, keepdims=True)
        s2 = jnp.sum(dy0 * v, axis=0, keepdims=True)
        r = lax.broadcasted_iota(jnp.int32, (8, cw), 0)
        dw = jnp.where(r == 0, s0, jnp.where(r == 1, s1, jnp.where(r == 2, s2, 0.0)))

        @pl.when(i == 0)
        def _():
            dw_ref[...] = dw

        @pl.when(i > 0)
        def _():
            dw_ref[...] += dw

    out_tile = pl.BlockSpec((tm, cw), lambda cb, i: (i, cb))
    act = jax.ShapeDtypeStruct((t, cc), BF16)
    return pl.pallas_call(
        body, grid=(cc // cw, n_i),
        in_specs=[_cur(tm, cw, u_col), _prev(tm, cw, u_col), _cur(tm, cw, c_col), _prev(tm, cw, c_col),
                  _cur(tm, cw, b_col), _next(tm, cw, b_col, t), _cur(tm, cw, z_col), _next(tm, cw, z_col, t),
                  _cur(tm, cw, 0), _next(tm, cw, 0, t),
                  pl.BlockSpec((8, cw), lambda cb, i: (0, cb))],
        out_specs=[out_tile, out_tile, out_tile, out_tile, pl.BlockSpec((8, cw), lambda cb, i: (0, cb))],
        out_shape=[act, act, act, act, jax.ShapeDtypeStruct((8, cc), F32)], name=name,
        compiler_params=_cparams(("arbitrary", "arbitrary"), 48),
    )(proj, proj, proj, proj, proj, proj, proj, proj, dyc, dyc, conv_w)


def _sum_matrix(prefix):
    r = lax.broadcasted_iota(jnp.int32, (KEY_CHUNK, KEY_CHUNK), 0)
    c = lax.broadcasted_iota(jnp.int32, (KEY_CHUNK, KEY_CHUNK), 1)
    return jnp.where((r <= c) if prefix else (r > c), 1.0, 0.0).astype(BF16)


def _chunk_sums(val, mat, prefix):
    hi = val.astype(BF16)
    part = jnp.dot(hi, mat, preferred_element_type=F32)
    if prefix:
        return part, part[:, KEY_CHUNK - 1:]
    return part, part[:, :1] + hi[:, :1].astype(F32)


def _stick_logits(qs, kb, strict):
    z = lax.dot_general(qs, kb, (((1,), (1,)), ((), ())), preferred_element_type=F32)
    minus_abs = pltpu.bitcast(pltpu.bitcast(z, jnp.int32) | jnp.int32(-2 ** 31), F32)
    sp = jnp.maximum(z, 0.0) + jnp.log(1.0 + jnp.exp(minus_abs))
    log_beta = z - sp
    if strict is not None:
        sp = jnp.where(strict, sp, 0.0)
    return sp, log_beta


def _running(val, mat, carry, upwards):
    n = val.shape[1] // KEY_CHUNK
    out = [None] * n
    for c in (range(n) if upwards else reversed(range(n))):
        part, total = _chunk_sums(val[:, c * KEY_CHUNK:(c + 1) * KEY_CHUNK], mat, upwards)
        out[c] = part + carry
        carry = carry + total
    return jnp.concatenate(out, axis=1), carry


def _attn_cols(d):
    h = d // HEAD_DIM
    return h, 2 * h, 3 * h, 4 * h


def _sweep(i, per, block, carry, upwards):
    n = i * per
    odd = n % 2 == 1
    same = lambda cr: cr
    if upwards:
        carry = lax.fori_loop(0, n // 2, lambda it, cr: block(2 * it + 1, block(2 * it, cr, False), False), carry)
        carry = lax.cond(odd, lambda cr: block(n - 1, cr, False), same, carry)
        for dgl in range(per):
            carry = block(n + dgl, carry, True)
        return carry
    for dgl in range(per - 1, -1, -1):
        carry = block(n + dgl, carry, True)
    carry = lax.fori_loop(0, n // 2,
                          lambda it, cr: block(n - 2 - 2 * it, block(n - 1 - 2 * it, cr, False), False), carry)
    return lax.cond(odd, lambda cr: block(0, cr, False), same, carry)


def _diagonal_halves(tq):
    assert tq == KEY_BLOCK == 2 * KEY_CHUNK
    return (slice(0, tq // 2), tq // 2), (slice(tq // 2, tq), tq)


def _diagonal_mask(rows, width):
    n = rows.stop - rows.start
    t_idx = rows.start + lax.broadcasted_iota(jnp.int32, (n, width), 0)
    return lax.broadcasted_iota(jnp.int32, (n, width), 1) < t_idx


def _attn_fwd(proj, *, d, tq, name):
    t = proj.shape[0]
    groups = d // HEAD_DIM // HEADS_PER_STEP
    width = HEADS_PER_STEP * HEAD_DIM
    lanes = [slice(s * HEAD_DIM, (s + 1) * HEAD_DIM) for s in range(HEADS_PER_STEP)]
    q_col, k_col, v_col, z_col = (c // HEADS_PER_STEP for c in _attn_cols(d))

    def body(q_ref, k_ref, v_ref, z_ref, o_ref, y_ref, tot_ref):
        i = pl.program_id(1)
        qs = [(q_ref[:, ln].astype(F32) * (HEAD_DIM ** -0.5)).astype(BF16) for ln in lanes]
        mat = _sum_matrix(prefix=False)

        def block(kb, carry, masked):
            off = pl.multiple_of(kb * KEY_BLOCK, KEY_BLOCK)
            out = []
            for s, ln in enumerate(lanes):
                pieces = []
                for rows, width in (_diagonal_halves(tq) if masked else ((slice(0, tq), KEY_BLOCK),)):
                    acc, after = (c[rows] for c in carry[s])
                    strict = _diagonal_mask(rows, width) if masked else None
                    sp, log_beta = _stick_logits(qs[s][rows], k_ref[pl.ds(off, width), ln], strict)
                    between, after = _running(sp, mat, after, upwards=False)
                    a = jnp.exp(log_beta - between)
                    if masked:
                        a = jnp.where(strict, a, 0.0)
                    acc = acc + jnp.dot(a.astype(BF16), v_ref[pl.ds(off, width), ln], preferred_element_type=F32)
                    pieces.append((acc, after))
                out.append(tuple(jnp.concatenate(c, axis=0) for c in zip(*pieces)))
            return tuple(out)

        start = (jnp.zeros((tq, HEAD_DIM), F32), jnp.zeros((tq, 1), F32))
        done = _sweep(i, tq // KEY_BLOCK, block, (start,) * len(lanes), upwards=False)
        for (o, total), ln in zip(done, lanes):
            o_ref[:, ln] = o.astype(BF16)
            tot_ref[:, ln] = jnp.broadcast_to(total, (tq, HEAD_DIM))
            z = z_ref[:, ln].astype(F32)
            y_ref[:, ln] = (o * (z * _sigmoid(z))).astype(BF16)

    tile = lambda col: pl.BlockSpec((tq, width), lambda h, i: (i, col + h))
    full = lambda col: pl.BlockSpec((t, width), lambda h, i: (0, col + h))
    act = jax.ShapeDtypeStruct((t, d), BF16)
    return pl.pallas_call(
        body, grid=(groups, t // tq),
        in_specs=[tile(q_col), full(k_col), full(v_col), tile(z_col)],
        out_specs=[tile(0), tile(0), tile(0)], out_shape=[act, act, jax.ShapeDtypeStruct((t, d), F32)],
        name=name, compiler_params=_cparams(("arbitrary", "arbitrary"), 48),
    )(proj, proj, proj, proj)


def _attn_bwd(dyb, o, tot, proj, *, d, tq, name, after=None):
    t = proj.shape[0]
    groups = d // HEAD_DIM // HEADS_PER_STEP
    width = HEADS_PER_STEP * HEAD_DIM
    lanes = [slice(s * HEAD_DIM, (s + 1) * HEAD_DIM) for s in range(HEADS_PER_STEP)]
    n_i = t // tq
    q_col, k_col, v_col, z_col = (c // HEADS_PER_STEP for c in _attn_cols(d))
    scale = HEAD_DIM ** -0.5

    def body(*refs):
        q_ref, k_ref, v_ref, z_ref, o_ref, tot_ref, dy_ref = refs[:7]
        dq_ref, dk_ref, dv_ref, dz_ref, dk_acc, dv_acc = refs[-6:]
        i = pl.program_id(1)

        @pl.when(i == 0)
        def _():
            dk_acc[...] = jnp.zeros_like(dk_acc)
            dv_acc[...] = jnp.zeros_like(dv_acc)

        qs, do16, all_keep = [], [], []
        for ln in lanes:
            qs.append((q_ref[:, ln].astype(F32) * scale).astype(BF16))
            z = z_ref[:, ln].astype(F32)
            sg = _sigmoid(z)
            dy = dy_ref[:, ln].astype(F32)
            dz_ref[:, ln] = (dy * o_ref[:, ln].astype(F32) * (sg * (1.0 + z * (1.0 - sg)))).astype(BF16)
            do16.append((dy * (z * sg)).astype(BF16))
            all_keep.append(jnp.concatenate([tot_ref[:, ln]] * (KEY_BLOCK // HEAD_DIM), axis=1))
        mat = _sum_matrix(prefix=True)

        def block(kb, carry, masked):
            off = pl.multiple_of(kb * KEY_BLOCK, KEY_BLOCK)
            out = []
            for s, ln in enumerate(lanes):
                pieces = []
                for rows, width in (_diagonal_halves(tq) if masked else ((slice(0, tq), KEY_BLOCK),)):
                    dq, sp_seen, g_seen = (c[rows] for c in carry[s])
                    kblk = k_ref[pl.ds(off, width), ln]
                    vblk = v_ref[pl.ds(off, width), ln]
                    strict = _diagonal_mask(rows, width) if masked else None
                    sp, log_beta = _stick_logits(qs[s][rows], kblk, strict)
                    sp_upto, sp_seen = _running(sp, mat, sp_seen, upwards=True)
                    a = jnp.exp(log_beta - (all_keep[s][rows, :width] - sp_upto))
                    if masked:
                        a = jnp.where(strict, a, 0.0)
                    g = a * lax.dot_general(do16[s][rows], vblk, (((1,), (1,)), ((), ())),
                                            preferred_element_type=F32)
                    g_upto, g_seen = _running(g, mat, g_seen, upwards=True)
                    dz_ts = g - jnp.exp(log_beta) * g_upto
                    if masked:
                        dz_ts = jnp.where(strict, dz_ts, 0.0)
                    dz16 = dz_ts.astype(BF16)
                    dq = dq + jnp.dot(dz16, kblk, preferred_element_type=F32)
                    dk_acc[pl.ds(off, width), ln] += lax.dot_general(
                        dz16, qs[s][rows], (((0,), (0,)), ((), ())), preferred_element_type=F32)
                    dv_acc[pl.ds(off, width), ln] += lax.dot_general(
                        a.astype(BF16), do16[s][rows], (((0,), (0,)), ((), ())), preferred_element_type=F32)
                    pieces.append((dq, sp_seen, g_seen))
                out.append(tuple(jnp.concatenate(c, axis=0) for c in zip(*pieces)))
            return tuple(out)

        column = jnp.zeros((tq, 1), F32)
        start = (jnp.zeros((tq, HEAD_DIM), F32), column, column)
        done = _sweep(i, tq // KEY_BLOCK, block, (start,) * len(lanes), upwards=True)
        for (dq, _, _), ln in zip(done, lanes):
            dq_ref[:, ln] = (dq * scale).astype(BF16)

        @pl.when(i == n_i - 1)
        def _():
            dk_ref[...] = dk_acc[...].astype(BF16)
            dv_ref[...] = dv_acc[...].astype(BF16)

    tile = lambda col: pl.BlockSpec((tq, width), lambda h, i: (i, col + h))
    full = lambda col: pl.BlockSpec((t, width), lambda h, i: (0, col + h))
    act = jax.ShapeDtypeStruct((t, d), BF16)
    in_specs = [tile(q_col), full(k_col), full(v_col), tile(z_col), tile(0), tile(0), tile(0)]
    args = [proj, proj, proj, proj, o, tot, dyb]
    if after is not None:
        in_specs.append(pl.BlockSpec(memory_space=pl.ANY))
        args.append(after)
    return pl.pallas_call(
        body, grid=(groups, n_i), in_specs=in_specs,
        out_specs=[tile(0), full(0), full(0), tile(0)], out_shape=[act, act, act, act],
        scratch_shapes=[pltpu.VMEM((t, width), F32), pltpu.VMEM((t, width), F32)], name=name,
        compiler_params=_cparams(("arbitrary", "arbitrary"), 56),
    )(*args)


def _merge_fwd(gl, bst, *, d, tm, tn, name):
    t = gl.shape[0]
    nk = d // tn

    def body(g0, g1, g2, b0, b1, b2, m_ref):
        acc = _sigmoid(g0[...].astype(F32)) * b0[...].astype(F32)
        acc += _sigmoid(g1[...].astype(F32)) * b1[...].astype(F32)
        acc += _sigmoid(g2[...].astype(F32)) * b2[...].astype(F32)
        m_ref[...] = acc.astype(BF16)

    spec = lambda b: pl.BlockSpec((tm, tn), lambda i, kb: (i, b * nk + kb))
    return pl.pallas_call(
        body, grid=(t // tm, nk), in_specs=[spec(0), spec(1), spec(2)] * 2,
        out_specs=pl.BlockSpec((tm, tn), lambda i, kb: (i, kb)),
        out_shape=jax.ShapeDtypeStruct((t, d), BF16), name=name,
        compiler_params=_cparams(("arbitrary", "arbitrary"), 40),
    )(gl, gl, gl, bst, bst, bst)


def _merge_bwd(dmerged, gl, bst, *, d, tm, tn, name):
    t = gl.shape[0]
    nk = d // tn

    def body(dm_ref, g_ref, b_ref, db_ref, dg_ref, dbias_ref):
        i = pl.program_id(2)
        dm = dm_ref[...].astype(F32)
        sg = _sigmoid(g_ref[...].astype(F32))
        db_ref[...] = (dm * sg).astype(BF16)
        dgl = dm * b_ref[...].astype(F32) * (sg * (1.0 - sg))
        dg_ref[...] = dgl.astype(BF16)
        dbias = jnp.sum(dgl, axis=0, keepdims=True)

        @pl.when(i == 0)
        def _():
            dbias_ref[...] = dbias

        @pl.when(i > 0)
        def _():
            dbias_ref[...] += dbias

    wide = pl.BlockSpec((tm, tn), lambda b, kb, i: (i, b * nk + kb))
    act = jax.ShapeDtypeStruct((t, 3 * d), BF16)
    return pl.pallas_call(
        body, grid=(3, nk, t // tm),
        in_specs=[pl.BlockSpec((tm, tn), lambda b, kb, i: (i, kb)), wide, wide],
        out_specs=[wide, wide, pl.BlockSpec((1, tn), lambda b, kb, i: (0, b * nk + kb))],
        out_shape=[act, act, jax.ShapeDtypeStruct((1, 3 * d), F32)], name=name,
        compiler_params=_cparams(("arbitrary", "arbitrary", "arbitrary"), 40),
    )(dmerged, gl, bst)


def _adamw(w, m, v, parts, layer, prev, *, tr, name):
    depth, r, c = w.shape
    n = parts.shape[0]

    def body(*refs):
        w_ref, m_ref, v_ref, p_ref = refs[:4]
        g_ref, d_ref, nm_ref, nv_ref = refs[-4:]
        g = p_ref[0].astype(F32)
        for s in range(1, n):
            g = g + p_ref[s].astype(F32)
        wv = w_ref[...]
        nm = ADAM_B1 * m_ref[...] + (1.0 - ADAM_B1) * g
        nv = ADAM_B2 * v_ref[...] + (1.0 - ADAM_B2) * (g * g)
        m_hat = nm / (1.0 - ADAM_B1 ** ADAM_STEP)
        v_hat = nv / (1.0 - ADAM_B2 ** ADAM_STEP)
        g_ref[...] = g
        nm_ref[...] = nm
        nv_ref[...] = nv
        d_ref[...] = -ADAM_LR * (m_hat / (jnp.sqrt(v_hat) + ADAM_EPS) + ADAM_WD * wv)

    tile = pl.BlockSpec((None, tr, c), lambda i: (layer, i, 0))
    in_specs = [tile, tile, tile, pl.BlockSpec((n, tr, c), lambda i: (0, i, 0))]
    args = [w, m, v, parts]
    aliases = {}
    if prev is not None:
        in_specs += [pl.BlockSpec(memory_space=pl.ANY)] * 4
        aliases = {4 + k: k for k in range(4)}
        args += list(prev)
    full = jax.ShapeDtypeStruct((depth, r, c), F32)
    return pl.pallas_call(
        body, grid=(r // tr,), in_specs=in_specs, out_specs=[tile] * 4, out_shape=[full] * 4,
        input_output_aliases=aliases, name=name, compiler_params=_cparams(("arbitrary",), 48),
    )(*args)


def _adamw_layers(w, m, v, parts_by_layer, *, tr, name):
    lead = w.shape[0]
    flat = lambda a: a.reshape(lead, -1, a.shape[-1])
    w, m, v = flat(w), flat(m), flat(v)
    out = None
    for layer in reversed(range(lead)):
        parts = parts_by_layer[layer]
        parts = parts.reshape(parts.shape[0], -1, parts.shape[-1])
        out = _adamw(w, m, v, parts, layer, out, tr=min(tr, w.shape[1]), name=f"{name}_l{layer}")
    return out


def _place():
    return lax.axis_index("x"), lax.axis_index("y"), lax.axis_index("c")


def _index(px, py, pc):
    return 4 * px + 2 * py + pc


def _whole(ref, idx):
    return ref.at[idx]


def _row_block(rows):
    return lambda ref, idx: ref.at[:, pl.ds(idx * rows, rows), :]


def _lane_block(cols):
    return lambda ref, idx: ref.at[:, pl.ds(idx * cols, cols)]


def _all_gather(shards, out_shapes, placers, *, name):
    n = len(shards)

    def body(*refs):
        src, out = refs[:n], refs[n:2 * n]
        send_sems, recv_sems, local_sems = refs[2 * n:]
        x, y, c = _place()
        me, sibling = (x, y, c), (x, y, 1 - c)
        chips = [(1 - x, y), (x, 1 - y), (1 - x, 1 - y)]

        def copy(a, k, block, to, from_shard=False):
            window = placers[a](out[a], _index(*block))
            return pltpu.make_async_remote_copy(
                src_ref=src[a] if from_shard else window, dst_ref=window,
                send_sem=send_sems.at[a * 7 + k], recv_sem=recv_sems.at[a * 7 + k],
                device_id=to, device_id_type=MESH)

        mine = [pltpu.make_async_copy(src[a], placers[a](out[a], _index(*me)), local_sems.at[a])
                for a in range(n)]
        started = []
        for a in range(n):
            mine[a].start()
            started.append(copy(a, 0, me, sibling, True))
            started += [copy(a, 1 + j, me, (*chip, c), True) for j, chip in enumerate(chips)]
        for cp in started:
            cp.start()
        for j, chip in enumerate(chips):
            for a in range(n):
                copy(a, 1 + j, (*chip, c), me).wait_recv()
                passed = copy(a, 4 + j, (*chip, c), sibling)
                passed.start()
                started.append(passed)
        for a in range(n):
            copy(a, 0, sibling, me).wait_recv()
            for j, chip in enumerate(chips):
                copy(a, 4 + j, (*chip, 1 - c), me).wait_recv()
        for cp in started:
            cp.wait_send()
        for cp in mine:
            cp.wait()

    hbm = pl.BlockSpec(memory_space=pl.ANY)
    return pl.pallas_call(
        body, in_specs=[hbm] * n, out_specs=[hbm] * n, out_shape=out_shapes,
        scratch_shapes=[pltpu.SemaphoreType.DMA((7 * n,)), pltpu.SemaphoreType.DMA((7 * n,)),
                        pltpu.SemaphoreType.DMA((n,))],
        name=name,
    )(*shards)


def _scatter_parts(grads, pickers, part_shapes, *, name):
    n = len(grads)

    def body(*refs):
        src, out = refs[:n], refs[n:2 * n]
        send_sems, recv_sems, local_sems = refs[2 * n:]
        x, y, c = _place()
        my_idx = _index(x, y, c)
        peers = [(x ^ (k >> 2), y ^ ((k >> 1) & 1), c ^ (k & 1)) for k in range(1, N_DEV)]

        def copy(a, k, peer):
            return pltpu.make_async_remote_copy(
                src_ref=pickers[a](src[a], _index(*peer)), dst_ref=out[a].at[my_idx],
                send_sem=send_sems.at[a * 7 + k], recv_sem=recv_sems.at[a * 7 + k],
                device_id=peer, device_id_type=MESH)

        def arrival(a, k, peer):
            return pltpu.make_async_remote_copy(
                src_ref=pickers[a](src[a], my_idx), dst_ref=out[a].at[_index(*peer)],
                send_sem=send_sems.at[a * 7 + k], recv_sem=recv_sems.at[a * 7 + k],
                device_id=peer, device_id_type=MESH)

        mine = [pltpu.make_async_copy(pickers[a](src[a], my_idx), out[a].at[my_idx], local_sems.at[a])
                for a in range(n)]
        sent = [copy(a, k, peer) for a in range(n) for k, peer in enumerate(peers)]
        for cp in mine + sent:
            cp.start()
        for a in range(n):
            for k, peer in enumerate(peers):
                arrival(a, k, peer).wait_recv()
        for cp in sent:
            cp.wait_send()
        for cp in mine:
            cp.wait()

    hbm = pl.BlockSpec(memory_space=pl.ANY)
    return pl.pallas_call(
        body, in_specs=[hbm] * n, out_specs=[hbm] * n, out_shape=part_shapes,
        scratch_shapes=[pltpu.SemaphoreType.DMA((7 * n,)), pltpu.SemaphoreType.DMA((7 * n,)),
                        pltpu.SemaphoreType.DMA((n,))],
        name=name,
    )(*grads)


def _peers(x, y, c):
    return [(x ^ (k >> 2), y ^ ((k >> 1) & 1), c ^ (k & 1)) for k in range(1, N_DEV)]


_HBM = pl.BlockSpec(memory_space=pltpu.HBM)
_SEM = pl.BlockSpec(memory_space=pltpu.SEMAPHORE)
_EFFECT = pltpu.SideEffectType.DATAFLOW_SIDE_EFFECTING


def _exchange_start(srcs, land_shapes, src_win, dst_win, after, *, name):
    n = len(srcs)

    def body(*refs):
        src, land = refs[:n], refs[n:2 * n]
        send_sems, recv_sems, token = refs[2 * n + len(after)], refs[2 * n + len(after) + 1], refs[-1]
        x, y, c = _place()
        my_idx = _index(x, y, c)
        for a in range(n):
            for k, peer in enumerate(_peers(x, y, c)):
                pltpu.make_async_remote_copy(
                    src_ref=src_win[a](src[a], _index(*peer)), dst_ref=dst_win[a](land[a], my_idx),
                    send_sem=send_sems.at[a * 7 + k], recv_sem=recv_sems.at[a * 7 + k],
                    device_id=peer, device_id_type=MESH).start()
        token[...] = jnp.zeros_like(token)

    hbm = lambda a: pltpu.with_memory_space_constraint(a, pltpu.HBM)
    lands = [hbm(lax.empty(s.shape, s.dtype)) for s in land_shapes]
    out = pl.pallas_call(
        body, name=name,
        out_shape=(pltpu.SemaphoreType.DMA((7 * n,)), pltpu.SemaphoreType.DMA((7 * n,)),
                   *[pltpu.HBM(a.shape, a.dtype) for a in lands], jax.ShapeDtypeStruct((8, LANE), F32)),
        in_specs=[_HBM] * (2 * n) + [pl.BlockSpec(memory_space=pl.ANY)] * len(after),
        out_specs=(_SEM, _SEM, *([_HBM] * n), pl.BlockSpec(memory_space=pltpu.VMEM)),
        input_output_aliases={n + i: 2 + i for i in range(n)},
        compiler_params=pltpu.CompilerParams(has_side_effects=_EFFECT),
    )(*srcs, *lands, *after)
    return out[0], out[1], list(out[2:2 + n]), out[-1][0, 0]


def _exchange_wait(send_sems, recv_sems, srcs, lands, src_win, dst_win, after, *, name):
    n = len(srcs)

    def body(*refs):
        src, land = refs[:n], refs[n:2 * n]
        send_sems, recv_sems = refs[2 * n], refs[2 * n + 1]
        x, y, c = _place()
        for a in range(n):
            for k, peer in enumerate(_peers(x, y, c)):
                sender = _index(*peer)
                copy = pltpu.make_async_remote_copy(
                    src_ref=src_win[a](src[a], sender), dst_ref=dst_win[a](land[a], sender),
                    send_sem=send_sems.at[a * 7 + k], recv_sem=recv_sems.at[a * 7 + k],
                    device_id=peer, device_id_type=MESH)
                copy.wait_send()
                copy.wait_recv()

    out = pl.pallas_call(
        body, name=name, out_shape=tuple(pltpu.HBM(a.shape, a.dtype) for a in lands),
        in_specs=[_HBM] * (2 * n) + [_SEM, _SEM] + [pl.BlockSpec(memory_space=pl.ANY)] * len(after),
        out_specs=tuple([_HBM] * n),
        input_output_aliases={n + i: i for i in range(n)},
        compiler_params=pltpu.CompilerParams(has_side_effects=_EFFECT),
    )(*srcs, *lands, send_sems, recv_sems, *after)
    return list(out)


COPY_TILE_BYTES = 2 << 20


def _copy_own_block(src, land, idx, *, from_stack, name):
    k, n = land.shape[1:]
    tk = k
    while tk * n * land.dtype.itemsize > COPY_TILE_BYTES and tk % 32 == 0:
        tk //= 2

    def body(idx_ref, src_ref, land_ref, out_ref):
        out_ref[...] = src_ref[...]

    own = pl.BlockSpec((None, tk, n), lambda i, idx_ref: (idx_ref[0], i, 0))
    grid_spec = pltpu.PrefetchScalarGridSpec(
        num_scalar_prefetch=1, grid=(k // tk,),
        in_specs=[own if from_stack else pl.BlockSpec((tk, n), lambda i, idx_ref: (i, 0)),
                  pl.BlockSpec(memory_space=pl.ANY)],
        out_specs=own)
    return pl.pallas_call(
        body, grid_spec=grid_spec, out_shape=jax.ShapeDtypeStruct(land.shape, land.dtype),
        input_output_aliases={2: 0}, name=name,
    )(idx, src, land)


def _place_own(srcs, lands, src_win, dst_win, *, name):
    lands = list(lands)
    idx = _index(*_place()).astype(jnp.int32).reshape(1)
    rest = []
    for a in range(len(srcs)):
        if dst_win[a] is _whole and src_win[a] in (_whole, _shard_itself) and lands[a].ndim == 3:
            lands[a] = _copy_own_block(srcs[a], lands[a], idx, from_stack=src_win[a] is _whole,
                                       name=f"{name}_{a}")
        else:
            rest.append(a)
    if rest:
        placed = _dma_own([srcs[a] for a in rest], [lands[a] for a in rest], [src_win[a] for a in rest],
                          [dst_win[a] for a in rest], name=f"{name}_dma")
        for a, land in zip(rest, placed):
            lands[a] = land
    return lands


def _dma_own(srcs, lands, src_win, dst_win, *, name):
    n = len(srcs)

    def body(*refs):
        src, land = refs[:n], refs[n:2 * n]
        sems = refs[-1]
        my_idx = _index(*_place())
        copies = [pltpu.make_async_copy(src_win[a](src[a], my_idx), dst_win[a](land[a], my_idx), sems.at[a])
                  for a in range(n)]
        for cp in copies:
            cp.start()
        for cp in copies:
            cp.wait()

    hbm = pl.BlockSpec(memory_space=pl.ANY)
    return list(pl.pallas_call(
        body, name=name, in_specs=[hbm] * (2 * n), out_specs=[hbm] * n,
        out_shape=[jax.ShapeDtypeStruct(a.shape, a.dtype) for a in lands],
        input_output_aliases={n + i: i for i in range(n)},
        scratch_shapes=[pltpu.SemaphoreType.DMA((n,))],
    )(*srcs, *lands))


def _ada_fwd(c8, w_ada, b_ada8, *, name):
    depth, d, n = w_ada.shape

    def body(c_ref, w_ref, b_ref, mod_ref, sc_ref, c_all, prod, got, send_sems, recv_sems):
        x, y, c = _place()
        my_idx = _index(x, y, c)
        peers = [(x ^ (k >> 2), y ^ ((k >> 1) & 1), c ^ (k & 1)) for k in range(1, N_DEV)]

        def slab(ref, idx):
            return ref.at[pl.ds(pl.multiple_of(idx * 8, 8), 8)]

        def c_copy(k, peer, sender):
            return pltpu.make_async_remote_copy(
                src_ref=c_ref, dst_ref=slab(c_all, sender), send_sem=send_sems.at[k],
                recv_sem=recv_sems.at[k], device_id=peer, device_id_type=MESH)

        def m_copy(k, peer, sender):
            return pltpu.make_async_remote_copy(
                src_ref=prod.at[_index(*peer)], dst_ref=got.at[sender], send_sem=send_sems.at[7 + k],
                recv_sem=recv_sems.at[7 + k], device_id=peer, device_id_type=MESH)

        sends = [c_copy(k, peer, my_idx) for k, peer in enumerate(peers)]
        for cp in sends:
            cp.start()
        slab(c_all, my_idx)[...] = c_ref[...]
        for k, peer in enumerate(peers):
            c_copy(k, peer, _index(*peer)).wait_recv()
        cv = c_all[...]
        sc = cv * _sigmoid(cv)
        sc_ref[...] = sc
        for layer in range(depth):
            p = jnp.dot(sc, w_ref[layer], preferred_element_type=F32, precision=lax.Precision.HIGHEST)
            for s in range(N_DEV):
                prod[s, layer] = p[8 * s:8 * s + 8]
        sends2 = [m_copy(k, peer, my_idx) for k, peer in enumerate(peers)]
        for cp in sends2:
            cp.start()
        got[my_idx] = prod[my_idx]
        for k, peer in enumerate(peers):
            m_copy(k, peer, _index(*peer)).wait_recv()
        for layer in range(depth):
            for s in range(N_DEV):
                mod_ref[layer, :, s * n:(s + 1) * n] = got[s, layer] + b_ref[layer, :, s * n:(s + 1) * n]
        for cp in sends + sends2:
            cp.wait_send()

    vmem = pl.BlockSpec(memory_space=pltpu.VMEM)
    return pl.pallas_call(
        body, in_specs=[vmem] * 3, out_specs=[vmem] * 2,
        out_shape=[jax.ShapeDtypeStruct((depth, 8, N_DEV * n), F32),
                   jax.ShapeDtypeStruct((8 * N_DEV, d), F32)],
        scratch_shapes=[pltpu.VMEM((8 * N_DEV, d), F32), pltpu.VMEM((N_DEV, depth, 8, n), F32),
                        pltpu.VMEM((N_DEV, depth, 8, n), F32),
                        pltpu.SemaphoreType.DMA((14,)), pltpu.SemaphoreType.DMA((14,))],
        name=name, compiler_params=pltpu.CompilerParams(vmem_limit_bytes=56 << 20),
    )(c8, w_ada, b_ada8)


def _ada_bwd(sc8, dmod, *, name):
    depth, _, n = dmod.shape
    d = sc8.shape[1]

    def body(sc_ref, dm_ref, o_ref):
        for layer in range(depth):
            o_ref[layer] = lax.dot_general(sc_ref[...], dm_ref[layer], (((0,), (0,)), ((), ())),
                                           preferred_element_type=F32, precision=lax.Precision.HIGHEST)

    vmem = pl.BlockSpec(memory_space=pltpu.VMEM)
    return pl.pallas_call(
        body, in_specs=[vmem] * 2, out_specs=vmem, out_shape=jax.ShapeDtypeStruct((depth, d, n), F32),
        name=name, compiler_params=pltpu.CompilerParams(vmem_limit_bytes=40 << 20),
    )(sc8, dmod)


def _gather_small(vec, *, name):
    r = vec.shape[0]

    def body(v_ref, o_ref, send_sems, recv_sems):
        x, y, c = _place()
        my_idx = _index(x, y, c)
        peers = [(x ^ (k >> 2), y ^ ((k >> 1) & 1), c ^ (k & 1)) for k in range(1, N_DEV)]

        def copy(k, peer, sender):
            return pltpu.make_async_remote_copy(
                src_ref=v_ref, dst_ref=o_ref.at[sender], send_sem=send_sems.at[k],
                recv_sem=recv_sems.at[k], device_id=peer, device_id_type=MESH)

        sends = [copy(k, peer, my_idx) for k, peer in enumerate(peers)]
        for cp in sends:
            cp.start()
        o_ref[my_idx] = v_ref[...]
        for k, peer in enumerate(peers):
            copy(k, peer, _index(*peer)).wait_recv()
        for cp in sends:
            cp.wait_send()

    vmem = pl.BlockSpec(memory_space=pltpu.VMEM)
    return pl.pallas_call(
        body, in_specs=[vmem], out_specs=vmem, out_shape=jax.ShapeDtypeStruct((N_DEV, r, LANE), F32),
        scratch_shapes=[pltpu.SemaphoreType.DMA((7,)), pltpu.SemaphoreType.DMA((7,))], name=name,
    )(vec)


TM = 512
TM_WIDE = 2048
TMM_DEEP = 1024
TQ = 512


def _layer_fwd(x, mod, small, gw, *, d, tag, more_weights=None):
    shift, scale, res_gate = mod
    h = _prenorm(x, small["norm_g"], scale, shift, tm=TM, name=f"prenorm_{tag}")
    proj = _mm_nn(h, gw["w_in"], out_dtype=BF16, tm=TM, name=f"in_proj_{tag}")
    ya = _pool_fwd(proj, gw["pool_w"], small["pool_scale"], d=d, tm=TM, name=f"pool_fwd_{tag}")
    o, yb, tot = _attn_fwd(proj, d=d, tq=TQ, name=f"attn_fwd_{tag}")
    yc = _conv_fwd(proj, gw["conv_w"], d=d, tm=TM, cw=LANE * 2, name=f"conv_fwd_{tag}")
    if more_weights is not None:
        gw = dict(gw, **more_weights(o))
    gl = _mm_nn(h, gw["w_gate"], out_dtype=BF16, tm=TM, bias=small["b_gate"], name=f"gate_proj_{tag}")
    nblk = gw["w_br_a"].shape[0]
    bst = _mm_nn(ya, gw["w_br_a"], out_dtype=BF16, tm=TM_WIDE, out_cols=3 * d, name=f"branch_a_{tag}")
    bst = _mm_nn(yb, gw["w_br_b"], out_dtype=BF16, tm=TM, out=bst, out_col_block=1, name=f"branch_b_{tag}")
    bst = _mm_nn(yc, gw["w_br_c"], out_dtype=BF16, tm=TM_WIDE, out=bst, out_col_block=2 * nblk,
                 name=f"branch_c_{tag}")
    merged = _merge_fwd(gl, bst, d=d, tm=TM, tn=TM, name=f"merge_fwd_{tag}")
    x_new, mo = _out_fwd(merged, gw["w_out"][0], x, res_gate, tm=TM, name=f"out_fwd_{tag}")
    return x_new, dict(x=x, h=h, proj=proj, gl=gl, ya=ya, yb=yb, yc=yc, o=o, tot=tot, bst=bst, merged=merged,
                       mo=mo), gw


EARLY = ("w_gate", "w_br_a", "w_br_b", "w_br_c", "w_out")
LATE = ("w_in", "pool_w")


def _layer_bwd(dout, sv, mod, small, gw, *, d, tag, send_early=None, send_late=None):
    shift, scale, res_gate = mod
    nblk = gw["w_br_a"].shape[0]
    dmo, d_res_gate = _out_bwd(dout, sv["mo"], res_gate, tm=TM, name=f"out_bwd_{tag}")
    g_w_out = _mm_tn(sv["merged"], dmo, nb=1, n=d, tk=TM, tmm=TMM_DEEP, name=f"dw_out_{tag}")
    dmerged = _mm_nt(dmo, gw["w_out"], out_dtype=BF16, tm=TM, name=f"dmerged_{tag}")
    dbst, dgl, d_b_gate = _merge_bwd(dmerged, sv["gl"], sv["bst"], d=d, tm=TM, tn=TM, name=f"merge_bwd_{tag}")
    g_w_gate = _mm_tn(sv["h"], dgl, nb=nblk, n=3 * d // nblk, tk=d // 2, tmm=TMM_DEEP,
                      name=f"dw_gate_{tag}")
    g_w_br_a = _mm_tn(sv["ya"], dbst, nb=nblk, n=d // nblk, tk=d // 2, tmm=TM_WIDE,
                      name=f"dw_br_a_{tag}")
    g_w_br_b = _mm_tn(sv["yb"], dbst, nb=1, n=d, tk=TM, tmm=TMM_DEEP, col_block=1, name=f"dw_br_b_{tag}")
    g_w_br_c = _mm_tn(sv["yc"], dbst, nb=nblk, n=d // nblk, tk=d // 2, tmm=TM_WIDE, col_block=2 * nblk,
                      name=f"dw_br_c_{tag}")
    dya = _mm_nt(dbst, gw["w_br_a"], out_dtype=BF16, tm=TM_WIDE, name=f"dya_{tag}")
    dyb = _mm_nt(dbst, gw["w_br_b"], out_dtype=BF16, tm=TM, col_block=1, name=f"dyb_{tag}")
    dyc = _mm_nt(dbst, gw["w_br_c"], out_dtype=BF16, tm=TM_WIDE, col_block=2 * nblk, name=f"dyc_{tag}")
    big = dict(w_gate=g_w_gate, w_br_a=g_w_br_a, w_br_b=g_w_br_b, w_br_c=g_w_br_c, w_out=g_w_out)
    early_sent = None
    if send_early is not None:
        token = send_early(big)
        small = dict(small, pool_scale=small["pool_scale"] + token, norm_g=small["norm_g"] + token)
        early_sent = token.reshape(1, 1)
    dxa, dza, g_pool_w, d_pool_scale = _pool_bwd(dya, sv["proj"], gw["pool_w"], small["pool_scale"], d=d, tm=TM,
                                                 name=f"pool_bwd_{tag}")
    dq, dk, dv, dzb = _attn_bwd(dyb, sv["o"], sv["tot"], sv["proj"], d=d, tq=TQ, name=f"attn_bwd_{tag}",
                                after=early_sent)
    du, dbg, dcg, dzc, d_conv_w = _conv_bwd(dyc, sv["proj"], gw["conv_w"], d=d, tm=TM, cw=LANE * 2,
                                            name=f"conv_bwd_{tag}")
    dproj = jnp.concatenate([dxa, dza, dq, dk, dv, dzb, du, dbg, dcg, dzc], axis=1)
    g_w_in = _mm_tn(sv["h"], dproj, nb=nblk, n=7 * d // nblk, tk=d // 2, tmm=TMM_DEEP,
                    name=f"dw_in_{tag}")
    big.update(w_in=g_w_in, pool_w=g_pool_w.astype(BF16))
    late_sent = send_late(big).reshape(1, 1) if send_late is not None else None
    dh = _mm_nt_pair(dproj, gw["w_in"], dgl, gw["w_gate"], tm=TM, after=late_sent, name=f"dh_{tag}")
    dx, d_norm_g, d_scale, d_shift = _prenorm_bwd(dh, sv["x"], dout, small["norm_g"], scale, tm=TM,
                                                  name=f"prenorm_bwd_{tag}")
    part = dict(norm_g=d_norm_g, mod=jnp.concatenate([d_shift, d_scale, d_res_gate], axis=1),
                pool_scale=d_pool_scale, b_gate=d_b_gate, conv_w=d_conv_w[:3])
    return dx, big, part


BIG = ("w_in", "w_gate", "w_br_a", "w_br_b", "w_br_c", "w_out", "pool_w")
SMALL_PER_LAYER = ("norm_g", "mod", "pool_scale", "b_gate", "conv_w")


def _gather_layer(w, layer, *, d):
    nd = N_DEV
    gd = d // 2 // N_POOL_GROUPS
    cc = d // 2
    shards = [w[k][layer].astype(BF16) for k in BIG]
    shards.append(jnp.pad(w["conv_w"][layer], ((0, 5), (0, 0))))
    blocks = [jax.ShapeDtypeStruct((nd,) + s.shape, BF16) for s in shards[:6]]
    blocks.append(jax.ShapeDtypeStruct((N_POOL_GROUPS, gd, gd), BF16))
    blocks.append(jax.ShapeDtypeStruct((8, cc), F32))
    placers = [_whole] * 6 + [_row_block(gd // nd), _lane_block(cc // nd)]
    return shards, blocks, placers


GATHERED = BIG + ("conv_w",)
FIRST = (0, 6, 7)
REST = (1, 2, 3, 4, 5)


def _as_weights(got, d, which=tuple(range(len(GATHERED)))):
    gw = dict(zip([GATHERED[i] for i in which], got))
    for k in ("w_br_b", "w_out"):
        if k in gw:
            gw[k] = gw[k].reshape(1, d, d)
    return gw


def _shard_itself(ref, idx):
    return ref


def _scatter_spec(big, keys, *, d):
    nd = N_DEV
    gd = d // 2 // N_POOL_GROUPS
    grads, pickers, shapes = [], [], []
    for k in keys:
        g = big[k]
        if k == "pool_w":
            pickers.append(_row_block(gd // nd))
            shapes.append(jax.ShapeDtypeStruct((nd, N_POOL_GROUPS, gd // nd, gd), BF16))
        else:
            if g.shape[0] == 1:
                g = g.reshape(nd, g.shape[1] // nd, g.shape[2])
            pickers.append(_whole)
            shapes.append(jax.ShapeDtypeStruct(g.shape, BF16))
        grads.append(g)
    return grads, pickers, shapes


class _Behind:
    def __init__(self, srcs, land_shapes, src_win, dst_win, after, name):
        srcs = [pltpu.with_memory_space_constraint(s, pltpu.HBM) for s in srcs]
        self.srcs, self.src_win, self.dst_win, self.name = srcs, src_win, dst_win, name
        self.send, self.recv, self.lands, self.token = _exchange_start(
            srcs, land_shapes, src_win, dst_win, after, name=f"{name}_start")

    def finish(self, after):
        lands = _exchange_wait(self.send, self.recv, self.srcs, self.lands, self.src_win, self.dst_win,
                               after, name=f"{self.name}_wait")
        return _place_own(self.srcs, lands, self.src_win, self.dst_win, name=f"{self.name}_own")


def _pack(pieces):
    return jnp.concatenate([p.reshape(-1, LANE) for p in pieces], axis=0)


def kernel(x, c, norm_g, w_ada, b_ada, w_in, pool_w, pool_scale, conv_w, w_br_a, w_br_b, w_br_c, w_gate, b_gate, w_out, final_g, loss_target, m_norm_g, m_w_ada, m_b_ada, m_w_in, m_pool_w, m_pool_scale, m_conv_w, m_w_br_a, m_w_br_b, m_w_br_c, m_w_gate, m_b_gate, m_w_out, m_final_g, v_norm_g, v_w_ada, v_b_ada, v_w_in, v_pool_w, v_pool_scale, v_conv_w, v_w_br_a, v_w_br_b, v_w_br_c, v_w_gate, v_b_gate, v_w_out, v_final_g):
    names = ("norm_g", "w_ada", "b_ada", "w_in", "pool_w", "pool_scale", "conv_w", "w_br_a", "w_br_b",
             "w_br_c", "w_gate", "b_gate", "w_out", "final_g")
    w = dict(zip(names, (norm_g, w_ada, b_ada, w_in, pool_w, pool_scale, conv_w, w_br_a, w_br_b, w_br_c,
                         w_gate, b_gate, w_out, final_g)))
    m = dict(zip(names, (m_norm_g, m_w_ada, m_b_ada, m_w_in, m_pool_w, m_pool_scale, m_conv_w, m_w_br_a,
                         m_w_br_b, m_w_br_c, m_w_gate, m_b_gate, m_w_out, m_final_g)))
    v = dict(zip(names, (v_norm_g, v_w_ada, v_b_ada, v_w_in, v_pool_w, v_pool_scale, v_conv_w, v_w_br_a,
                         v_w_br_b, v_w_br_c, v_w_gate, v_b_gate, v_w_out, v_final_g)))
    _, t, d = x.shape
    depth = norm_g.shape[0]
    cc = d // 2
    my_idx = _index(*_place())

    mod8, sc_all = _ada_fwd(jnp.broadcast_to(c, (8, d)), w_ada,
                            jnp.broadcast_to(b_ada[:, None, :], (depth, 8, 3 * d)), name="ada_fwd")
    mods = [tuple(mod8[l, 0:1, k * d:(k + 1) * d] for k in range(3)) for l in range(depth)]
    small = [dict(norm_g=norm_g[l][None], b_gate=b_gate[l][None], pool_scale=pool_scale[l][None])
             for l in range(depth)]

    shards, blocks, placers = _gather_layer(w, 0, d=d)
    pick = lambda seq, which: [seq[i] for i in which]
    first = _all_gather(pick(shards, FIRST), pick(blocks, FIRST), pick(placers, FIRST), name="gather_first_l0")
    rest = _Behind(pick(shards, REST), pick(blocks, REST), [_shard_itself] * len(REST), pick(placers, REST),
                   (mod8, first[-1]), "gather_rest_l0")
    gws = [_as_weights(first, d, FIRST)]
    order = rest.token
    xs = x[0]
    saved = []
    for l in range(depth):
        coming = None
        if l + 1 < depth:
            shards, blocks, placers = _gather_layer(w, l + 1, d=d)
            coming = []
            for which, part in ((FIRST, "first"), (REST, "rest")):
                coming.append(_Behind(pick(shards, which), pick(blocks, which), [_shard_itself] * len(which),
                                      pick(placers, which), (mod8, gws[l]["conv_w"], order.reshape(1, 1)),
                                      f"gather_{part}_l{l + 1}"))
                order = order + coming[-1].token
        shift, scale, res_gate = mods[l]
        mods[l] = (shift + order, scale, res_gate)
        xs, sv, gws[l] = _layer_fwd(xs, mods[l], small[l], gws[l], d=d, tag=f"l{l}",
                                    more_weights=lambda o, rest=rest: _as_weights(rest.finish((o,)), d, REST))
        saved.append(sv)
        if coming is not None:
            gws.append(_as_weights(coming[0].finish((xs,)), d, FIRST))
            rest = coming[1]
    dx, loss8, d_final_g = _loss_head(xs, loss_target[0], final_g[None], tm=TM, name="loss_head")
    loss = lax.psum(loss8[0, 0], ("x", "y", "c"))

    parts, small_parts = [dict() for _ in range(depth)], [None] * depth
    going, last = [], []

    def send(queue, keys, l, name):
        def start(grads_so_far):
            grads, pickers, shapes = _scatter_spec(grads_so_far, keys, d=d)
            queue.append((l, keys, _Behind(grads, shapes, pickers, [_whole] * len(grads), (grads[0],), name)))
            return queue[-1][2].token
        return start

    for l in reversed(range(depth)):
        if going:
            shift, scale, res_gate = mods[l]
            mods[l] = (shift, scale, res_gate + going[-1][2].token)
        lowest = l == 0
        dx, big, small_parts[l] = _layer_bwd(
            dx, saved[l], mods[l], small[l], gws[l], d=d, tag=f"l{l}",
            send_early=send(going, EARLY, l, f"scatter_early_l{l}") if lowest else None,
            send_late=send(last, LATE, l, f"scatter_late_l{l}") if lowest else None)
        if not lowest:
            send(going, BIG, l, f"scatter_grads_l{l}")(big)
    for l, keys, behind in going:
        parts[l].update(zip(keys, behind.finish((dx,))))

    pieces = [small_parts[l][k] for l in range(depth) for k in SMALL_PER_LAYER] + [d_final_g]
    sizes = [p.size for p in pieces]
    gathered = _gather_small(_pack(pieces), name="gather_small_grads")
    zero_conv = jnp.zeros((3, cc), F32)

    def packed(src):
        per_layer = lambda l: [src["norm_g"][l], src["b_ada"][l], src["pool_scale"][l], src["b_gate"][l],
                               zero_conv]
        return _pack([p for l in range(depth) for p in per_layer(l)] + [src["final_g"]])[None]

    small_out = _adamw(packed(w), packed(m), packed(v), gathered, 0, None, tr=gathered.shape[1],
                       name="adamw_small")

    def unpack(flat):
        flat = flat.reshape(-1)
        out, off = [], 0
        for n in sizes:
            out.append(flat[off:off + n])
            off += n
        return out

    small_res = [unpack(a) for a in small_out]
    k_per = len(SMALL_PER_LAYER)

    def small_stack(which, pos, shape):
        return jnp.stack([small_res[which][l * k_per + pos] for l in range(depth)]).reshape(shape)

    res = {}
    for pos, name in ((0, "norm_g"), (1, "b_ada"), (2, "pool_scale"), (3, "b_gate")):
        res[name] = tuple(small_stack(k, pos, w[name].shape) for k in range(4))
    res["final_g"] = tuple(small_res[k][-1].reshape(final_g.shape) for k in range(4))

    ncol = cc // N_DEV
    conv_total = small_stack(0, 4, (depth * 3, cc))
    conv_mine = lax.dynamic_slice_in_dim(conv_total, my_idx * ncol, ncol, axis=1)
    pad8 = lambda a: jnp.pad(a.reshape(depth * 3, ncol), ((0, 8 - depth * 3), (0, 0)))[None]
    conv_out = _adamw(pad8(conv_w), pad8(m["conv_w"]), pad8(v["conv_w"]), pad8(conv_mine), 0, None, tr=8,
                      name="adamw_conv_w")
    res["conv_w"] = tuple(a[0, :depth * 3].reshape(conv_w.shape) for a in conv_out)

    n_ada = 3 * d // N_DEV
    mod_off = [sum(sizes[:l * k_per + 1]) // LANE for l in range(depth)]
    dmod = jnp.stack([
        lax.dynamic_slice_in_dim(gathered[:, mod_off[l]:mod_off[l] + 3 * d // LANE].reshape(N_DEV, 3 * d),
                                 my_idx * n_ada, n_ada, axis=1) for l in range(depth)])
    g_w_ada = _ada_bwd(sc_all[::8], dmod, name="ada_bwd")
    res["w_ada"] = _adamw_layers(w_ada, m["w_ada"], v["w_ada"], [g_w_ada[l][None] for l in range(depth)],
                                 tr=128, name="adamw_w_ada")

    def big_adamw(name):
        out = _adamw_layers(w[name], m[name], v[name], [parts[l][name] for l in range(depth)], tr=128,
                            name=f"adamw_{name}")
        res[name] = tuple(a.reshape(w[name].shape) for a in out)

    for name in EARLY:
        big_adamw(name)
    for l, keys, behind in last:
        parts[l].update(zip(keys, behind.finish(tuple(res[name][1] for name in ("w_ada",) + EARLY))))
    for name in LATE:
        big_adamw(name)

    outs = [loss, dx[None]]
    for k in range(4):
        outs += [res[name][k] for name in names]
    return tuple(outs)
```
